```python
import math
import jax, jax.numpy as jnp
from jax import lax
import numpy as np

D_MODEL = 4096
BATCH = 2
SEQ = 4096
DEPTH = 1
DEC_BATCH = 128
DEC_SEQ = 4
PAST_LEN = 2048
PAGE_SIZE = 128

MEM_LEN = 256
M_HEADS = 8
M_DQK = D_MODEL // (2 * M_HEADS)
M_DV = D_MODEL // M_HEADS
M_CHUNK = 64
A_HEAD_DIM = 128
A_HEADS = D_MODEL // 512
DIL_GROUPS = ((128, 1), (512, 4), (2048, 16))
N_GROUPS = 3
ROPE_THETA = 10000.0
X_HEADS = 4
X_HEAD_DIM = 128
D_FF = 4 * D_MODEL
EPS = 1e-6

W_MQK = M_HEADS * M_DQK
W_MV = M_HEADS * M_DV
W_AH = A_HEADS * A_HEAD_DIM
W_XQ = X_HEADS * X_HEAD_DIM
SPLIT_SIZES = (W_MQK, W_MQK, W_MV, W_MV, M_HEADS, M_HEADS, 3 * N_GROUPS * W_AH, W_XQ, 3 * D_MODEL)
D_IN = 2 * W_MQK + 2 * W_MV + 2 * M_HEADS + 3 * N_GROUPS * W_AH + W_XQ + 3 * D_MODEL

kernel_name = 'hybrid_mlstm_dilated_decoder_step'


def _rmsnorm(x, g):
    xf = x.astype(jnp.float32)
    y = xf * lax.rsqrt(jnp.mean(xf * xf, axis=-1, keepdims=True) + EPS)
    return (y * g.astype(jnp.float32)).astype(x.dtype)


def _rope(x, pos):
    hd = x.shape[-1]
    inv = ROPE_THETA ** (-jnp.arange(0, hd, 2, dtype=jnp.float32) / hd)
    ang = pos.astype(jnp.float32)[:, None] * inv[None, :]
    cos = jnp.cos(ang)[:, None, :]
    sin = jnp.sin(ang)[:, None, :]
    xf = x.astype(jnp.float32)
    x1, x2 = xf[..., : hd // 2], xf[..., hd // 2:]
    return jnp.concatenate([x1 * cos - x2 * sin, x1 * sin + x2 * cos], axis=-1).astype(x.dtype)


def _split_cols(z):
    offs, acc = [], 0
    for s in SPLIT_SIZES[:-1]:
        acc += s
        offs.append(acc)
    return jnp.split(z, offs, axis=-1)


def _mlstm_chunkwise(q, k, v, i_pre, logf, C0, n0, m0):
    f32 = jnp.float32
    B, T, H, dk = q.shape
    dv = v.shape[-1]
    L = math.gcd(T, M_CHUNK)
    nc = T // L

    def chunks(a):
        a = a.astype(f32).reshape((B, nc, L, H) + a.shape[3:])
        return jnp.swapaxes(jnp.moveaxis(a, 1, 0), 2, 3)

    causal = jnp.tril(jnp.ones((L, L), dtype=bool))

    def step(carry, xs):
        C, n, m = carry
        qc, kc, vc, ic, fc = xs
        b = jnp.cumsum(fc, axis=-1)
        a = b + m[..., None]
        logw = jnp.where(causal, b[..., :, None] - b[..., None, :] + ic[..., None, :], -jnp.inf)
        mt = jnp.maximum(a, jnp.max(logw, axis=-1))
        w_inter = jnp.exp(a - mt)
        w_intra = jnp.exp(logw - mt[..., None])
        s = jnp.einsum('bhtk,bhsk->bhts', qc, kc) * w_intra
        num = w_inter[..., None] * jnp.einsum('bhvk,bhtk->bhtv', C, qc) + jnp.einsum('bhts,bhsv->bhtv', s, vc)
        nq = w_inter * jnp.einsum('bhk,bhtk->bht', n, qc) + jnp.sum(s, axis=-1)
        h = num / jnp.maximum(jnp.abs(nq), jnp.exp(-mt))[..., None]
        m_end = mt[..., -1]
        w_c = jnp.exp(a[..., -1] - m_end)
        w_s = w_intra[..., -1, :]
        C_new = w_c[..., None, None] * C + jnp.einsum('bhsv,bhsk->bhvk', vc * w_s[..., None], kc)
        n_new = w_c[..., None] * n + jnp.einsum('bhs,bhsk->bhk', w_s, kc)
        return (C_new, n_new, m_end), h

    (C1, n1, m1), hs = lax.scan(
        step, (C0.astype(f32), n0.astype(f32), m0.astype(f32)),
        (chunks(q), chunks(k), chunks(v), chunks(i_pre), chunks(logf)))
    h = jnp.swapaxes(jnp.moveaxis(hs, 0, 1), 2, 3).reshape(B, T, H, dv)
    return h, (C1, n1, m1)


def _dilated_prompt(q, k, v, window, r):
    f32 = jnp.float32
    B, T, H, hd = q.shape
    span = window // r
    n_sub = T // r
    nb = -(-n_sub // span)
    pad = nb * span - n_sub

    def blocks(a):
        a = a.reshape(B, n_sub, r, H, hd).transpose(0, 2, 1, 3, 4)
        a = jnp.pad(a, ((0, 0), (0, 0), (0, pad), (0, 0), (0, 0)))
        return a.reshape(B, r, nb, span, H, hd)

    def with_prev(a):
        prev = jnp.concatenate([jnp.zeros_like(a[:, :, :1]), a[:, :, :-1]], axis=2)
        return jnp.concatenate([prev, a], axis=3)

    qb = blocks(q)
    kk = with_prev(blocks(k))
    vv = with_prev(blocks(v))
    s = jnp.einsum('bcnqhd,bcnkhd->bcnhqk', qb, kk).astype(f32) * (hd ** -0.5)
    qi = jnp.arange(span)[:, None]
    ki = jnp.arange(2 * span)[None, :]
    band = (ki >= qi) & (ki <= qi + span)
    has_prev = (jnp.arange(nb) > 0)[:, None, None] | (ki >= span)[None]
    mask = band[None] & has_prev
    s = jnp.where(mask[None, None, :, None], s, -jnp.inf)
    m = jnp.max(s, axis=-1, keepdims=True)
    p = jnp.exp(s - m)
    den = jnp.sum(p, axis=-1)
    o = jnp.einsum('bcnhqk,bcnkhd->bcnqhd', p.astype(v.dtype), vv).astype(f32)
    o = o / jnp.moveaxis(den, -1, -2)[..., None]
    lse = jnp.moveaxis(m[..., 0] + jnp.log(den), -1, -2)

    def unblock(a):
        a = a.reshape((B, r, nb * span) + a.shape[4:])[:, :, :n_sub]
        a = jnp.moveaxis(a, 1, 2)
        return a.reshape((B, T) + a.shape[3:])

    return unblock(o), unblock(lse)


def _dilated_sample(q, k, v, k_buf, v_buf, window, r):
    f32 = jnp.float32
    B, S, H, hd = q.shape
    Lb = k_buf.shape[1]
    span = window // r
    kc = jnp.concatenate([k_buf, k], axis=1)
    vc = jnp.concatenate([v_buf, v], axis=1)
    idx = Lb + jnp.arange(S)[:, None] - r * jnp.arange(span + 1)[None, :]
    valid = idx >= 0
    flat = jnp.clip(idx, 0).reshape(-1)
    kg = jnp.take(kc, flat, axis=1).reshape(B, S, span + 1, H, hd)
    vg = jnp.take(vc, flat, axis=1).reshape(B, S, span + 1, H, hd)
    s = jnp.einsum('bqhd,bqjhd->bhqj', q, kg).astype(f32) * (hd ** -0.5)
    s = jnp.where(valid[None, None], s, -jnp.inf)
    m = jnp.max(s, axis=-1, keepdims=True)
    p = jnp.exp(s - m)
    den = jnp.sum(p, axis=-1)
    o = jnp.einsum('bhqj,bqjhd->bqhd', p.astype(v.dtype), vg).astype(f32)
    o = o / jnp.transpose(den, (0, 2, 1))[..., None]
    lse = jnp.transpose(m[..., 0] + jnp.log(den), (0, 2, 1))
    return o, lse


def _mem_kv(mem, g_mem, w_mem_kv):
    B, M, _ = mem.shape
    kv = _rmsnorm(mem, g_mem) @ w_mem_kv
    k, v = jnp.split(kv, 2, axis=-1)
    return k.reshape(B, M, X_HEADS, X_HEAD_DIM), v.reshape(B, M, X_HEADS, X_HEAD_DIM)


def _cross(q, mk, mv):
    s = jnp.einsum('bthd,bmhd->bhtm', q, mk).astype(jnp.float32) * (X_HEAD_DIM ** -0.5)
    p = jax.nn.softmax(s, axis=-1).astype(mv.dtype)
    return jnp.einsum('bhtm,bmhd->bthd', p, mv)


def _layer(x, pos, mem_k, mem_v, mstate, wbufs, g_mix, w_in, b_igate, b_fgate, b_branch,
           w_br_a, w_br_b, w_br_c, w_out, g_mlp, w_up, w_down):
    f32 = jnp.float32
    B, T, _ = x.shape
    h = _rmsnorm(x, g_mix)
    mq, mk, mv, mo, ig, fg, aqkv, xq, br = _split_cols(h @ w_in)
    q = mq.reshape(B, T, M_HEADS, M_DQK)
    k = mk.reshape(B, T, M_HEADS, M_DQK) * (M_DQK ** -0.5)
    v = mv.reshape(B, T, M_HEADS, M_DV)
    i_pre = ig.astype(f32) + b_igate.astype(f32)
    logf = jax.nn.log_sigmoid(fg.astype(f32) + b_fgate.astype(f32))
    if mstate is None:
        C0 = jnp.zeros((B, M_HEADS, M_DV, M_DQK), f32)
        n0 = jnp.zeros((B, M_HEADS, M_DQK), f32)
        m0 = jnp.zeros((B, M_HEADS), f32)
    else:
        C0, n0, m0 = mstate
    hm, new_m = _mlstm_chunkwise(q, k, v, i_pre, logf, C0, n0, m0)
    y_a = (jax.nn.sigmoid(mo) * hm.reshape(B, T, W_MV).astype(x.dtype)) @ w_br_a
    aqkv = aqkv.reshape(B, T, N_GROUPS, 3, A_HEADS, A_HEAD_DIM)
    outs, lses, rows = [], [], []
    for g, (window, r) in enumerate(DIL_GROUPS):
        qg = _rope(aqkv[:, :, g, 0], pos)
        kg = _rope(aqkv[:, :, g, 1], pos)
        vg = aqkv[:, :, g, 2]
        if wbufs is None:
            o, lse = _dilated_prompt(qg, kg, vg, window, r)
            keep = min(window, T)
            rows += [kg[:, T - keep:], vg[:, T - keep:]]
        else:
            o, lse = _dilated_sample(qg, kg, vg, wbufs[g][0], wbufs[g][1], window, r)
            rows += [kg, vg]
        outs.append(o)
        lses.append(lse)
    alpha = jax.nn.softmax(jnp.stack(lses, axis=0), axis=0)
    yb = jnp.sum(alpha[..., None] * jnp.stack(outs, axis=0), axis=0)
    y_b = yb.reshape(B, T, W_AH).astype(x.dtype) @ w_br_b
    y_c = _cross(xq.reshape(B, T, X_HEADS, X_HEAD_DIM), mem_k, mem_v).reshape(B, T, W_XQ) @ w_br_c
    g_a, g_b, g_c = jnp.split(jax.nn.sigmoid(br + b_branch), 3, axis=-1)
    x = x + (g_a * y_a + g_b * y_b + g_c * y_c) @ w_out
    u = jax.nn.relu(_rmsnorm(x, g_mlp) @ w_up)
    x = x + (u * u) @ w_down
    return x, new_m, rows


def setup_inputs(seed: int = 0) -> dict:
    key = jax.random.key(seed)
    ks = jax.random.split(key, 32)
    f32 = jnp.float32

    def nrm(k, shape, scale=1.0):
        return jax.random.normal(k, shape, f32) * scale

    D = D_MODEL
    lb = [min(w, PAST_LEN) for w, _ in DIL_GROUPS]
    wshape = lambda L: (DEPTH, DEC_BATCH, L, A_HEADS, A_HEAD_DIM)
    return {
        'x_prompt': nrm(ks[0], (BATCH, SEQ, D)),
        'x_sample': nrm(ks[1], (DEC_BATCH, DEC_SEQ, D)),
        'state_mlstm_C': nrm(ks[2], (DEPTH, DEC_BATCH, M_HEADS, M_DV, M_DQK)),
        'state_mlstm_n': nrm(ks[3], (DEPTH, DEC_BATCH, M_HEADS, M_DQK)),
        'state_mlstm_m': nrm(ks[4], (DEPTH, DEC_BATCH, M_HEADS)),
        'cache_win_k_g0': nrm(ks[5], wshape(lb[0])),
        'cache_win_v_g0': nrm(ks[6], wshape(lb[0])),
        'cache_win_k_g1': nrm(ks[7], wshape(lb[1])),
        'cache_win_v_g1': nrm(ks[8], wshape(lb[1])),
        'cache_win_k_g2': nrm(ks[9], wshape(lb[2])),
        'cache_win_v_g2': nrm(ks[10], wshape(lb[2])),
        'cache_mem_k': nrm(ks[11], (DEPTH, DEC_BATCH, MEM_LEN, X_HEADS, X_HEAD_DIM)),
        'cache_mem_v': nrm(ks[12], (DEPTH, DEC_BATCH, MEM_LEN, X_HEADS, X_HEAD_DIM)),
        'mem_prompt': nrm(ks[13], (BATCH, MEM_LEN, D)),
        'g_mix': 1.0 + nrm(ks[14], (DEPTH, D), 0.02),
        'w_in': nrm(ks[15], (DEPTH, D, D_IN), D ** -0.5),
        'b_igate': nrm(ks[16], (DEPTH, M_HEADS), 0.1),
        'b_fgate': 3.0 + jnp.linspace(0.0, 3.0, M_HEADS, dtype=f32)[None, :] + nrm(ks[17], (DEPTH, M_HEADS), 0.1),
        'b_branch': nrm(ks[18], (DEPTH, 3 * D), 0.01),
        'g_mem': 1.0 + nrm(ks[19], (DEPTH, D), 0.02),
        'w_mem_kv': nrm(ks[20], (DEPTH, D, 2 * W_XQ), D ** -0.5),
        'w_br_a': nrm(ks[21], (DEPTH, W_MV, D), W_MV ** -0.5),
        'w_br_b': nrm(ks[22], (DEPTH, W_AH, D), W_AH ** -0.5),
        'w_br_c': nrm(ks[23], (DEPTH, W_XQ, D), W_XQ ** -0.5),
        'w_out': nrm(ks[24], (DEPTH, D, D), D ** -0.5),
        'g_mlp': 1.0 + nrm(ks[25], (DEPTH, D), 0.02),
        'w_up': nrm(ks[26], (DEPTH, D, D_FF), D ** -0.5),
        'w_down': nrm(ks[27], (DEPTH, D_FF, D), D_FF ** -0.5),
        'g_final': 1.0 + nrm(ks[28], (D,), 0.02),
    }


def reference(x_prompt, x_sample, state_mlstm_C, state_mlstm_n, state_mlstm_m,
              cache_win_k_g0, cache_win_v_g0, cache_win_k_g1, cache_win_v_g1,
              cache_win_k_g2, cache_win_v_g2, cache_mem_k, cache_mem_v, mem_prompt,
              g_mix, w_in, b_igate, b_fgate, b_branch, g_mem, w_mem_kv,
              w_br_a, w_br_b, w_br_c, w_out, g_mlp, w_up, w_down, g_final):
    sdt = state_mlstm_C.dtype
    pos_p = jnp.arange(x_prompt.shape[1], dtype=jnp.int32)
    pos_s = PAST_LEN + jnp.arange(x_sample.shape[1], dtype=jnp.int32)
    xp, xs = x_prompt, x_sample
    p_C, p_n, p_m, p_mk, p_mv = [], [], [], [], []
    s_C, s_n, s_m = [], [], []
    p_rows = [[] for _ in range(2 * N_GROUPS)]
    s_rows = [[] for _ in range(2 * N_GROUPS)]
    for l in range(DEPTH):
        wts = (g_mix[l], w_in[l], b_igate[l], b_fgate[l], b_branch[l],
               w_br_a[l], w_br_b[l], w_br_c[l], w_out[l], g_mlp[l], w_up[l], w_down[l])
        mk_p, mv_p = _mem_kv(mem_prompt, g_mem[l], w_mem_kv[l])
        xp, (C1, n1, m1), rows = _layer(xp, pos_p, mk_p, mv_p, None, None, *wts)
        p_C.append(C1.astype(sdt))
        p_n.append(n1.astype(sdt))
        p_m.append(m1.astype(sdt))
        p_mk.append(mk_p)
        p_mv.append(mv_p)
        for i in range(2 * N_GROUPS):
            p_rows[i].append(rows[i])
        bufs = ((cache_win_k_g0[l], cache_win_v_g0[l]),
                (cache_win_k_g1[l], cache_win_v_g1[l]),
                (cache_win_k_g2[l], cache_win_v_g2[l]))
        mst = (state_mlstm_C[l], state_mlstm_n[l], state_mlstm_m[l])
        xs, (C2, n2, m2), rows2 = _layer(xs, pos_s, cache_mem_k[l], cache_mem_v[l], mst, bufs, *wts)
        s_C.append(C2.astype(sdt))
        s_n.append(n2.astype(sdt))
        s_m.append(m2.astype(sdt))
        for i in range(2 * N_GROUPS):
            s_rows[i].append(rows2[i])
    stk = jnp.stack
    return (_rmsnorm(xp, g_final), _rmsnorm(xs, g_final),
            stk(p_C), stk(p_n), stk(p_m),
            stk(p_rows[0]), stk(p_rows[1]), stk(p_rows[2]), stk(p_rows[3]), stk(p_rows[4]), stk(p_rows[5]),
            stk(p_mk), stk(p_mv),
            stk(s_C), stk(s_n), stk(s_m),
            stk(s_rows[0]), stk(s_rows[1]), stk(s_rows[2]), stk(s_rows[3]), stk(s_rows[4]), stk(s_rows[5]))
```

```python
import functools
import math

import numpy as np
import jax
import jax.numpy as jnp
from jax import lax
from jax.experimental import pallas as pl
from jax.experimental.pallas import tpu as pltpu

F32 = jnp.float32
BF16 = jnp.bfloat16
NEG_INF = float("-inf")

D_MODEL = 4096
BATCH = 2
SEQ = 4096
DEC_BATCH = 128
DEC_SEQ = 4
PAST_LEN = 2048
MEM_LEN = 256
M_HEADS = 8
M_DQK = D_MODEL // (2 * M_HEADS)
M_DV = D_MODEL // M_HEADS
A_HEAD_DIM = 128
A_HEADS = D_MODEL // 512
DIL_GROUPS = ((128, 1), (512, 4), (2048, 16))
N_GROUPS = 3
ROPE_THETA = 10000.0
X_HEADS = 4
X_HEAD_DIM = 128
D_FF = 4 * D_MODEL
EPS = 1e-6

W_MQK = M_HEADS * M_DQK
W_MV = M_HEADS * M_DV
W_AH = A_HEADS * A_HEAD_DIM
W_XQ = X_HEADS * X_HEAD_DIM
N_PROMPT = BATCH * SEQ
N_SAMPLE = DEC_BATCH * DEC_SEQ
N_ALL = N_PROMPT + N_SAMPLE

COL_GATES = 2 * W_MQK + 2 * W_MV
COL_REST = COL_GATES + 2 * M_HEADS
W_AQKV = 3 * N_GROUPS * W_AH
COL2_XQ = W_AQKV
COL2_BR = W_AQKV + W_XQ
W_REST = W_AQKV + W_XQ + 3 * D_MODEL

LANES = 128
VMEM_LIMIT_BYTES = 56 * 1024 * 1024

MM_TM = 1088
MM_TN = 512
MERGE_TM = 544
DOWN_TM = 2176
DOWN_TK = 1024
NORM_ROWS = 256
CHUNK = 128
SPAN = 128


def _params(n_axes):
    return pltpu.CompilerParams(
        dimension_semantics=("arbitrary",) * n_axes,
        vmem_limit_bytes=VMEM_LIMIT_BYTES,
    )


def _rmsnorm_body(x_ref, g_ref, o_ref):
    x = x_ref[...]
    ms = jnp.mean(x * x, axis=-1, keepdims=True)
    o_ref[...] = ((x * lax.rsqrt(ms + EPS)) * g_ref[...]).astype(o_ref.dtype)


def _rmsnorm(x, g, out_dtype, *, row_start=0, n_rows=None):
    n_rows = x.shape[0] if n_rows is None else n_rows
    d = x.shape[1]
    off = row_start // NORM_ROWS
    return pl.pallas_call(
        _rmsnorm_body,
        grid=(n_rows // NORM_ROWS,),
        in_specs=[
            pl.BlockSpec((NORM_ROWS, d), lambda i: (i + off, 0)),
            pl.BlockSpec((1, d), lambda i: (0, 0)),
        ],
        out_specs=pl.BlockSpec((NORM_ROWS, d), lambda i: (i, 0)),
        out_shape=jax.ShapeDtypeStruct((n_rows, d), out_dtype),
        compiler_params=_params(1),
        name="rmsnorm",
    )(x, g.reshape(1, d))


def _ep_plain(acc, extra, o_ref):
    o_ref[...] = acc.astype(o_ref.dtype)


def _ep_residual(acc, extra, o_ref):
    o_ref[...] = extra[0][...] + acc


def _ep_relu2(acc, extra, o_ref):
    r = jnp.maximum(acc, 0.0)
    o_ref[...] = (r * r).astype(o_ref.dtype)


def _ep_rope(acc, extra, o_ref):
    cos_ref, sin_ref = extra
    j = pl.program_id(0)
    tiles_per_part = W_AH // MM_TN
    is_rope = (j < W_AQKV // MM_TN) & ((j // tiles_per_part) % 3 != 2)

    @pl.when(is_rope)
    def _():
        cos2 = cos_ref[...]
        sin2 = sin_ref[...]
        for s in range(MM_TN // A_HEAD_DIM):
            x = acc[:, s * A_HEAD_DIM:(s + 1) * A_HEAD_DIM]
            o_ref[:, s * A_HEAD_DIM:(s + 1) * A_HEAD_DIM] = (
                x * cos2 + pltpu.roll(x, A_HEAD_DIM // 2, axis=1) * sin2)

    @pl.when(jnp.logical_not(is_rope))
    def _():
        o_ref[...] = acc


def _mm_body(*refs, cast_b, epilogue, n_extra):
    a_ref, b_ref = refs[0], refs[1]
    extra = refs[2:2 + n_extra]
    o_ref = refs[2 + n_extra]
    if cast_b:
        b_scr = refs[3 + n_extra]

        @pl.when(pl.program_id(1) == 0)
        def _():
            b_scr[...] = b_ref[...].astype(BF16)

        b = b_scr[...]
    else:
        b = b_ref[...]
    acc = jnp.dot(a_ref[...], b, preferred_element_type=F32)
    epilogue(acc, extra, o_ref)


def _matmul(a, b, *, n_cols, col_start=0, tm=MM_TM, tn=MM_TN, out_dtype=F32,
            epilogue=_ep_plain, extras=(), extra_specs=(), name="matmul"):
    m, k = a.shape
    off = col_start // tn
    cast_b = b.dtype != BF16
    scratch = [pltpu.VMEM((k, tn), BF16)] if cast_b else []
    return pl.pallas_call(
        functools.partial(_mm_body, cast_b=cast_b, epilogue=epilogue, n_extra=len(extras)),
        grid=(n_cols // tn, m // tm),
        in_specs=[
            pl.BlockSpec((tm, k), lambda j, i: (i, 0)),
            pl.BlockSpec((k, tn), lambda j, i: (0, j + off)),
            *extra_specs,
        ],
        out_specs=pl.BlockSpec((tm, tn), lambda j, i: (i, j)),
        out_shape=jax.ShapeDtypeStruct((m, n_cols), out_dtype),
        scratch_shapes=scratch,
        compiler_params=_params(2),
        name=name,
    )(a, b, *extras)


def _mm_ksplit_body(a_ref, b_ref, res_ref, o_ref, acc_ref):
    kk = pl.program_id(2)

    @pl.when(kk == 0)
    def _():
        acc_ref[...] = jnp.zeros_like(acc_ref)

    acc_ref[...] += jnp.dot(a_ref[...], b_ref[...].astype(BF16), preferred_element_type=F32)

    @pl.when(kk == pl.num_programs(2) - 1)
    def _():
        o_ref[...] = res_ref[...] + acc_ref[...]


def _matmul_ksplit_residual(a, b, res, *, tm=DOWN_TM, tn=MM_TN, tk=DOWN_TK):
    m, k = a.shape
    n = b.shape[1]
    return pl.pallas_call(
        _mm_ksplit_body,
        grid=(n // tn, m // tm, k // tk),
        in_specs=[
            pl.BlockSpec((tm, tk), lambda j, i, kk: (i, kk)),
            pl.BlockSpec((tk, tn), lambda j, i, kk: (kk, j)),
            pl.BlockSpec((tm, tn), lambda j, i, kk: (i, j)),
        ],
        out_specs=pl.BlockSpec((tm, tn), lambda j, i, kk: (i, j)),
        out_shape=jax.ShapeDtypeStruct((m, n), F32),
        scratch_shapes=[pltpu.VMEM((tm, tn), F32)],
        compiler_params=_params(3),
        name="mlp_down",
    )(a, b, res)


def _sigmoid(x):
    return 1.0 / (1.0 + jnp.exp(-x))


def _merge_body(a_ref, yb_ref, yc_ref, wa_ref, wb_ref, wc_ref,
                ga_ref, gb_ref, gc_ref, ba_ref, bb_ref, bc_ref,
                o_ref, wa_scr, wb_scr, wc_scr):
    @pl.when(pl.program_id(1) == 0)
    def _():
        wa_scr[...] = wa_ref[...].astype(BF16)
        wb_scr[...] = wb_ref[...].astype(BF16)
        wc_scr[...] = wc_ref[...].astype(BF16)

    y_a = jnp.dot(a_ref[...], wa_scr[...], preferred_element_type=F32)
    y_b = jnp.dot(yb_ref[...], wb_scr[...], preferred_element_type=F32)
    y_c = jnp.dot(yc_ref[...], wc_scr[...], preferred_element_type=F32)
    g_a = _sigmoid(ga_ref[...] + ba_ref[...])
    g_b = _sigmoid(gb_ref[...] + bb_ref[...])
    g_c = _sigmoid(gc_ref[...] + bc_ref[...])
    o_ref[...] = (g_a * y_a + g_b * y_b + g_c * y_c).astype(o_ref.dtype)


def _branch_merge(a_in, yb, yc, w_a, w_b, w_c, z_rest, b_branch, *, tm=MERGE_TM, tn=MM_TN):
    m = a_in.shape[0]
    nj = D_MODEL // tn
    br0 = COL2_BR // tn

    def gate_spec(part):
        return pl.BlockSpec((tm, tn), lambda j, i: (i, br0 + part * nj + j))

    def bias_spec(part):
        return pl.BlockSpec((1, tn), lambda j, i: (0, part * nj + j))

    return pl.pallas_call(
        _merge_body,
        grid=(nj, m // tm),
        in_specs=[
            pl.BlockSpec((tm, W_MV), lambda j, i: (i, 0)),
            pl.BlockSpec((tm, W_AH), lambda j, i: (i, 0)),
            pl.BlockSpec((tm, W_XQ), lambda j, i: (i, 0)),
            pl.BlockSpec((W_MV, tn), lambda j, i: (0, j)),
            pl.BlockSpec((W_AH, tn), lambda j, i: (0, j)),
            pl.BlockSpec((W_XQ, tn), lambda j, i: (0, j)),
            gate_spec(0), gate_spec(1), gate_spec(2),
            bias_spec(0), bias_spec(1), bias_spec(2),
        ],
        out_specs=pl.BlockSpec((tm, tn), lambda j, i: (i, j)),
        out_shape=jax.ShapeDtypeStruct((m, D_MODEL), BF16),
        scratch_shapes=[pltpu.VMEM((W_MV, tn), BF16), pltpu.VMEM((W_AH, tn), BF16),
                        pltpu.VMEM((W_XQ, tn), BF16)],
        compiler_params=_params(2),
        name="branch_merge",
    )(a_in, yb, yc, w_a, w_b, w_c, z_rest, z_rest, z_rest, b_branch, b_branch, b_branch)


def _log_sigmoid(x):
    return jnp.minimum(x, 0.0) - jnp.log1p(jnp.exp(-jnp.abs(x)))


def _mlstm_chunk(q, k, v, irow, frow, c_state, n_state, m_state, n_valid):
    L = q.shape[0]
    ti = lax.broadcasted_iota(jnp.int32, (L, L), 0)
    si = lax.broadcasted_iota(jnp.int32, (L, L), 1)
    causal = si <= ti
    eye = si == ti
    f_b = jnp.broadcast_to(frow, (L, L))
    i_b = jnp.broadcast_to(irow, (L, L))
    bcol = jnp.sum(jnp.where(causal, f_b, 0.0), axis=1, keepdims=True)
    fcol = jnp.sum(jnp.where(eye, f_b, 0.0), axis=1, keepdims=True)
    icol = jnp.sum(jnp.where(eye, i_b, 0.0), axis=1, keepdims=True)
    brow = jnp.sum(jnp.where(ti <= si, jnp.broadcast_to(fcol, (L, L)), 0.0), axis=0, keepdims=True)

    acol = bcol + m_state
    logw = jnp.where(causal, bcol - brow + irow, NEG_INF)
    mt = jnp.maximum(acol, jnp.max(logw, axis=1, keepdims=True))
    w_inter = jnp.exp(acol - mt)
    w_intra = jnp.exp(logw - mt)

    qb = q.astype(BF16)
    kb = k.astype(BF16)
    nt = (((1,), (1,)), ((), ()))
    s = lax.dot_general(qb, kb, nt, preferred_element_type=F32) * w_intra
    inter = lax.dot_general(qb, c_state.astype(BF16), nt, preferred_element_type=F32)
    num = w_inter * inter + jnp.dot(s.astype(BF16), v.astype(BF16), preferred_element_type=F32)
    nq = w_inter * jnp.sum(q * n_state, axis=1, keepdims=True) + jnp.sum(s, axis=1, keepdims=True)
    h = num / jnp.maximum(jnp.abs(nq), jnp.exp(-mt))

    last = slice(n_valid - 1, n_valid)
    m_end = mt[last, :]
    w_c = jnp.exp(acol[last, :] - m_end)
    w_s = jnp.exp(bcol[last, :] - bcol + icol - m_end)
    if n_valid < L:
        w_s = jnp.where(lax.broadcasted_iota(jnp.int32, (L, 1), 0) < n_valid, w_s, 0.0)
    tn = (((0,), (0,)), ((), ()))
    c_new = w_c * c_state + lax.dot_general((v * w_s).astype(BF16), kb, tn, preferred_element_type=F32)
    n_new = w_c * n_state + jnp.sum(w_s * k, axis=0, keepdims=True)
    return h, c_new, n_new, m_end


def _gate_rows(g_ref_val, bias_ref, head):
    irow = g_ref_val[0:1, :] + bias_ref[0, head]
    frow = _log_sigmoid(g_ref_val[1:2, :] + bias_ref[1, head])
    return irow, frow


def _mlstm_prompt_body(bias_ref, q_ref, k_ref, v_ref, mo_ref, g_ref,
                       a_ref, c_ref, n_ref, m_ref):
    head = pl.program_id(1)

    @pl.when(pl.program_id(2) == 0)
    def _():
        c_ref[...] = jnp.zeros_like(c_ref)
        n_ref[...] = jnp.zeros_like(n_ref)
        m_ref[...] = jnp.zeros_like(m_ref)

    irow, frow = _gate_rows(g_ref[0], bias_ref, head)
    h, c_new, n_new, m_end = _mlstm_chunk(
        q_ref[...], k_ref[...] * (M_DQK ** -0.5), v_ref[...], irow, frow,
        c_ref[0, 0], n_ref[0], m_ref[0][:, 0:1], CHUNK)
    a_ref[...] = (_sigmoid(mo_ref[...]) * h).astype(a_ref.dtype)
    c_ref[0, 0] = c_new
    n_ref[0] = n_new
    m_ref[0] = jnp.broadcast_to(m_end, (1, LANES))


def _mlstm_prompt(z_main, gates_rows, gate_bias):
    nc = SEQ // CHUNK
    kq = W_MQK // M_DQK
    kv = 2 * W_MQK // M_DV
    ko = kv + M_HEADS
    return pl.pallas_call(
        _mlstm_prompt_body,
        grid=(BATCH, M_HEADS, nc),
        in_specs=[
            pl.BlockSpec(memory_space=pltpu.SMEM),
            pl.BlockSpec((CHUNK, M_DQK), lambda b, h, c: (b * nc + c, h)),
            pl.BlockSpec((CHUNK, M_DQK), lambda b, h, c: (b * nc + c, kq + h)),
            pl.BlockSpec((CHUNK, M_DV), lambda b, h, c: (b * nc + c, kv + h)),
            pl.BlockSpec((CHUNK, M_DV), lambda b, h, c: (b * nc + c, ko + h)),
            pl.BlockSpec((1, 2, CHUNK), lambda b, h, c: (h, 0, b * nc + c)),
        ],
        out_specs=[
            pl.BlockSpec((CHUNK, M_DV), lambda b, h, c: (b * nc + c, h)),
            pl.BlockSpec((1, 1, M_DV, M_DQK), lambda b, h, c: (b, h, 0, 0)),
            pl.BlockSpec((1, 1, M_DQK), lambda b, h, c: (b * M_HEADS + h, 0, 0)),
            pl.BlockSpec((1, 1, LANES), lambda b, h, c: (b * M_HEADS + h, 0, 0)),
        ],
        out_shape=[
            jax.ShapeDtypeStruct((N_PROMPT, W_MV), BF16),
            jax.ShapeDtypeStruct((BATCH, M_HEADS, M_DV, M_DQK), F32),
            jax.ShapeDtypeStruct((BATCH * M_HEADS, 1, M_DQK), F32),
            jax.ShapeDtypeStruct((BATCH * M_HEADS, 1, LANES), F32),
        ],
        compiler_params=_params(3),
        name="mlstm_prompt",
    )(gate_bias, z_main, z_main, z_main, z_main, gates_rows)


def _mlstm_sample_body(bias_ref, q_ref, k_ref, v_ref, mo_ref, g_ref, c0_ref, n0_ref, m0_ref,
                       a_ref, c_ref, n_ref, m_ref, q_scr, k_scr, v_scr):
    head = pl.program_id(1)

    @pl.when((pl.program_id(0) == 0) & (head == 0))
    def _():
        q_scr[...] = jnp.zeros_like(q_scr)
        k_scr[...] = jnp.zeros_like(k_scr)
        v_scr[...] = jnp.zeros_like(v_scr)

    q_scr[0:DEC_SEQ, :] = q_ref[0]
    k_scr[0:DEC_SEQ, :] = k_ref[0] * (M_DQK ** -0.5)
    v_scr[0:DEC_SEQ, :] = v_ref[0]
    irow, frow = _gate_rows(g_ref[0, 0], bias_ref, head)
    h, c_new, n_new, m_end = _mlstm_chunk(
        q_scr[...], k_scr[...], v_scr[...], irow, frow,
        c0_ref[0, 0, 0], n0_ref[0], m0_ref[0], DEC_SEQ)
    a_ref[0] = _sigmoid(mo_ref[0]) * h[0:DEC_SEQ, :]
    c_ref[0, 0] = c_new
    n_ref[0] = n_new
    m_ref[0] = jnp.broadcast_to(m_end, (1, LANES))


def _mlstm_sample(z_s3, gates_rows, gate_bias, c0, n0, m0):
    kq = W_MQK // M_DQK
    kv = 2 * W_MQK // M_DV
    ko = kv + M_HEADS
    bh = DEC_BATCH * M_HEADS
    return pl.pallas_call(
        _mlstm_sample_body,
        grid=(DEC_BATCH, M_HEADS),
        in_specs=[
            pl.BlockSpec(memory_space=pltpu.SMEM),
            pl.BlockSpec((1, DEC_SEQ, M_DQK), lambda b, h: (b, 0, h)),
            pl.BlockSpec((1, DEC_SEQ, M_DQK), lambda b, h: (b, 0, kq + h)),
            pl.BlockSpec((1, DEC_SEQ, M_DV), lambda b, h: (b, 0, kv + h)),
            pl.BlockSpec((1, DEC_SEQ, M_DV), lambda b, h: (b, 0, ko + h)),
            pl.BlockSpec((1, 1, 2, CHUNK), lambda b, h: (b, h, 0, 0)),
            pl.BlockSpec((1, 1, 1, M_DV, M_DQK), lambda b, h: (0, b, h, 0, 0)),
            pl.BlockSpec((1, 1, M_DQK), lambda b, h: (b * M_HEADS + h, 0, 0)),
            pl.BlockSpec((1, 1, 1), lambda b, h: (b * M_HEADS + h, 0, 0)),
        ],
        out_specs=[
            pl.BlockSpec((1, DEC_SEQ, M_DV), lambda b, h: (b, 0, h)),
            pl.BlockSpec((1, 1, M_DV, M_DQK), lambda b, h: (b, h, 0, 0)),
            pl.BlockSpec((1, 1, M_DQK), lambda b, h: (b * M_HEADS + h, 0, 0)),
            pl.BlockSpec((1, 1, LANES), lambda b, h: (b * M_HEADS + h, 0, 0)),
        ],
        out_shape=[
            jax.ShapeDtypeStruct((DEC_BATCH, DEC_SEQ, W_MV), F32),
            jax.ShapeDtypeStruct((DEC_BATCH, M_HEADS, M_DV, M_DQK), F32),
            jax.ShapeDtypeStruct((bh, 1, M_DQK), F32),
            jax.ShapeDtypeStruct((bh, 1, LANES), F32),
        ],
        scratch_shapes=[pltpu.VMEM((CHUNK, M_DQK), F32), pltpu.VMEM((CHUNK, M_DQK), F32),
                        pltpu.VMEM((CHUNK, M_DV), F32)],
        compiler_params=_params(2),
        name="mlstm_sample",
    )(gate_bias, z_s3, z_s3, z_s3, z_s3, gates_rows, c0, n0, m0)


def _dil_prompt_body(q_ref, kp_ref, kc_ref, vp_ref, vc_ref, o_ref, l_ref, *, r, hb):
    first_key = jnp.where(pl.program_id(2) > 0, 0, SPAN)
    qi = lax.broadcasted_iota(jnp.int32, (SPAN, 2 * SPAN), 0)
    ki = lax.broadcasted_iota(jnp.int32, (SPAN, 2 * SPAN), 1)
    ok = (ki >= qi) & (ki <= qi + SPAN) & (ki >= first_key)
    bias = jnp.where(ok, 0.0, NEG_INF)
    scale = A_HEAD_DIM ** -0.5
    nt = (((1,), (1,)), ((), ()))
    for c in range(r):
        rows = pl.ds(c, SPAN, stride=r) if r > 1 else pl.ds(0, SPAN)
        for hh in range(hb):
            cols = pl.ds(hh * A_HEAD_DIM, A_HEAD_DIM)
            q = q_ref[rows, cols].astype(BF16)
            kk = jnp.concatenate([kp_ref[rows, cols], kc_ref[rows, cols]], axis=0).astype(BF16)
            vv = jnp.concatenate([vp_ref[rows, cols], vc_ref[rows, cols]], axis=0).astype(BF16)
            s = lax.dot_general(q, kk, nt, preferred_element_type=F32) * scale + bias
            m = jnp.max(s, axis=1, keepdims=True)
            p = jnp.exp(s - m)
            den = jnp.sum(p, axis=1, keepdims=True)
            o = jnp.dot(p.astype(BF16), vv, preferred_element_type=F32) / den
            o_ref[rows, cols] = o
            l_ref[rows, cols] = jnp.broadcast_to(m + jnp.log(den), (SPAN, A_HEAD_DIM))


def _dilated_prompt(z_rest, g, r, hb):
    rows = SPAN * r
    nblk = SEQ // rows
    wcol = hb * A_HEAD_DIM
    per_part = W_AH // wcol

    def spec(part, prev):
        def imap(b, hg, n):
            nn = jnp.maximum(n - 1, 0) if prev else n
            return (b * nblk + nn, (3 * g + part) * per_part + hg)
        return pl.BlockSpec((rows, wcol), imap)

    out_spec = pl.BlockSpec((rows, wcol), lambda b, hg, n: (b * nblk + n, hg))
    return pl.pallas_call(
        functools.partial(_dil_prompt_body, r=r, hb=hb),
        grid=(BATCH, A_HEADS // hb, nblk),
        in_specs=[spec(0, False), spec(1, True), spec(1, False), spec(2, True), spec(2, False)],
        out_specs=[out_spec, out_spec],
        out_shape=[jax.ShapeDtypeStruct((N_PROMPT, W_AH), F32)] * 2,
        compiler_params=_params(3),
        name=f"dilated_prompt_g{g}",
    )(z_rest, z_rest, z_rest, z_rest, z_rest)


def _cached_attn_body(q_ref, kn_ref, vn_ref, kc_ref, vc_ref, bias_ref, o_ref, l_ref, *, scale):
    n_cached = kc_ref.shape[1] * kc_ref.shape[2]
    q = q_ref[0].astype(BF16)
    kk = jnp.concatenate([kc_ref[0].reshape(n_cached, A_HEAD_DIM), kn_ref[0]], axis=0).astype(BF16)
    vv = jnp.concatenate([vc_ref[0].reshape(n_cached, A_HEAD_DIM), vn_ref[0]], axis=0).astype(BF16)
    nt = (((1,), (1,)), ((), ()))
    s = lax.dot_general(q, kk, nt, preferred_element_type=F32) * scale + bias_ref[...]
    m = jnp.max(s, axis=1, keepdims=True)
    p = jnp.exp(s - m)
    den = jnp.sum(p, axis=1, keepdims=True)
    o_ref[0] = jnp.dot(p.astype(BF16), vv, preferred_element_type=F32) / den
    l_ref[0] = jnp.broadcast_to(m + jnp.log(den), l_ref.shape[1:])


def _dilated_sample_bias(window, r, lb, rc):
    span = window // r
    n_c = (lb // r) * rc * A_HEADS
    rows = np.arange(DEC_SEQ * A_HEADS)
    s_q, h_q = rows // A_HEADS, rows % A_HEADS
    col = np.arange(n_c)
    m_k = col // (rc * A_HEADS)
    c_k = (col % (rc * A_HEADS)) // A_HEADS
    h_k = col % A_HEADS
    delta = (lb + s_q)[:, None] - (m_k * r + c_k)[None, :]
    ok_c = (h_q[:, None] == h_k[None, :]) & (delta % r == 0) & (delta // r <= span) & (delta >= 0)
    coln = np.arange(DEC_SEQ * A_HEADS)
    s_n, h_n = coln // A_HEADS, coln % A_HEADS
    dn = s_q[:, None] - s_n[None, :]
    ok_n = (h_q[:, None] == h_n[None, :]) & (dn >= 0) & (dn % r == 0) & (dn // r <= span)
    ok = np.concatenate([ok_c, ok_n], axis=1)
    return np.where(ok, 0.0, -np.inf).astype(np.float32)


def _dilated_sample(q, kn, vn, cache_k, cache_v, window, r, g):
    lb = cache_k.shape[1]
    assert lb % r == 0 and window % r == 0
    rc = min(r, DEC_SEQ)
    nm = lb // r
    ck = cache_k.reshape(DEC_BATCH, nm, r * A_HEADS, A_HEAD_DIM)
    cv = cache_v.reshape(DEC_BATCH, nm, r * A_HEADS, A_HEAD_DIM)
    bias = jnp.asarray(_dilated_sample_bias(window, r, lb, rc))
    rq = DEC_SEQ * A_HEADS
    row_spec = pl.BlockSpec((1, rq, A_HEAD_DIM), lambda b: (b, 0, 0))
    cache_spec = pl.BlockSpec((1, nm, rc * A_HEADS, A_HEAD_DIM), lambda b: (b, 0, 0, 0))
    return pl.pallas_call(
        functools.partial(_cached_attn_body, scale=A_HEAD_DIM ** -0.5),
        grid=(DEC_BATCH,),
        in_specs=[row_spec, row_spec, row_spec, cache_spec, cache_spec,
                  pl.BlockSpec(bias.shape, lambda b: (0, 0))],
        out_specs=[row_spec, row_spec],
        out_shape=[jax.ShapeDtypeStruct((DEC_BATCH, rq, A_HEAD_DIM), F32)] * 2,
        compiler_params=_params(1),
        name=f"dilated_sample_g{g}",
    )(q, kn, vn, ck, cv, bias)


def _cross_sample_body(q_ref, k_ref, v_ref, bias_ref, o_ref, *, scale):
    q = q_ref[0].astype(BF16)
    nt = (((1,), (1,)), ((), ()))
    s = lax.dot_general(q, k_ref[0].astype(BF16), nt, preferred_element_type=F32) * scale + bias_ref[...]
    m = jnp.max(s, axis=1, keepdims=True)
    e = jnp.exp(s - m)
    p = e / jnp.sum(e, axis=1, keepdims=True)
    o_ref[0] = jnp.dot(p.astype(BF16), v_ref[0].astype(BF16), preferred_element_type=F32)


def _cross_sample(q, mem_k, mem_v):
    rq = DEC_SEQ * X_HEADS
    nk = MEM_LEN * X_HEADS
    ok = (np.arange(rq) % X_HEADS)[:, None] == (np.arange(nk) % X_HEADS)[None, :]
    bias = jnp.asarray(np.where(ok, 0.0, -np.inf).astype(np.float32))
    return pl.pallas_call(
        functools.partial(_cross_sample_body, scale=X_HEAD_DIM ** -0.5),
        grid=(DEC_BATCH,),
        in_specs=[
            pl.BlockSpec((1, rq, X_HEAD_DIM), lambda b: (b, 0, 0)),
            pl.BlockSpec((1, nk, X_HEAD_DIM), lambda b: (b, 0, 0)),
            pl.BlockSpec((1, nk, X_HEAD_DIM), lambda b: (b, 0, 0)),
            pl.BlockSpec((rq, nk), lambda b: (0, 0)),
        ],
        out_specs=pl.BlockSpec((1, rq, X_HEAD_DIM), lambda b: (b, 0, 0)),
        out_shape=jax.ShapeDtypeStruct((DEC_BATCH, rq, X_HEAD_DIM), F32),
        compiler_params=_params(1),
        name="cross_sample",
    )(q, mem_k, mem_v, bias)


def _cross_prompt_body(q_ref, k_ref, v_ref, o_ref):
    scale = X_HEAD_DIM ** -0.5
    nt = (((1,), (1,)), ((), ()))
    for h in range(X_HEADS):
        cols = pl.ds(h * X_HEAD_DIM, X_HEAD_DIM)
        s = lax.dot_general(q_ref[:, cols].astype(BF16), k_ref[:, cols].astype(BF16), nt,
                            preferred_element_type=F32) * scale
        m = jnp.max(s, axis=1, keepdims=True)
        e = jnp.exp(s - m)
        p = e / jnp.sum(e, axis=1, keepdims=True)
        o_ref[:, cols] = jnp.dot(p.astype(BF16), v_ref[:, cols].astype(BF16),
                                 preferred_element_type=F32).astype(o_ref.dtype)


def _cross_prompt(z_rest, mem_kv, *, tq=512):
    nq = SEQ // tq
    return pl.pallas_call(
        _cross_prompt_body,
        grid=(BATCH, nq),
        in_specs=[
            pl.BlockSpec((tq, W_XQ), lambda b, i: (b * nq + i, COL2_XQ // W_XQ)),
            pl.BlockSpec((MEM_LEN, W_XQ), lambda b, i: (b, 0)),
            pl.BlockSpec((MEM_LEN, W_XQ), lambda b, i: (b, 1)),
        ],
        out_specs=pl.BlockSpec((tq, W_XQ), lambda b, i: (b * nq + i, 0)),
        out_shape=jax.ShapeDtypeStruct((N_PROMPT, W_XQ), BF16),
        compiler_params=_params(2),
        name="cross_prompt",
    )(z_rest, mem_kv, mem_kv)


def _combine_body(o0, o1, o2, l0, l1, l2, y_ref):
    a0, a1, a2 = l0[...], l1[...], l2[...]
    mx = jnp.maximum(jnp.maximum(a0, a1), a2)
    e0, e1, e2 = jnp.exp(a0 - mx), jnp.exp(a1 - mx), jnp.exp(a2 - mx)
    tot = e0 + e1 + e2
    y = (e0 / tot) * o0[...] + (e1 / tot) * o1[...] + (e2 / tot) * o2[...]
    y_ref[...] = y.astype(y_ref.dtype)


def _combine_groups(outs, lses, *, rows_per_step):
    n, w = outs[0].shape
    spec = pl.BlockSpec((rows_per_step, w), lambda i: (i, 0))
    return pl.pallas_call(
        _combine_body,
        grid=(n // rows_per_step,),
        in_specs=[spec] * 6,
        out_specs=spec,
        out_shape=jax.ShapeDtypeStruct((n, w), BF16),
        compiler_params=_params(1),
        name="combine_groups",
    )(*outs, *lses)


def _rope_tables():
    pos = jnp.concatenate([
        jnp.tile(jnp.arange(SEQ, dtype=jnp.int32), BATCH),
        jnp.tile(PAST_LEN + jnp.arange(DEC_SEQ, dtype=jnp.int32), DEC_BATCH)])
    inv = ROPE_THETA ** (-jnp.arange(0, A_HEAD_DIM, 2, dtype=F32) / A_HEAD_DIM)
    ang = pos.astype(F32)[:, None] * inv[None, :]
    cos, sin = jnp.cos(ang), jnp.sin(ang)
    return jnp.concatenate([cos, cos], axis=1), jnp.concatenate([-sin, sin], axis=1)


def _layer(x_all, mem_prompt, state_c, state_n, state_m, caches, cache_mem_k, cache_mem_v,
           g_mix, w_in, b_igate, b_fgate, b_branch, g_mem, w_mem_kv,
           w_br_a, w_br_b, w_br_c, w_out, g_mlp, w_up, w_down):
    h_all = _rmsnorm(x_all, g_mix, BF16)

    z_main = _matmul(h_all, w_in, n_cols=COL_GATES, name="proj_main")
    w_gate = jnp.pad(w_in[:, COL_GATES:COL_REST], ((0, 0), (0, LANES - 2 * M_HEADS)))
    z_gate = _matmul(h_all, w_gate, n_cols=LANES, tn=LANES, name="proj_gates")[:, :2 * M_HEADS]
    w_rest = w_in[:, COL_REST:].astype(BF16)
    cos2, sin2 = _rope_tables()
    table_spec = pl.BlockSpec((MM_TM, A_HEAD_DIM), lambda j, i: (i, 0))
    z_rest = _matmul(h_all, w_rest, n_cols=W_REST, epilogue=_ep_rope,
                     extras=(cos2, sin2), extra_specs=(table_spec, table_spec), name="proj_rest")

    gate_bias = jnp.stack([b_igate, b_fgate]).astype(F32)

    gp = z_gate[:N_PROMPT].reshape(N_PROMPT, 2, M_HEADS).transpose(2, 1, 0)
    a_p, c_p, n_p, m_p = _mlstm_prompt(z_main, gp, gate_bias)
    gs = z_gate[N_PROMPT:].reshape(DEC_BATCH, DEC_SEQ, 2, M_HEADS).transpose(0, 3, 2, 1)
    gs = jnp.pad(gs, ((0, 0), (0, 0), (0, 0), (0, CHUNK - DEC_SEQ)))
    z_main_s = z_main[N_PROMPT:].reshape(DEC_BATCH, DEC_SEQ, COL_GATES)
    a_s, c_s, n_s, m_s = _mlstm_sample(
        z_main_s, gs, gate_bias, state_c,
        state_n.reshape(DEC_BATCH * M_HEADS, 1, M_DQK), state_m.reshape(DEC_BATCH * M_HEADS, 1, 1))
    a_in = jnp.concatenate([a_p, a_s.reshape(N_SAMPLE, W_MV).astype(BF16)], axis=0)

    z_rest_s = z_rest[N_PROMPT:]
    heads_per_step = (8, 1, 1)
    outs_p, lses_p, outs_s, lses_s, rows_p, rows_s = [], [], [], [], [], []
    rq = DEC_SEQ * A_HEADS
    for g, (window, r) in enumerate(DIL_GROUPS):
        o, l = _dilated_prompt(z_rest, g, r, heads_per_step[g])
        outs_p.append(o)
        lses_p.append(l)
        c0 = 3 * g * W_AH
        qs = z_rest_s[:, c0:c0 + W_AH].reshape(DEC_BATCH, rq, A_HEAD_DIM)
        ks = z_rest_s[:, c0 + W_AH:c0 + 2 * W_AH].reshape(DEC_BATCH, rq, A_HEAD_DIM)
        vs = z_rest_s[:, c0 + 2 * W_AH:c0 + 3 * W_AH].reshape(DEC_BATCH, rq, A_HEAD_DIM)
        o, l = _dilated_sample(qs, ks, vs, caches[g][0], caches[g][1], window, r, g)
        outs_s.append(o.reshape(DEC_BATCH * rq, A_HEAD_DIM))
        lses_s.append(l.reshape(DEC_BATCH * rq, A_HEAD_DIM))
        keep = min(window, SEQ)
        zp = z_rest[:N_PROMPT].reshape(BATCH, SEQ, W_REST)[:, SEQ - keep:]
        rows_p += [zp[:, :, c0 + W_AH:c0 + 2 * W_AH].reshape(BATCH, keep, A_HEADS, A_HEAD_DIM),
                   zp[:, :, c0 + 2 * W_AH:c0 + 3 * W_AH].reshape(BATCH, keep, A_HEADS, A_HEAD_DIM)]
        rows_s += [ks.reshape(DEC_BATCH, DEC_SEQ, A_HEADS, A_HEAD_DIM),
                   vs.reshape(DEC_BATCH, DEC_SEQ, A_HEADS, A_HEAD_DIM)]
    yb_p = _combine_groups(outs_p, lses_p, rows_per_step=256)
    yb_s = _combine_groups(outs_s, lses_s, rows_per_step=DEC_BATCH * rq)
    yb = jnp.concatenate([yb_p, yb_s.reshape(N_SAMPLE, W_AH)], axis=0)

    mem_h = _rmsnorm(mem_prompt, g_mem, BF16)
    mem_kv = _matmul(mem_h, w_mem_kv, n_cols=2 * W_XQ, tm=BATCH * MEM_LEN, name="mem_kv")
    yc_p = _cross_prompt(z_rest, mem_kv)
    xq_s = z_rest_s[:, COL2_XQ:COL2_XQ + W_XQ].reshape(DEC_BATCH, DEC_SEQ * X_HEADS, X_HEAD_DIM)
    yc_s = _cross_sample(
        xq_s,
        cache_mem_k.reshape(DEC_BATCH, MEM_LEN * X_HEADS, X_HEAD_DIM),
        cache_mem_v.reshape(DEC_BATCH, MEM_LEN * X_HEADS, X_HEAD_DIM))
    yc = jnp.concatenate([yc_p, yc_s.reshape(N_SAMPLE, W_XQ).astype(BF16)], axis=0)

    merged = _branch_merge(a_in, yb, yc, w_br_a, w_br_b, w_br_c, z_rest, b_branch.reshape(1, 3 * D_MODEL))
    res_spec = pl.BlockSpec((MM_TM, MM_TN), lambda j, i: (i, j))
    x1 = _matmul(merged, w_out, n_cols=D_MODEL, epilogue=_ep_residual,
                 extras=(x_all,), extra_specs=(res_spec,), name="out_proj")

    h2 = _rmsnorm(x1, g_mlp, BF16)
    u = _matmul(h2, w_up, n_cols=D_FF, out_dtype=BF16, epilogue=_ep_relu2, name="mlp_up")
    x2 = _matmul_ksplit_residual(u, w_down, x1)

    mem_k = mem_kv[:, :W_XQ].reshape(BATCH, MEM_LEN, X_HEADS, X_HEAD_DIM)
    mem_v = mem_kv[:, W_XQ:].reshape(BATCH, MEM_LEN, X_HEADS, X_HEAD_DIM)
    prompt_state = (c_p, n_p.reshape(BATCH, M_HEADS, M_DQK), m_p[:, 0, 0].reshape(BATCH, M_HEADS))
    sample_state = (c_s, n_s.reshape(DEC_BATCH, M_HEADS, M_DQK), m_s[:, 0, 0].reshape(DEC_BATCH, M_HEADS))
    return x2, prompt_state, sample_state, rows_p, rows_s, mem_k, mem_v


def kernel(x_prompt, x_sample, state_mlstm_C, state_mlstm_n, state_mlstm_m,
           cache_win_k_g0, cache_win_v_g0, cache_win_k_g1, cache_win_v_g1,
           cache_win_k_g2, cache_win_v_g2, cache_mem_k, cache_mem_v, mem_prompt,
           g_mix, w_in, b_igate, b_fgate, b_branch, g_mem, w_mem_kv,
           w_br_a, w_br_b, w_br_c, w_out, g_mlp, w_up, w_down, g_final):
    depth = g_mix.shape[0]
    assert depth == 1, "single-layer stack"
    x_all = jnp.concatenate([x_prompt.reshape(N_PROMPT, D_MODEL), x_sample.reshape(N_SAMPLE, D_MODEL)], axis=0)
    caches = ((cache_win_k_g0[0], cache_win_v_g0[0]),
              (cache_win_k_g1[0], cache_win_v_g1[0]),
              (cache_win_k_g2[0], cache_win_v_g2[0]))
    x2, p_state, s_state, rows_p, rows_s, mem_k, mem_v = _layer(
        x_all, mem_prompt.reshape(BATCH * MEM_LEN, D_MODEL),
        state_mlstm_C, state_mlstm_n[0], state_mlstm_m[0], caches, cache_mem_k[0], cache_mem_v[0],
        g_mix[0], w_in[0], b_igate[0], b_fgate[0], b_branch[0], g_mem[0], w_mem_kv[0],
        w_br_a[0], w_br_b[0], w_br_c[0], w_out[0], g_mlp[0], w_up[0], w_down[0])
    y_prompt = _rmsnorm(x2, g_final, F32, row_start=0, n_rows=N_PROMPT).reshape(BATCH, SEQ, D_MODEL)
    y_sample = _rmsnorm(x2, g_final, F32, row_start=N_PROMPT, n_rows=N_SAMPLE).reshape(DEC_BATCH, DEC_SEQ, D_MODEL)
    lead = lambda a: a[None]
    return (y_prompt, y_sample,
            lead(p_state[0]), lead(p_state[1]), lead(p_state[2]),
            *[lead(r) for r in rows_p],
            lead(mem_k), lead(mem_v),
            lead(s_state[0]), lead(s_state[1]), lead(s_state[2]),
            *[lead(r) for r in rows_s])
```

```python
import functools
import math

import numpy as np
import jax
import jax.numpy as jnp
from jax import lax
from jax.experimental import pallas as pl
from jax.experimental.pallas import tpu as pltpu

F32 = jnp.float32
BF16 = jnp.bfloat16
NEG_INF = float("-inf")

D_MODEL = 4096
BATCH = 2
SEQ = 4096
DEC_BATCH = 128
DEC_SEQ = 4
PAST_LEN = 2048
MEM_LEN = 256
M_HEADS = 8
M_DQK = D_MODEL // (2 * M_HEADS)
M_DV = D_MODEL // M_HEADS
A_HEAD_DIM = 128
A_HEADS = D_MODEL // 512
DIL_GROUPS = ((128, 1), (512, 4), (2048, 16))
N_GROUPS = 3
ROPE_THETA = 10000.0
X_HEADS = 4
X_HEAD_DIM = 128
D_FF = 4 * D_MODEL
EPS = 1e-6

W_MQK = M_HEADS * M_DQK
W_MV = M_HEADS * M_DV
W_AH = A_HEADS * A_HEAD_DIM
W_XQ = X_HEADS * X_HEAD_DIM
N_PROMPT = BATCH * SEQ
N_SAMPLE = DEC_BATCH * DEC_SEQ
N_ALL = N_PROMPT + N_SAMPLE

COL_GATES = 2 * W_MQK + 2 * W_MV
COL_REST = COL_GATES + 2 * M_HEADS
W_AQKV = 3 * N_GROUPS * W_AH
COL2_XQ = W_AQKV
COL2_BR = W_AQKV + W_XQ
W_REST = W_AQKV + W_XQ + 3 * D_MODEL

LANES = 128
SUBLANES = 8
VMEM_LIMIT_BYTES = 56 * 1024 * 1024

MM_TM = 1088
MM_TN = 512
MERGE_TM = 544
DOWN_TM = 2176
DOWN_TK = 1024
NORM_ROWS = 256
CHUNK = 128
SAMPLE_CHUNK = 16
SPAN = 128


def _params(n_axes):
    return pltpu.CompilerParams(
        dimension_semantics=("arbitrary",) * n_axes,
        vmem_limit_bytes=VMEM_LIMIT_BYTES,
    )


def _rmsnorm_body(x_ref, g_ref, o_ref):
    x = x_ref[...]
    ms = jnp.mean(x * x, axis=-1, keepdims=True)
    o_ref[...] = ((x * lax.rsqrt(ms + EPS)) * g_ref[...]).astype(o_ref.dtype)


def _rmsnorm(x, g, out_dtype, *, row_start=0, n_rows=None):
    n_rows = x.shape[0] if n_rows is None else n_rows
    d = x.shape[1]
    off = row_start // NORM_ROWS
    return pl.pallas_call(
        _rmsnorm_body,
        grid=(n_rows // NORM_ROWS,),
        in_specs=[
            pl.BlockSpec((NORM_ROWS, d), lambda i: (i + off, 0)),
            pl.BlockSpec((1, d), lambda i: (0, 0)),
        ],
        out_specs=pl.BlockSpec((NORM_ROWS, d), lambda i: (i, 0)),
        out_shape=jax.ShapeDtypeStruct((n_rows, d), out_dtype),
        compiler_params=_params(1),
        name="rmsnorm",
    )(x, g.reshape(1, d))


def _ep_plain(acc, extra, o_ref):
    o_ref[...] = acc.astype(o_ref.dtype)


def _ep_residual(acc, extra, o_ref):
    o_ref[...] = extra[0][...] + acc


def _ep_relu2(acc, extra, o_ref):
    r = jnp.maximum(acc, 0.0)
    o_ref[...] = (r * r).astype(o_ref.dtype)


def _ep_rope(acc, extra, o_ref):
    cos_ref, sin_ref = extra
    j = pl.program_id(0)
    tiles_per_part = W_AH // MM_TN
    is_rope = (j < W_AQKV // MM_TN) & ((j // tiles_per_part) % 3 != 2)

    @pl.when(is_rope)
    def _():
        cos2 = cos_ref[...]
        sin2 = sin_ref[...]
        for s in range(MM_TN // A_HEAD_DIM):
            x = acc[:, s * A_HEAD_DIM:(s + 1) * A_HEAD_DIM]
            o_ref[:, s * A_HEAD_DIM:(s + 1) * A_HEAD_DIM] = (
                x * cos2 + pltpu.roll(x, A_HEAD_DIM // 2, axis=1) * sin2)

    @pl.when(jnp.logical_not(is_rope))
    def _():
        o_ref[...] = acc


def _mm_body(*refs, b_rows_are_outputs, epilogue, n_extra):
    a_ref, b_ref = refs[0], refs[1]
    extra = refs[2:2 + n_extra]
    o_ref = refs[2 + n_extra]
    b_scr = refs[3 + n_extra]

    @pl.when(pl.program_id(1) == 0)
    def _():
        b_scr[...] = b_ref[...].astype(BF16)

    if b_rows_are_outputs:
        acc = lax.dot_general(a_ref[...], b_scr[...], (((1,), (1,)), ((), ())),
                              preferred_element_type=F32)
    else:
        acc = jnp.dot(a_ref[...], b_scr[...], preferred_element_type=F32)
    epilogue(acc, extra, o_ref)


def _matmul(a, b, *, n_cols, col_start=0, b_rows_are_outputs=False, tm=MM_TM, tn=MM_TN,
            out_dtype=F32, epilogue=_ep_plain, extras=(), extra_specs=(), name="matmul"):
    m, k = a.shape
    if b_rows_are_outputs:
        if col_start % tn == 0:
            b_spec = pl.BlockSpec((tn, k), lambda j, i: (j + col_start // tn, 0))
        else:
            assert col_start % SUBLANES == 0 and tn % SUBLANES == 0
            b_spec = pl.BlockSpec((pl.Element(tn), pl.Element(k)),
                                  lambda j, i: (pl.multiple_of(col_start + j * tn, SUBLANES), 0))
        b_block = (tn, k)
    else:
        assert col_start % tn == 0
        b_spec = pl.BlockSpec((k, tn), lambda j, i: (0, j + col_start // tn))
        b_block = (k, tn)
    return pl.pallas_call(
        functools.partial(_mm_body, b_rows_are_outputs=b_rows_are_outputs, epilogue=epilogue,
                          n_extra=len(extras)),
        grid=(n_cols // tn, m // tm),
        in_specs=[pl.BlockSpec((tm, k), lambda j, i: (i, 0)), b_spec, *extra_specs],
        out_specs=pl.BlockSpec((tm, tn), lambda j, i: (i, j)),
        out_shape=jax.ShapeDtypeStruct((m, n_cols), out_dtype),
        scratch_shapes=[pltpu.VMEM(b_block, BF16)],
        compiler_params=_params(2),
        name=name,
    )(a, b, *extras)


def _mm_ksplit_body(a_ref, b_ref, res_ref, o_ref, acc_ref):
    kk = pl.program_id(2)

    @pl.when(kk == 0)
    def _():
        acc_ref[...] = jnp.zeros_like(acc_ref)

    acc_ref[...] += jnp.dot(a_ref[...], b_ref[...].astype(BF16), preferred_element_type=F32)

    @pl.when(kk == pl.num_programs(2) - 1)
    def _():
        o_ref[...] = res_ref[...] + acc_ref[...]


def _matmul_ksplit_residual(a, b, res, *, tm=DOWN_TM, tn=MM_TN, tk=DOWN_TK):
    m, k = a.shape
    n = b.shape[1]
    return pl.pallas_call(
        _mm_ksplit_body,
        grid=(n // tn, m // tm, k // tk),
        in_specs=[
            pl.BlockSpec((tm, tk), lambda j, i, kk: (i, kk)),
            pl.BlockSpec((tk, tn), lambda j, i, kk: (kk, j)),
            pl.BlockSpec((tm, tn), lambda j, i, kk: (i, j)),
        ],
        out_specs=pl.BlockSpec((tm, tn), lambda j, i, kk: (i, j)),
        out_shape=jax.ShapeDtypeStruct((m, n), F32),
        scratch_shapes=[pltpu.VMEM((tm, tn), F32)],
        compiler_params=_params(3),
        name="mlp_down",
    )(a, b, res)


def _sigmoid(x):
    return 1.0 / (1.0 + jnp.exp(-x))


def _merge_body(a_ref, yb_ref, yc_ref, wa_ref, wb_ref, wc_ref,
                ga_ref, gb_ref, gc_ref, ba_ref, bb_ref, bc_ref,
                o_ref, wa_scr, wb_scr, wc_scr):
    @pl.when(pl.program_id(1) == 0)
    def _():
        wa_scr[...] = wa_ref[...].astype(BF16)
        wb_scr[...] = wb_ref[...].astype(BF16)
        wc_scr[...] = wc_ref[...].astype(BF16)

    y_a = jnp.dot(a_ref[...], wa_scr[...], preferred_element_type=F32)
    y_b = jnp.dot(yb_ref[...], wb_scr[...], preferred_element_type=F32)
    y_c = jnp.dot(yc_ref[...], wc_scr[...], preferred_element_type=F32)
    g_a = _sigmoid(ga_ref[...] + ba_ref[...])
    g_b = _sigmoid(gb_ref[...] + bb_ref[...])
    g_c = _sigmoid(gc_ref[...] + bc_ref[...])
    o_ref[...] = (g_a * y_a + g_b * y_b + g_c * y_c).astype(o_ref.dtype)


def _branch_merge(a_in, yb, yc, w_a, w_b, w_c, z_rest, b_branch, *, tm=MERGE_TM, tn=MM_TN):
    m = a_in.shape[0]
    nj = D_MODEL // tn
    br0 = COL2_BR // tn

    def gate_spec(part):
        return pl.BlockSpec((tm, tn), lambda j, i: (i, br0 + part * nj + j))

    def bias_spec(part):
        return pl.BlockSpec((1, tn), lambda j, i: (0, part * nj + j))

    return pl.pallas_call(
        _merge_body,
        grid=(nj, m // tm),
        in_specs=[
            pl.BlockSpec((tm, W_MV), lambda j, i: (i, 0)),
            pl.BlockSpec((tm, W_AH), lambda j, i: (i, 0)),
            pl.BlockSpec((tm, W_XQ), lambda j, i: (i, 0)),
            pl.BlockSpec((W_MV, tn), lambda j, i: (0, j)),
            pl.BlockSpec((W_AH, tn), lambda j, i: (0, j)),
            pl.BlockSpec((W_XQ, tn), lambda j, i: (0, j)),
            gate_spec(0), gate_spec(1), gate_spec(2),
            bias_spec(0), bias_spec(1), bias_spec(2),
        ],
        out_specs=pl.BlockSpec((tm, tn), lambda j, i: (i, j)),
        out_shape=jax.ShapeDtypeStruct((m, D_MODEL), BF16),
        scratch_shapes=[pltpu.VMEM((W_MV, tn), BF16), pltpu.VMEM((W_AH, tn), BF16),
                        pltpu.VMEM((W_XQ, tn), BF16)],
        compiler_params=_params(2),
        name="branch_merge",
    )(a_in, yb, yc, w_a, w_b, w_c, z_rest, z_rest, z_rest, b_branch, b_branch, b_branch)


def _log_sigmoid(x):
    return jnp.minimum(x, 0.0) - jnp.log1p(jnp.exp(-jnp.abs(x)))


def _mlstm_chunk(q, k, v, irow, frow, c_state, n_state, m_state, n_valid):
    L = q.shape[0]
    ti = lax.broadcasted_iota(jnp.int32, (L, L), 0)
    si = lax.broadcasted_iota(jnp.int32, (L, L), 1)
    causal = si <= ti
    eye = si == ti
    f_b = jnp.broadcast_to(frow, (L, L))
    i_b = jnp.broadcast_to(irow, (L, L))
    bcol = jnp.sum(jnp.where(causal, f_b, 0.0), axis=1, keepdims=True)
    fcol = jnp.sum(jnp.where(eye, f_b, 0.0), axis=1, keepdims=True)
    icol = jnp.sum(jnp.where(eye, i_b, 0.0), axis=1, keepdims=True)
    brow = jnp.sum(jnp.where(ti <= si, jnp.broadcast_to(fcol, (L, L)), 0.0), axis=0, keepdims=True)

    acol = bcol + m_state
    logw = jnp.where(causal, bcol - brow + irow, NEG_INF)
    mt = jnp.maximum(acol, jnp.max(logw, axis=1, keepdims=True))
    w_inter = jnp.exp(acol - mt)
    w_intra = jnp.exp(logw - mt)

    qb = q.astype(BF16)
    kb = k.astype(BF16)
    nt = (((1,), (1,)), ((), ()))
    s = lax.dot_general(qb, kb, nt, preferred_element_type=F32) * w_intra
    inter = lax.dot_general(qb, c_state.astype(BF16), nt, preferred_element_type=F32)
    num = w_inter * inter + jnp.dot(s.astype(BF16), v.astype(BF16), preferred_element_type=F32)
    nq = w_inter * jnp.sum(q * n_state, axis=1, keepdims=True) + jnp.sum(s, axis=1, keepdims=True)
    h = num / jnp.maximum(jnp.abs(nq), jnp.exp(-mt))

    last = slice(n_valid - 1, n_valid)
    m_end = mt[last, :]
    w_c = jnp.exp(acol[last, :] - m_end)
    w_s = jnp.exp(bcol[last, :] - bcol + icol - m_end)
    if n_valid < L:
        w_s = jnp.where(lax.broadcasted_iota(jnp.int32, (L, 1), 0) < n_valid, w_s, 0.0)
    tn = (((0,), (0,)), ((), ()))
    c_new = w_c * c_state + lax.dot_general((v * w_s).astype(BF16), kb, tn, preferred_element_type=F32)
    n_new = w_c * n_state + jnp.sum(w_s * k, axis=0, keepdims=True)
    return h, c_new, n_new, m_end


def _gate_rows(g_ref_val, bias_ref, head):
    irow = g_ref_val[0:1, :] + bias_ref[0, head]
    frow = _log_sigmoid(g_ref_val[1:2, :] + bias_ref[1, head])
    return irow, frow


def _mlstm_prompt_body(bias_ref, q_ref, k_ref, v_ref, mo_ref, g_ref,
                       a_ref, c_ref, n_ref, m_ref):
    @pl.when(pl.program_id(1) == 0)
    def _():
        c_ref[...] = jnp.zeros_like(c_ref)
        n_ref[...] = jnp.zeros_like(n_ref)
        m_ref[...] = jnp.zeros_like(m_ref)

    for head in range(M_HEADS):
        qk = pl.ds(head * M_DQK, M_DQK)
        vo = pl.ds(head * M_DV, M_DV)
        one = pl.ds(head, 1)
        irow, frow = _gate_rows(g_ref[head], bias_ref, head)
        h, c_new, n_new, m_end = _mlstm_chunk(
            q_ref[:, qk], k_ref[:, qk] * (M_DQK ** -0.5), v_ref[:, vo], irow, frow,
            c_ref[0, head], n_ref[0, one, :], m_ref[0, one, 0:1], CHUNK)
        a_ref[:, vo] = (_sigmoid(mo_ref[:, vo]) * h).astype(a_ref.dtype)
        c_ref[0, head] = c_new
        n_ref[0, one, :] = n_new
        m_ref[0, one, :] = jnp.broadcast_to(m_end, (1, LANES))


def _mlstm_prompt(z_main, gates_rows, gate_bias):
    nc = SEQ // CHUNK
    return pl.pallas_call(
        _mlstm_prompt_body,
        grid=(BATCH, nc),
        in_specs=[
            pl.BlockSpec(memory_space=pltpu.SMEM),
            pl.BlockSpec((CHUNK, W_MQK), lambda b, c: (b * nc + c, 0)),
            pl.BlockSpec((CHUNK, W_MQK), lambda b, c: (b * nc + c, 1)),
            pl.BlockSpec((CHUNK, W_MV), lambda b, c: (b * nc + c, 2 * W_MQK // W_MV)),
            pl.BlockSpec((CHUNK, W_MV), lambda b, c: (b * nc + c, 2 * W_MQK // W_MV + 1)),
            pl.BlockSpec((M_HEADS, 2, CHUNK), lambda b, c: (0, 0, b * nc + c)),
        ],
        out_specs=[
            pl.BlockSpec((CHUNK, W_MV), lambda b, c: (b * nc + c, 0)),
            pl.BlockSpec((1, M_HEADS, M_DV, M_DQK), lambda b, c: (b, 0, 0, 0)),
            pl.BlockSpec((1, M_HEADS, M_DQK), lambda b, c: (b, 0, 0)),
            pl.BlockSpec((1, M_HEADS, LANES), lambda b, c: (b, 0, 0)),
        ],
        out_shape=[
            jax.ShapeDtypeStruct((N_PROMPT, W_MV), BF16),
            jax.ShapeDtypeStruct((BATCH, M_HEADS, M_DV, M_DQK), F32),
            jax.ShapeDtypeStruct((BATCH, M_HEADS, M_DQK), F32),
            jax.ShapeDtypeStruct((BATCH, M_HEADS, LANES), F32),
        ],
        compiler_params=_params(2),
        name="mlstm_prompt",
    )(gate_bias, z_main, z_main, z_main, z_main, gates_rows)


def _mlstm_sample_body(bias_ref, q_ref, k_ref, v_ref, mo_ref, g_ref, c0_ref, n0_ref, m0_ref,
                       a_ref, c_ref, n_ref, m_ref, q_scr, k_scr, v_scr):
    @pl.when(pl.program_id(0) == 0)
    def _():
        q_scr[...] = jnp.zeros_like(q_scr)
        k_scr[...] = jnp.zeros_like(k_scr)
        v_scr[...] = jnp.zeros_like(v_scr)

    q_scr[0:DEC_SEQ, :] = q_ref[0]
    k_scr[0:DEC_SEQ, :] = k_ref[0] * (M_DQK ** -0.5)
    v_scr[0:DEC_SEQ, :] = v_ref[0]
    for head in range(M_HEADS):
        qk = pl.ds(head * M_DQK, M_DQK)
        vo = pl.ds(head * M_DV, M_DV)
        one = pl.ds(head, 1)
        irow, frow = _gate_rows(g_ref[0, head], bias_ref, head)
        h, c_new, n_new, m_end = _mlstm_chunk(
            q_scr[:, qk], k_scr[:, qk], v_scr[:, vo], irow, frow,
            c0_ref[0, 0, head], n0_ref[0, 0, one, :], m0_ref[0, one, :], DEC_SEQ)
        a_ref[0, :, vo] = _sigmoid(mo_ref[0, :, vo]) * h[0:DEC_SEQ, :]
        c_ref[0, head] = c_new
        n_ref[0, one, :] = n_new
        m_ref[0, one, :] = jnp.broadcast_to(m_end, (1, LANES))


def _mlstm_sample(z_s3, gates_rows, gate_bias, c0, n0, m0):
    return pl.pallas_call(
        _mlstm_sample_body,
        grid=(DEC_BATCH,),
        in_specs=[
            pl.BlockSpec(memory_space=pltpu.SMEM),
            pl.BlockSpec((1, DEC_SEQ, W_MQK), lambda b: (b, 0, 0)),
            pl.BlockSpec((1, DEC_SEQ, W_MQK), lambda b: (b, 0, 1)),
            pl.BlockSpec((1, DEC_SEQ, W_MV), lambda b: (b, 0, 2 * W_MQK // W_MV)),
            pl.BlockSpec((1, DEC_SEQ, W_MV), lambda b: (b, 0, 2 * W_MQK // W_MV + 1)),
            pl.BlockSpec((1, M_HEADS, 2, SAMPLE_CHUNK), lambda b: (b, 0, 0, 0)),
            pl.BlockSpec((1, 1, M_HEADS, M_DV, M_DQK), lambda b: (0, b, 0, 0, 0)),
            pl.BlockSpec((1, 1, M_HEADS, M_DQK), lambda b: (0, b, 0, 0)),
            pl.BlockSpec((1, M_HEADS, 1), lambda b: (b, 0, 0)),
        ],
        out_specs=[
            pl.BlockSpec((1, DEC_SEQ, W_MV), lambda b: (b, 0, 0)),
            pl.BlockSpec((1, M_HEADS, M_DV, M_DQK), lambda b: (b, 0, 0, 0)),
            pl.BlockSpec((1, M_HEADS, M_DQK), lambda b: (b, 0, 0)),
            pl.BlockSpec((1, M_HEADS, LANES), lambda b: (b, 0, 0)),
        ],
        out_shape=[
            jax.ShapeDtypeStruct((DEC_BATCH, DEC_SEQ, W_MV), F32),
            jax.ShapeDtypeStruct((DEC_BATCH, M_HEADS, M_DV, M_DQK), F32),
            jax.ShapeDtypeStruct((DEC_BATCH, M_HEADS, M_DQK), F32),
            jax.ShapeDtypeStruct((DEC_BATCH, M_HEADS, LANES), F32),
        ],
        scratch_shapes=[pltpu.VMEM((SAMPLE_CHUNK, W_MQK), F32), pltpu.VMEM((SAMPLE_CHUNK, W_MQK), F32),
                        pltpu.VMEM((SAMPLE_CHUNK, W_MV), F32)],
        compiler_params=_params(1),
        name="mlstm_sample",
    )(gate_bias, z_s3, z_s3, z_s3, z_s3, gates_rows, c0, n0, m0)


def _dil_prompt_body(q_ref, kp_ref, kc_ref, vp_ref, vc_ref, o_ref, l_ref, *, r, hb):
    first_key = jnp.where(pl.program_id(2) > 0, 0, SPAN)
    qi = lax.broadcasted_iota(jnp.int32, (SPAN, 2 * SPAN), 0)
    ki = lax.broadcasted_iota(jnp.int32, (SPAN, 2 * SPAN), 1)
    ok = (ki >= qi) & (ki <= qi + SPAN) & (ki >= first_key)
    bias = jnp.where(ok, 0.0, NEG_INF)
    scale = A_HEAD_DIM ** -0.5
    nt = (((1,), (1,)), ((), ()))
    for c in range(r):
        rows = pl.ds(c, SPAN, stride=r) if r > 1 else pl.ds(0, SPAN)
        for hh in range(hb):
            cols = pl.ds(hh * A_HEAD_DIM, A_HEAD_DIM)
            q = q_ref[rows, cols].astype(BF16)
            kk = jnp.concatenate([kp_ref[rows, cols], kc_ref[rows, cols]], axis=0).astype(BF16)
            vv = jnp.concatenate([vp_ref[rows, cols], vc_ref[rows, cols]], axis=0).astype(BF16)
            s = lax.dot_general(q, kk, nt, preferred_element_type=F32) * scale + bias
            m = jnp.max(s, axis=1, keepdims=True)
            p = jnp.exp(s - m)
            den = jnp.sum(p, axis=1, keepdims=True)
            o = jnp.dot(p.astype(BF16), vv, preferred_element_type=F32) / den
            o_ref[rows, cols] = o
            l_ref[rows, cols] = jnp.broadcast_to(m + jnp.log(den), (SPAN, A_HEAD_DIM))


def _dilated_prompt(z_rest, g, r, hb):
    rows = SPAN * r
    nblk = SEQ // rows
    wcol = hb * A_HEAD_DIM
    per_part = W_AH // wcol

    def spec(part, prev):
        def imap(b, hg, n):
            nn = jnp.maximum(n - 1, 0) if prev else n
            return (b * nblk + nn, (3 * g + part) * per_part + hg)
        return pl.BlockSpec((rows, wcol), imap)

    out_spec = pl.BlockSpec((rows, wcol), lambda b, hg, n: (b * nblk + n, hg))
    return pl.pallas_call(
        functools.partial(_dil_prompt_body, r=r, hb=hb),
        grid=(BATCH, A_HEADS // hb, nblk),
        in_specs=[spec(0, False), spec(1, True), spec(1, False), spec(2, True), spec(2, False)],
        out_specs=[out_spec, out_spec],
        out_shape=[jax.ShapeDtypeStruct((N_PROMPT, W_AH), F32)] * 2,
        compiler_params=_params(3),
        name=f"dilated_prompt_g{g}",
    )(z_rest, z_rest, z_rest, z_rest, z_rest)


def _cached_attn_body(q_ref, kn_ref, vn_ref, kc_ref, vc_ref, bias_ref, o_ref, l_ref, *, scale):
    n_cached = kc_ref.shape[1] * kc_ref.shape[2]
    q = q_ref[0].astype(BF16)
    kk = jnp.concatenate([kc_ref[0].reshape(n_cached, A_HEAD_DIM), kn_ref[0]], axis=0).astype(BF16)
    vv = jnp.concatenate([vc_ref[0].reshape(n_cached, A_HEAD_DIM), vn_ref[0]], axis=0).astype(BF16)
    nt = (((1,), (1,)), ((), ()))
    s = lax.dot_general(q, kk, nt, preferred_element_type=F32) * scale + bias_ref[...]
    m = jnp.max(s, axis=1, keepdims=True)
    p = jnp.exp(s - m)
    den = jnp.sum(p, axis=1, keepdims=True)
    o_ref[0] = jnp.dot(p.astype(BF16), vv, preferred_element_type=F32) / den
    l_ref[0] = jnp.broadcast_to(m + jnp.log(den), l_ref.shape[1:])


def _dilated_sample_bias(window, r, lb, rc):
    span = window // r
    n_c = (lb // r) * rc * A_HEADS
    rows = np.arange(DEC_SEQ * A_HEADS)
    s_q, h_q = rows // A_HEADS, rows % A_HEADS
    col = np.arange(n_c)
    m_k = col // (rc * A_HEADS)
    c_k = (col % (rc * A_HEADS)) // A_HEADS
    h_k = col % A_HEADS
    delta = (lb + s_q)[:, None] - (m_k * r + c_k)[None, :]
    ok_c = (h_q[:, None] == h_k[None, :]) & (delta % r == 0) & (delta // r <= span) & (delta >= 0)
    coln = np.arange(DEC_SEQ * A_HEADS)
    s_n, h_n = coln // A_HEADS, coln % A_HEADS
    dn = s_q[:, None] - s_n[None, :]
    ok_n = (h_q[:, None] == h_n[None, :]) & (dn >= 0) & (dn % r == 0) & (dn // r <= span)
    ok = np.concatenate([ok_c, ok_n], axis=1)
    return np.where(ok, 0.0, -np.inf).astype(np.float32)


def _dilated_sample(q, kn, vn, cache_k, cache_v, window, r, g):
    lb = cache_k.shape[1]
    assert lb % r == 0 and window % r == 0
    rc = min(r, DEC_SEQ)
    nm = lb // r
    ck = cache_k.reshape(DEC_BATCH, nm, r * A_HEADS, A_HEAD_DIM)
    cv = cache_v.reshape(DEC_BATCH, nm, r * A_HEADS, A_HEAD_DIM)
    bias = jnp.asarray(_dilated_sample_bias(window, r, lb, rc))
    rq = DEC_SEQ * A_HEADS
    row_spec = pl.BlockSpec((1, rq, A_HEAD_DIM), lambda b: (b, 0, 0))
    cache_spec = pl.BlockSpec((1, nm, rc * A_HEADS, A_HEAD_DIM), lambda b: (b, 0, 0, 0))
    return pl.pallas_call(
        functools.partial(_cached_attn_body, scale=A_HEAD_DIM ** -0.5),
        grid=(DEC_BATCH,),
        in_specs=[row_spec, row_spec, row_spec, cache_spec, cache_spec,
                  pl.BlockSpec(bias.shape, lambda b: (0, 0))],
        out_specs=[row_spec, row_spec],
        out_shape=[jax.ShapeDtypeStruct((DEC_BATCH, rq, A_HEAD_DIM), F32)] * 2,
        compiler_params=_params(1),
        name=f"dilated_sample_g{g}",
    )(q, kn, vn, ck, cv, bias)


def _cross_sample_body(q_ref, k_ref, v_ref, bias_ref, o_ref, *, scale):
    q = q_ref[0].astype(BF16)
    nt = (((1,), (1,)), ((), ()))
    s = lax.dot_general(q, k_ref[0].astype(BF16), nt, preferred_element_type=F32) * scale + bias_ref[...]
    m = jnp.max(s, axis=1, keepdims=True)
    e = jnp.exp(s - m)
    p = e / jnp.sum(e, axis=1, keepdims=True)
    o_ref[0] = jnp.dot(p.astype(BF16), v_ref[0].astype(BF16), preferred_element_type=F32)


def _cross_sample(q, mem_k, mem_v):
    rq = DEC_SEQ * X_HEADS
    nk = MEM_LEN * X_HEADS
    ok = (np.arange(rq) % X_HEADS)[:, None] == (np.arange(nk) % X_HEADS)[None, :]
    bias = jnp.asarray(np.where(ok, 0.0, -np.inf).astype(np.float32))
    return pl.pallas_call(
        functools.partial(_cross_sample_body, scale=X_HEAD_DIM ** -0.5),
        grid=(DEC_BATCH,),
        in_specs=[
            pl.BlockSpec((1, rq, X_HEAD_DIM), lambda b: (b, 0, 0)),
            pl.BlockSpec((1, nk, X_HEAD_DIM), lambda b: (b, 0, 0)),
            pl.BlockSpec((1, nk, X_HEAD_DIM), lambda b: (b, 0, 0)),
            pl.BlockSpec((rq, nk), lambda b: (0, 0)),
        ],
        out_specs=pl.BlockSpec((1, rq, X_HEAD_DIM), lambda b: (b, 0, 0)),
        out_shape=jax.ShapeDtypeStruct((DEC_BATCH, rq, X_HEAD_DIM), F32),
        compiler_params=_params(1),
        name="cross_sample",
    )(q, mem_k, mem_v, bias)


def _cross_prompt_body(q_ref, k_ref, v_ref, o_ref):
    scale = X_HEAD_DIM ** -0.5
    nt = (((1,), (1,)), ((), ()))
    for h in range(X_HEADS):
        cols = pl.ds(h * X_HEAD_DIM, X_HEAD_DIM)
        s = lax.dot_general(q_ref[:, cols].astype(BF16), k_ref[:, cols].astype(BF16), nt,
                            preferred_element_type=F32) * scale
        m = jnp.max(s, axis=1, keepdims=True)
        e = jnp.exp(s - m)
        p = e / jnp.sum(e, axis=1, keepdims=True)
        o_ref[:, cols] = jnp.dot(p.astype(BF16), v_ref[:, cols].astype(BF16),
                                 preferred_element_type=F32).astype(o_ref.dtype)


def _cross_prompt(z_rest, mem_kv, *, tq=512):
    nq = SEQ // tq
    return pl.pallas_call(
        _cross_prompt_body,
        grid=(BATCH, nq),
        in_specs=[
            pl.BlockSpec((tq, W_XQ), lambda b, i: (b * nq + i, COL2_XQ // W_XQ)),
            pl.BlockSpec((MEM_LEN, W_XQ), lambda b, i: (b, 0)),
            pl.BlockSpec((MEM_LEN, W_XQ), lambda b, i: (b, 1)),
        ],
        out_specs=pl.BlockSpec((tq, W_XQ), lambda b, i: (b * nq + i, 0)),
        out_shape=jax.ShapeDtypeStruct((N_PROMPT, W_XQ), BF16),
        compiler_params=_params(2),
        name="cross_prompt",
    )(z_rest, mem_kv, mem_kv)


def _combine_body(o0, o1, o2, l0, l1, l2, y_ref):
    a0, a1, a2 = l0[...], l1[...], l2[...]
    mx = jnp.maximum(jnp.maximum(a0, a1), a2)
    e0, e1, e2 = jnp.exp(a0 - mx), jnp.exp(a1 - mx), jnp.exp(a2 - mx)
    tot = e0 + e1 + e2
    y = (e0 / tot) * o0[...] + (e1 / tot) * o1[...] + (e2 / tot) * o2[...]
    y_ref[...] = y.astype(y_ref.dtype)


def _combine_groups(outs, lses, *, rows_per_step):
    n, w = outs[0].shape
    spec = pl.BlockSpec((rows_per_step, w), lambda i: (i, 0))
    return pl.pallas_call(
        _combine_body,
        grid=(n // rows_per_step,),
        in_specs=[spec] * 6,
        out_specs=spec,
        out_shape=jax.ShapeDtypeStruct((n, w), BF16),
        compiler_params=_params(1),
        name="combine_groups",
    )(*outs, *lses)


def _rope_tables():
    pos = jnp.concatenate([
        jnp.tile(jnp.arange(SEQ, dtype=jnp.int32), BATCH),
        jnp.tile(PAST_LEN + jnp.arange(DEC_SEQ, dtype=jnp.int32), DEC_BATCH)])
    inv = ROPE_THETA ** (-jnp.arange(0, A_HEAD_DIM, 2, dtype=F32) / A_HEAD_DIM)
    ang = pos.astype(F32)[:, None] * inv[None, :]
    cos, sin = jnp.cos(ang), jnp.sin(ang)
    return jnp.concatenate([cos, cos], axis=1), jnp.concatenate([-sin, sin], axis=1)


def _layer(x_all, mem_prompt, state_c, state_n, state_m, caches, cache_mem_k, cache_mem_v,
           g_mix, w_in, b_igate, b_fgate, b_branch, g_mem, w_mem_kv,
           w_br_a, w_br_b, w_br_c, w_out, g_mlp, w_up, w_down):
    h_all = _rmsnorm(x_all, g_mix, BF16)

    w_in_t = w_in.T
    z_main = _matmul(h_all, w_in_t, n_cols=COL_GATES, b_rows_are_outputs=True, name="proj_main")
    z_gate = _matmul(h_all, w_in_t, n_cols=2 * M_HEADS, col_start=COL_GATES, tn=2 * M_HEADS,
                     b_rows_are_outputs=True, name="proj_gates")
    cos2, sin2 = _rope_tables()
    table_spec = pl.BlockSpec((MM_TM, A_HEAD_DIM), lambda j, i: (i, 0))
    z_rest = _matmul(h_all, w_in_t, n_cols=W_REST, col_start=COL_REST, b_rows_are_outputs=True,
                     epilogue=_ep_rope, extras=(cos2, sin2), extra_specs=(table_spec, table_spec),
                     name="proj_rest")

    gate_bias = jnp.stack([b_igate, b_fgate]).astype(F32)

    gp = z_gate[:N_PROMPT].reshape(N_PROMPT, 2, M_HEADS).transpose(2, 1, 0)
    a_p, c_p, n_p, m_p = _mlstm_prompt(z_main, gp, gate_bias)
    gs = z_gate[N_PROMPT:].reshape(DEC_BATCH, DEC_SEQ, 2, M_HEADS).transpose(0, 3, 2, 1)
    gs = jnp.pad(gs, ((0, 0), (0, 0), (0, 0), (0, SAMPLE_CHUNK - DEC_SEQ)))
    z_main_s = z_main[N_PROMPT:].reshape(DEC_BATCH, DEC_SEQ, COL_GATES)
    a_s, c_s, n_s, m_s = _mlstm_sample(
        z_main_s, gs, gate_bias, state_c, state_n, state_m.reshape(DEC_BATCH, M_HEADS, 1))
    a_in = jnp.concatenate([a_p, a_s.reshape(N_SAMPLE, W_MV).astype(BF16)], axis=0)

    z_rest_s = z_rest[N_PROMPT:]
    heads_per_step = (8, 1, 1)
    outs_p, lses_p, outs_s, lses_s, rows_p, rows_s = [], [], [], [], [], []
    rq = DEC_SEQ * A_HEADS
    for g, (window, r) in enumerate(DIL_GROUPS):
        o, l = _dilated_prompt(z_rest, g, r, heads_per_step[g])
        outs_p.append(o)
        lses_p.append(l)
        c0 = 3 * g * W_AH
        qs = z_rest_s[:, c0:c0 + W_AH].reshape(DEC_BATCH, rq, A_HEAD_DIM)
        ks = z_rest_s[:, c0 + W_AH:c0 + 2 * W_AH].reshape(DEC_BATCH, rq, A_HEAD_DIM)
        vs = z_rest_s[:, c0 + 2 * W_AH:c0 + 3 * W_AH].reshape(DEC_BATCH, rq, A_HEAD_DIM)
        o, l = _dilated_sample(qs, ks, vs, caches[g][0], caches[g][1], window, r, g)
        outs_s.append(o.reshape(DEC_BATCH * rq, A_HEAD_DIM))
        lses_s.append(l.reshape(DEC_BATCH * rq, A_HEAD_DIM))
        keep = min(window, SEQ)
        for part in (1, 2):
            cs = c0 + part * W_AH
            kept = [z_rest[(b + 1) * SEQ - keep:(b + 1) * SEQ, cs:cs + W_AH] for b in range(BATCH)]
            rows_p.append(jnp.stack(kept).reshape(BATCH, keep, A_HEADS, A_HEAD_DIM))
        rows_s += [ks.reshape(DEC_BATCH, DEC_SEQ, A_HEADS, A_HEAD_DIM),
                   vs.reshape(DEC_BATCH, DEC_SEQ, A_HEADS, A_HEAD_DIM)]
    yb_p = _combine_groups(outs_p, lses_p, rows_per_step=256)
    yb_s = _combine_groups(outs_s, lses_s, rows_per_step=DEC_BATCH * rq)
    yb = jnp.concatenate([yb_p, yb_s.reshape(N_SAMPLE, W_AH)], axis=0)

    mem_h = _rmsnorm(mem_prompt, g_mem, BF16)
    mem_kv = _matmul(mem_h, w_mem_kv, n_cols=2 * W_XQ, tm=BATCH * MEM_LEN, name="mem_kv")
    yc_p = _cross_prompt(z_rest, mem_kv)
    xq_s = z_rest_s[:, COL2_XQ:COL2_XQ + W_XQ].reshape(DEC_BATCH, DEC_SEQ * X_HEADS, X_HEAD_DIM)
    yc_s = _cross_sample(
        xq_s,
        cache_mem_k.reshape(DEC_BATCH, MEM_LEN * X_HEADS, X_HEAD_DIM),
        cache_mem_v.reshape(DEC_BATCH, MEM_LEN * X_HEADS, X_HEAD_DIM))
    yc = jnp.concatenate([yc_p, yc_s.reshape(N_SAMPLE, W_XQ).astype(BF16)], axis=0)

    merged = _branch_merge(a_in, yb, yc, w_br_a, w_br_b, w_br_c, z_rest, b_branch.reshape(1, 3 * D_MODEL))
    res_spec = pl.BlockSpec((MM_TM, MM_TN), lambda j, i: (i, j))
    x1 = _matmul(merged, w_out, n_cols=D_MODEL, epilogue=_ep_residual,
                 extras=(x_all,), extra_specs=(res_spec,), name="out_proj")

    h2 = _rmsnorm(x1, g_mlp, BF16)
    u = _matmul(h2, w_up, n_cols=D_FF, out_dtype=BF16, epilogue=_ep_relu2, name="mlp_up")
    x2 = _matmul_ksplit_residual(u, w_down, x1)

    mem_k = mem_kv[:, :W_XQ].reshape(BATCH, MEM_LEN, X_HEADS, X_HEAD_DIM)
    mem_v = mem_kv[:, W_XQ:].reshape(BATCH, MEM_LEN, X_HEADS, X_HEAD_DIM)
    prompt_state = (c_p, n_p, m_p[:, :, 0])
    sample_state = (c_s, n_s, m_s[:, :, 0])
    return x2, prompt_state, sample_state, rows_p, rows_s, mem_k, mem_v


def kernel(x_prompt, x_sample, state_mlstm_C, state_mlstm_n, state_mlstm_m,
           cache_win_k_g0, cache_win_v_g0, cache_win_k_g1, cache_win_v_g1,
           cache_win_k_g2, cache_win_v_g2, cache_mem_k, cache_mem_v, mem_prompt,
           g_mix, w_in, b_igate, b_fgate, b_branch, g_mem, w_mem_kv,
           w_br_a, w_br_b, w_br_c, w_out, g_mlp, w_up, w_down, g_final):
    depth = g_mix.shape[0]
    assert depth == 1, "single-layer stack"
    x_all = jnp.concatenate([x_prompt.reshape(N_PROMPT, D_MODEL), x_sample.reshape(N_SAMPLE, D_MODEL)], axis=0)
    caches = ((cache_win_k_g0[0], cache_win_v_g0[0]),
              (cache_win_k_g1[0], cache_win_v_g1[0]),
              (cache_win_k_g2[0], cache_win_v_g2[0]))
    x2, p_state, s_state, rows_p, rows_s, mem_k, mem_v = _layer(
        x_all, mem_prompt.reshape(BATCH * MEM_LEN, D_MODEL),
        state_mlstm_C, state_mlstm_n, state_mlstm_m[0], caches, cache_mem_k[0], cache_mem_v[0],
        g_mix[0], w_in[0], b_igate[0], b_fgate[0], b_branch[0], g_mem[0], w_mem_kv[0],
        w_br_a[0], w_br_b[0], w_br_c[0], w_out[0], g_mlp[0], w_up[0], w_down[0])
    y_prompt = _rmsnorm(x2, g_final, F32, row_start=0, n_rows=N_PROMPT).reshape(BATCH, SEQ, D_MODEL)
    y_sample = _rmsnorm(x2, g_final, F32, row_start=N_PROMPT, n_rows=N_SAMPLE).reshape(DEC_BATCH, DEC_SEQ, D_MODEL)
    lead = lambda a: a[None]
    return (y_prompt, y_sample,
            lead(p_state[0]), lead(p_state[1]), lead(p_state[2]),
            *[lead(r) for r in rows_p],
            lead(mem_k), lead(mem_v),
            lead(s_state[0]), lead(s_state[1]), lead(s_state[2]),
            *[lead(r) for r in rows_s])
```

```python
import functools
import math

import numpy as np
import jax
import jax.numpy as jnp
from jax import lax
from jax.experimental import pallas as pl
from jax.experimental.pallas import tpu as pltpu

F32 = jnp.float32
BF16 = jnp.bfloat16
NEG_INF = float("-inf")

D_MODEL = 4096
BATCH = 2
SEQ = 4096
DEC_BATCH = 128
DEC_SEQ = 4
PAST_LEN = 2048
MEM_LEN = 256
M_HEADS = 8
M_DQK = D_MODEL // (2 * M_HEADS)
M_DV = D_MODEL // M_HEADS
A_HEAD_DIM = 128
A_HEADS = D_MODEL // 512
DIL_GROUPS = ((128, 1), (512, 4), (2048, 16))
N_GROUPS = 3
ROPE_THETA = 10000.0
X_HEADS = 4
X_HEAD_DIM = 128
D_FF = 4 * D_MODEL
EPS = 1e-6

W_MQK = M_HEADS * M_DQK
W_MV = M_HEADS * M_DV
W_AH = A_HEADS * A_HEAD_DIM
W_XQ = X_HEADS * X_HEAD_DIM
N_PROMPT = BATCH * SEQ
N_SAMPLE = DEC_BATCH * DEC_SEQ
N_ALL = N_PROMPT + N_SAMPLE

COL_GATES = 2 * W_MQK + 2 * W_MV
COL_REST = COL_GATES + 2 * M_HEADS
W_AQKV = 3 * N_GROUPS * W_AH
COL2_XQ = W_AQKV
COL2_BR = W_AQKV + W_XQ
W_REST = W_AQKV + W_XQ + 3 * D_MODEL

LANES = 128
SUBLANES = 8
VMEM_LIMIT_BYTES = 56 * 1024 * 1024

MM_TM = 1088
MM_TN = 512
MERGE_TM = 544
DOWN_TM = 2176
DOWN_TK = 2048
NORM_ROWS = 256
CHUNK = 128
SAMPLE_CHUNK = 16
SPAN = 128


def _params(n_axes):
    return pltpu.CompilerParams(
        dimension_semantics=("arbitrary",) * n_axes,
        vmem_limit_bytes=VMEM_LIMIT_BYTES,
    )


def _rmsnorm_body(x_ref, g_ref, o_ref):
    x = x_ref[...]
    ms = jnp.mean(x * x, axis=-1, keepdims=True)
    o_ref[...] = ((x * lax.rsqrt(ms + EPS)) * g_ref[...]).astype(o_ref.dtype)


def _rmsnorm(x, g, out_dtype, *, row_start=0, n_rows=None):
    n_rows = x.shape[0] if n_rows is None else n_rows
    d = x.shape[1]
    off = row_start // NORM_ROWS
    return pl.pallas_call(
        _rmsnorm_body,
        grid=(n_rows // NORM_ROWS,),
        in_specs=[
            pl.BlockSpec((NORM_ROWS, d), lambda i: (i + off, 0)),
            pl.BlockSpec((1, d), lambda i: (0, 0)),
        ],
        out_specs=pl.BlockSpec((NORM_ROWS, d), lambda i: (i, 0)),
        out_shape=jax.ShapeDtypeStruct((n_rows, d), out_dtype),
        compiler_params=_params(1),
        name="rmsnorm",
    )(x, g.reshape(1, d))


def _rmsnorm_two_groups_body(xp_ref, xs_ref, g_ref, o_ref, *, prompt_steps):
    i = pl.program_id(0)

    @pl.when(i < prompt_steps)
    def _():
        _rmsnorm_body(xp_ref, g_ref, o_ref)

    @pl.when(i >= prompt_steps)
    def _():
        _rmsnorm_body(xs_ref, g_ref, o_ref)


def _rmsnorm_two_groups(xp, xs, g, out_dtype):
    d = xp.shape[1]
    ps, ss = xp.shape[0] // NORM_ROWS, xs.shape[0] // NORM_ROWS
    return pl.pallas_call(
        functools.partial(_rmsnorm_two_groups_body, prompt_steps=ps),
        grid=(ps + ss,),
        in_specs=[
            pl.BlockSpec((NORM_ROWS, d), lambda i: (jnp.minimum(i, ps - 1), 0)),
            pl.BlockSpec((NORM_ROWS, d), lambda i: (jnp.maximum(i - ps, 0), 0)),
            pl.BlockSpec((1, d), lambda i: (0, 0)),
        ],
        out_specs=pl.BlockSpec((NORM_ROWS, d), lambda i: (i, 0)),
        out_shape=jax.ShapeDtypeStruct((xp.shape[0] + xs.shape[0], d), out_dtype),
        compiler_params=_params(1),
        name="rmsnorm_two_groups",
    )(xp, xs, g.reshape(1, d))


def _ep_plain(acc, extra, o_ref):
    o_ref[...] = acc.astype(o_ref.dtype)


def _ep_residual(acc, extra, o_ref):
    o_ref[...] = extra[0][...] + acc


def _ep_residual_two_groups(acc, extra, o_ref):
    xp_ref, xs_ref = extra
    tm = acc.shape[0]
    n_full = N_PROMPT // tm
    rem = N_PROMPT - n_full * tm
    assert tm - rem == N_SAMPLE and rem % SUBLANES == 0
    i = pl.program_id(1)

    @pl.when(i < n_full)
    def _():
        o_ref[...] = xp_ref[...] + acc

    @pl.when(i >= n_full)
    def _():
        o_ref[0:rem, :] = xp_ref[tm - rem:tm, :] + acc[0:rem]
        o_ref[rem:tm, :] = xs_ref[...] + acc[rem:tm]


def _two_group_specs(tm, width, col_of_j, row_align):
    def p_map(j, i):
        row = jnp.minimum(i * tm, N_PROMPT - tm)
        col = col_of_j(j) * width
        return (pl.multiple_of(row, row_align), col if isinstance(col, int) else pl.multiple_of(col, LANES))
    p_spec = pl.BlockSpec((pl.Element(tm), pl.Element(width)), p_map)
    s_spec = pl.BlockSpec((N_SAMPLE, width), lambda j, i: (0, col_of_j(j)))
    return p_spec, s_spec


def _ep_relu2(acc, extra, o_ref):
    r = jnp.maximum(acc, 0.0)
    o_ref[...] = (r * r).astype(o_ref.dtype)


def _ep_rope(acc, extra, o_ref):
    cos_ref, sin_ref = extra
    j = pl.program_id(0)
    tiles_per_part = W_AH // MM_TN
    is_rope = (j < W_AQKV // MM_TN) & ((j // tiles_per_part) % 3 != 2)

    @pl.when(is_rope)
    def _():
        cos2 = cos_ref[...]
        sin2 = sin_ref[...]
        for s in range(MM_TN // A_HEAD_DIM):
            x = acc[:, s * A_HEAD_DIM:(s + 1) * A_HEAD_DIM]
            o_ref[:, s * A_HEAD_DIM:(s + 1) * A_HEAD_DIM] = (
                x * cos2 + pltpu.roll(x, A_HEAD_DIM // 2, axis=1) * sin2)

    @pl.when(jnp.logical_not(is_rope))
    def _():
        o_ref[...] = acc


def _mm_body(*refs, b_rows_are_outputs, epilogue, n_extra):
    a_ref, b_ref = refs[0], refs[1]
    extra = refs[2:2 + n_extra]
    o_ref = refs[2 + n_extra]
    b_scr = refs[3 + n_extra]

    @pl.when(pl.program_id(1) == 0)
    def _():
        b_scr[...] = b_ref[...].astype(BF16)

    if b_rows_are_outputs:
        acc = lax.dot_general(a_ref[...], b_scr[...], (((1,), (1,)), ((), ())),
                              preferred_element_type=F32)
    else:
        acc = jnp.dot(a_ref[...], b_scr[...], preferred_element_type=F32)
    epilogue(acc, extra, o_ref)


def _matmul(a, b, *, n_cols, col_start=0, b_rows_are_outputs=False, tm=MM_TM, tn=MM_TN,
            out_dtype=F32, epilogue=_ep_plain, extras=(), extra_specs=(), name="matmul"):
    m, k = a.shape
    if b_rows_are_outputs:
        if col_start % tn == 0:
            b_spec = pl.BlockSpec((tn, k), lambda j, i: (j + col_start // tn, 0))
        else:
            assert col_start % SUBLANES == 0 and tn % SUBLANES == 0
            b_spec = pl.BlockSpec((pl.Element(tn), pl.Element(k)),
                                  lambda j, i: (pl.multiple_of(col_start + j * tn, SUBLANES), 0))
        b_block = (tn, k)
    else:
        assert col_start % tn == 0
        b_spec = pl.BlockSpec((k, tn), lambda j, i: (0, j + col_start // tn))
        b_block = (k, tn)
    return pl.pallas_call(
        functools.partial(_mm_body, b_rows_are_outputs=b_rows_are_outputs, epilogue=epilogue,
                          n_extra=len(extras)),
        grid=(n_cols // tn, m // tm),
        in_specs=[pl.BlockSpec((tm, k), lambda j, i: (i, 0)), b_spec, *extra_specs],
        out_specs=pl.BlockSpec((tm, tn), lambda j, i: (i, j)),
        out_shape=jax.ShapeDtypeStruct((m, n_cols), out_dtype),
        scratch_shapes=[pltpu.VMEM(b_block, BF16)],
        compiler_params=_params(2),
        name=name,
    )(a, b, *extras)


def _mm_stream_body(*refs, b_rows_are_outputs, epilogue, n_extra, slice_rows):
    a_ref, b0_ref, bs_ref = refs[0], refs[1], refs[2]
    extra = refs[3:3 + n_extra]
    o_ref = refs[3 + n_extra]
    b_scr = refs[4 + n_extra]
    j, i = pl.program_id(0), pl.program_id(1)
    cur = j % 2

    @pl.when((j == 0) & (i == 0))
    def _():
        b_scr[0] = b0_ref[...].astype(BF16)

    r0 = pl.multiple_of(i * slice_rows, slice_rows)
    b_scr[1 - cur, pl.ds(r0, slice_rows), :] = bs_ref[...].astype(BF16)

    if b_rows_are_outputs:
        acc = lax.dot_general(a_ref[...], b_scr[cur], (((1,), (1,)), ((), ())),
                              preferred_element_type=F32)
    else:
        acc = jnp.dot(a_ref[...], b_scr[cur], preferred_element_type=F32)
    epilogue(acc, extra, o_ref)


def _matmul_stream(a, b, *, n_cols, col_start=0, b_rows_are_outputs=False, tm=MM_TM, tn=MM_TN,
                   out_dtype=F32, epilogue=_ep_plain, extras=(), extra_specs=(), name="matmul"):
    m, k = a.shape
    ni, nj = m // tm, n_cols // tn
    assert col_start % SUBLANES == 0

    def nxt(j):
        return jnp.minimum(j + 1, nj - 1)

    if b_rows_are_outputs:
        tile = (tn, k)
        slice_rows = tn // ni
        b0_spec = pl.BlockSpec((pl.Element(tn), pl.Element(k)), lambda j, i: (col_start, 0),
                               pipeline_mode=pl.Buffered(1))
        bs_spec = pl.BlockSpec(
            (pl.Element(slice_rows), pl.Element(k)),
            lambda j, i: (pl.multiple_of(col_start + nxt(j) * tn + i * slice_rows, SUBLANES), 0))
    else:
        assert col_start % tn == 0
        tile = (k, tn)
        slice_rows = k // ni
        b0_spec = pl.BlockSpec((k, tn), lambda j, i: (0, col_start // tn), pipeline_mode=pl.Buffered(1))
        bs_spec = pl.BlockSpec((slice_rows, tn), lambda j, i: (i, col_start // tn + nxt(j)))
    assert slice_rows * ni == tile[0] and slice_rows % 16 == 0
    return pl.pallas_call(
        functools.partial(_mm_stream_body, b_rows_are_outputs=b_rows_are_outputs, epilogue=epilogue,
                          n_extra=len(extras), slice_rows=slice_rows),
        grid=(nj, ni),
        in_specs=[pl.BlockSpec((tm, k), lambda j, i: (i, 0)), b0_spec, bs_spec, *extra_specs],
        out_specs=pl.BlockSpec((tm, tn), lambda j, i: (i, j)),
        out_shape=jax.ShapeDtypeStruct((m, n_cols), out_dtype),
        scratch_shapes=[pltpu.VMEM((2,) + tile, BF16)],
        compiler_params=_params(2),
        name=name,
    )(a, b, b, *extras)


def _mm_ksplit_body(a_ref, b_ref, res_ref, o_ref, acc_ref):
    kk = pl.program_id(2)

    @pl.when(kk == 0)
    def _():
        acc_ref[...] = jnp.zeros_like(acc_ref)

    acc_ref[...] += jnp.dot(a_ref[...], b_ref[...].astype(BF16), preferred_element_type=F32)

    @pl.when(kk == pl.num_programs(2) - 1)
    def _():
        o_ref[...] = res_ref[...] + acc_ref[...]


def _matmul_ksplit_residual(a, b, res, *, tm=DOWN_TM, tn=MM_TN, tk=DOWN_TK):
    m, k = a.shape
    n = b.shape[1]
    return pl.pallas_call(
        _mm_ksplit_body,
        grid=(n // tn, m // tm, k // tk),
        in_specs=[
            pl.BlockSpec((tm, tk), lambda j, i, kk: (i, kk)),
            pl.BlockSpec((tk, tn), lambda j, i, kk: (kk, j)),
            pl.BlockSpec((tm, tn), lambda j, i, kk: (i, j)),
        ],
        out_specs=pl.BlockSpec((tm, tn), lambda j, i, kk: (i, j)),
        out_shape=jax.ShapeDtypeStruct((m, n), F32),
        scratch_shapes=[pltpu.VMEM((tm, tn), F32)],
        compiler_params=_params(3),
        name="mlp_down",
    )(a, b, res)


def _sigmoid(x):
    return 1.0 / (1.0 + jnp.exp(-x))


def _two_group_rows(p_ref, s_ref, is_prompt_tile):
    tm = p_ref.shape[0]
    rem = tm - s_ref.shape[0]
    straddle = jnp.concatenate([p_ref[tm - rem:tm, :], s_ref[...]], axis=0)
    return jnp.where(is_prompt_tile, p_ref[...], straddle)


def _merge_body(ap_ref, as_ref, ybp_ref, ybs_ref, ycp_ref, ycs_ref, wa_ref, wb_ref, wc_ref,
                ga_ref, gb_ref, gc_ref, ba_ref, bb_ref, bc_ref,
                o_ref, wa_scr, wb_scr, wc_scr):
    @pl.when(pl.program_id(1) == 0)
    def _():
        wa_scr[...] = wa_ref[...].astype(BF16)
        wb_scr[...] = wb_ref[...].astype(BF16)
        wc_scr[...] = wc_ref[...].astype(BF16)

    is_prompt_tile = pl.program_id(1) < N_PROMPT // ap_ref.shape[0]
    y_a = jnp.dot(_two_group_rows(ap_ref, as_ref, is_prompt_tile), wa_scr[...], preferred_element_type=F32)
    y_b = jnp.dot(_two_group_rows(ybp_ref, ybs_ref, is_prompt_tile), wb_scr[...], preferred_element_type=F32)
    y_c = jnp.dot(_two_group_rows(ycp_ref, ycs_ref, is_prompt_tile), wc_scr[...], preferred_element_type=F32)
    g_a = _sigmoid(ga_ref[...] + ba_ref[...])
    g_b = _sigmoid(gb_ref[...] + bb_ref[...])
    g_c = _sigmoid(gc_ref[...] + bc_ref[...])
    o_ref[...] = (g_a * y_a + g_b * y_b + g_c * y_c).astype(o_ref.dtype)


def _branch_merge(a_ps, yb_ps, yc_ps, w_a, w_b, w_c, z_rest, b_branch, *, tm=MERGE_TM, tn=MM_TN):
    m = N_ALL
    assert (N_PROMPT // tm + 1) * tm == N_ALL
    nj = D_MODEL // tn
    br0 = COL2_BR // tn

    def gate_spec(part):
        return pl.BlockSpec((tm, tn), lambda j, i: (i, br0 + part * nj + j))

    def bias_spec(part):
        return pl.BlockSpec((1, tn), lambda j, i: (0, part * nj + j))

    bf16_rows = 16
    return pl.pallas_call(
        _merge_body,
        grid=(nj, m // tm),
        in_specs=[
            *_two_group_specs(tm, W_MV, lambda j: 0, bf16_rows),
            *_two_group_specs(tm, W_AH, lambda j: 0, bf16_rows),
            *_two_group_specs(tm, W_XQ, lambda j: 0, bf16_rows),
            pl.BlockSpec((W_MV, tn), lambda j, i: (0, j), pipeline_mode=pl.Buffered(1)),
            pl.BlockSpec((W_AH, tn), lambda j, i: (0, j), pipeline_mode=pl.Buffered(1)),
            pl.BlockSpec((W_XQ, tn), lambda j, i: (0, j), pipeline_mode=pl.Buffered(1)),
            gate_spec(0), gate_spec(1), gate_spec(2),
            bias_spec(0), bias_spec(1), bias_spec(2),
        ],
        out_specs=pl.BlockSpec((tm, tn), lambda j, i: (i, j)),
        out_shape=jax.ShapeDtypeStruct((m, D_MODEL), BF16),
        scratch_shapes=[pltpu.VMEM((W_MV, tn), BF16), pltpu.VMEM((W_AH, tn), BF16),
                        pltpu.VMEM((W_XQ, tn), BF16)],
        compiler_params=_params(2),
        name="branch_merge",
    )(*a_ps, *yb_ps, *yc_ps, w_a, w_b, w_c, z_rest, z_rest, z_rest, b_branch, b_branch, b_branch)


def _log_sigmoid(x):
    return jnp.minimum(x, 0.0) - jnp.log1p(jnp.exp(-jnp.abs(x)))


def _mlstm_chunk(q, k, v, irow, frow, c_state, n_state, m_state, n_valid):
    L = q.shape[0]
    ti = lax.broadcasted_iota(jnp.int32, (L, L), 0)
    si = lax.broadcasted_iota(jnp.int32, (L, L), 1)
    causal = si <= ti
    eye = si == ti
    f_b = jnp.broadcast_to(frow, (L, L))
    i_b = jnp.broadcast_to(irow, (L, L))
    bcol = jnp.sum(jnp.where(causal, f_b, 0.0), axis=1, keepdims=True)
    fcol = jnp.sum(jnp.where(eye, f_b, 0.0), axis=1, keepdims=True)
    icol = jnp.sum(jnp.where(eye, i_b, 0.0), axis=1, keepdims=True)
    brow = jnp.sum(jnp.where(ti <= si, jnp.broadcast_to(fcol, (L, L)), 0.0), axis=0, keepdims=True)

    acol = bcol + m_state
    logw = jnp.where(causal, bcol - brow + irow, NEG_INF)
    mt = jnp.maximum(acol, jnp.max(logw, axis=1, keepdims=True))
    w_inter = jnp.exp(acol - mt)
    w_intra = jnp.exp(logw - mt)

    qb = q.astype(BF16)
    kb = k.astype(BF16)
    nt = (((1,), (1,)), ((), ()))
    s = lax.dot_general(qb, kb, nt, preferred_element_type=F32) * w_intra
    inter = lax.dot_general(qb, c_state.astype(BF16), nt, preferred_element_type=F32)
    num = w_inter * inter + jnp.dot(s.astype(BF16), v.astype(BF16), preferred_element_type=F32)
    nq = w_inter * jnp.sum(q * n_state, axis=1, keepdims=True) + jnp.sum(s, axis=1, keepdims=True)
    h = num / jnp.maximum(jnp.abs(nq), jnp.exp(-mt))

    last = slice(n_valid - 1, n_valid)
    m_end = mt[last, :]
    w_c = jnp.exp(acol[last, :] - m_end)
    w_s = jnp.exp(bcol[last, :] - bcol + icol - m_end)
    if n_valid < L:
        w_s = jnp.where(lax.broadcasted_iota(jnp.int32, (L, 1), 0) < n_valid, w_s, 0.0)
    tn = (((0,), (0,)), ((), ()))
    c_new = w_c * c_state + lax.dot_general((v * w_s).astype(BF16), kb, tn, preferred_element_type=F32)
    n_new = w_c * n_state + jnp.sum(w_s * k, axis=0, keepdims=True)
    return h, c_new, n_new, m_end


def _gate_rows(g_ref_val, bias_ref, head):
    irow = g_ref_val[0:1, :] + bias_ref[0, head]
    frow = _log_sigmoid(g_ref_val[1:2, :] + bias_ref[1, head])
    return irow, frow


def _mlstm_prompt_body(bias_ref, q_ref, k_ref, v_ref, mo_ref, g_ref,
                       a_ref, c_ref, n_ref, m_ref):
    @pl.when(pl.program_id(1) == 0)
    def _():
        c_ref[...] = jnp.zeros_like(c_ref)
        n_ref[...] = jnp.zeros_like(n_ref)
        m_ref[...] = jnp.zeros_like(m_ref)

    for head in range(M_HEADS):
        qk = pl.ds(head * M_DQK, M_DQK)
        vo = pl.ds(head * M_DV, M_DV)
        one = pl.ds(head, 1)
        irow, frow = _gate_rows(g_ref[head], bias_ref, head)
        h, c_new, n_new, m_end = _mlstm_chunk(
            q_ref[:, qk], k_ref[:, qk] * (M_DQK ** -0.5), v_ref[:, vo], irow, frow,
            c_ref[0, head], n_ref[0, one, :], m_ref[0, one, 0:1], CHUNK)
        a_ref[:, vo] = (_sigmoid(mo_ref[:, vo]) * h).astype(a_ref.dtype)
        c_ref[0, head] = c_new
        n_ref[0, one, :] = n_new
        m_ref[0, one, :] = jnp.broadcast_to(m_end, (1, LANES))


def _mlstm_prompt(z_main, gates_rows, gate_bias):
    nc = SEQ // CHUNK
    return pl.pallas_call(
        _mlstm_prompt_body,
        grid=(BATCH, nc),
        in_specs=[
            pl.BlockSpec(memory_space=pltpu.SMEM),
            pl.BlockSpec((CHUNK, W_MQK), lambda b, c: (b * nc + c, 0)),
            pl.BlockSpec((CHUNK, W_MQK), lambda b, c: (b * nc + c, 1)),
            pl.BlockSpec((CHUNK, W_MV), lambda b, c: (b * nc + c, 2 * W_MQK // W_MV)),
            pl.BlockSpec((CHUNK, W_MV), lambda b, c: (b * nc + c, 2 * W_MQK // W_MV + 1)),
            pl.BlockSpec((M_HEADS, 2, CHUNK), lambda b, c: (0, 0, b * nc + c)),
        ],
        out_specs=[
            pl.BlockSpec((CHUNK, W_MV), lambda b, c: (b * nc + c, 0)),
            pl.BlockSpec((1, M_HEADS, M_DV, M_DQK), lambda b, c: (b, 0, 0, 0)),
            pl.BlockSpec((1, M_HEADS, M_DQK), lambda b, c: (b, 0, 0)),
            pl.BlockSpec((1, M_HEADS, LANES), lambda b, c: (b, 0, 0)),
        ],
        out_shape=[
            jax.ShapeDtypeStruct((N_PROMPT, W_MV), BF16),
            jax.ShapeDtypeStruct((BATCH, M_HEADS, M_DV, M_DQK), F32),
            jax.ShapeDtypeStruct((BATCH, M_HEADS, M_DQK), F32),
            jax.ShapeDtypeStruct((BATCH, M_HEADS, LANES), F32),
        ],
        compiler_params=_params(2),
        name="mlstm_prompt",
    )(gate_bias, z_main, z_main, z_main, z_main, gates_rows)


def _mlstm_sample_body(bias_ref, q_ref, k_ref, v_ref, mo_ref, g_ref, c0_ref, n0_ref, m0_ref,
                       a_ref, c_ref, n_ref, m_ref, q_scr, k_scr, v_scr):
    @pl.when(pl.program_id(0) == 0)
    def _():
        q_scr[...] = jnp.zeros_like(q_scr)
        k_scr[...] = jnp.zeros_like(k_scr)
        v_scr[...] = jnp.zeros_like(v_scr)

    q_scr[0:DEC_SEQ, :] = q_ref[0]
    k_scr[0:DEC_SEQ, :] = k_ref[0] * (M_DQK ** -0.5)
    v_scr[0:DEC_SEQ, :] = v_ref[0]
    for head in range(M_HEADS):
        qk = pl.ds(head * M_DQK, M_DQK)
        vo = pl.ds(head * M_DV, M_DV)
        one = pl.ds(head, 1)
        irow, frow = _gate_rows(g_ref[0, head], bias_ref, head)
        h, c_new, n_new, m_end = _mlstm_chunk(
            q_scr[:, qk], k_scr[:, qk], v_scr[:, vo], irow, frow,
            c0_ref[0, 0, head], n0_ref[0, 0, one, :], m0_ref[0, one, :], DEC_SEQ)
        a_ref[0, :, vo] = _sigmoid(mo_ref[0, :, vo]) * h[0:DEC_SEQ, :]
        c_ref[0, head] = c_new
        n_ref[0, one, :] = n_new
        m_ref[0, one, :] = jnp.broadcast_to(m_end, (1, LANES))


def _mlstm_sample(z_s3, gates_rows, gate_bias, c0, n0, m0):
    return pl.pallas_call(
        _mlstm_sample_body,
        grid=(DEC_BATCH,),
        in_specs=[
            pl.BlockSpec(memory_space=pltpu.SMEM),
            pl.BlockSpec((1, DEC_SEQ, W_MQK), lambda b: (b, 0, 0)),
            pl.BlockSpec((1, DEC_SEQ, W_MQK), lambda b: (b, 0, 1)),
            pl.BlockSpec((1, DEC_SEQ, W_MV), lambda b: (b, 0, 2 * W_MQK // W_MV)),
            pl.BlockSpec((1, DEC_SEQ, W_MV), lambda b: (b, 0, 2 * W_MQK // W_MV + 1)),
            pl.BlockSpec((1, M_HEADS, 2, SAMPLE_CHUNK), lambda b: (b, 0, 0, 0)),
            pl.BlockSpec((1, 1, M_HEADS, M_DV, M_DQK), lambda b: (0, b, 0, 0, 0)),
            pl.BlockSpec((1, 1, M_HEADS, M_DQK), lambda b: (0, b, 0, 0)),
            pl.BlockSpec((1, M_HEADS, 1), lambda b: (b, 0, 0)),
        ],
        out_specs=[
            pl.BlockSpec((1, DEC_SEQ, W_MV), lambda b: (b, 0, 0)),
            pl.BlockSpec((1, M_HEADS, M_DV, M_DQK), lambda b: (b, 0, 0, 0)),
            pl.BlockSpec((1, M_HEADS, M_DQK), lambda b: (b, 0, 0)),
            pl.BlockSpec((1, M_HEADS, LANES), lambda b: (b, 0, 0)),
        ],
        out_shape=[
            jax.ShapeDtypeStruct((DEC_BATCH, DEC_SEQ, W_MV), F32),
            jax.ShapeDtypeStruct((DEC_BATCH, M_HEADS, M_DV, M_DQK), F32),
            jax.ShapeDtypeStruct((DEC_BATCH, M_HEADS, M_DQK), F32),
            jax.ShapeDtypeStruct((DEC_BATCH, M_HEADS, LANES), F32),
        ],
        scratch_shapes=[pltpu.VMEM((SAMPLE_CHUNK, W_MQK), F32), pltpu.VMEM((SAMPLE_CHUNK, W_MQK), F32),
                        pltpu.VMEM((SAMPLE_CHUNK, W_MV), F32)],
        compiler_params=_params(1),
        name="mlstm_sample",
    )(gate_bias, z_s3, z_s3, z_s3, z_s3, gates_rows, c0, n0, m0)


def _dil_prompt_body(q_ref, kp_ref, kc_ref, vp_ref, vc_ref, o_ref, l_ref, *, r, hb):
    first_key = jnp.where(pl.program_id(2) > 0, 0, SPAN)
    qi = lax.broadcasted_iota(jnp.int32, (SPAN, 2 * SPAN), 0)
    ki = lax.broadcasted_iota(jnp.int32, (SPAN, 2 * SPAN), 1)
    ok = (ki >= qi) & (ki <= qi + SPAN) & (ki >= first_key)
    bias = jnp.where(ok, 0.0, NEG_INF)
    scale = A_HEAD_DIM ** -0.5
    nt = (((1,), (1,)), ((), ()))
    for c in range(r):
        rows = pl.ds(c, SPAN, stride=r) if r > 1 else pl.ds(0, SPAN)
        for hh in range(hb):
            cols = pl.ds(hh * A_HEAD_DIM, A_HEAD_DIM)
            q = q_ref[rows, cols].astype(BF16)
            kk = jnp.concatenate([kp_ref[rows, cols], kc_ref[rows, cols]], axis=0).astype(BF16)
            vv = jnp.concatenate([vp_ref[rows, cols], vc_ref[rows, cols]], axis=0).astype(BF16)
            s = lax.dot_general(q, kk, nt, preferred_element_type=F32) * scale + bias
            m = jnp.max(s, axis=1, keepdims=True)
            p = jnp.exp(s - m)
            den = jnp.sum(p, axis=1, keepdims=True)
            o = jnp.dot(p.astype(BF16), vv, preferred_element_type=F32) / den
            o_ref[rows, cols] = o
            l_ref[rows, cols] = jnp.broadcast_to(m + jnp.log(den), (SPAN, A_HEAD_DIM))


def _dilated_prompt(z_rest, g, r, hb):
    rows = SPAN * r
    nblk = SEQ // rows
    wcol = hb * A_HEAD_DIM
    per_part = W_AH // wcol

    def spec(part, prev):
        def imap(b, hg, n):
            nn = jnp.maximum(n - 1, 0) if prev else n
            return (b * nblk + nn, (3 * g + part) * per_part + hg)
        return pl.BlockSpec((rows, wcol), imap)

    out_spec = pl.BlockSpec((rows, wcol), lambda b, hg, n: (b * nblk + n, hg))
    return pl.pallas_call(
        functools.partial(_dil_prompt_body, r=r, hb=hb),
        grid=(BATCH, A_HEADS // hb, nblk),
        in_specs=[spec(0, False), spec(1, True), spec(1, False), spec(2, True), spec(2, False)],
        out_specs=[out_spec, out_spec],
        out_shape=[jax.ShapeDtypeStruct((N_PROMPT, W_AH), F32)] * 2,
        compiler_params=_params(3),
        name=f"dilated_prompt_g{g}",
    )(z_rest, z_rest, z_rest, z_rest, z_rest)


def _cached_attn_body(q_ref, kn_ref, vn_ref, kc_ref, vc_ref, bias_ref, o_ref, l_ref, *, scale):
    n_cached = kc_ref.shape[1] * kc_ref.shape[2]
    q = q_ref[0].astype(BF16)
    kk = jnp.concatenate([kc_ref[0].reshape(n_cached, A_HEAD_DIM), kn_ref[0]], axis=0).astype(BF16)
    vv = jnp.concatenate([vc_ref[0].reshape(n_cached, A_HEAD_DIM), vn_ref[0]], axis=0).astype(BF16)
    nt = (((1,), (1,)), ((), ()))
    s = lax.dot_general(q, kk, nt, preferred_element_type=F32) * scale + bias_ref[...]
    m = jnp.max(s, axis=1, keepdims=True)
    p = jnp.exp(s - m)
    den = jnp.sum(p, axis=1, keepdims=True)
    o_ref[0] = jnp.dot(p.astype(BF16), vv, preferred_element_type=F32) / den
    l_ref[0] = jnp.broadcast_to(m + jnp.log(den), l_ref.shape[1:])


def _dilated_sample_bias(window, r, lb, rc):
    span = window // r
    n_c = (lb // r) * rc * A_HEADS
    rows = np.arange(DEC_SEQ * A_HEADS)
    s_q, h_q = rows // A_HEADS, rows % A_HEADS
    col = np.arange(n_c)
    m_k = col // (rc * A_HEADS)
    c_k = (col % (rc * A_HEADS)) // A_HEADS
    h_k = col % A_HEADS
    delta = (lb + s_q)[:, None] - (m_k * r + c_k)[None, :]
    ok_c = (h_q[:, None] == h_k[None, :]) & (delta % r == 0) & (delta // r <= span) & (delta >= 0)
    coln = np.arange(DEC_SEQ * A_HEADS)
    s_n, h_n = coln // A_HEADS, coln % A_HEADS
    dn = s_q[:, None] - s_n[None, :]
    ok_n = (h_q[:, None] == h_n[None, :]) & (dn >= 0) & (dn % r == 0) & (dn // r <= span)
    ok = np.concatenate([ok_c, ok_n], axis=1)
    return np.where(ok, 0.0, -np.inf).astype(np.float32)


def _dilated_sample(q, kn, vn, cache_k, cache_v, window, r, g):
    lb = cache_k.shape[1]
    assert lb % r == 0 and window % r == 0
    rc = min(r, DEC_SEQ)
    nm = lb // r
    ck = cache_k.reshape(DEC_BATCH, nm, r * A_HEADS, A_HEAD_DIM)
    cv = cache_v.reshape(DEC_BATCH, nm, r * A_HEADS, A_HEAD_DIM)
    bias = jnp.asarray(_dilated_sample_bias(window, r, lb, rc))
    rq = DEC_SEQ * A_HEADS
    row_spec = pl.BlockSpec((1, rq, A_HEAD_DIM), lambda b: (b, 0, 0))
    cache_spec = pl.BlockSpec((1, nm, rc * A_HEADS, A_HEAD_DIM), lambda b: (b, 0, 0, 0))
    return pl.pallas_call(
        functools.partial(_cached_attn_body, scale=A_HEAD_DIM ** -0.5),
        grid=(DEC_BATCH,),
        in_specs=[row_spec, row_spec, row_spec, cache_spec, cache_spec,
                  pl.BlockSpec(bias.shape, lambda b: (0, 0))],
        out_specs=[row_spec, row_spec],
        out_shape=[jax.ShapeDtypeStruct((DEC_BATCH, rq, A_HEAD_DIM), F32)] * 2,
        compiler_params=_params(1),
        name=f"dilated_sample_g{g}",
    )(q, kn, vn, ck, cv, bias)


def _cross_sample_body(q_ref, k_ref, v_ref, bias_ref, o_ref, *, scale):
    q = q_ref[0].astype(BF16)
    nt = (((1,), (1,)), ((), ()))
    s = lax.dot_general(q, k_ref[0].astype(BF16), nt, preferred_element_type=F32) * scale + bias_ref[...]
    m = jnp.max(s, axis=1, keepdims=True)
    e = jnp.exp(s - m)
    p = e / jnp.sum(e, axis=1, keepdims=True)
    o_ref[0] = jnp.dot(p.astype(BF16), v_ref[0].astype(BF16), preferred_element_type=F32)


def _cross_sample(q, mem_k, mem_v):
    rq = DEC_SEQ * X_HEADS
    nk = MEM_LEN * X_HEADS
    ok = (np.arange(rq) % X_HEADS)[:, None] == (np.arange(nk) % X_HEADS)[None, :]
    bias = jnp.asarray(np.where(ok, 0.0, -np.inf).astype(np.float32))
    return pl.pallas_call(
        functools.partial(_cross_sample_body, scale=X_HEAD_DIM ** -0.5),
        grid=(DEC_BATCH,),
        in_specs=[
            pl.BlockSpec((1, rq, X_HEAD_DIM), lambda b: (b, 0, 0)),
            pl.BlockSpec((1, nk, X_HEAD_DIM), lambda b: (b, 0, 0)),
            pl.BlockSpec((1, nk, X_HEAD_DIM), lambda b: (b, 0, 0)),
            pl.BlockSpec((rq, nk), lambda b: (0, 0)),
        ],
        out_specs=pl.BlockSpec((1, rq, X_HEAD_DIM), lambda b: (b, 0, 0)),
        out_shape=jax.ShapeDtypeStruct((DEC_BATCH, rq, X_HEAD_DIM), F32),
        compiler_params=_params(1),
        name="cross_sample",
    )(q, mem_k, mem_v, bias)


def _cross_prompt_body(q_ref, k_ref, v_ref, o_ref):
    scale = X_HEAD_DIM ** -0.5
    nt = (((1,), (1,)), ((), ()))
    for h in range(X_HEADS):
        cols = pl.ds(h * X_HEAD_DIM, X_HEAD_DIM)
        s = lax.dot_general(q_ref[:, cols].astype(BF16), k_ref[:, cols].astype(BF16), nt,
                            preferred_element_type=F32) * scale
        m = jnp.max(s, axis=1, keepdims=True)
        e = jnp.exp(s - m)
        p = e / jnp.sum(e, axis=1, keepdims=True)
        o_ref[:, cols] = jnp.dot(p.astype(BF16), v_ref[:, cols].astype(BF16),
                                 preferred_element_type=F32).astype(o_ref.dtype)


def _cross_prompt(z_rest, mem_kv, *, tq=512):
    nq = SEQ // tq
    return pl.pallas_call(
        _cross_prompt_body,
        grid=(BATCH, nq),
        in_specs=[
            pl.BlockSpec((tq, W_XQ), lambda b, i: (b * nq + i, COL2_XQ // W_XQ)),
            pl.BlockSpec((MEM_LEN, W_XQ), lambda b, i: (b, 0)),
            pl.BlockSpec((MEM_LEN, W_XQ), lambda b, i: (b, 1)),
        ],
        out_specs=pl.BlockSpec((tq, W_XQ), lambda b, i: (b * nq + i, 0)),
        out_shape=jax.ShapeDtypeStruct((N_PROMPT, W_XQ), BF16),
        compiler_params=_params(2),
        name="cross_prompt",
    )(z_rest, mem_kv, mem_kv)


def _combine_body(o0, o1, o2, l0, l1, l2, y_ref):
    a0, a1, a2 = l0[...], l1[...], l2[...]
    mx = jnp.maximum(jnp.maximum(a0, a1), a2)
    e0, e1, e2 = jnp.exp(a0 - mx), jnp.exp(a1 - mx), jnp.exp(a2 - mx)
    tot = e0 + e1 + e2
    y = (e0 / tot) * o0[...] + (e1 / tot) * o1[...] + (e2 / tot) * o2[...]
    y_ref[...] = y.astype(y_ref.dtype)


def _combine_groups(outs, lses, *, rows_per_step):
    n, w = outs[0].shape
    spec = pl.BlockSpec((rows_per_step, w), lambda i: (i, 0))
    return pl.pallas_call(
        _combine_body,
        grid=(n // rows_per_step,),
        in_specs=[spec] * 6,
        out_specs=spec,
        out_shape=jax.ShapeDtypeStruct((n, w), BF16),
        compiler_params=_params(1),
        name="combine_groups",
    )(*outs, *lses)


def _rope_tables():
    pos = jnp.concatenate([
        jnp.tile(jnp.arange(SEQ, dtype=jnp.int32), BATCH),
        jnp.tile(PAST_LEN + jnp.arange(DEC_SEQ, dtype=jnp.int32), DEC_BATCH)])
    inv = ROPE_THETA ** (-jnp.arange(0, A_HEAD_DIM, 2, dtype=F32) / A_HEAD_DIM)
    ang = pos.astype(F32)[:, None] * inv[None, :]
    cos, sin = jnp.cos(ang), jnp.sin(ang)
    return jnp.concatenate([cos, cos], axis=1), jnp.concatenate([-sin, sin], axis=1)


def _layer(x_p, x_s, mem_prompt, state_c, state_n, state_m, caches, cache_mem_k, cache_mem_v,
           g_mix, w_in, b_igate, b_fgate, b_branch, g_mem, w_mem_kv,
           w_br_a, w_br_b, w_br_c, w_out, g_mlp, w_up, w_down):
    h_all = _rmsnorm_two_groups(x_p, x_s, g_mix, BF16)

    w_in_t = w_in.T
    z_main = _matmul_stream(h_all, w_in_t, n_cols=COL_GATES, b_rows_are_outputs=True, name="proj_main")
    z_gate = _matmul(h_all, w_in_t, n_cols=2 * M_HEADS, col_start=COL_GATES, tn=2 * M_HEADS,
                     b_rows_are_outputs=True, name="proj_gates")
    cos2, sin2 = _rope_tables()
    table_spec = pl.BlockSpec((MM_TM, A_HEAD_DIM), lambda j, i: (i, 0))
    z_rest = _matmul_stream(h_all, w_in_t, n_cols=W_REST, col_start=COL_REST, b_rows_are_outputs=True,
                            epilogue=_ep_rope, extras=(cos2, sin2), extra_specs=(table_spec, table_spec),
                            name="proj_rest")

    gate_bias = jnp.stack([b_igate, b_fgate]).astype(F32)

    gp = z_gate[:N_PROMPT].reshape(N_PROMPT, 2, M_HEADS).transpose(2, 1, 0)
    a_p, c_p, n_p, m_p = _mlstm_prompt(z_main, gp, gate_bias)
    gs = z_gate[N_PROMPT:].reshape(DEC_BATCH, DEC_SEQ, 2, M_HEADS).transpose(0, 3, 2, 1)
    gs = jnp.pad(gs, ((0, 0), (0, 0), (0, 0), (0, SAMPLE_CHUNK - DEC_SEQ)))
    z_main_s = z_main[N_PROMPT:].reshape(DEC_BATCH, DEC_SEQ, COL_GATES)
    a_s, c_s, n_s, m_s = _mlstm_sample(
        z_main_s, gs, gate_bias, state_c, state_n, state_m.reshape(DEC_BATCH, M_HEADS, 1))
    a_ps = (a_p, a_s.reshape(N_SAMPLE, W_MV).astype(BF16))

    z_rest_s = z_rest[N_PROMPT:]
    heads_per_step = (8, 1, 1)
    outs_p, lses_p, outs_s, lses_s, rows_p, rows_s = [], [], [], [], [], []
    rq = DEC_SEQ * A_HEADS
    for g, (window, r) in enumerate(DIL_GROUPS):
        o, l = _dilated_prompt(z_rest, g, r, heads_per_step[g])
        outs_p.append(o)
        lses_p.append(l)
        c0 = 3 * g * W_AH
        qs = z_rest_s[:, c0:c0 + W_AH].reshape(DEC_BATCH, rq, A_HEAD_DIM)
        ks = z_rest_s[:, c0 + W_AH:c0 + 2 * W_AH].reshape(DEC_BATCH, rq, A_HEAD_DIM)
        vs = z_rest_s[:, c0 + 2 * W_AH:c0 + 3 * W_AH].reshape(DEC_BATCH, rq, A_HEAD_DIM)
        o, l = _dilated_sample(qs, ks, vs, caches[g][0], caches[g][1], window, r, g)
        outs_s.append(o.reshape(DEC_BATCH * rq, A_HEAD_DIM))
        lses_s.append(l.reshape(DEC_BATCH * rq, A_HEAD_DIM))
        keep = min(window, SEQ)
        for part in (1, 2):
            cs = c0 + part * W_AH
            kept = [z_rest[(b + 1) * SEQ - keep:(b + 1) * SEQ, cs:cs + W_AH] for b in range(BATCH)]
            rows_p.append(jnp.stack(kept).reshape(BATCH, keep, A_HEADS, A_HEAD_DIM))
        rows_s += [ks.reshape(DEC_BATCH, DEC_SEQ, A_HEADS, A_HEAD_DIM),
                   vs.reshape(DEC_BATCH, DEC_SEQ, A_HEADS, A_HEAD_DIM)]
    yb_p = _combine_groups(outs_p, lses_p, rows_per_step=256)
    yb_s = _combine_groups(outs_s, lses_s, rows_per_step=DEC_BATCH * rq)
    yb_ps = (yb_p, yb_s.reshape(N_SAMPLE, W_AH))

    mem_h = _rmsnorm(mem_prompt, g_mem, BF16)
    mem_kv = _matmul(mem_h, w_mem_kv, n_cols=2 * W_XQ, tm=BATCH * MEM_LEN, name="mem_kv")
    yc_p = _cross_prompt(z_rest, mem_kv)
    xq_s = z_rest_s[:, COL2_XQ:COL2_XQ + W_XQ].reshape(DEC_BATCH, DEC_SEQ * X_HEADS, X_HEAD_DIM)
    yc_s = _cross_sample(
        xq_s,
        cache_mem_k.reshape(DEC_BATCH, MEM_LEN * X_HEADS, X_HEAD_DIM),
        cache_mem_v.reshape(DEC_BATCH, MEM_LEN * X_HEADS, X_HEAD_DIM))
    yc_ps = (yc_p, yc_s.reshape(N_SAMPLE, W_XQ).astype(BF16))

    merged = _branch_merge(a_ps, yb_ps, yc_ps, w_br_a, w_br_b, w_br_c, z_rest, b_branch.reshape(1, 3 * D_MODEL))
    x1 = _matmul_stream(merged, w_out, n_cols=D_MODEL, epilogue=_ep_residual_two_groups, extras=(x_p, x_s),
                        extra_specs=_two_group_specs(MM_TM, MM_TN, lambda j: j, SUBLANES), name="out_proj")

    h2 = _rmsnorm(x1, g_mlp, BF16)
    u = _matmul_stream(h2, w_up, n_cols=D_FF, out_dtype=BF16, epilogue=_ep_relu2, name="mlp_up")
    x2 = _matmul_ksplit_residual(u, w_down, x1)

    mem_k = mem_kv[:, :W_XQ].reshape(BATCH, MEM_LEN, X_HEADS, X_HEAD_DIM)
    mem_v = mem_kv[:, W_XQ:].reshape(BATCH, MEM_LEN, X_HEADS, X_HEAD_DIM)
    prompt_state = (c_p, n_p, m_p[:, :, 0])
    sample_state = (c_s, n_s, m_s[:, :, 0])
    return x2, prompt_state, sample_state, rows_p, rows_s, mem_k, mem_v


def kernel(x_prompt, x_sample, state_mlstm_C, state_mlstm_n, state_mlstm_m,
           cache_win_k_g0, cache_win_v_g0, cache_win_k_g1, cache_win_v_g1,
           cache_win_k_g2, cache_win_v_g2, cache_mem_k, cache_mem_v, mem_prompt,
           g_mix, w_in, b_igate, b_fgate, b_branch, g_mem, w_mem_kv,
           w_br_a, w_br_b, w_br_c, w_out, g_mlp, w_up, w_down, g_final):
    depth = g_mix.shape[0]
    assert depth == 1, "single-layer stack"
    caches = ((cache_win_k_g0[0], cache_win_v_g0[0]),
              (cache_win_k_g1[0], cache_win_v_g1[0]),
              (cache_win_k_g2[0], cache_win_v_g2[0]))
    x2, p_state, s_state, rows_p, rows_s, mem_k, mem_v = _layer(
        x_prompt.reshape(N_PROMPT, D_MODEL), x_sample.reshape(N_SAMPLE, D_MODEL),
        mem_prompt.reshape(BATCH * MEM_LEN, D_MODEL),
        state_mlstm_C, state_mlstm_n, state_mlstm_m[0], caches, cache_mem_k[0], cache_mem_v[0],
        g_mix[0], w_in[0], b_igate[0], b_fgate[0], b_branch[0], g_mem[0], w_mem_kv[0],
        w_br_a[0], w_br_b[0], w_br_c[0], w_out[0], g_mlp[0], w_up[0], w_down[0])
    y_prompt = _rmsnorm(x2, g_final, F32, row_start=0, n_rows=N_PROMPT).reshape(BATCH, SEQ, D_MODEL)
    y_sample = _rmsnorm(x2, g_final, F32, row_start=N_PROMPT, n_rows=N_SAMPLE).reshape(DEC_BATCH, DEC_SEQ, D_MODEL)
    lead = lambda a: a[None]
    return (y_prompt, y_sample,
            lead(p_state[0]), lead(p_state[1]), lead(p_state[2]),
            *[lead(r) for r in rows_p],
            lead(mem_k), lead(mem_v),
            lead(s_state[0]), lead(s_state[1]), lead(s_state[2]),
            *[lead(r) for r in rows_s])
```

```python
import functools
import math

import numpy as np
import jax
import jax.numpy as jnp
from jax import lax
from jax.experimental import pallas as pl
from jax.experimental.pallas import tpu as pltpu

F32 = jnp.float32
BF16 = jnp.bfloat16
NEG_INF = float("-inf")

D_MODEL = 4096
BATCH = 2
SEQ = 4096
DEC_BATCH = 128
DEC_SEQ = 4
PAST_LEN = 2048
MEM_LEN = 256
M_HEADS = 8
M_DQK = D_MODEL // (2 * M_HEADS)
M_DV = D_MODEL // M_HEADS
A_HEAD_DIM = 128
A_HEADS = D_MODEL // 512
DIL_GROUPS = ((128, 1), (512, 4), (2048, 16))
N_GROUPS = 3
ROPE_THETA = 10000.0
X_HEADS = 4
X_HEAD_DIM = 128
D_FF = 4 * D_MODEL
EPS = 1e-6

W_MQK = M_HEADS * M_DQK
W_MV = M_HEADS * M_DV
W_AH = A_HEADS * A_HEAD_DIM
W_XQ = X_HEADS * X_HEAD_DIM
N_PROMPT = BATCH * SEQ
N_SAMPLE = DEC_BATCH * DEC_SEQ
N_ALL = N_PROMPT + N_SAMPLE

COL_GATES = 2 * W_MQK + 2 * W_MV
COL_REST = COL_GATES + 2 * M_HEADS
W_AQKV = 3 * N_GROUPS * W_AH
COL2_XQ = W_AQKV
COL2_BR = W_AQKV + W_XQ
W_REST = W_AQKV + W_XQ + 3 * D_MODEL

LANES = 128
SUBLANES = 8
VMEM_LIMIT_BYTES = 56 * 1024 * 1024

MM_TM = 1088
MM_TN = 512
MERGE_TM = 544
DOWN_TM = 2176
DOWN_TK = 2048
NORM_ROWS = 256
CHUNK = 128
SAMPLE_CHUNK = 16
SPAN = 128


def _params(n_axes):
    return pltpu.CompilerParams(
        dimension_semantics=("arbitrary",) * n_axes,
        vmem_limit_bytes=VMEM_LIMIT_BYTES,
    )


def _rmsnorm_body(x_ref, g_ref, o_ref):
    x = x_ref[...]
    ms = jnp.mean(x * x, axis=-1, keepdims=True)
    o_ref[...] = ((x * lax.rsqrt(ms + EPS)) * g_ref[...]).astype(o_ref.dtype)


def _rmsnorm(x, g, out_dtype, *, row_start=0, n_rows=None):
    n_rows = x.shape[0] if n_rows is None else n_rows
    d = x.shape[1]
    off = row_start // NORM_ROWS
    return pl.pallas_call(
        _rmsnorm_body,
        grid=(n_rows // NORM_ROWS,),
        in_specs=[
            pl.BlockSpec((NORM_ROWS, d), lambda i: (i + off, 0)),
            pl.BlockSpec((1, d), lambda i: (0, 0)),
        ],
        out_specs=pl.BlockSpec((NORM_ROWS, d), lambda i: (i, 0)),
        out_shape=jax.ShapeDtypeStruct((n_rows, d), out_dtype),
        compiler_params=_params(1),
        name="rmsnorm",
    )(x, g.reshape(1, d))


def _rmsnorm_two_groups_body(xp_ref, xs_ref, g_ref, o_ref, *, prompt_steps):
    i = pl.program_id(0)

    @pl.when(i < prompt_steps)
    def _():
        _rmsnorm_body(xp_ref, g_ref, o_ref)

    @pl.when(i >= prompt_steps)
    def _():
        _rmsnorm_body(xs_ref, g_ref, o_ref)


def _rmsnorm_two_groups(xp, xs, g, out_dtype):
    d = xp.shape[1]
    ps, ss = xp.shape[0] // NORM_ROWS, xs.shape[0] // NORM_ROWS
    return pl.pallas_call(
        functools.partial(_rmsnorm_two_groups_body, prompt_steps=ps),
        grid=(ps + ss,),
        in_specs=[
            pl.BlockSpec((NORM_ROWS, d), lambda i: (jnp.minimum(i, ps - 1), 0)),
            pl.BlockSpec((NORM_ROWS, d), lambda i: (jnp.maximum(i - ps, 0), 0)),
            pl.BlockSpec((1, d), lambda i: (0, 0)),
        ],
        out_specs=pl.BlockSpec((NORM_ROWS, d), lambda i: (i, 0)),
        out_shape=jax.ShapeDtypeStruct((xp.shape[0] + xs.shape[0], d), out_dtype),
        compiler_params=_params(1),
        name="rmsnorm_two_groups",
    )(xp, xs, g.reshape(1, d))


def _ep_plain(acc, extra, o_ref):
    o_ref[...] = acc.astype(o_ref.dtype)


def _ep_residual(acc, extra, o_ref):
    o_ref[...] = extra[0][...] + acc


def _ep_residual_two_groups(acc, extra, o_ref):
    xp_ref, xs_ref = extra
    tm = acc.shape[0]
    n_full = N_PROMPT // tm
    rem = N_PROMPT - n_full * tm
    assert tm - rem == N_SAMPLE and rem % SUBLANES == 0
    i = pl.program_id(1)

    @pl.when(i < n_full)
    def _():
        o_ref[...] = xp_ref[...] + acc

    @pl.when(i >= n_full)
    def _():
        o_ref[0:rem, :] = xp_ref[tm - rem:tm, :] + acc[0:rem]
        o_ref[rem:tm, :] = xs_ref[...] + acc[rem:tm]


def _two_group_specs(tm, width, col_of_j, row_align):
    def p_map(j, i):
        row = jnp.minimum(i * tm, N_PROMPT - tm)
        col = col_of_j(j) * width
        return (pl.multiple_of(row, row_align), col if isinstance(col, int) else pl.multiple_of(col, LANES))
    p_spec = pl.BlockSpec((pl.Element(tm), pl.Element(width)), p_map)
    s_spec = pl.BlockSpec((N_SAMPLE, width), lambda j, i: (0, col_of_j(j)), pipeline_mode=pl.Buffered(1))
    return p_spec, s_spec


def _ep_relu2(acc, extra, o_ref):
    r = jnp.maximum(acc, 0.0)
    o_ref[...] = (r * r).astype(o_ref.dtype)


def _ep_rope(acc, extra, o_ref):
    cos_ref, sin_ref = extra
    j = pl.program_id(0)
    tiles_per_part = W_AH // MM_TN
    is_rope = (j < W_AQKV // MM_TN) & ((j // tiles_per_part) % 3 != 2)

    @pl.when(is_rope)
    def _():
        cos2 = cos_ref[...]
        sin2 = sin_ref[...]
        for s in range(MM_TN // A_HEAD_DIM):
            x = acc[:, s * A_HEAD_DIM:(s + 1) * A_HEAD_DIM]
            o_ref[:, s * A_HEAD_DIM:(s + 1) * A_HEAD_DIM] = (
                x * cos2 + pltpu.roll(x, A_HEAD_DIM // 2, axis=1) * sin2)

    @pl.when(jnp.logical_not(is_rope))
    def _():
        o_ref[...] = acc


def _mm_body(*refs, b_rows_are_outputs, epilogue, n_extra):
    a_ref, b_ref = refs[0], refs[1]
    extra = refs[2:2 + n_extra]
    o_ref = refs[2 + n_extra]
    b_scr = refs[3 + n_extra]

    @pl.when(pl.program_id(1) == 0)
    def _():
        b_scr[...] = b_ref[...].astype(BF16)

    if b_rows_are_outputs:
        acc = lax.dot_general(a_ref[...], b_scr[...], (((1,), (1,)), ((), ())),
                              preferred_element_type=F32)
    else:
        acc = jnp.dot(a_ref[...], b_scr[...], preferred_element_type=F32)
    epilogue(acc, extra, o_ref)


def _matmul(a, b, *, n_cols, col_start=0, b_rows_are_outputs=False, tm=MM_TM, tn=MM_TN,
            out_dtype=F32, epilogue=_ep_plain, extras=(), extra_specs=(), name="matmul"):
    m, k = a.shape
    if b_rows_are_outputs:
        if col_start % tn == 0:
            b_spec = pl.BlockSpec((tn, k), lambda j, i: (j + col_start // tn, 0))
        else:
            assert col_start % SUBLANES == 0 and tn % SUBLANES == 0
            b_spec = pl.BlockSpec((pl.Element(tn), pl.Element(k)),
                                  lambda j, i: (pl.multiple_of(col_start + j * tn, SUBLANES), 0))
        b_block = (tn, k)
    else:
        assert col_start % tn == 0
        b_spec = pl.BlockSpec((k, tn), lambda j, i: (0, j + col_start // tn))
        b_block = (k, tn)
    return pl.pallas_call(
        functools.partial(_mm_body, b_rows_are_outputs=b_rows_are_outputs, epilogue=epilogue,
                          n_extra=len(extras)),
        grid=(n_cols // tn, m // tm),
        in_specs=[pl.BlockSpec((tm, k), lambda j, i: (i, 0)), b_spec, *extra_specs],
        out_specs=pl.BlockSpec((tm, tn), lambda j, i: (i, j)),
        out_shape=jax.ShapeDtypeStruct((m, n_cols), out_dtype),
        scratch_shapes=[pltpu.VMEM(b_block, BF16)],
        compiler_params=_params(2),
        name=name,
    )(a, b, *extras)


def _mm_stream_body(*refs, b_rows_are_outputs, epilogue, n_extra, slice_rows):
    a_ref, b0_ref, bs_ref = refs[0], refs[1], refs[2]
    extra = refs[3:3 + n_extra]
    o_ref = refs[3 + n_extra]
    b_scr = refs[4 + n_extra]
    j, i = pl.program_id(0), pl.program_id(1)
    cur = j % 2

    @pl.when((j == 0) & (i == 0))
    def _():
        b_scr[0] = b0_ref[...].astype(BF16)

    r0 = pl.multiple_of(i * slice_rows, slice_rows)
    b_scr[1 - cur, pl.ds(r0, slice_rows), :] = bs_ref[...].astype(BF16)

    if b_rows_are_outputs:
        acc = lax.dot_general(a_ref[...], b_scr[cur], (((1,), (1,)), ((), ())),
                              preferred_element_type=F32)
    else:
        acc = jnp.dot(a_ref[...], b_scr[cur], preferred_element_type=F32)
    epilogue(acc, extra, o_ref)


def _matmul_stream(a, b, *, n_cols, col_start=0, b_rows_are_outputs=False, tm=MM_TM, tn=MM_TN,
                   out_dtype=F32, epilogue=_ep_plain, extras=(), extra_specs=(), name="matmul"):
    m, k = a.shape
    ni, nj = m // tm, n_cols // tn
    assert col_start % SUBLANES == 0

    def nxt(j):
        return jnp.minimum(j + 1, nj - 1)

    if b_rows_are_outputs:
        tile = (tn, k)
        slice_rows = tn // ni
        b0_spec = pl.BlockSpec((pl.Element(tn), pl.Element(k)), lambda j, i: (col_start, 0),
                               pipeline_mode=pl.Buffered(1))
        bs_spec = pl.BlockSpec(
            (pl.Element(slice_rows), pl.Element(k)),
            lambda j, i: (pl.multiple_of(col_start + nxt(j) * tn + i * slice_rows, SUBLANES), 0))
    else:
        assert col_start % tn == 0
        tile = (k, tn)
        slice_rows = k // ni
        b0_spec = pl.BlockSpec((k, tn), lambda j, i: (0, col_start // tn), pipeline_mode=pl.Buffered(1))
        bs_spec = pl.BlockSpec((slice_rows, tn), lambda j, i: (i, col_start // tn + nxt(j)))
    assert slice_rows * ni == tile[0] and slice_rows % 16 == 0
    return pl.pallas_call(
        functools.partial(_mm_stream_body, b_rows_are_outputs=b_rows_are_outputs, epilogue=epilogue,
                          n_extra=len(extras), slice_rows=slice_rows),
        grid=(nj, ni),
        in_specs=[pl.BlockSpec((tm, k), lambda j, i: (i, 0)), b0_spec, bs_spec, *extra_specs],
        out_specs=pl.BlockSpec((tm, tn), lambda j, i: (i, j)),
        out_shape=jax.ShapeDtypeStruct((m, n_cols), out_dtype),
        scratch_shapes=[pltpu.VMEM((2,) + tile, BF16)],
        compiler_params=_params(2),
        name=name,
    )(a, b, b, *extras)


def _mm_ksplit_body(a_ref, b_ref, res_ref, o_ref, acc_ref):
    kk = pl.program_id(2)

    @pl.when(kk == 0)
    def _():
        acc_ref[...] = jnp.zeros_like(acc_ref)

    acc_ref[...] += jnp.dot(a_ref[...], b_ref[...].astype(BF16), preferred_element_type=F32)

    @pl.when(kk == pl.num_programs(2) - 1)
    def _():
        o_ref[...] = res_ref[...] + acc_ref[...]


def _matmul_ksplit_residual(a, b, res, *, tm=DOWN_TM, tn=MM_TN, tk=DOWN_TK):
    m, k = a.shape
    n = b.shape[1]
    return pl.pallas_call(
        _mm_ksplit_body,
        grid=(n // tn, m // tm, k // tk),
        in_specs=[
            pl.BlockSpec((tm, tk), lambda j, i, kk: (i, kk)),
            pl.BlockSpec((tk, tn), lambda j, i, kk: (kk, j)),
            pl.BlockSpec((tm, tn), lambda j, i, kk: (i, j)),
        ],
        out_specs=pl.BlockSpec((tm, tn), lambda j, i, kk: (i, j)),
        out_shape=jax.ShapeDtypeStruct((m, n), F32),
        scratch_shapes=[pltpu.VMEM((tm, tn), F32)],
        compiler_params=_params(3),
        name="mlp_down",
    )(a, b, res)


def _sigmoid(x):
    return 1.0 / (1.0 + jnp.exp(-x))


def _two_group_rows(p_ref, s_ref, is_prompt_tile):
    tm = p_ref.shape[0]
    rem = tm - s_ref.shape[0]
    straddle = jnp.concatenate([p_ref[tm - rem:tm, :], s_ref[...]], axis=0)
    return jnp.where(is_prompt_tile, p_ref[...], straddle)


def _merge_body(ap_ref, as_ref, ybp_ref, ybs_ref, ycp_ref, ycs_ref,
                wa0_ref, wb0_ref, wc0_ref, was_ref, wbs_ref, wcs_ref,
                ga_ref, gb_ref, gc_ref, ba_ref, bb_ref, bc_ref,
                o_ref, wa_scr, wb_scr, wc_scr):
    j, i = pl.program_id(0), pl.program_id(1)
    cur = j % 2

    @pl.when((j == 0) & (i == 0))
    def _():
        wa_scr[0] = wa0_ref[...].astype(BF16)
        wb_scr[0] = wb0_ref[...].astype(BF16)
        wc_scr[0] = wc0_ref[...].astype(BF16)

    for scr, slice_ref in ((wa_scr, was_ref), (wb_scr, wbs_ref), (wc_scr, wcs_ref)):
        rows = slice_ref.shape[0]
        scr[1 - cur, pl.ds(pl.multiple_of(i * rows, rows), rows), :] = slice_ref[...].astype(BF16)

    is_prompt_tile = i < N_PROMPT // ap_ref.shape[0]
    y_a = jnp.dot(_two_group_rows(ap_ref, as_ref, is_prompt_tile), wa_scr[cur], preferred_element_type=F32)
    y_b = jnp.dot(_two_group_rows(ybp_ref, ybs_ref, is_prompt_tile), wb_scr[cur], preferred_element_type=F32)
    y_c = jnp.dot(_two_group_rows(ycp_ref, ycs_ref, is_prompt_tile), wc_scr[cur], preferred_element_type=F32)
    g_a = _sigmoid(ga_ref[...] + ba_ref[...])
    g_b = _sigmoid(gb_ref[...] + bb_ref[...])
    g_c = _sigmoid(gc_ref[...] + bc_ref[...])
    o_ref[...] = (g_a * y_a + g_b * y_b + g_c * y_c).astype(o_ref.dtype)


def _branch_merge(a_ps, yb_ps, yc_ps, w_a, w_b, w_c, z_rest, b_branch, *, tm=MERGE_TM, tn=MM_TN):
    m = N_ALL
    assert (N_PROMPT // tm + 1) * tm == N_ALL
    nj = D_MODEL // tn
    br0 = COL2_BR // tn

    def gate_spec(part):
        return pl.BlockSpec((tm, tn), lambda j, i: (i, br0 + part * nj + j))

    def bias_spec(part):
        return pl.BlockSpec((1, tn), lambda j, i: (0, part * nj + j))

    ni = m // tm
    widths = (W_MV, W_AH, W_XQ)
    assert all(k % (16 * ni) == 0 for k in widths)

    def first_tile_spec(k):
        return pl.BlockSpec((k, tn), lambda j, i: (0, 0), pipeline_mode=pl.Buffered(1))

    def next_slice_spec(k):
        return pl.BlockSpec((k // ni, tn), lambda j, i: (i, jnp.minimum(j + 1, nj - 1)))

    bf16_rows = 16
    return pl.pallas_call(
        _merge_body,
        grid=(nj, ni),
        in_specs=[
            *_two_group_specs(tm, W_MV, lambda j: 0, bf16_rows),
            *_two_group_specs(tm, W_AH, lambda j: 0, bf16_rows),
            *_two_group_specs(tm, W_XQ, lambda j: 0, bf16_rows),
            *[first_tile_spec(k) for k in widths],
            *[next_slice_spec(k) for k in widths],
            gate_spec(0), gate_spec(1), gate_spec(2),
            bias_spec(0), bias_spec(1), bias_spec(2),
        ],
        out_specs=pl.BlockSpec((tm, tn), lambda j, i: (i, j)),
        out_shape=jax.ShapeDtypeStruct((m, D_MODEL), BF16),
        scratch_shapes=[pltpu.VMEM((2, k, tn), BF16) for k in widths],
        compiler_params=_params(2),
        name="branch_merge",
    )(*a_ps, *yb_ps, *yc_ps, w_a, w_b, w_c, w_a, w_b, w_c,
      z_rest, z_rest, z_rest, b_branch, b_branch, b_branch)


def _log_sigmoid(x):
    return jnp.minimum(x, 0.0) - jnp.log1p(jnp.exp(-jnp.abs(x)))


def _mlstm_chunk(q, k, v, irow, frow, c_state, n_state, m_state, n_valid):
    L = q.shape[0]
    ti = lax.broadcasted_iota(jnp.int32, (L, L), 0)
    si = lax.broadcasted_iota(jnp.int32, (L, L), 1)
    causal = si <= ti
    eye = si == ti
    f_b = jnp.broadcast_to(frow, (L, L))
    i_b = jnp.broadcast_to(irow, (L, L))
    bcol = jnp.sum(jnp.where(causal, f_b, 0.0), axis=1, keepdims=True)
    fcol = jnp.sum(jnp.where(eye, f_b, 0.0), axis=1, keepdims=True)
    icol = jnp.sum(jnp.where(eye, i_b, 0.0), axis=1, keepdims=True)
    brow = jnp.sum(jnp.where(ti <= si, jnp.broadcast_to(fcol, (L, L)), 0.0), axis=0, keepdims=True)

    acol = bcol + m_state
    logw = jnp.where(causal, bcol - brow + irow, NEG_INF)
    mt = jnp.maximum(acol, jnp.max(logw, axis=1, keepdims=True))
    w_inter = jnp.exp(acol - mt)
    w_intra = jnp.exp(logw - mt)

    qb = q.astype(BF16)
    kb = k.astype(BF16)
    nt = (((1,), (1,)), ((), ()))
    s = lax.dot_general(qb, kb, nt, preferred_element_type=F32) * w_intra
    inter = lax.dot_general(qb, c_state.astype(BF16), nt, preferred_element_type=F32)
    num = w_inter * inter + jnp.dot(s.astype(BF16), v.astype(BF16), preferred_element_type=F32)
    nq = w_inter * jnp.sum(q * n_state, axis=1, keepdims=True) + jnp.sum(s, axis=1, keepdims=True)
    h = num / jnp.maximum(jnp.abs(nq), jnp.exp(-mt))

    last = slice(n_valid - 1, n_valid)
    m_end = mt[last, :]
    w_c = jnp.exp(acol[last, :] - m_end)
    w_s = jnp.exp(bcol[last, :] - bcol + icol - m_end)
    if n_valid < L:
        w_s = jnp.where(lax.broadcasted_iota(jnp.int32, (L, 1), 0) < n_valid, w_s, 0.0)
    tn = (((0,), (0,)), ((), ()))
    c_new = w_c * c_state + lax.dot_general((v * w_s).astype(BF16), kb, tn, preferred_element_type=F32)
    n_new = w_c * n_state + jnp.sum(w_s * k, axis=0, keepdims=True)
    return h, c_new, n_new, m_end


def _gate_rows(g_ref_val, bias_ref, head):
    irow = g_ref_val[0:1, :] + bias_ref[0, head]
    frow = _log_sigmoid(g_ref_val[1:2, :] + bias_ref[1, head])
    return irow, frow


def _mlstm_prompt_body(bias_ref, q_ref, k_ref, v_ref, mo_ref, g_ref,
                       a_ref, c_ref, n_ref, m_ref):
    @pl.when(pl.program_id(1) == 0)
    def _():
        c_ref[...] = jnp.zeros_like(c_ref)
        n_ref[...] = jnp.zeros_like(n_ref)
        m_ref[...] = jnp.zeros_like(m_ref)

    for head in range(M_HEADS):
        qk = pl.ds(head * M_DQK, M_DQK)
        vo = pl.ds(head * M_DV, M_DV)
        one = pl.ds(head, 1)
        irow, frow = _gate_rows(g_ref[head], bias_ref, head)
        h, c_new, n_new, m_end = _mlstm_chunk(
            q_ref[:, qk], k_ref[:, qk] * (M_DQK ** -0.5), v_ref[:, vo], irow, frow,
            c_ref[0, head], n_ref[0, one, :], m_ref[0, one, 0:1], CHUNK)
        a_ref[:, vo] = (_sigmoid(mo_ref[:, vo]) * h).astype(a_ref.dtype)
        c_ref[0, head] = c_new
        n_ref[0, one, :] = n_new
        m_ref[0, one, :] = jnp.broadcast_to(m_end, (1, LANES))


def _mlstm_prompt(z_main, gates_rows, gate_bias):
    nc = SEQ // CHUNK
    return pl.pallas_call(
        _mlstm_prompt_body,
        grid=(BATCH, nc),
        in_specs=[
            pl.BlockSpec(memory_space=pltpu.SMEM),
            pl.BlockSpec((CHUNK, W_MQK), lambda b, c: (b * nc + c, 0)),
            pl.BlockSpec((CHUNK, W_MQK), lambda b, c: (b * nc + c, 1)),
            pl.BlockSpec((CHUNK, W_MV), lambda b, c: (b * nc + c, 2 * W_MQK // W_MV)),
            pl.BlockSpec((CHUNK, W_MV), lambda b, c: (b * nc + c, 2 * W_MQK // W_MV + 1)),
            pl.BlockSpec((M_HEADS, 2, CHUNK), lambda b, c: (0, 0, b * nc + c)),
        ],
        out_specs=[
            pl.BlockSpec((CHUNK, W_MV), lambda b, c: (b * nc + c, 0)),
            pl.BlockSpec((1, M_HEADS, M_DV, M_DQK), lambda b, c: (b, 0, 0, 0)),
            pl.BlockSpec((1, M_HEADS, M_DQK), lambda b, c: (b, 0, 0)),
            pl.BlockSpec((1, M_HEADS, LANES), lambda b, c: (b, 0, 0)),
        ],
        out_shape=[
            jax.ShapeDtypeStruct((N_PROMPT, W_MV), BF16),
            jax.ShapeDtypeStruct((BATCH, M_HEADS, M_DV, M_DQK), F32),
            jax.ShapeDtypeStruct((BATCH, M_HEADS, M_DQK), F32),
            jax.ShapeDtypeStruct((BATCH, M_HEADS, LANES), F32),
        ],
        compiler_params=_params(2),
        name="mlstm_prompt",
    )(gate_bias, z_main, z_main, z_main, z_main, gates_rows)


def _mlstm_sample_body(bias_ref, q_ref, k_ref, v_ref, mo_ref, g_ref, c0_ref, n0_ref, m0_ref,
                       a_ref, c_ref, n_ref, m_ref, q_scr, k_scr, v_scr):
    @pl.when(pl.program_id(0) == 0)
    def _():
        q_scr[...] = jnp.zeros_like(q_scr)
        k_scr[...] = jnp.zeros_like(k_scr)
        v_scr[...] = jnp.zeros_like(v_scr)

    q_scr[0:DEC_SEQ, :] = q_ref[0]
    k_scr[0:DEC_SEQ, :] = k_ref[0] * (M_DQK ** -0.5)
    v_scr[0:DEC_SEQ, :] = v_ref[0]
    for head in range(M_HEADS):
        qk = pl.ds(head * M_DQK, M_DQK)
        vo = pl.ds(head * M_DV, M_DV)
        one = pl.ds(head, 1)
        irow, frow = _gate_rows(g_ref[0, head], bias_ref, head)
        h, c_new, n_new, m_end = _mlstm_chunk(
            q_scr[:, qk], k_scr[:, qk], v_scr[:, vo], irow, frow,
            c0_ref[0, 0, head], n0_ref[0, 0, one, :], m0_ref[0, one, :], DEC_SEQ)
        a_ref[0, :, vo] = _sigmoid(mo_ref[0, :, vo]) * h[0:DEC_SEQ, :]
        c_ref[0, head] = c_new
        n_ref[0, one, :] = n_new
        m_ref[0, one, :] = jnp.broadcast_to(m_end, (1, LANES))


def _mlstm_sample(z_s3, gates_rows, gate_bias, c0, n0, m0):
    return pl.pallas_call(
        _mlstm_sample_body,
        grid=(DEC_BATCH,),
        in_specs=[
            pl.BlockSpec(memory_space=pltpu.SMEM),
            pl.BlockSpec((1, DEC_SEQ, W_MQK), lambda b: (b, 0, 0)),
            pl.BlockSpec((1, DEC_SEQ, W_MQK), lambda b: (b, 0, 1)),
            pl.BlockSpec((1, DEC_SEQ, W_MV), lambda b: (b, 0, 2 * W_MQK // W_MV)),
            pl.BlockSpec((1, DEC_SEQ, W_MV), lambda b: (b, 0, 2 * W_MQK // W_MV + 1)),
            pl.BlockSpec((1, M_HEADS, 2, SAMPLE_CHUNK), lambda b: (b, 0, 0, 0)),
            pl.BlockSpec((1, 1, M_HEADS, M_DV, M_DQK), lambda b: (0, b, 0, 0, 0)),
            pl.BlockSpec((1, 1, M_HEADS, M_DQK), lambda b: (0, b, 0, 0)),
            pl.BlockSpec((1, M_HEADS, 1), lambda b: (b, 0, 0)),
        ],
        out_specs=[
            pl.BlockSpec((1, DEC_SEQ, W_MV), lambda b: (b, 0, 0)),
            pl.BlockSpec((1, M_HEADS, M_DV, M_DQK), lambda b: (b, 0, 0, 0)),
            pl.BlockSpec((1, M_HEADS, M_DQK), lambda b: (b, 0, 0)),
            pl.BlockSpec((1, M_HEADS, LANES), lambda b: (b, 0, 0)),
        ],
        out_shape=[
            jax.ShapeDtypeStruct((DEC_BATCH, DEC_SEQ, W_MV), F32),
            jax.ShapeDtypeStruct((DEC_BATCH, M_HEADS, M_DV, M_DQK), F32),
            jax.ShapeDtypeStruct((DEC_BATCH, M_HEADS, M_DQK), F32),
            jax.ShapeDtypeStruct((DEC_BATCH, M_HEADS, LANES), F32),
        ],
        scratch_shapes=[pltpu.VMEM((SAMPLE_CHUNK, W_MQK), F32), pltpu.VMEM((SAMPLE_CHUNK, W_MQK), F32),
                        pltpu.VMEM((SAMPLE_CHUNK, W_MV), F32)],
        compiler_params=_params(1),
        name="mlstm_sample",
    )(gate_bias, z_s3, z_s3, z_s3, z_s3, gates_rows, c0, n0, m0)


def _dil_prompt_body(q_ref, kp_ref, kc_ref, vp_ref, vc_ref, o_ref, l_ref, *, r, hb):
    first_key = jnp.where(pl.program_id(2) > 0, 0, SPAN)
    qi = lax.broadcasted_iota(jnp.int32, (SPAN, 2 * SPAN), 0)
    ki = lax.broadcasted_iota(jnp.int32, (SPAN, 2 * SPAN), 1)
    ok = (ki >= qi) & (ki <= qi + SPAN) & (ki >= first_key)
    bias = jnp.where(ok, 0.0, NEG_INF)
    scale = A_HEAD_DIM ** -0.5
    nt = (((1,), (1,)), ((), ()))
    for c in range(r):
        rows = pl.ds(c, SPAN, stride=r) if r > 1 else pl.ds(0, SPAN)
        for hh in range(hb):
            cols = pl.ds(hh * A_HEAD_DIM, A_HEAD_DIM)
            q = q_ref[rows, cols].astype(BF16)
            kk = jnp.concatenate([kp_ref[rows, cols], kc_ref[rows, cols]], axis=0).astype(BF16)
            vv = jnp.concatenate([vp_ref[rows, cols], vc_ref[rows, cols]], axis=0).astype(BF16)
            s = lax.dot_general(q, kk, nt, preferred_element_type=F32) * scale + bias
            m = jnp.max(s, axis=1, keepdims=True)
            p = jnp.exp(s - m)
            den = jnp.sum(p, axis=1, keepdims=True)
            o = jnp.dot(p.astype(BF16), vv, preferred_element_type=F32) / den
            o_ref[rows, cols] = o
            l_ref[rows, cols] = jnp.broadcast_to(m + jnp.log(den), (SPAN, A_HEAD_DIM))


def _dilated_prompt(z_rest, g, r, hb):
    rows = SPAN * r
    nblk = SEQ // rows
    wcol = hb * A_HEAD_DIM
    per_part = W_AH // wcol

    def spec(part, prev):
        def imap(b, hg, n):
            nn = jnp.maximum(n - 1, 0) if prev else n
            return (b * nblk + nn, (3 * g + part) * per_part + hg)
        return pl.BlockSpec((rows, wcol), imap)

    out_spec = pl.BlockSpec((rows, wcol), lambda b, hg, n: (b * nblk + n, hg))
    return pl.pallas_call(
        functools.partial(_dil_prompt_body, r=r, hb=hb),
        grid=(BATCH, A_HEADS // hb, nblk),
        in_specs=[spec(0, False), spec(1, True), spec(1, False), spec(2, True), spec(2, False)],
        out_specs=[out_spec, out_spec],
        out_shape=[jax.ShapeDtypeStruct((N_PROMPT, W_AH), F32)] * 2,
        compiler_params=_params(3),
        name=f"dilated_prompt_g{g}",
    )(z_rest, z_rest, z_rest, z_rest, z_rest)


def _sample_attn_body(qkv_ref, k0_ref, v0_ref, k1_ref, v1_ref, k2_ref, v2_ref, b0_ref, b1_ref, b2_ref,
                      xq_ref, mk_ref, mv_ref, bx_ref, yb_ref, yc_ref):
    nt = (((1,), (1,)), ((), ()))
    outs, lses = [], []
    groups = ((k0_ref, v0_ref, b0_ref), (k1_ref, v1_ref, b1_ref), (k2_ref, v2_ref, b2_ref))
    for g, (kc_ref, vc_ref, bias_ref) in enumerate(groups):
        n_cached = kc_ref.shape[1] * kc_ref.shape[2]
        q = qkv_ref[0, 3 * g].astype(BF16)
        kk = jnp.concatenate([kc_ref[0].reshape(n_cached, A_HEAD_DIM), qkv_ref[0, 3 * g + 1]],
                             axis=0).astype(BF16)
        vv = jnp.concatenate([vc_ref[0].reshape(n_cached, A_HEAD_DIM), qkv_ref[0, 3 * g + 2]],
                             axis=0).astype(BF16)
        s = lax.dot_general(q, kk, nt, preferred_element_type=F32) * (A_HEAD_DIM ** -0.5) + bias_ref[...]
        m = jnp.max(s, axis=1, keepdims=True)
        p = jnp.exp(s - m)
        den = jnp.sum(p, axis=1, keepdims=True)
        outs.append(jnp.dot(p.astype(BF16), vv, preferred_element_type=F32) / den)
        lses.append(m + jnp.log(den))
    mx = jnp.maximum(jnp.maximum(lses[0], lses[1]), lses[2])
    e = [jnp.exp(l - mx) for l in lses]
    tot = e[0] + e[1] + e[2]
    yb = (e[0] / tot) * outs[0] + (e[1] / tot) * outs[1] + (e[2] / tot) * outs[2]
    yb_ref[0] = yb.astype(yb_ref.dtype)

    s = lax.dot_general(xq_ref[0].astype(BF16), mk_ref[0].astype(BF16), nt,
                        preferred_element_type=F32) * (X_HEAD_DIM ** -0.5) + bx_ref[...]
    m = jnp.max(s, axis=1, keepdims=True)
    ex = jnp.exp(s - m)
    p = ex / jnp.sum(ex, axis=1, keepdims=True)
    yc_ref[0] = jnp.dot(p.astype(BF16), mv_ref[0].astype(BF16),
                        preferred_element_type=F32).astype(yc_ref.dtype)


def _dilated_sample_bias(window, r, lb, rc):
    span = window // r
    n_c = (lb // r) * rc * A_HEADS
    rows = np.arange(DEC_SEQ * A_HEADS)
    s_q, h_q = rows // A_HEADS, rows % A_HEADS
    col = np.arange(n_c)
    m_k = col // (rc * A_HEADS)
    c_k = (col % (rc * A_HEADS)) // A_HEADS
    h_k = col % A_HEADS
    delta = (lb + s_q)[:, None] - (m_k * r + c_k)[None, :]
    ok_c = (h_q[:, None] == h_k[None, :]) & (delta % r == 0) & (delta // r <= span) & (delta >= 0)
    coln = np.arange(DEC_SEQ * A_HEADS)
    s_n, h_n = coln // A_HEADS, coln % A_HEADS
    dn = s_q[:, None] - s_n[None, :]
    ok_n = (h_q[:, None] == h_n[None, :]) & (dn >= 0) & (dn % r == 0) & (dn // r <= span)
    ok = np.concatenate([ok_c, ok_n], axis=1)
    return np.where(ok, 0.0, -np.inf).astype(np.float32)


def _sample_attention(qkv, caches, xq, mem_k, mem_v):
    rq = DEC_SEQ * A_HEADS
    rx = DEC_SEQ * X_HEADS
    nk = MEM_LEN * X_HEADS
    cache_args, cache_specs, biases = [], [], []
    for (window, r), (cache_k, cache_v) in zip(DIL_GROUPS, caches):
        lb = cache_k.shape[1]
        assert lb % r == 0 and window % r == 0
        rc = min(r, DEC_SEQ)
        nm = lb // r
        spec = pl.BlockSpec((1, nm, rc * A_HEADS, A_HEAD_DIM), lambda b: (b, 0, 0, 0))
        for c in (cache_k, cache_v):
            cache_args.append(c.reshape(DEC_BATCH, nm, r * A_HEADS, A_HEAD_DIM))
            cache_specs.append(spec)
        biases.append(jnp.asarray(_dilated_sample_bias(window, r, lb, rc)))
    ok = (np.arange(rx) % X_HEADS)[:, None] == (np.arange(nk) % X_HEADS)[None, :]
    bias_x = jnp.asarray(np.where(ok, 0.0, -np.inf).astype(np.float32))

    def const_spec(a):
        return pl.BlockSpec(a.shape, lambda b: (0, 0))

    return pl.pallas_call(
        _sample_attn_body,
        grid=(DEC_BATCH,),
        in_specs=[
            pl.BlockSpec((1, 3 * N_GROUPS, rq, A_HEAD_DIM), lambda b: (b, 0, 0, 0)),
            *cache_specs,
            *[const_spec(a) for a in biases],
            pl.BlockSpec((1, rx, X_HEAD_DIM), lambda b: (b, 0, 0)),
            pl.BlockSpec((1, nk, X_HEAD_DIM), lambda b: (b, 0, 0)),
            pl.BlockSpec((1, nk, X_HEAD_DIM), lambda b: (b, 0, 0)),
            const_spec(bias_x),
        ],
        out_specs=[pl.BlockSpec((1, rq, A_HEAD_DIM), lambda b: (b, 0, 0)),
                   pl.BlockSpec((1, rx, X_HEAD_DIM), lambda b: (b, 0, 0))],
        out_shape=[jax.ShapeDtypeStruct((DEC_BATCH, rq, A_HEAD_DIM), BF16),
                   jax.ShapeDtypeStruct((DEC_BATCH, rx, X_HEAD_DIM), BF16)],
        compiler_params=_params(1),
        name="sample_attention",
    )(qkv, *cache_args, *biases, xq, mem_k, mem_v, bias_x)


def _cross_prompt_body(q_ref, k_ref, v_ref, o_ref):
    scale = X_HEAD_DIM ** -0.5
    nt = (((1,), (1,)), ((), ()))
    for h in range(X_HEADS):
        cols = pl.ds(h * X_HEAD_DIM, X_HEAD_DIM)
        s = lax.dot_general(q_ref[:, cols].astype(BF16), k_ref[:, cols].astype(BF16), nt,
                            preferred_element_type=F32) * scale
        m = jnp.max(s, axis=1, keepdims=True)
        e = jnp.exp(s - m)
        p = e / jnp.sum(e, axis=1, keepdims=True)
        o_ref[:, cols] = jnp.dot(p.astype(BF16), v_ref[:, cols].astype(BF16),
                                 preferred_element_type=F32).astype(o_ref.dtype)


def _cross_prompt(z_rest, mem_kv, *, tq=512):
    nq = SEQ // tq
    return pl.pallas_call(
        _cross_prompt_body,
        grid=(BATCH, nq),
        in_specs=[
            pl.BlockSpec((tq, W_XQ), lambda b, i: (b * nq + i, COL2_XQ // W_XQ)),
            pl.BlockSpec((MEM_LEN, W_XQ), lambda b, i: (b, 0)),
            pl.BlockSpec((MEM_LEN, W_XQ), lambda b, i: (b, 1)),
        ],
        out_specs=pl.BlockSpec((tq, W_XQ), lambda b, i: (b * nq + i, 0)),
        out_shape=jax.ShapeDtypeStruct((N_PROMPT, W_XQ), BF16),
        compiler_params=_params(2),
        name="cross_prompt",
    )(z_rest, mem_kv, mem_kv)


def _combine_body(o0, o1, o2, l0, l1, l2, y_ref):
    a0, a1, a2 = l0[...], l1[...], l2[...]
    mx = jnp.maximum(jnp.maximum(a0, a1), a2)
    e0, e1, e2 = jnp.exp(a0 - mx), jnp.exp(a1 - mx), jnp.exp(a2 - mx)
    tot = e0 + e1 + e2
    y = (e0 / tot) * o0[...] + (e1 / tot) * o1[...] + (e2 / tot) * o2[...]
    y_ref[...] = y.astype(y_ref.dtype)


def _combine_groups(outs, lses, *, rows_per_step):
    n, w = outs[0].shape
    spec = pl.BlockSpec((rows_per_step, w), lambda i: (i, 0))
    return pl.pallas_call(
        _combine_body,
        grid=(n // rows_per_step,),
        in_specs=[spec] * 6,
        out_specs=spec,
        out_shape=jax.ShapeDtypeStruct((n, w), BF16),
        compiler_params=_params(1),
        name="combine_groups",
    )(*outs, *lses)


def _rope_tables():
    pos = jnp.concatenate([
        jnp.tile(jnp.arange(SEQ, dtype=jnp.int32), BATCH),
        jnp.tile(PAST_LEN + jnp.arange(DEC_SEQ, dtype=jnp.int32), DEC_BATCH)])
    inv = ROPE_THETA ** (-jnp.arange(0, A_HEAD_DIM, 2, dtype=F32) / A_HEAD_DIM)
    ang = pos.astype(F32)[:, None] * inv[None, :]
    cos, sin = jnp.cos(ang), jnp.sin(ang)
    return jnp.concatenate([cos, cos], axis=1), jnp.concatenate([-sin, sin], axis=1)


def _layer(x_p, x_s, mem_prompt, state_c, state_n, state_m, caches, cache_mem_k, cache_mem_v,
           g_mix, w_in, b_igate, b_fgate, b_branch, g_mem, w_mem_kv,
           w_br_a, w_br_b, w_br_c, w_out, g_mlp, w_up, w_down):
    h_all = _rmsnorm_two_groups(x_p, x_s, g_mix, BF16)

    w_in_t = w_in.T
    z_main = _matmul_stream(h_all, w_in_t, n_cols=COL_GATES, b_rows_are_outputs=True, name="proj_main")
    z_gate = _matmul(h_all, w_in_t, n_cols=2 * M_HEADS, col_start=COL_GATES, tn=2 * M_HEADS,
                     b_rows_are_outputs=True, name="proj_gates")
    cos2, sin2 = _rope_tables()
    table_spec = pl.BlockSpec((MM_TM, A_HEAD_DIM), lambda j, i: (i, 0))
    z_rest = _matmul_stream(h_all, w_in_t, n_cols=W_REST, col_start=COL_REST, b_rows_are_outputs=True,
                            epilogue=_ep_rope, extras=(cos2, sin2), extra_specs=(table_spec, table_spec),
                            name="proj_rest")

    gate_bias = jnp.stack([b_igate, b_fgate]).astype(F32)

    gp = z_gate[:N_PROMPT].reshape(N_PROMPT, 2, M_HEADS).transpose(2, 1, 0)
    a_p, c_p, n_p, m_p = _mlstm_prompt(z_main, gp, gate_bias)
    gs = z_gate[N_PROMPT:].reshape(DEC_BATCH, DEC_SEQ, 2, M_HEADS).transpose(0, 3, 2, 1)
    gs = jnp.pad(gs, ((0, 0), (0, 0), (0, 0), (0, SAMPLE_CHUNK - DEC_SEQ)))
    z_main_s = z_main[N_PROMPT:].reshape(DEC_BATCH, DEC_SEQ, COL_GATES)
    a_s, c_s, n_s, m_s = _mlstm_sample(
        z_main_s, gs, gate_bias, state_c, state_n, state_m.reshape(DEC_BATCH, M_HEADS, 1))
    a_ps = (a_p, a_s.reshape(N_SAMPLE, W_MV).astype(BF16))

    z_rest_s = z_rest[N_PROMPT:]
    heads_per_step = (8, 1, 1)
    outs_p, lses_p, rows_p, rows_s = [], [], [], []
    rq = DEC_SEQ * A_HEADS
    qkv_s = z_rest_s[:, :W_AQKV].reshape(DEC_BATCH, DEC_SEQ, 3 * N_GROUPS, A_HEADS, A_HEAD_DIM)
    qkv_s = qkv_s.transpose(0, 2, 1, 3, 4).reshape(DEC_BATCH, 3 * N_GROUPS, rq, A_HEAD_DIM)
    for g, (window, r) in enumerate(DIL_GROUPS):
        o, l = _dilated_prompt(z_rest, g, r, heads_per_step[g])
        outs_p.append(o)
        lses_p.append(l)
        c0 = 3 * g * W_AH
        keep = min(window, SEQ)
        for part in (1, 2):
            cs = c0 + part * W_AH
            kept = [z_rest[(b + 1) * SEQ - keep:(b + 1) * SEQ, cs:cs + W_AH] for b in range(BATCH)]
            rows_p.append(jnp.stack(kept).reshape(BATCH, keep, A_HEADS, A_HEAD_DIM))
            rows_s.append(qkv_s[:, 3 * g + part].reshape(DEC_BATCH, DEC_SEQ, A_HEADS, A_HEAD_DIM))
    yb_p = _combine_groups(outs_p, lses_p, rows_per_step=256)

    mem_h = _rmsnorm(mem_prompt, g_mem, BF16)
    mem_kv = _matmul(mem_h, w_mem_kv, n_cols=2 * W_XQ, tm=BATCH * MEM_LEN, name="mem_kv")
    yc_p = _cross_prompt(z_rest, mem_kv)

    xq_s = z_rest_s[:, COL2_XQ:COL2_XQ + W_XQ].reshape(DEC_BATCH, DEC_SEQ * X_HEADS, X_HEAD_DIM)
    yb_s, yc_s = _sample_attention(
        qkv_s, caches, xq_s,
        cache_mem_k.reshape(DEC_BATCH, MEM_LEN * X_HEADS, X_HEAD_DIM),
        cache_mem_v.reshape(DEC_BATCH, MEM_LEN * X_HEADS, X_HEAD_DIM))
    yb_ps = (yb_p, yb_s.reshape(N_SAMPLE, W_AH))
    yc_ps = (yc_p, yc_s.reshape(N_SAMPLE, W_XQ))

    merged = _branch_merge(a_ps, yb_ps, yc_ps, w_br_a, w_br_b, w_br_c, z_rest, b_branch.reshape(1, 3 * D_MODEL))
    x1 = _matmul_stream(merged, w_out, n_cols=D_MODEL, epilogue=_ep_residual_two_groups, extras=(x_p, x_s),
                        extra_specs=_two_group_specs(MM_TM, MM_TN, lambda j: j, SUBLANES), name="out_proj")

    h2 = _rmsnorm(x1, g_mlp, BF16)
    u = _matmul_stream(h2, w_up, n_cols=D_FF, out_dtype=BF16, epilogue=_ep_relu2, name="mlp_up")
    x2 = _matmul_ksplit_residual(u, w_down, x1)

    mem_k = mem_kv[:, :W_XQ].reshape(BATCH, MEM_LEN, X_HEADS, X_HEAD_DIM)
    mem_v = mem_kv[:, W_XQ:].reshape(BATCH, MEM_LEN, X_HEADS, X_HEAD_DIM)
    prompt_state = (c_p, n_p, m_p[:, :, 0])
    sample_state = (c_s, n_s, m_s[:, :, 0])
    return x2, prompt_state, sample_state, rows_p, rows_s, mem_k, mem_v


def kernel(x_prompt, x_sample, state_mlstm_C, state_mlstm_n, state_mlstm_m,
           cache_win_k_g0, cache_win_v_g0, cache_win_k_g1, cache_win_v_g1,
           cache_win_k_g2, cache_win_v_g2, cache_mem_k, cache_mem_v, mem_prompt,
           g_mix, w_in, b_igate, b_fgate, b_branch, g_mem, w_mem_kv,
           w_br_a, w_br_b, w_br_c, w_out, g_mlp, w_up, w_down, g_final):
    depth = g_mix.shape[0]
    assert depth == 1, "single-layer stack"
    caches = ((cache_win_k_g0[0], cache_win_v_g0[0]),
              (cache_win_k_g1[0], cache_win_v_g1[0]),
              (cache_win_k_g2[0], cache_win_v_g2[0]))
    x2, p_state, s_state, rows_p, rows_s, mem_k, mem_v = _layer(
        x_prompt.reshape(N_PROMPT, D_MODEL), x_sample.reshape(N_SAMPLE, D_MODEL),
        mem_prompt.reshape(BATCH * MEM_LEN, D_MODEL),
        state_mlstm_C, state_mlstm_n, state_mlstm_m[0], caches, cache_mem_k[0], cache_mem_v[0],
        g_mix[0], w_in[0], b_igate[0], b_fgate[0], b_branch[0], g_mem[0], w_mem_kv[0],
        w_br_a[0], w_br_b[0], w_br_c[0], w_out[0], g_mlp[0], w_up[0], w_down[0])
    y_prompt = _rmsnorm(x2, g_final, F32, row_start=0, n_rows=N_PROMPT).reshape(BATCH, SEQ, D_MODEL)
    y_sample = _rmsnorm(x2, g_final, F32, row_start=N_PROMPT, n_rows=N_SAMPLE).reshape(DEC_BATCH, DEC_SEQ, D_MODEL)
    lead = lambda a: a[None]
    return (y_prompt, y_sample,
            lead(p_state[0]), lead(p_state[1]), lead(p_state[2]),
            *[lead(r) for r in rows_p],
            lead(mem_k), lead(mem_v),
            lead(s_state[0]), lead(s_state[1]), lead(s_state[2]),
            *[lead(r) for r in rows_s])
```

```python
import functools
import math

import numpy as np
import jax
import jax.numpy as jnp
from jax import lax
from jax.experimental import pallas as pl
from jax.experimental.pallas import tpu as pltpu

F32 = jnp.float32
BF16 = jnp.bfloat16
NEG_INF = float("-inf")

D_MODEL = 4096
BATCH = 2
SEQ = 4096
DEC_BATCH = 128
DEC_SEQ = 4
PAST_LEN = 2048
MEM_LEN = 256
M_HEADS = 8
M_DQK = D_MODEL // (2 * M_HEADS)
M_DV = D_MODEL // M_HEADS
A_HEAD_DIM = 128
A_HEADS = D_MODEL // 512
DIL_GROUPS = ((128, 1), (512, 4), (2048, 16))
N_GROUPS = 3
ROPE_THETA = 10000.0
X_HEADS = 4
X_HEAD_DIM = 128
D_FF = 4 * D_MODEL
EPS = 1e-6

W_MQK = M_HEADS * M_DQK
W_MV = M_HEADS * M_DV
W_AH = A_HEADS * A_HEAD_DIM
W_XQ = X_HEADS * X_HEAD_DIM
N_PROMPT = BATCH * SEQ
N_SAMPLE = DEC_BATCH * DEC_SEQ
N_ALL = N_PROMPT + N_SAMPLE

COL_GATES = 2 * W_MQK + 2 * W_MV
COL_REST = COL_GATES + 2 * M_HEADS
W_AQKV = 3 * N_GROUPS * W_AH

LANES = 128
SUBLANES = 8
VMEM_LIMIT_BYTES = 60 * 1024 * 1024

MM_TM = 1088
MM_TN = 512
MM_TN_WIDE = 1024
MERGE_TM = 544
DOWN_TM = 2176
DOWN_TK = 2048
NORM_ROWS = 256
CHUNK = 128
SAMPLE_CHUNK = 16
SPAN = 128


def _params(n_axes):
    return pltpu.CompilerParams(
        dimension_semantics=("arbitrary",) * n_axes,
        vmem_limit_bytes=VMEM_LIMIT_BYTES,
    )


def _rmsnorm_body(x_ref, g_ref, o_ref):
    x = x_ref[...]
    ms = jnp.mean(x * x, axis=-1, keepdims=True)
    o_ref[...] = ((x * lax.rsqrt(ms + EPS)) * g_ref[...]).astype(o_ref.dtype)


def _rmsnorm(x, g, out_dtype, *, row_start=0, n_rows=None):
    n_rows = x.shape[0] if n_rows is None else n_rows
    d = x.shape[1]
    off = row_start // NORM_ROWS
    return pl.pallas_call(
        _rmsnorm_body,
        grid=(n_rows // NORM_ROWS,),
        in_specs=[
            pl.BlockSpec((NORM_ROWS, d), lambda i: (i + off, 0)),
            pl.BlockSpec((1, d), lambda i: (0, 0)),
        ],
        out_specs=pl.BlockSpec((NORM_ROWS, d), lambda i: (i, 0)),
        out_shape=jax.ShapeDtypeStruct((n_rows, d), out_dtype),
        compiler_params=_params(1),
        name="rmsnorm",
    )(x, g.reshape(1, d))


def _rmsnorm_two_groups_body(xp_ref, xs_ref, g_ref, o_ref, *, prompt_steps):
    i = pl.program_id(0)

    @pl.when(i < prompt_steps)
    def _():
        _rmsnorm_body(xp_ref, g_ref, o_ref)

    @pl.when(i >= prompt_steps)
    def _():
        _rmsnorm_body(xs_ref, g_ref, o_ref)


def _rmsnorm_two_groups(xp, xs, g, out_dtype):
    d = xp.shape[1]
    ps, ss = xp.shape[0] // NORM_ROWS, xs.shape[0] // NORM_ROWS
    return pl.pallas_call(
        functools.partial(_rmsnorm_two_groups_body, prompt_steps=ps),
        grid=(ps + ss,),
        in_specs=[
            pl.BlockSpec((NORM_ROWS, d), lambda i: (jnp.minimum(i, ps - 1), 0)),
            pl.BlockSpec((NORM_ROWS, d), lambda i: (jnp.maximum(i - ps, 0), 0)),
            pl.BlockSpec((1, d), lambda i: (0, 0)),
        ],
        out_specs=pl.BlockSpec((NORM_ROWS, d), lambda i: (i, 0)),
        out_shape=jax.ShapeDtypeStruct((xp.shape[0] + xs.shape[0], d), out_dtype),
        compiler_params=_params(1),
        name="rmsnorm_two_groups",
    )(xp, xs, g.reshape(1, d))


def _ep_plain(acc, extra, o_ref, j):
    o_ref[...] = acc.astype(o_ref.dtype)


def _ep_residual_two_groups(acc, extra, o_ref, j):
    xp_ref, xs_ref = extra
    tm = acc.shape[0]
    n_full = N_PROMPT // tm
    rem = N_PROMPT - n_full * tm
    assert tm - rem == N_SAMPLE and rem % SUBLANES == 0
    i = pl.program_id(1)

    @pl.when(i < n_full)
    def _():
        o_ref[...] = xp_ref[...] + acc

    @pl.when(i >= n_full)
    def _():
        o_ref[0:rem, :] = xp_ref[tm - rem:tm, :] + acc[0:rem]
        o_ref[rem:tm, :] = xs_ref[...] + acc[rem:tm]


def _two_group_specs(tm, width, col_of_j, row_align, sample_block_is_constant):
    def p_map(j, i):
        row = jnp.minimum(i * tm, N_PROMPT - tm)
        col = col_of_j(j) * width
        return (pl.multiple_of(row, row_align), col if isinstance(col, int) else pl.multiple_of(col, LANES))
    p_spec = pl.BlockSpec((pl.Element(tm), pl.Element(width)), p_map)
    mode = pl.Buffered(1) if sample_block_is_constant else None
    s_spec = pl.BlockSpec((N_SAMPLE, width), lambda j, i: (0, col_of_j(j)), pipeline_mode=mode)
    return p_spec, s_spec


def _ep_relu2(acc, extra, o_ref, j):
    r = jnp.maximum(acc, 0.0)
    o_ref[...] = (r * r).astype(o_ref.dtype)


def _ep_rope(acc, extra, o_ref, j):
    cos_ref, sin_ref = extra
    tn = acc.shape[1]
    assert W_AH % tn == 0
    is_rope = (j // (W_AH // tn)) % 3 != 2

    @pl.when(is_rope)
    def _():
        cos2 = cos_ref[...]
        sin2 = sin_ref[...]
        for s in range(tn // A_HEAD_DIM):
            x = acc[:, s * A_HEAD_DIM:(s + 1) * A_HEAD_DIM]
            o_ref[:, s * A_HEAD_DIM:(s + 1) * A_HEAD_DIM] = (
                x * cos2 + pltpu.roll(x, A_HEAD_DIM // 2, axis=1) * sin2)

    @pl.when(jnp.logical_not(is_rope))
    def _():
        o_ref[...] = acc


def _mm_body(*refs, b_rows_are_outputs, epilogue, n_extra):
    a_ref, b_ref = refs[0], refs[1]
    extra = refs[2:2 + n_extra]
    o_ref = refs[2 + n_extra]
    b_scr = refs[3 + n_extra]

    @pl.when(pl.program_id(1) == 0)
    def _():
        b_scr[...] = b_ref[...].astype(BF16)

    if b_rows_are_outputs:
        acc = lax.dot_general(a_ref[...], b_scr[...], (((1,), (1,)), ((), ())),
                              preferred_element_type=F32)
    else:
        acc = jnp.dot(a_ref[...], b_scr[...], preferred_element_type=F32)
    epilogue(acc, extra, o_ref, pl.program_id(0))


def _matmul(a, b, *, n_cols, col_start=0, b_rows_are_outputs=False, tm=MM_TM, tn=MM_TN,
            out_dtype=F32, epilogue=_ep_plain, extras=(), extra_specs=(), name="matmul"):
    m, k = a.shape
    if b_rows_are_outputs:
        if col_start % tn == 0:
            b_spec = pl.BlockSpec((tn, k), lambda j, i: (j + col_start // tn, 0))
        else:
            assert col_start % SUBLANES == 0 and tn % SUBLANES == 0
            b_spec = pl.BlockSpec((pl.Element(tn), pl.Element(k)),
                                  lambda j, i: (pl.multiple_of(col_start + j * tn, SUBLANES), 0))
        b_block = (tn, k)
    else:
        assert col_start % tn == 0
        b_spec = pl.BlockSpec((k, tn), lambda j, i: (0, j + col_start // tn))
        b_block = (k, tn)
    return pl.pallas_call(
        functools.partial(_mm_body, b_rows_are_outputs=b_rows_are_outputs, epilogue=epilogue,
                          n_extra=len(extras)),
        grid=(n_cols // tn, m // tm),
        in_specs=[pl.BlockSpec((tm, k), lambda j, i: (i, 0)), b_spec, *extra_specs],
        out_specs=pl.BlockSpec((tm, tn), lambda j, i: (i, j)),
        out_shape=jax.ShapeDtypeStruct((m, n_cols), out_dtype),
        scratch_shapes=[pltpu.VMEM(b_block, BF16)],
        compiler_params=_params(2),
        name=name,
    )(a, b, *extras)


def _mm_stream_body(*refs, b_rows_are_outputs, epilogue, n_extra, slice_rows):
    a_ref, bs_ref = refs[0], refs[1]
    extra = refs[2:2 + n_extra]
    o_ref = refs[2 + n_extra]
    b_scr = refs[3 + n_extra]
    p, i = pl.program_id(0), pl.program_id(1)

    r0 = pl.multiple_of(i * slice_rows, slice_rows)
    b_scr[p % 2, pl.ds(r0, slice_rows), :] = bs_ref[...].astype(BF16)

    @pl.when(p > 0)
    def _():
        b = b_scr[(p - 1) % 2]
        if b_rows_are_outputs:
            acc = lax.dot_general(a_ref[...], b, (((1,), (1,)), ((), ())), preferred_element_type=F32)
        else:
            acc = jnp.dot(a_ref[...], b, preferred_element_type=F32)
        epilogue(acc, extra, o_ref, p - 1)


def _matmul_stream(a, b, *, n_cols, col_start=0, b_rows_are_outputs=False, tm=MM_TM, tn=MM_TN_WIDE,
                   out_dtype=F32, epilogue=_ep_plain, extras=(), extra_specs=(), name="matmul"):
    m, k = a.shape
    ni, nj = m // tm, n_cols // tn
    assert col_start % SUBLANES == 0 and nj * tn == n_cols

    def staged(p):
        return jnp.minimum(p, nj - 1)

    def shifted(index_map):
        return lambda p, i: index_map(jnp.maximum(p - 1, 0), jnp.where(p == 0, 0, i))

    if b_rows_are_outputs:
        tile = (tn, k)
        slice_rows = tn // ni
        bs_spec = pl.BlockSpec(
            (pl.Element(slice_rows), pl.Element(k)),
            lambda p, i: (pl.multiple_of(col_start + staged(p) * tn + i * slice_rows, SUBLANES), 0))
    else:
        assert col_start % tn == 0
        tile = (k, tn)
        slice_rows = k // ni
        bs_spec = pl.BlockSpec((slice_rows, tn), lambda p, i: (i, col_start // tn + staged(p)))
    assert slice_rows * ni == tile[0] and slice_rows % 16 == 0
    extra_specs = [pl.BlockSpec(s.block_shape, shifted(s.index_map), pipeline_mode=s.pipeline_mode)
                   for s in extra_specs]
    return pl.pallas_call(
        functools.partial(_mm_stream_body, b_rows_are_outputs=b_rows_are_outputs, epilogue=epilogue,
                          n_extra=len(extras), slice_rows=slice_rows),
        grid=(nj + 1, ni),
        in_specs=[pl.BlockSpec((tm, k), shifted(lambda j, i: (i, 0))), bs_spec, *extra_specs],
        out_specs=pl.BlockSpec((tm, tn), shifted(lambda j, i: (i, j))),
        out_shape=jax.ShapeDtypeStruct((m, n_cols), out_dtype),
        scratch_shapes=[pltpu.VMEM((2,) + tile, BF16)],
        compiler_params=_params(2),
        name=name,
    )(a, b, *extras)


def _mm_ksplit_body(a_ref, b_ref, res_ref, o_ref, acc_ref):
    kk = pl.program_id(2)

    @pl.when(kk == 0)
    def _():
        acc_ref[...] = jnp.zeros_like(acc_ref)

    acc_ref[...] += jnp.dot(a_ref[...], b_ref[...].astype(BF16), preferred_element_type=F32)

    @pl.when(kk == pl.num_programs(2) - 1)
    def _():
        o_ref[...] = res_ref[...] + acc_ref[...]


def _matmul_ksplit_residual(a, b, res, *, tm=DOWN_TM, tn=MM_TN, tk=DOWN_TK):
    m, k = a.shape
    n = b.shape[1]
    return pl.pallas_call(
        _mm_ksplit_body,
        grid=(n // tn, m // tm, k // tk),
        in_specs=[
            pl.BlockSpec((tm, tk), lambda j, i, kk: (i, kk)),
            pl.BlockSpec((tk, tn), lambda j, i, kk: (kk, j)),
            pl.BlockSpec((tm, tn), lambda j, i, kk: (i, j)),
        ],
        out_specs=pl.BlockSpec((tm, tn), lambda j, i, kk: (i, j)),
        out_shape=jax.ShapeDtypeStruct((m, n), F32),
        scratch_shapes=[pltpu.VMEM((tm, tn), F32)],
        compiler_params=_params(3),
        name="mlp_down",
    )(a, b, res)


def _sigmoid(x):
    return 1.0 / (1.0 + jnp.exp(-x))


def _two_group_rows(p_ref, s_ref, is_prompt_tile):
    tm = p_ref.shape[0]
    rem = tm - s_ref.shape[0]
    straddle = jnp.concatenate([p_ref[tm - rem:tm, :], s_ref[...]], axis=0)
    return jnp.where(is_prompt_tile, p_ref[...], straddle)


def _merge_body(ap_ref, as_ref, ybp_ref, ybs_ref, ycp_ref, ycs_ref,
                wa0_ref, wb0_ref, wc0_ref, was_ref, wbs_ref, wcs_ref,
                ga_ref, gb_ref, gc_ref, ba_ref, bb_ref, bc_ref,
                o_ref, wa_scr, wb_scr, wc_scr):
    j, i = pl.program_id(0), pl.program_id(1)
    cur = j % 2

    @pl.when((j == 0) & (i == 0))
    def _():
        wa_scr[0] = wa0_ref[...].astype(BF16)
        wb_scr[0] = wb0_ref[...].astype(BF16)
        wc_scr[0] = wc0_ref[...].astype(BF16)

    for scr, slice_ref in ((wa_scr, was_ref), (wb_scr, wbs_ref), (wc_scr, wcs_ref)):
        rows = slice_ref.shape[0]
        scr[1 - cur, pl.ds(pl.multiple_of(i * rows, rows), rows), :] = slice_ref[...].astype(BF16)

    is_prompt_tile = i < N_PROMPT // ap_ref.shape[0]
    y_a = jnp.dot(_two_group_rows(ap_ref, as_ref, is_prompt_tile), wa_scr[cur], preferred_element_type=F32)
    y_b = jnp.dot(_two_group_rows(ybp_ref, ybs_ref, is_prompt_tile), wb_scr[cur], preferred_element_type=F32)
    y_c = jnp.dot(_two_group_rows(ycp_ref, ycs_ref, is_prompt_tile), wc_scr[cur], preferred_element_type=F32)
    g_a = _sigmoid(ga_ref[...] + ba_ref[...])
    g_b = _sigmoid(gb_ref[...] + bb_ref[...])
    g_c = _sigmoid(gc_ref[...] + bc_ref[...])
    o_ref[...] = (g_a * y_a + g_b * y_b + g_c * y_c).astype(o_ref.dtype)


def _branch_merge(a_ps, yb_ps, yc_ps, w_a, w_b, w_c, z_br, b_branch, *, tm=MERGE_TM, tn=MM_TN):
    m = N_ALL
    assert (N_PROMPT // tm + 1) * tm == N_ALL
    nj = D_MODEL // tn

    def gate_spec(part):
        return pl.BlockSpec((tm, tn), lambda j, i: (i, part * nj + j))

    def bias_spec(part):
        return pl.BlockSpec((1, tn), lambda j, i: (0, part * nj + j))

    ni = m // tm
    widths = (W_MV, W_AH, W_XQ)
    assert all(k % (16 * ni) == 0 for k in widths)

    def first_tile_spec(k):
        return pl.BlockSpec((k, tn), lambda j, i: (0, 0), pipeline_mode=pl.Buffered(1))

    def next_slice_spec(k):
        return pl.BlockSpec((k // ni, tn), lambda j, i: (i, jnp.minimum(j + 1, nj - 1)))

    bf16_rows = 16
    return pl.pallas_call(
        _merge_body,
        grid=(nj, ni),
        in_specs=[
            *_two_group_specs(tm, W_MV, lambda j: 0, bf16_rows, True),
            *_two_group_specs(tm, W_AH, lambda j: 0, bf16_rows, True),
            *_two_group_specs(tm, W_XQ, lambda j: 0, bf16_rows, True),
            *[first_tile_spec(k) for k in widths],
            *[next_slice_spec(k) for k in widths],
            gate_spec(0), gate_spec(1), gate_spec(2),
            bias_spec(0), bias_spec(1), bias_spec(2),
        ],
        out_specs=pl.BlockSpec((tm, tn), lambda j, i: (i, j)),
        out_shape=jax.ShapeDtypeStruct((m, D_MODEL), BF16),
        scratch_shapes=[pltpu.VMEM((2, k, tn), BF16) for k in widths],
        compiler_params=_params(2),
        name="branch_merge",
    )(*a_ps, *yb_ps, *yc_ps, w_a, w_b, w_c, w_a, w_b, w_c,
      z_br, z_br, z_br, b_branch, b_branch, b_branch)


def _log_sigmoid(x):
    return jnp.minimum(x, 0.0) - jnp.log1p(jnp.exp(-jnp.abs(x)))


def _mlstm_chunk(q, k, v, irow, frow, c_state, n_state, m_state, n_valid):
    L = q.shape[0]
    ti = lax.broadcasted_iota(jnp.int32, (L, L), 0)
    si = lax.broadcasted_iota(jnp.int32, (L, L), 1)
    causal = si <= ti
    eye = si == ti
    f_b = jnp.broadcast_to(frow, (L, L))
    i_b = jnp.broadcast_to(irow, (L, L))
    bcol = jnp.sum(jnp.where(causal, f_b, 0.0), axis=1, keepdims=True)
    fcol = jnp.sum(jnp.where(eye, f_b, 0.0), axis=1, keepdims=True)
    icol = jnp.sum(jnp.where(eye, i_b, 0.0), axis=1, keepdims=True)
    brow = jnp.sum(jnp.where(ti <= si, jnp.broadcast_to(fcol, (L, L)), 0.0), axis=0, keepdims=True)

    acol = bcol + m_state
    logw = jnp.where(causal, bcol - brow + irow, NEG_INF)
    mt = jnp.maximum(acol, jnp.max(logw, axis=1, keepdims=True))
    w_inter = jnp.exp(acol - mt)
    w_intra = jnp.exp(logw - mt)

    qb = q.astype(BF16)
    kb = k.astype(BF16)
    nt = (((1,), (1,)), ((), ()))
    s = lax.dot_general(qb, kb, nt, preferred_element_type=F32) * w_intra
    inter = lax.dot_general(qb, c_state.astype(BF16), nt, preferred_element_type=F32)
    num = w_inter * inter + jnp.dot(s.astype(BF16), v.astype(BF16), preferred_element_type=F32)
    nq = w_inter * jnp.sum(q * n_state, axis=1, keepdims=True) + jnp.sum(s, axis=1, keepdims=True)
    h = num / jnp.maximum(jnp.abs(nq), jnp.exp(-mt))

    last = slice(n_valid - 1, n_valid)
    m_end = mt[last, :]
    w_c = jnp.exp(acol[last, :] - m_end)
    w_s = jnp.exp(bcol[last, :] - bcol + icol - m_end)
    if n_valid < L:
        w_s = jnp.where(lax.broadcasted_iota(jnp.int32, (L, 1), 0) < n_valid, w_s, 0.0)
    tn = (((0,), (0,)), ((), ()))
    c_new = w_c * c_state + lax.dot_general((v * w_s).astype(BF16), kb, tn, preferred_element_type=F32)
    n_new = w_c * n_state + jnp.sum(w_s * k, axis=0, keepdims=True)
    return h, c_new, n_new, m_end


def _gate_rows(g_ref_val, bias_ref, head):
    irow = g_ref_val[0:1, :] + bias_ref[0, head]
    frow = _log_sigmoid(g_ref_val[1:2, :] + bias_ref[1, head])
    return irow, frow


def _mlstm_prompt_body(bias_ref, q_ref, k_ref, v_ref, mo_ref, g_ref,
                       a_ref, c_ref, n_ref, m_ref):
    @pl.when(pl.program_id(1) == 0)
    def _():
        c_ref[...] = jnp.zeros_like(c_ref)
        n_ref[...] = jnp.zeros_like(n_ref)
        m_ref[...] = jnp.zeros_like(m_ref)

    for head in range(M_HEADS):
        qk = pl.ds(head * M_DQK, M_DQK)
        vo = pl.ds(head * M_DV, M_DV)
        one = pl.ds(head, 1)
        irow, frow = _gate_rows(g_ref[head], bias_ref, head)
        h, c_new, n_new, m_end = _mlstm_chunk(
            q_ref[:, qk], k_ref[:, qk] * (M_DQK ** -0.5), v_ref[:, vo], irow, frow,
            c_ref[0, head], n_ref[0, one, :], m_ref[0, one, 0:1], CHUNK)
        a_ref[:, vo] = (_sigmoid(mo_ref[:, vo]) * h).astype(a_ref.dtype)
        c_ref[0, head] = c_new
        n_ref[0, one, :] = n_new
        m_ref[0, one, :] = jnp.broadcast_to(m_end, (1, LANES))


def _mlstm_prompt(z_main, gates_rows, gate_bias):
    nc = SEQ // CHUNK
    return pl.pallas_call(
        _mlstm_prompt_body,
        grid=(BATCH, nc),
        in_specs=[
            pl.BlockSpec(memory_space=pltpu.SMEM),
            pl.BlockSpec((CHUNK, W_MQK), lambda b, c: (b * nc + c, 0)),
            pl.BlockSpec((CHUNK, W_MQK), lambda b, c: (b * nc + c, 1)),
            pl.BlockSpec((CHUNK, W_MV), lambda b, c: (b * nc + c, 2 * W_MQK // W_MV)),
            pl.BlockSpec((CHUNK, W_MV), lambda b, c: (b * nc + c, 2 * W_MQK // W_MV + 1)),
            pl.BlockSpec((M_HEADS, 2, CHUNK), lambda b, c: (0, 0, b * nc + c)),
        ],
        out_specs=[
            pl.BlockSpec((CHUNK, W_MV), lambda b, c: (b * nc + c, 0)),
            pl.BlockSpec((1, M_HEADS, M_DV, M_DQK), lambda b, c: (b, 0, 0, 0)),
            pl.BlockSpec((1, M_HEADS, M_DQK), lambda b, c: (b, 0, 0)),
            pl.BlockSpec((1, M_HEADS, LANES), lambda b, c: (b, 0, 0)),
        ],
        out_shape=[
            jax.ShapeDtypeStruct((N_PROMPT, W_MV), BF16),
            jax.ShapeDtypeStruct((BATCH, M_HEADS, M_DV, M_DQK), F32),
            jax.ShapeDtypeStruct((BATCH, M_HEADS, M_DQK), F32),
            jax.ShapeDtypeStruct((BATCH, M_HEADS, LANES), F32),
        ],
        compiler_params=_params(2),
        name="mlstm_prompt",
    )(gate_bias, z_main, z_main, z_main, z_main, gates_rows)


def _mlstm_sample_body(bias_ref, q_ref, k_ref, v_ref, mo_ref, g_ref, c0_ref, n0_ref, m0_ref,
                       a_ref, c_ref, n_ref, m_ref, q_scr, k_scr, v_scr):
    @pl.when(pl.program_id(0) == 0)
    def _():
        q_scr[...] = jnp.zeros_like(q_scr)
        k_scr[...] = jnp.zeros_like(k_scr)
        v_scr[...] = jnp.zeros_like(v_scr)

    q_scr[0:DEC_SEQ, :] = q_ref[0]
    k_scr[0:DEC_SEQ, :] = k_ref[0] * (M_DQK ** -0.5)
    v_scr[0:DEC_SEQ, :] = v_ref[0]
    for head in range(M_HEADS):
        qk = pl.ds(head * M_DQK, M_DQK)
        vo = pl.ds(head * M_DV, M_DV)
        one = pl.ds(head, 1)
        irow, frow = _gate_rows(g_ref[0, head], bias_ref, head)
        h, c_new, n_new, m_end = _mlstm_chunk(
            q_scr[:, qk], k_scr[:, qk], v_scr[:, vo], irow, frow,
            c0_ref[0, 0, head], n0_ref[0, 0, one, :], m0_ref[0, one, :], DEC_SEQ)
        a_ref[0, :, vo] = _sigmoid(mo_ref[0, :, vo]) * h[0:DEC_SEQ, :]
        c_ref[0, head] = c_new
        n_ref[0, one, :] = n_new
        m_ref[0, one, :] = jnp.broadcast_to(m_end, (1, LANES))


def _mlstm_sample(z_s3, gates_rows, gate_bias, c0, n0, m0):
    return pl.pallas_call(
        _mlstm_sample_body,
        grid=(DEC_BATCH,),
        in_specs=[
            pl.BlockSpec(memory_space=pltpu.SMEM),
            pl.BlockSpec((1, DEC_SEQ, W_MQK), lambda b: (b, 0, 0)),
            pl.BlockSpec((1, DEC_SEQ, W_MQK), lambda b: (b, 0, 1)),
            pl.BlockSpec((1, DEC_SEQ, W_MV), lambda b: (b, 0, 2 * W_MQK // W_MV)),
            pl.BlockSpec((1, DEC_SEQ, W_MV), lambda b: (b, 0, 2 * W_MQK // W_MV + 1)),
            pl.BlockSpec((1, M_HEADS, 2, SAMPLE_CHUNK), lambda b: (b, 0, 0, 0)),
            pl.BlockSpec((1, 1, M_HEADS, M_DV, M_DQK), lambda b: (0, b, 0, 0, 0)),
            pl.BlockSpec((1, 1, M_HEADS, M_DQK), lambda b: (0, b, 0, 0)),
            pl.BlockSpec((1, M_HEADS, 1), lambda b: (b, 0, 0)),
        ],
        out_specs=[
            pl.BlockSpec((1, DEC_SEQ, W_MV), lambda b: (b, 0, 0)),
            pl.BlockSpec((1, M_HEADS, M_DV, M_DQK), lambda b: (b, 0, 0, 0)),
            pl.BlockSpec((1, M_HEADS, M_DQK), lambda b: (b, 0, 0)),
            pl.BlockSpec((1, M_HEADS, LANES), lambda b: (b, 0, 0)),
        ],
        out_shape=[
            jax.ShapeDtypeStruct((DEC_BATCH, DEC_SEQ, W_MV), F32),
            jax.ShapeDtypeStruct((DEC_BATCH, M_HEADS, M_DV, M_DQK), F32),
            jax.ShapeDtypeStruct((DEC_BATCH, M_HEADS, M_DQK), F32),
            jax.ShapeDtypeStruct((DEC_BATCH, M_HEADS, LANES), F32),
        ],
        scratch_shapes=[pltpu.VMEM((SAMPLE_CHUNK, W_MQK), F32), pltpu.VMEM((SAMPLE_CHUNK, W_MQK), F32),
                        pltpu.VMEM((SAMPLE_CHUNK, W_MV), F32)],
        compiler_params=_params(1),
        name="mlstm_sample",
    )(gate_bias, z_s3, z_s3, z_s3, z_s3, gates_rows, c0, n0, m0)


def _dil_prompt_body(q_ref, kp_ref, kc_ref, vp_ref, vc_ref, o_ref, l_ref, *, r, hb):
    first_key = jnp.where(pl.program_id(2) > 0, 0, SPAN)
    qi = lax.broadcasted_iota(jnp.int32, (SPAN, 2 * SPAN), 0)
    ki = lax.broadcasted_iota(jnp.int32, (SPAN, 2 * SPAN), 1)
    ok = (ki >= qi) & (ki <= qi + SPAN) & (ki >= first_key)
    bias = jnp.where(ok, 0.0, NEG_INF)
    scale = A_HEAD_DIM ** -0.5
    nt = (((1,), (1,)), ((), ()))
    for c in range(r):
        rows = pl.ds(c, SPAN, stride=r) if r > 1 else pl.ds(0, SPAN)
        for hh in range(hb):
            cols = pl.ds(hh * A_HEAD_DIM, A_HEAD_DIM)
            q = q_ref[rows, cols].astype(BF16)
            kk = jnp.concatenate([kp_ref[rows, cols], kc_ref[rows, cols]], axis=0).astype(BF16)
            vv = jnp.concatenate([vp_ref[rows, cols], vc_ref[rows, cols]], axis=0).astype(BF16)
            s = lax.dot_general(q, kk, nt, preferred_element_type=F32) * scale + bias
            m = jnp.max(s, axis=1, keepdims=True)
            p = jnp.exp(s - m)
            den = jnp.sum(p, axis=1, keepdims=True)
            o = jnp.dot(p.astype(BF16), vv, preferred_element_type=F32) / den
            o_ref[rows, cols] = o
            l_ref[rows, cols] = jnp.broadcast_to(m + jnp.log(den), (SPAN, A_HEAD_DIM))


def _dilated_prompt(z_aqkv, g, r, hb):
    rows = SPAN * r
    nblk = SEQ // rows
    wcol = hb * A_HEAD_DIM
    per_part = W_AH // wcol

    def spec(part, prev):
        def imap(b, hg, n):
            nn = jnp.maximum(n - 1, 0) if prev else n
            return (b * nblk + nn, (3 * g + part) * per_part + hg)
        return pl.BlockSpec((rows, wcol), imap)

    out_spec = pl.BlockSpec((rows, wcol), lambda b, hg, n: (b * nblk + n, hg))
    return pl.pallas_call(
        functools.partial(_dil_prompt_body, r=r, hb=hb),
        grid=(BATCH, A_HEADS // hb, nblk),
        in_specs=[spec(0, False), spec(1, True), spec(1, False), spec(2, True), spec(2, False)],
        out_specs=[out_spec, out_spec],
        out_shape=[jax.ShapeDtypeStruct((N_PROMPT, W_AH), F32)] * 2,
        compiler_params=_params(3),
        name=f"dilated_prompt_g{g}",
    )(z_aqkv, z_aqkv, z_aqkv, z_aqkv, z_aqkv)


def _sample_attn_body(qkv_ref, k0_ref, v0_ref, k1_ref, v1_ref, k2_ref, v2_ref, b0_ref, b1_ref, b2_ref,
                      xq_ref, mk_ref, mv_ref, bx_ref, yb_ref, yc_ref):
    nt = (((1,), (1,)), ((), ()))
    outs, lses = [], []
    groups = ((k0_ref, v0_ref, b0_ref), (k1_ref, v1_ref, b1_ref), (k2_ref, v2_ref, b2_ref))
    for g, (kc_ref, vc_ref, bias_ref) in enumerate(groups):
        n_cached = kc_ref.shape[1] * kc_ref.shape[2]
        q = qkv_ref[0, 3 * g].astype(BF16)
        kk = jnp.concatenate([kc_ref[0].reshape(n_cached, A_HEAD_DIM), qkv_ref[0, 3 * g + 1]],
                             axis=0).astype(BF16)
        vv = jnp.concatenate([vc_ref[0].reshape(n_cached, A_HEAD_DIM), qkv_ref[0, 3 * g + 2]],
                             axis=0).astype(BF16)
        s = lax.dot_general(q, kk, nt, preferred_element_type=F32) * (A_HEAD_DIM ** -0.5) + bias_ref[...]
        m = jnp.max(s, axis=1, keepdims=True)
        p = jnp.exp(s - m)
        den = jnp.sum(p, axis=1, keepdims=True)
        outs.append(jnp.dot(p.astype(BF16), vv, preferred_element_type=F32) / den)
        lses.append(m + jnp.log(den))
    mx = jnp.maximum(jnp.maximum(lses[0], lses[1]), lses[2])
    e = [jnp.exp(l - mx) for l in lses]
    tot = e[0] + e[1] + e[2]
    yb = (e[0] / tot) * outs[0] + (e[1] / tot) * outs[1] + (e[2] / tot) * outs[2]
    yb_ref[0] = yb.astype(yb_ref.dtype)

    s = lax.dot_general(xq_ref[0].astype(BF16), mk_ref[0].astype(BF16), nt,
                        preferred_element_type=F32) * (X_HEAD_DIM ** -0.5) + bx_ref[...]
    m = jnp.max(s, axis=1, keepdims=True)
    ex = jnp.exp(s - m)
    p = ex / jnp.sum(ex, axis=1, keepdims=True)
    yc_ref[0] = jnp.dot(p.astype(BF16), mv_ref[0].astype(BF16),
                        preferred_element_type=F32).astype(yc_ref.dtype)


def _dilated_sample_bias(window, r, lb, rc):
    span = window // r
    n_c = (lb // r) * rc * A_HEADS
    rows = np.arange(DEC_SEQ * A_HEADS)
    s_q, h_q = rows // A_HEADS, rows % A_HEADS
    col = np.arange(n_c)
    m_k = col // (rc * A_HEADS)
    c_k = (col % (rc * A_HEADS)) // A_HEADS
    h_k = col % A_HEADS
    delta = (lb + s_q)[:, None] - (m_k * r + c_k)[None, :]
    ok_c = (h_q[:, None] == h_k[None, :]) & (delta % r == 0) & (delta // r <= span) & (delta >= 0)
    coln = np.arange(DEC_SEQ * A_HEADS)
    s_n, h_n = coln // A_HEADS, coln % A_HEADS
    dn = s_q[:, None] - s_n[None, :]
    ok_n = (h_q[:, None] == h_n[None, :]) & (dn >= 0) & (dn % r == 0) & (dn // r <= span)
    ok = np.concatenate([ok_c, ok_n], axis=1)
    return np.where(ok, 0.0, -np.inf).astype(np.float32)


def _sample_attention(qkv, caches, xq, mem_k, mem_v):
    rq = DEC_SEQ * A_HEADS
    rx = DEC_SEQ * X_HEADS
    nk = MEM_LEN * X_HEADS
    cache_args, cache_specs, biases = [], [], []
    for (window, r), (cache_k, cache_v) in zip(DIL_GROUPS, caches):
        lb = cache_k.shape[1]
        assert lb % r == 0 and window % r == 0
        rc = min(r, DEC_SEQ)
        nm = lb // r
        spec = pl.BlockSpec((1, nm, rc * A_HEADS, A_HEAD_DIM), lambda b: (b, 0, 0, 0))
        for c in (cache_k, cache_v):
            cache_args.append(c.reshape(DEC_BATCH, nm, r * A_HEADS, A_HEAD_DIM))
            cache_specs.append(spec)
        biases.append(jnp.asarray(_dilated_sample_bias(window, r, lb, rc)))
    ok = (np.arange(rx) % X_HEADS)[:, None] == (np.arange(nk) % X_HEADS)[None, :]
    bias_x = jnp.asarray(np.where(ok, 0.0, -np.inf).astype(np.float32))

    def const_spec(a):
        return pl.BlockSpec(a.shape, lambda b: (0, 0))

    return pl.pallas_call(
        _sample_attn_body,
        grid=(DEC_BATCH,),
        in_specs=[
            pl.BlockSpec((1, 3 * N_GROUPS, rq, A_HEAD_DIM), lambda b: (b, 0, 0, 0)),
            *cache_specs,
            *[const_spec(a) for a in biases],
            pl.BlockSpec((1, rx, X_HEAD_DIM), lambda b: (b, 0, 0)),
            pl.BlockSpec((1, nk, X_HEAD_DIM), lambda b: (b, 0, 0)),
            pl.BlockSpec((1, nk, X_HEAD_DIM), lambda b: (b, 0, 0)),
            const_spec(bias_x),
        ],
        out_specs=[pl.BlockSpec((1, rq, A_HEAD_DIM), lambda b: (b, 0, 0)),
                   pl.BlockSpec((1, rx, X_HEAD_DIM), lambda b: (b, 0, 0))],
        out_shape=[jax.ShapeDtypeStruct((DEC_BATCH, rq, A_HEAD_DIM), BF16),
                   jax.ShapeDtypeStruct((DEC_BATCH, rx, X_HEAD_DIM), BF16)],
        compiler_params=_params(1),
        name="sample_attention",
    )(qkv, *cache_args, *biases, xq, mem_k, mem_v, bias_x)


def _cross_prompt_body(q_ref, k_ref, v_ref, o_ref):
    scale = X_HEAD_DIM ** -0.5
    nt = (((1,), (1,)), ((), ()))
    for h in range(X_HEADS):
        cols = pl.ds(h * X_HEAD_DIM, X_HEAD_DIM)
        s = lax.dot_general(q_ref[:, cols].astype(BF16), k_ref[:, cols].astype(BF16), nt,
                            preferred_element_type=F32) * scale
        m = jnp.max(s, axis=1, keepdims=True)
        e = jnp.exp(s - m)
        p = e / jnp.sum(e, axis=1, keepdims=True)
        o_ref[:, cols] = jnp.dot(p.astype(BF16), v_ref[:, cols].astype(BF16),
                                 preferred_element_type=F32).astype(o_ref.dtype)


def _cross_prompt(z_xq, mem_kv, *, tq=512):
    nq = SEQ // tq
    return pl.pallas_call(
        _cross_prompt_body,
        grid=(BATCH, nq),
        in_specs=[
            pl.BlockSpec((tq, W_XQ), lambda b, i: (b * nq + i, 0)),
            pl.BlockSpec((MEM_LEN, W_XQ), lambda b, i: (b, 0)),
            pl.BlockSpec((MEM_LEN, W_XQ), lambda b, i: (b, 1)),
        ],
        out_specs=pl.BlockSpec((tq, W_XQ), lambda b, i: (b * nq + i, 0)),
        out_shape=jax.ShapeDtypeStruct((N_PROMPT, W_XQ), BF16),
        compiler_params=_params(2),
        name="cross_prompt",
    )(z_xq, mem_kv, mem_kv)


def _combine_body(o0, o1, o2, l0, l1, l2, y_ref):
    a0, a1, a2 = l0[...], l1[...], l2[...]
    mx = jnp.maximum(jnp.maximum(a0, a1), a2)
    e0, e1, e2 = jnp.exp(a0 - mx), jnp.exp(a1 - mx), jnp.exp(a2 - mx)
    tot = e0 + e1 + e2
    y = (e0 / tot) * o0[...] + (e1 / tot) * o1[...] + (e2 / tot) * o2[...]
    y_ref[...] = y.astype(y_ref.dtype)


def _combine_groups(outs, lses, *, rows_per_step):
    n, w = outs[0].shape
    spec = pl.BlockSpec((rows_per_step, w), lambda i: (i, 0))
    return pl.pallas_call(
        _combine_body,
        grid=(n // rows_per_step,),
        in_specs=[spec] * 6,
        out_specs=spec,
        out_shape=jax.ShapeDtypeStruct((n, w), BF16),
        compiler_params=_params(1),
        name="combine_groups",
    )(*outs, *lses)


def _rope_tables():
    pos = jnp.concatenate([
        jnp.tile(jnp.arange(SEQ, dtype=jnp.int32), BATCH),
        jnp.tile(PAST_LEN + jnp.arange(DEC_SEQ, dtype=jnp.int32), DEC_BATCH)])
    inv = ROPE_THETA ** (-jnp.arange(0, A_HEAD_DIM, 2, dtype=F32) / A_HEAD_DIM)
    ang = pos.astype(F32)[:, None] * inv[None, :]
    cos, sin = jnp.cos(ang), jnp.sin(ang)
    return jnp.concatenate([cos, cos], axis=1), jnp.concatenate([-sin, sin], axis=1)


def _layer(x_p, x_s, mem_prompt, state_c, state_n, state_m, caches, cache_mem_k, cache_mem_v,
           g_mix, w_in, b_igate, b_fgate, b_branch, g_mem, w_mem_kv,
           w_br_a, w_br_b, w_br_c, w_out, g_mlp, w_up, w_down):
    h_all = _rmsnorm_two_groups(x_p, x_s, g_mix, BF16)

    w_in_t = w_in.T
    z_main = _matmul_stream(h_all, w_in_t, n_cols=COL_GATES, b_rows_are_outputs=True, name="proj_main")
    z_gate = _matmul(h_all, w_in_t, n_cols=2 * M_HEADS, col_start=COL_GATES, tn=2 * M_HEADS,
                     b_rows_are_outputs=True, name="proj_gates")
    cos2, sin2 = _rope_tables()
    table_spec = pl.BlockSpec((MM_TM, A_HEAD_DIM), lambda j, i: (i, 0))
    z_aqkv = _matmul_stream(h_all, w_in_t, n_cols=W_AQKV, col_start=COL_REST, b_rows_are_outputs=True,
                            epilogue=_ep_rope, extras=(cos2, sin2), extra_specs=(table_spec, table_spec),
                            name="proj_attn")
    z_xq = _matmul(h_all, w_in_t, n_cols=W_XQ, col_start=COL_REST + W_AQKV, b_rows_are_outputs=True,
                   name="proj_cross_q")
    z_br = _matmul_stream(h_all, w_in_t, n_cols=3 * D_MODEL, col_start=COL_REST + W_AQKV + W_XQ,
                          b_rows_are_outputs=True, name="proj_branch_gates")

    gate_bias = jnp.stack([b_igate, b_fgate]).astype(F32)

    gp = z_gate[:N_PROMPT].reshape(N_PROMPT, 2, M_HEADS).transpose(2, 1, 0)
    a_p, c_p, n_p, m_p = _mlstm_prompt(z_main, gp, gate_bias)
    gs = z_gate[N_PROMPT:].reshape(DEC_BATCH, DEC_SEQ, 2, M_HEADS).transpose(0, 3, 2, 1)
    gs = jnp.pad(gs, ((0, 0), (0, 0), (0, 0), (0, SAMPLE_CHUNK - DEC_SEQ)))
    z_main_s = z_main[N_PROMPT:].reshape(DEC_BATCH, DEC_SEQ, COL_GATES)
    a_s, c_s, n_s, m_s = _mlstm_sample(
        z_main_s, gs, gate_bias, state_c, state_n, state_m.reshape(DEC_BATCH, M_HEADS, 1))
    a_ps = (a_p, a_s.reshape(N_SAMPLE, W_MV).astype(BF16))

    heads_per_step = (8, 1, 1)
    outs_p, lses_p, rows_p, rows_s = [], [], [], []
    rq = DEC_SEQ * A_HEADS
    qkv_s = z_aqkv[N_PROMPT:].reshape(DEC_BATCH, DEC_SEQ, 3 * N_GROUPS, A_HEADS, A_HEAD_DIM)
    qkv_s = qkv_s.transpose(0, 2, 1, 3, 4).reshape(DEC_BATCH, 3 * N_GROUPS, rq, A_HEAD_DIM)
    for g, (window, r) in enumerate(DIL_GROUPS):
        o, l = _dilated_prompt(z_aqkv, g, r, heads_per_step[g])
        outs_p.append(o)
        lses_p.append(l)
        c0 = 3 * g * W_AH
        keep = min(window, SEQ)
        for part in (1, 2):
            cs = c0 + part * W_AH
            kept = [z_aqkv[(b + 1) * SEQ - keep:(b + 1) * SEQ, cs:cs + W_AH] for b in range(BATCH)]
            rows_p.append(jnp.stack(kept).reshape(BATCH, keep, A_HEADS, A_HEAD_DIM))
            rows_s.append(qkv_s[:, 3 * g + part].reshape(DEC_BATCH, DEC_SEQ, A_HEADS, A_HEAD_DIM))
    yb_p = _combine_groups(outs_p, lses_p, rows_per_step=256)

    mem_h = _rmsnorm(mem_prompt, g_mem, BF16)
    mem_kv = _matmul(mem_h, w_mem_kv, n_cols=2 * W_XQ, tm=BATCH * MEM_LEN, name="mem_kv")
    yc_p = _cross_prompt(z_xq, mem_kv)

    xq_s = z_xq[N_PROMPT:].reshape(DEC_BATCH, DEC_SEQ * X_HEADS, X_HEAD_DIM)
    yb_s, yc_s = _sample_attention(
        qkv_s, caches, xq_s,
        cache_mem_k.reshape(DEC_BATCH, MEM_LEN * X_HEADS, X_HEAD_DIM),
        cache_mem_v.reshape(DEC_BATCH, MEM_LEN * X_HEADS, X_HEAD_DIM))
    yb_ps = (yb_p, yb_s.reshape(N_SAMPLE, W_AH))
    yc_ps = (yc_p, yc_s.reshape(N_SAMPLE, W_XQ))

    merged = _branch_merge(a_ps, yb_ps, yc_ps, w_br_a, w_br_b, w_br_c, z_br, b_branch.reshape(1, 3 * D_MODEL))
    x1 = _matmul_stream(merged, w_out, n_cols=D_MODEL, tn=MM_TN, epilogue=_ep_residual_two_groups, extras=(x_p, x_s),
                        extra_specs=_two_group_specs(MM_TM, MM_TN, lambda j: j, SUBLANES, False),
                        name="out_proj")

    h2 = _rmsnorm(x1, g_mlp, BF16)
    u = _matmul_stream(h2, w_up, n_cols=D_FF, out_dtype=BF16, epilogue=_ep_relu2, name="mlp_up")
    x2 = _matmul_ksplit_residual(u, w_down, x1)

    mem_k = mem_kv[:, :W_XQ].reshape(BATCH, MEM_LEN, X_HEADS, X_HEAD_DIM)
    mem_v = mem_kv[:, W_XQ:].reshape(BATCH, MEM_LEN, X_HEADS, X_HEAD_DIM)
    prompt_state = (c_p, n_p, m_p[:, :, 0])
    sample_state = (c_s, n_s, m_s[:, :, 0])
    return x2, prompt_state, sample_state, rows_p, rows_s, mem_k, mem_v


def kernel(x_prompt, x_sample, state_mlstm_C, state_mlstm_n, state_mlstm_m,
           cache_win_k_g0, cache_win_v_g0, cache_win_k_g1, cache_win_v_g1,
           cache_win_k_g2, cache_win_v_g2, cache_mem_k, cache_mem_v, mem_prompt,
           g_mix, w_in, b_igate, b_fgate, b_branch, g_mem, w_mem_kv,
           w_br_a, w_br_b, w_br_c, w_out, g_mlp, w_up, w_down, g_final):
    depth = g_mix.shape[0]
    assert depth == 1, "single-layer stack"
    caches = ((cache_win_k_g0[0], cache_win_v_g0[0]),
              (cache_win_k_g1[0], cache_win_v_g1[0]),
              (cache_win_k_g2[0], cache_win_v_g2[0]))
    x2, p_state, s_state, rows_p, rows_s, mem_k, mem_v = _layer(
        x_prompt.reshape(N_PROMPT, D_MODEL), x_sample.reshape(N_SAMPLE, D_MODEL),
        mem_prompt.reshape(BATCH * MEM_LEN, D_MODEL),
        state_mlstm_C, state_mlstm_n, state_mlstm_m[0], caches, cache_mem_k[0], cache_mem_v[0],
        g_mix[0], w_in[0], b_igate[0], b_fgate[0], b_branch[0], g_mem[0], w_mem_kv[0],
        w_br_a[0], w_br_b[0], w_br_c[0], w_out[0], g_mlp[0], w_up[0], w_down[0])
    y_prompt = _rmsnorm(x2, g_final, F32, row_start=0, n_rows=N_PROMPT).reshape(BATCH, SEQ, D_MODEL)
    y_sample = _rmsnorm(x2, g_final, F32, row_start=N_PROMPT, n_rows=N_SAMPLE).reshape(DEC_BATCH, DEC_SEQ, D_MODEL)
    lead = lambda a: a[None]
    return (y_prompt, y_sample,
            lead(p_state[0]), lead(p_state[1]), lead(p_state[2]),
            *[lead(r) for r in rows_p],
            lead(mem_k), lead(mem_v),
            lead(s_state[0]), lead(s_state[1]), lead(s_state[2]),
            *[lead(r) for r in rows_s])
```

```python
import functools
import math

import numpy as np
import jax
import jax.numpy as jnp
from jax import lax
from jax.experimental import pallas as pl
from jax.experimental.pallas import tpu as pltpu

F32 = jnp.float32
BF16 = jnp.bfloat16
NEG_INF = float("-inf")

D_MODEL = 4096
BATCH = 2
SEQ = 4096
DEC_BATCH = 128
DEC_SEQ = 4
PAST_LEN = 2048
MEM_LEN = 256
M_HEADS = 8
M_DQK = D_MODEL // (2 * M_HEADS)
M_DV = D_MODEL // M_HEADS
A_HEAD_DIM = 128
A_HEADS = D_MODEL // 512
DIL_GROUPS = ((128, 1), (512, 4), (2048, 16))
N_GROUPS = 3
ROPE_THETA = 10000.0
X_HEADS = 4
X_HEAD_DIM = 128
D_FF = 4 * D_MODEL
EPS = 1e-6

W_MQK = M_HEADS * M_DQK
W_MV = M_HEADS * M_DV
W_AH = A_HEADS * A_HEAD_DIM
W_XQ = X_HEADS * X_HEAD_DIM
N_PROMPT = BATCH * SEQ
N_SAMPLE = DEC_BATCH * DEC_SEQ
N_ALL = N_PROMPT + N_SAMPLE

COL_GATES = 2 * W_MQK + 2 * W_MV
COL_REST = COL_GATES + 2 * M_HEADS
W_AQKV = 3 * N_GROUPS * W_AH

LANES = 128
SUBLANES = 8
VMEM_LIMIT_BYTES = 60 * 1024 * 1024

MM_TM = 1088
MM_TN = 512
MM_TN_WIDE = 1024
MERGE_TM = 544
DOWN_TM = 2176
DOWN_TK = 2048
NORM_ROWS = 256
CHUNK = 128
SAMPLE_CHUNK = 16
SAMPLE_SEQS_PER_STEP = 2
SPAN = 128


def _params(n_axes):
    return pltpu.CompilerParams(
        dimension_semantics=("arbitrary",) * n_axes,
        vmem_limit_bytes=VMEM_LIMIT_BYTES,
    )


def _rmsnorm_body(x_ref, g_ref, o_ref):
    x = x_ref[...]
    ms = jnp.mean(x * x, axis=-1, keepdims=True)
    o_ref[...] = ((x * lax.rsqrt(ms + EPS)) * g_ref[...]).astype(o_ref.dtype)


def _rmsnorm(x, g, out_dtype, *, row_start=0, n_rows=None):
    n_rows = x.shape[0] if n_rows is None else n_rows
    d = x.shape[1]
    off = row_start // NORM_ROWS
    return pl.pallas_call(
        _rmsnorm_body,
        grid=(n_rows // NORM_ROWS,),
        in_specs=[
            pl.BlockSpec((NORM_ROWS, d), lambda i: (i + off, 0)),
            pl.BlockSpec((1, d), lambda i: (0, 0)),
        ],
        out_specs=pl.BlockSpec((NORM_ROWS, d), lambda i: (i, 0)),
        out_shape=jax.ShapeDtypeStruct((n_rows, d), out_dtype),
        compiler_params=_params(1),
        name="rmsnorm",
    )(x, g.reshape(1, d))


def _rmsnorm_two_groups_body(xp_ref, xs_ref, g_ref, o_ref, *, prompt_steps):
    i = pl.program_id(0)

    @pl.when(i < prompt_steps)
    def _():
        _rmsnorm_body(xp_ref, g_ref, o_ref)

    @pl.when(i >= prompt_steps)
    def _():
        _rmsnorm_body(xs_ref, g_ref, o_ref)


def _rmsnorm_two_groups(xp, xs, g, out_dtype):
    d = xp.shape[1]
    ps, ss = xp.shape[0] // NORM_ROWS, xs.shape[0] // NORM_ROWS
    return pl.pallas_call(
        functools.partial(_rmsnorm_two_groups_body, prompt_steps=ps),
        grid=(ps + ss,),
        in_specs=[
            pl.BlockSpec((NORM_ROWS, d), lambda i: (jnp.minimum(i, ps - 1), 0)),
            pl.BlockSpec((NORM_ROWS, d), lambda i: (jnp.maximum(i - ps, 0), 0)),
            pl.BlockSpec((1, d), lambda i: (0, 0)),
        ],
        out_specs=pl.BlockSpec((NORM_ROWS, d), lambda i: (i, 0)),
        out_shape=jax.ShapeDtypeStruct((xp.shape[0] + xs.shape[0], d), out_dtype),
        compiler_params=_params(1),
        name="rmsnorm_two_groups",
    )(xp, xs, g.reshape(1, d))


def _ep_plain(acc, extra, o_ref, j):
    o_ref[...] = acc.astype(o_ref.dtype)


def _ep_residual_two_groups(acc, extra, o_ref, j):
    xp_ref, xs_ref = extra
    tm = acc.shape[0]
    n_full = N_PROMPT // tm
    rem = N_PROMPT - n_full * tm
    assert tm - rem == N_SAMPLE and rem % SUBLANES == 0
    i = pl.program_id(1)

    @pl.when(i < n_full)
    def _():
        o_ref[...] = xp_ref[...] + acc

    @pl.when(i >= n_full)
    def _():
        o_ref[0:rem, :] = xp_ref[tm - rem:tm, :] + acc[0:rem]
        o_ref[rem:tm, :] = xs_ref[...] + acc[rem:tm]


def _two_group_specs(tm, width, col_of_j, row_align, sample_block_is_constant):
    def p_map(j, i):
        row = jnp.minimum(i * tm, N_PROMPT - tm)
        col = col_of_j(j) * width
        return (pl.multiple_of(row, row_align), col if isinstance(col, int) else pl.multiple_of(col, LANES))
    p_spec = pl.BlockSpec((pl.Element(tm), pl.Element(width)), p_map)
    mode = pl.Buffered(1) if sample_block_is_constant else None
    s_spec = pl.BlockSpec((N_SAMPLE, width), lambda j, i: (0, col_of_j(j)), pipeline_mode=mode)
    return p_spec, s_spec


def _ep_relu2(acc, extra, o_ref, j):
    r = jnp.maximum(acc, 0.0)
    o_ref[...] = (r * r).astype(o_ref.dtype)


def _ep_rope(acc, extra, o_ref, j):
    cos_ref, sin_ref = extra
    tn = acc.shape[1]
    assert W_AH % tn == 0
    is_rope = (j // (W_AH // tn)) % 3 != 2
    cos2 = cos_ref[...]
    sin2 = sin_ref[...]
    for s in range(tn // A_HEAD_DIM):
        x = acc[:, s * A_HEAD_DIM:(s + 1) * A_HEAD_DIM]
        rotated = x * cos2 + pltpu.roll(x, A_HEAD_DIM // 2, axis=1) * sin2
        o_ref[:, s * A_HEAD_DIM:(s + 1) * A_HEAD_DIM] = jnp.where(is_rope, rotated, x)


def _mm_body(*refs, b_rows_are_outputs, epilogue, n_extra):
    a_ref, b_ref = refs[0], refs[1]
    extra = refs[2:2 + n_extra]
    o_ref = refs[2 + n_extra]
    b_scr = refs[3 + n_extra]

    @pl.when(pl.program_id(1) == 0)
    def _():
        b_scr[...] = b_ref[...].astype(BF16)

    if b_rows_are_outputs:
        acc = lax.dot_general(a_ref[...], b_scr[...], (((1,), (1,)), ((), ())),
                              preferred_element_type=F32)
    else:
        acc = jnp.dot(a_ref[...], b_scr[...], preferred_element_type=F32)
    epilogue(acc, extra, o_ref, pl.program_id(0))


def _matmul(a, b, *, n_cols, col_start=0, b_rows_are_outputs=False, tm=MM_TM, tn=MM_TN,
            out_dtype=F32, epilogue=_ep_plain, extras=(), extra_specs=(), name="matmul"):
    m, k = a.shape
    if b_rows_are_outputs:
        if col_start % tn == 0:
            b_spec = pl.BlockSpec((tn, k), lambda j, i: (j + col_start // tn, 0))
        else:
            assert col_start % SUBLANES == 0 and tn % SUBLANES == 0
            b_spec = pl.BlockSpec((pl.Element(tn), pl.Element(k)),
                                  lambda j, i: (pl.multiple_of(col_start + j * tn, SUBLANES), 0))
        b_block = (tn, k)
    else:
        assert col_start % tn == 0
        b_spec = pl.BlockSpec((k, tn), lambda j, i: (0, j + col_start // tn))
        b_block = (k, tn)
    return pl.pallas_call(
        functools.partial(_mm_body, b_rows_are_outputs=b_rows_are_outputs, epilogue=epilogue,
                          n_extra=len(extras)),
        grid=(n_cols // tn, m // tm),
        in_specs=[pl.BlockSpec((tm, k), lambda j, i: (i, 0)), b_spec, *extra_specs],
        out_specs=pl.BlockSpec((tm, tn), lambda j, i: (i, j)),
        out_shape=jax.ShapeDtypeStruct((m, n_cols), out_dtype),
        scratch_shapes=[pltpu.VMEM(b_block, BF16)],
        compiler_params=_params(2),
        name=name,
    )(a, b, *extras)


def _mm_stream_body(*refs, b_rows_are_outputs, epilogue, n_extra, slice_rows, split_cols):
    a_ref, bs_ref = refs[0], refs[1]
    extra = refs[2:2 + n_extra]
    o_ref = refs[2 + n_extra]
    b_scr = refs[3 + n_extra]
    p, i = pl.program_id(0), pl.program_id(1)

    r0 = pl.multiple_of(i * slice_rows, slice_rows)
    b_scr[p % 2, pl.ds(r0, slice_rows), :] = bs_ref[...].astype(BF16)

    @pl.when(p > 0)
    def _():
        slot = (p - 1) % 2
        a = a_ref[...]
        tn = o_ref.shape[1]
        n_split = tn // split_cols
        for h in range(n_split):
            cols = pl.ds(h * split_cols, split_cols)
            if b_rows_are_outputs:
                acc = lax.dot_general(a, b_scr[slot, cols, :], (((1,), (1,)), ((), ())),
                                      preferred_element_type=F32)
            else:
                acc = jnp.dot(a, b_scr[slot, :, cols], preferred_element_type=F32)
            epilogue(acc, extra, o_ref.at[:, cols], (p - 1) * n_split + h)


def _matmul_stream(a, b, *, n_cols, col_start=0, b_rows_are_outputs=False, tm=MM_TM, tn=MM_TN_WIDE,
                   out_dtype=F32, epilogue=_ep_plain, extras=(), extra_specs=(), name="matmul"):
    m, k = a.shape
    ni, nj = m // tm, n_cols // tn
    assert col_start % SUBLANES == 0 and nj * tn == n_cols

    def staged(p):
        return jnp.minimum(p, nj - 1)

    def shifted(index_map):
        return lambda p, i: index_map(jnp.maximum(p - 1, 0), jnp.where(p == 0, 0, i))

    if b_rows_are_outputs:
        tile = (tn, k)
        slice_rows = tn // ni
        bs_spec = pl.BlockSpec(
            (pl.Element(slice_rows), pl.Element(k)),
            lambda p, i: (pl.multiple_of(col_start + staged(p) * tn + i * slice_rows, SUBLANES), 0))
    else:
        assert col_start % tn == 0
        tile = (k, tn)
        slice_rows = k // ni
        bs_spec = pl.BlockSpec((slice_rows, tn), lambda p, i: (i, col_start // tn + staged(p)))
    assert slice_rows * ni == tile[0] and slice_rows % 16 == 0
    extra_specs = [pl.BlockSpec(s.block_shape, shifted(s.index_map), pipeline_mode=s.pipeline_mode)
                   for s in extra_specs]
    return pl.pallas_call(
        functools.partial(_mm_stream_body, b_rows_are_outputs=b_rows_are_outputs, epilogue=epilogue,
                          n_extra=len(extras), slice_rows=slice_rows, split_cols=min(tn, MM_TN)),
        grid=(nj + 1, ni),
        in_specs=[pl.BlockSpec((tm, k), shifted(lambda j, i: (i, 0))), bs_spec, *extra_specs],
        out_specs=pl.BlockSpec((tm, tn), shifted(lambda j, i: (i, j))),
        out_shape=jax.ShapeDtypeStruct((m, n_cols), out_dtype),
        scratch_shapes=[pltpu.VMEM((2,) + tile, BF16)],
        compiler_params=_params(2),
        name=name,
    )(a, b, *extras)


def _mm_ksplit_body(a_ref, b_ref, res_ref, o_ref, acc_ref):
    kk = pl.program_id(2)

    @pl.when(kk == 0)
    def _():
        acc_ref[...] = jnp.zeros_like(acc_ref)

    acc_ref[...] += jnp.dot(a_ref[...], b_ref[...].astype(BF16), preferred_element_type=F32)

    @pl.when(kk == pl.num_programs(2) - 1)
    def _():
        o_ref[...] = res_ref[...] + acc_ref[...]


def _matmul_ksplit_residual(a, b, res, *, tm=DOWN_TM, tn=MM_TN, tk=DOWN_TK):
    m, k = a.shape
    n = b.shape[1]
    return pl.pallas_call(
        _mm_ksplit_body,
        grid=(n // tn, m // tm, k // tk),
        in_specs=[
            pl.BlockSpec((tm, tk), lambda j, i, kk: (i, kk)),
            pl.BlockSpec((tk, tn), lambda j, i, kk: (kk, j)),
            pl.BlockSpec((tm, tn), lambda j, i, kk: (i, j)),
        ],
        out_specs=pl.BlockSpec((tm, tn), lambda j, i, kk: (i, j)),
        out_shape=jax.ShapeDtypeStruct((m, n), F32),
        scratch_shapes=[pltpu.VMEM((tm, tn), F32)],
        compiler_params=_params(3),
        name="mlp_down",
    )(a, b, res)


def _sigmoid(x):
    return 1.0 / (1.0 + jnp.exp(-x))


def _straddle_rows(p_ref, s_ref):
    tm = p_ref.shape[0]
    rem = tm - s_ref.shape[0]
    return jnp.concatenate([p_ref[tm - rem:tm, :], s_ref[...]], axis=0)


def _merge_body(ap_ref, as_ref, ybp_ref, ybs_ref, ycp_ref, ycs_ref,
                wa0_ref, wb0_ref, wc0_ref, was_ref, wbs_ref, wcs_ref,
                ga_ref, gb_ref, gc_ref, ba_ref, bb_ref, bc_ref,
                o_ref, wa_scr, wb_scr, wc_scr):
    j, i = pl.program_id(0), pl.program_id(1)
    cur = j % 2

    @pl.when((j == 0) & (i == 0))
    def _():
        wa_scr[0] = wa0_ref[...].astype(BF16)
        wb_scr[0] = wb0_ref[...].astype(BF16)
        wc_scr[0] = wc0_ref[...].astype(BF16)

    for scr, slice_ref in ((wa_scr, was_ref), (wb_scr, wbs_ref), (wc_scr, wcs_ref)):
        rows = slice_ref.shape[0]
        scr[1 - cur, pl.ds(pl.multiple_of(i * rows, rows), rows), :] = slice_ref[...].astype(BF16)

    def merge(a, yb, yc):
        y_a = jnp.dot(a, wa_scr[cur], preferred_element_type=F32)
        y_b = jnp.dot(yb, wb_scr[cur], preferred_element_type=F32)
        y_c = jnp.dot(yc, wc_scr[cur], preferred_element_type=F32)
        g_a = _sigmoid(ga_ref[...] + ba_ref[...])
        g_b = _sigmoid(gb_ref[...] + bb_ref[...])
        g_c = _sigmoid(gc_ref[...] + bc_ref[...])
        o_ref[...] = (g_a * y_a + g_b * y_b + g_c * y_c).astype(o_ref.dtype)

    is_prompt_tile = i < N_PROMPT // ap_ref.shape[0]

    @pl.when(is_prompt_tile)
    def _():
        merge(ap_ref[...], ybp_ref[...], ycp_ref[...])

    @pl.when(jnp.logical_not(is_prompt_tile))
    def _():
        merge(_straddle_rows(ap_ref, as_ref), _straddle_rows(ybp_ref, ybs_ref), _straddle_rows(ycp_ref, ycs_ref))


def _branch_merge(a_ps, yb_ps, yc_ps, w_a, w_b, w_c, z_br, b_branch, *, tm=MERGE_TM, tn=MM_TN):
    m = N_ALL
    assert (N_PROMPT // tm + 1) * tm == N_ALL
    nj = D_MODEL // tn

    def gate_spec(part):
        return pl.BlockSpec((tm, tn), lambda j, i: (i, part * nj + j))

    def bias_spec(part):
        return pl.BlockSpec((1, tn), lambda j, i: (0, part * nj + j))

    ni = m // tm
    widths = (W_MV, W_AH, W_XQ)
    assert all(k % (16 * ni) == 0 for k in widths)

    def first_tile_spec(k):
        return pl.BlockSpec((k, tn), lambda j, i: (0, 0), pipeline_mode=pl.Buffered(1))

    def next_slice_spec(k):
        return pl.BlockSpec((k // ni, tn), lambda j, i: (i, jnp.minimum(j + 1, nj - 1)))

    bf16_rows = 16
    return pl.pallas_call(
        _merge_body,
        grid=(nj, ni),
        in_specs=[
            *_two_group_specs(tm, W_MV, lambda j: 0, bf16_rows, True),
            *_two_group_specs(tm, W_AH, lambda j: 0, bf16_rows, True),
            *_two_group_specs(tm, W_XQ, lambda j: 0, bf16_rows, True),
            *[first_tile_spec(k) for k in widths],
            *[next_slice_spec(k) for k in widths],
            gate_spec(0), gate_spec(1), gate_spec(2),
            bias_spec(0), bias_spec(1), bias_spec(2),
        ],
        out_specs=pl.BlockSpec((tm, tn), lambda j, i: (i, j)),
        out_shape=jax.ShapeDtypeStruct((m, D_MODEL), BF16),
        scratch_shapes=[pltpu.VMEM((2, k, tn), BF16) for k in widths],
        compiler_params=_params(2),
        name="branch_merge",
    )(*a_ps, *yb_ps, *yc_ps, w_a, w_b, w_c, w_a, w_b, w_c,
      z_br, z_br, z_br, b_branch, b_branch, b_branch)


def _log_sigmoid(x):
    return jnp.minimum(x, 0.0) - jnp.log1p(jnp.exp(-jnp.abs(x)))


def _mlstm_chunk(q, k, v, irow, frow, c_state, n_state, m_state, n_valid):
    L = q.shape[0]
    ti = lax.broadcasted_iota(jnp.int32, (L, L), 0)
    si = lax.broadcasted_iota(jnp.int32, (L, L), 1)
    causal = si <= ti
    eye = si == ti
    f_b = jnp.broadcast_to(frow, (L, L))
    i_b = jnp.broadcast_to(irow, (L, L))
    bcol = jnp.sum(jnp.where(causal, f_b, 0.0), axis=1, keepdims=True)
    fcol = jnp.sum(jnp.where(eye, f_b, 0.0), axis=1, keepdims=True)
    icol = jnp.sum(jnp.where(eye, i_b, 0.0), axis=1, keepdims=True)
    brow = jnp.sum(jnp.where(ti <= si, jnp.broadcast_to(fcol, (L, L)), 0.0), axis=0, keepdims=True)

    acol = bcol + m_state
    logw = jnp.where(causal, bcol - brow + irow, NEG_INF)
    mt = jnp.maximum(acol, jnp.max(logw, axis=1, keepdims=True))
    w_inter = jnp.exp(acol - mt)
    w_intra = jnp.exp(logw - mt)

    qb = q.astype(BF16)
    kb = k.astype(BF16)
    nt = (((1,), (1,)), ((), ()))
    s = lax.dot_general(qb, kb, nt, preferred_element_type=F32) * w_intra
    inter = lax.dot_general(qb, c_state.astype(BF16), nt, preferred_element_type=F32)
    num = w_inter * inter + jnp.dot(s.astype(BF16), v.astype(BF16), preferred_element_type=F32)
    nq = w_inter * jnp.sum(q * n_state, axis=1, keepdims=True) + jnp.sum(s, axis=1, keepdims=True)
    h = num / jnp.maximum(jnp.abs(nq), jnp.exp(-mt))

    last = slice(n_valid - 1, n_valid)
    m_end = mt[last, :]
    w_c = jnp.exp(acol[last, :] - m_end)
    w_s = jnp.exp(bcol[last, :] - bcol + icol - m_end)
    if n_valid < L:
        w_s = jnp.where(lax.broadcasted_iota(jnp.int32, (L, 1), 0) < n_valid, w_s, 0.0)
    tn = (((0,), (0,)), ((), ()))
    c_new = w_c * c_state + lax.dot_general((v * w_s).astype(BF16), kb, tn, preferred_element_type=F32)
    n_new = w_c * n_state + jnp.sum(w_s * k, axis=0, keepdims=True)
    return h, c_new, n_new, m_end


def _gate_rows(g_ref_val, bias_ref, head):
    irow = g_ref_val[0:1, :] + bias_ref[0, head]
    frow = _log_sigmoid(g_ref_val[1:2, :] + bias_ref[1, head])
    return irow, frow


def _mlstm_prompt_body(bias_ref, q_ref, k_ref, v_ref, mo_ref, g_ref,
                       a_ref, c_ref, n_ref, m_ref):
    @pl.when(pl.program_id(1) == 0)
    def _():
        c_ref[...] = jnp.zeros_like(c_ref)
        n_ref[...] = jnp.zeros_like(n_ref)
        m_ref[...] = jnp.zeros_like(m_ref)

    for head in range(M_HEADS):
        qk = pl.ds(head * M_DQK, M_DQK)
        vo = pl.ds(head * M_DV, M_DV)
        one = pl.ds(head, 1)
        irow, frow = _gate_rows(g_ref[head], bias_ref, head)
        h, c_new, n_new, m_end = _mlstm_chunk(
            q_ref[:, qk], k_ref[:, qk] * (M_DQK ** -0.5), v_ref[:, vo], irow, frow,
            c_ref[0, head], n_ref[0, one, :], m_ref[0, one, 0:1], CHUNK)
        a_ref[:, vo] = (_sigmoid(mo_ref[:, vo]) * h).astype(a_ref.dtype)
        c_ref[0, head] = c_new
        n_ref[0, one, :] = n_new
        m_ref[0, one, :] = jnp.broadcast_to(m_end, (1, LANES))


def _mlstm_prompt(z_main, gates_rows, gate_bias):
    nc = SEQ // CHUNK
    return pl.pallas_call(
        _mlstm_prompt_body,
        grid=(BATCH, nc),
        in_specs=[
            pl.BlockSpec(memory_space=pltpu.SMEM),
            pl.BlockSpec((CHUNK, W_MQK), lambda b, c: (b * nc + c, 0)),
            pl.BlockSpec((CHUNK, W_MQK), lambda b, c: (b * nc + c, 1)),
            pl.BlockSpec((CHUNK, W_MV), lambda b, c: (b * nc + c, 2 * W_MQK // W_MV)),
            pl.BlockSpec((CHUNK, W_MV), lambda b, c: (b * nc + c, 2 * W_MQK // W_MV + 1)),
            pl.BlockSpec((M_HEADS, 2, CHUNK), lambda b, c: (0, 0, b * nc + c)),
        ],
        out_specs=[
            pl.BlockSpec((CHUNK, W_MV), lambda b, c: (b * nc + c, 0)),
            pl.BlockSpec((1, M_HEADS, M_DV, M_DQK), lambda b, c: (b, 0, 0, 0)),
            pl.BlockSpec((1, M_HEADS, M_DQK), lambda b, c: (b, 0, 0)),
            pl.BlockSpec((1, M_HEADS, LANES), lambda b, c: (b, 0, 0)),
        ],
        out_shape=[
            jax.ShapeDtypeStruct((N_PROMPT, W_MV), BF16),
            jax.ShapeDtypeStruct((BATCH, M_HEADS, M_DV, M_DQK), F32),
            jax.ShapeDtypeStruct((BATCH, M_HEADS, M_DQK), F32),
            jax.ShapeDtypeStruct((BATCH, M_HEADS, LANES), F32),
        ],
        compiler_params=_params(2),
        name="mlstm_prompt",
    )(gate_bias, z_main, z_main, z_main, z_main, gates_rows)


def _mlstm_sample_body(bias_ref, q_ref, k_ref, v_ref, mo_ref, g_ref, c0_ref, n0_ref, m0_ref,
                       a_ref, c_ref, n_ref, m_ref, q_scr, k_scr, v_scr):
    @pl.when(pl.program_id(0) == 0)
    def _():
        q_scr[...] = jnp.zeros_like(q_scr)
        k_scr[...] = jnp.zeros_like(k_scr)
        v_scr[...] = jnp.zeros_like(v_scr)

    for s in range(SAMPLE_SEQS_PER_STEP):
        rows = pl.ds(s * DEC_SEQ, DEC_SEQ)
        q_scr[s, 0:DEC_SEQ, :] = q_ref[rows, :]
        k_scr[s, 0:DEC_SEQ, :] = k_ref[rows, :] * (M_DQK ** -0.5)
        v_scr[s, 0:DEC_SEQ, :] = v_ref[rows, :]
        for head in range(M_HEADS):
            qk = pl.ds(head * M_DQK, M_DQK)
            vo = pl.ds(head * M_DV, M_DV)
            one = pl.ds(head, 1)
            irow, frow = _gate_rows(g_ref[s, head], bias_ref, head)
            h, c_new, n_new, m_end = _mlstm_chunk(
                q_scr[s, :, qk], k_scr[s, :, qk], v_scr[s, :, vo], irow, frow,
                c0_ref[0, s, head], n0_ref[0, s, one, :], m0_ref[s, one, :], DEC_SEQ)
            a_ref[rows, vo] = _sigmoid(mo_ref[rows, vo]) * h[0:DEC_SEQ, :]
            c_ref[s, head] = c_new
            n_ref[s, one, :] = n_new
            m_ref[s, one, :] = jnp.broadcast_to(m_end, (1, LANES))


def _mlstm_sample(z_main, gates_rows, gate_bias, c0, n0, m0):
    ns = SAMPLE_SEQS_PER_STEP
    rows = ns * DEC_SEQ
    assert rows % SUBLANES == 0 and N_PROMPT % rows == 0 and DEC_BATCH % ns == 0
    r0 = N_PROMPT // rows
    return pl.pallas_call(
        _mlstm_sample_body,
        grid=(DEC_BATCH // ns,),
        in_specs=[
            pl.BlockSpec(memory_space=pltpu.SMEM),
            pl.BlockSpec((rows, W_MQK), lambda b: (r0 + b, 0)),
            pl.BlockSpec((rows, W_MQK), lambda b: (r0 + b, 1)),
            pl.BlockSpec((rows, W_MV), lambda b: (r0 + b, 2 * W_MQK // W_MV)),
            pl.BlockSpec((rows, W_MV), lambda b: (r0 + b, 2 * W_MQK // W_MV + 1)),
            pl.BlockSpec((ns, M_HEADS, 2, SAMPLE_CHUNK), lambda b: (b, 0, 0, 0)),
            pl.BlockSpec((1, ns, M_HEADS, M_DV, M_DQK), lambda b: (0, b, 0, 0, 0)),
            pl.BlockSpec((1, ns, M_HEADS, M_DQK), lambda b: (0, b, 0, 0)),
            pl.BlockSpec((ns, M_HEADS, 1), lambda b: (b, 0, 0)),
        ],
        out_specs=[
            pl.BlockSpec((rows, W_MV), lambda b: (b, 0)),
            pl.BlockSpec((ns, M_HEADS, M_DV, M_DQK), lambda b: (b, 0, 0, 0)),
            pl.BlockSpec((ns, M_HEADS, M_DQK), lambda b: (b, 0, 0)),
            pl.BlockSpec((ns, M_HEADS, LANES), lambda b: (b, 0, 0)),
        ],
        out_shape=[
            jax.ShapeDtypeStruct((N_SAMPLE, W_MV), F32),
            jax.ShapeDtypeStruct((DEC_BATCH, M_HEADS, M_DV, M_DQK), F32),
            jax.ShapeDtypeStruct((DEC_BATCH, M_HEADS, M_DQK), F32),
            jax.ShapeDtypeStruct((DEC_BATCH, M_HEADS, LANES), F32),
        ],
        scratch_shapes=[pltpu.VMEM((ns, SAMPLE_CHUNK, W_MQK), F32), pltpu.VMEM((ns, SAMPLE_CHUNK, W_MQK), F32),
                        pltpu.VMEM((ns, SAMPLE_CHUNK, W_MV), F32)],
        compiler_params=_params(1),
        name="mlstm_sample",
    )(gate_bias, z_main, z_main, z_main, z_main, gates_rows, c0, n0, m0)


def _dil_prompt_body(q_ref, kp_ref, kc_ref, vp_ref, vc_ref, o_ref, l_ref, *, r, hb):
    first_key = jnp.where(pl.program_id(2) > 0, 0, SPAN)
    qi = lax.broadcasted_iota(jnp.int32, (SPAN, 2 * SPAN), 0)
    ki = lax.broadcasted_iota(jnp.int32, (SPAN, 2 * SPAN), 1)
    ok = (ki >= qi) & (ki <= qi + SPAN) & (ki >= first_key)
    bias = jnp.where(ok, 0.0, NEG_INF)
    scale = A_HEAD_DIM ** -0.5
    nt = (((1,), (1,)), ((), ()))
    for c in range(r):
        rows = pl.ds(c, SPAN, stride=r) if r > 1 else pl.ds(0, SPAN)
        for hh in range(hb):
            cols = pl.ds(hh * A_HEAD_DIM, A_HEAD_DIM)
            q = q_ref[rows, cols].astype(BF16)
            kk = jnp.concatenate([kp_ref[rows, cols], kc_ref[rows, cols]], axis=0).astype(BF16)
            vv = jnp.concatenate([vp_ref[rows, cols], vc_ref[rows, cols]], axis=0).astype(BF16)
            s = lax.dot_general(q, kk, nt, preferred_element_type=F32) * scale + bias
            m = jnp.max(s, axis=1, keepdims=True)
            p = jnp.exp(s - m)
            den = jnp.sum(p, axis=1, keepdims=True)
            o = jnp.dot(p.astype(BF16), vv, preferred_element_type=F32) / den
            o_ref[rows, cols] = o
            l_ref[rows, cols] = jnp.broadcast_to(m + jnp.log(den), (SPAN, A_HEAD_DIM))


def _dilated_prompt(z_aqkv, g, r, hb):
    rows = SPAN * r
    nblk = SEQ // rows
    wcol = hb * A_HEAD_DIM
    per_part = W_AH // wcol

    def spec(part, prev):
        def imap(b, hg, n):
            nn = jnp.maximum(n - 1, 0) if prev else n
            return (b * nblk + nn, (3 * g + part) * per_part + hg)
        return pl.BlockSpec((rows, wcol), imap)

    out_spec = pl.BlockSpec((rows, wcol), lambda b, hg, n: (b * nblk + n, hg))
    return pl.pallas_call(
        functools.partial(_dil_prompt_body, r=r, hb=hb),
        grid=(BATCH, A_HEADS // hb, nblk),
        in_specs=[spec(0, False), spec(1, True), spec(1, False), spec(2, True), spec(2, False)],
        out_specs=[out_spec, out_spec],
        out_shape=[jax.ShapeDtypeStruct((N_PROMPT, W_AH), F32)] * 2,
        compiler_params=_params(3),
        name=f"dilated_prompt_g{g}",
    )(z_aqkv, z_aqkv, z_aqkv, z_aqkv, z_aqkv)


def _sample_attn_body(qkv_ref, k0_ref, v0_ref, k1_ref, v1_ref, k2_ref, v2_ref, b0_ref, b1_ref, b2_ref,
                      xq_ref, mk_ref, mv_ref, bx_ref, yb_ref, yc_ref):
    nt = (((1,), (1,)), ((), ()))
    outs, lses = [], []
    groups = ((k0_ref, v0_ref, b0_ref), (k1_ref, v1_ref, b1_ref), (k2_ref, v2_ref, b2_ref))
    for g, (kc_ref, vc_ref, bias_ref) in enumerate(groups):
        n_cached = kc_ref.shape[1] * kc_ref.shape[2]
        q = qkv_ref[0, 3 * g].astype(BF16)
        kk = jnp.concatenate([kc_ref[0].reshape(n_cached, A_HEAD_DIM), qkv_ref[0, 3 * g + 1]],
                             axis=0).astype(BF16)
        vv = jnp.concatenate([vc_ref[0].reshape(n_cached, A_HEAD_DIM), qkv_ref[0, 3 * g + 2]],
                             axis=0).astype(BF16)
        s = lax.dot_general(q, kk, nt, preferred_element_type=F32) * (A_HEAD_DIM ** -0.5) + bias_ref[...]
        m = jnp.max(s, axis=1, keepdims=True)
        p = jnp.exp(s - m)
        den = jnp.sum(p, axis=1, keepdims=True)
        outs.append(jnp.dot(p.astype(BF16), vv, preferred_element_type=F32) / den)
        lses.append(m + jnp.log(den))
    mx = jnp.maximum(jnp.maximum(lses[0], lses[1]), lses[2])
    e = [jnp.exp(l - mx) for l in lses]
    tot = e[0] + e[1] + e[2]
    yb = (e[0] / tot) * outs[0] + (e[1] / tot) * outs[1] + (e[2] / tot) * outs[2]
    yb_ref[0] = yb.astype(yb_ref.dtype)

    s = lax.dot_general(xq_ref[0].astype(BF16), mk_ref[0].astype(BF16), nt,
                        preferred_element_type=F32) * (X_HEAD_DIM ** -0.5) + bx_ref[...]
    m = jnp.max(s, axis=1, keepdims=True)
    ex = jnp.exp(s - m)
    p = ex / jnp.sum(ex, axis=1, keepdims=True)
    yc_ref[0] = jnp.dot(p.astype(BF16), mv_ref[0].astype(BF16),
                        preferred_element_type=F32).astype(yc_ref.dtype)


def _dilated_sample_bias(window, r, lb, rc):
    span = window // r
    n_c = (lb // r) * rc * A_HEADS
    rows = np.arange(DEC_SEQ * A_HEADS)
    s_q, h_q = rows // A_HEADS, rows % A_HEADS
    col = np.arange(n_c)
    m_k = col // (rc * A_HEADS)
    c_k = (col % (rc * A_HEADS)) // A_HEADS
    h_k = col % A_HEADS
    delta = (lb + s_q)[:, None] - (m_k * r + c_k)[None, :]
    ok_c = (h_q[:, None] == h_k[None, :]) & (delta % r == 0) & (delta // r <= span) & (delta >= 0)
    coln = np.arange(DEC_SEQ * A_HEADS)
    s_n, h_n = coln // A_HEADS, coln % A_HEADS
    dn = s_q[:, None] - s_n[None, :]
    ok_n = (h_q[:, None] == h_n[None, :]) & (dn >= 0) & (dn % r == 0) & (dn // r <= span)
    ok = np.concatenate([ok_c, ok_n], axis=1)
    return np.where(ok, 0.0, -np.inf).astype(np.float32)


def _sample_attention(qkv, caches, xq, mem_k, mem_v):
    rq = DEC_SEQ * A_HEADS
    rx = DEC_SEQ * X_HEADS
    nk = MEM_LEN * X_HEADS
    cache_args, cache_specs, biases = [], [], []
    for (window, r), (cache_k, cache_v) in zip(DIL_GROUPS, caches):
        lb = cache_k.shape[1]
        assert lb % r == 0 and window % r == 0
        rc = min(r, DEC_SEQ)
        nm = lb // r
        spec = pl.BlockSpec((1, nm, rc * A_HEADS, A_HEAD_DIM), lambda b: (b, 0, 0, 0))
        for c in (cache_k, cache_v):
            cache_args.append(c.reshape(DEC_BATCH, nm, r * A_HEADS, A_HEAD_DIM))
            cache_specs.append(spec)
        biases.append(jnp.asarray(_dilated_sample_bias(window, r, lb, rc)))
    ok = (np.arange(rx) % X_HEADS)[:, None] == (np.arange(nk) % X_HEADS)[None, :]
    bias_x = jnp.asarray(np.where(ok, 0.0, -np.inf).astype(np.float32))

    def const_spec(a):
        return pl.BlockSpec(a.shape, lambda b: (0, 0))

    return pl.pallas_call(
        _sample_attn_body,
        grid=(DEC_BATCH,),
        in_specs=[
            pl.BlockSpec((1, 3 * N_GROUPS, rq, A_HEAD_DIM), lambda b: (b, 0, 0, 0)),
            *cache_specs,
            *[const_spec(a) for a in biases],
            pl.BlockSpec((1, rx, X_HEAD_DIM), lambda b: (b, 0, 0)),
            pl.BlockSpec((1, nk, X_HEAD_DIM), lambda b: (b, 0, 0)),
            pl.BlockSpec((1, nk, X_HEAD_DIM), lambda b: (b, 0, 0)),
            const_spec(bias_x),
        ],
        out_specs=[pl.BlockSpec((1, rq, A_HEAD_DIM), lambda b: (b, 0, 0)),
                   pl.BlockSpec((1, rx, X_HEAD_DIM), lambda b: (b, 0, 0))],
        out_shape=[jax.ShapeDtypeStruct((DEC_BATCH, rq, A_HEAD_DIM), BF16),
                   jax.ShapeDtypeStruct((DEC_BATCH, rx, X_HEAD_DIM), BF16)],
        compiler_params=_params(1),
        name="sample_attention",
    )(qkv, *cache_args, *biases, xq, mem_k, mem_v, bias_x)


def _cross_prompt_body(q_ref, k_ref, v_ref, o_ref):
    scale = X_HEAD_DIM ** -0.5
    nt = (((1,), (1,)), ((), ()))
    for h in range(X_HEADS):
        cols = pl.ds(h * X_HEAD_DIM, X_HEAD_DIM)
        s = lax.dot_general(q_ref[:, cols].astype(BF16), k_ref[:, cols].astype(BF16), nt,
                            preferred_element_type=F32) * scale
        m = jnp.max(s, axis=1, keepdims=True)
        e = jnp.exp(s - m)
        p = e / jnp.sum(e, axis=1, keepdims=True)
        o_ref[:, cols] = jnp.dot(p.astype(BF16), v_ref[:, cols].astype(BF16),
                                 preferred_element_type=F32).astype(o_ref.dtype)


def _cross_prompt(z_xq, mem_kv, *, tq=512):
    nq = SEQ // tq
    return pl.pallas_call(
        _cross_prompt_body,
        grid=(BATCH, nq),
        in_specs=[
            pl.BlockSpec((tq, W_XQ), lambda b, i: (b * nq + i, 0)),
            pl.BlockSpec((MEM_LEN, W_XQ), lambda b, i: (b, 0)),
            pl.BlockSpec((MEM_LEN, W_XQ), lambda b, i: (b, 1)),
        ],
        out_specs=pl.BlockSpec((tq, W_XQ), lambda b, i: (b * nq + i, 0)),
        out_shape=jax.ShapeDtypeStruct((N_PROMPT, W_XQ), BF16),
        compiler_params=_params(2),
        name="cross_prompt",
    )(z_xq, mem_kv, mem_kv)


def _combine_body(o0, o1, o2, l0, l1, l2, y_ref):
    a0, a1, a2 = l0[...], l1[...], l2[...]
    mx = jnp.maximum(jnp.maximum(a0, a1), a2)
    e0, e1, e2 = jnp.exp(a0 - mx), jnp.exp(a1 - mx), jnp.exp(a2 - mx)
    tot = e0 + e1 + e2
    y = (e0 / tot) * o0[...] + (e1 / tot) * o1[...] + (e2 / tot) * o2[...]
    y_ref[...] = y.astype(y_ref.dtype)


def _combine_groups(outs, lses, *, rows_per_step):
    n, w = outs[0].shape
    spec = pl.BlockSpec((rows_per_step, w), lambda i: (i, 0))
    return pl.pallas_call(
        _combine_body,
        grid=(n // rows_per_step,),
        in_specs=[spec] * 6,
        out_specs=spec,
        out_shape=jax.ShapeDtypeStruct((n, w), BF16),
        compiler_params=_params(1),
        name="combine_groups",
    )(*outs, *lses)


def _rope_tables():
    pos = jnp.concatenate([
        jnp.tile(jnp.arange(SEQ, dtype=jnp.int32), BATCH),
        jnp.tile(PAST_LEN + jnp.arange(DEC_SEQ, dtype=jnp.int32), DEC_BATCH)])
    inv = ROPE_THETA ** (-jnp.arange(0, A_HEAD_DIM, 2, dtype=F32) / A_HEAD_DIM)
    ang = pos.astype(F32)[:, None] * inv[None, :]
    cos, sin = jnp.cos(ang), jnp.sin(ang)
    return jnp.concatenate([cos, cos], axis=1), jnp.concatenate([-sin, sin], axis=1)


def _layer(x_p, x_s, mem_prompt, state_c, state_n, state_m, caches, cache_mem_k, cache_mem_v,
           g_mix, w_in, b_igate, b_fgate, b_branch, g_mem, w_mem_kv,
           w_br_a, w_br_b, w_br_c, w_out, g_mlp, w_up, w_down):
    h_all = _rmsnorm_two_groups(x_p, x_s, g_mix, BF16)

    w_in_t = w_in.T
    z_main = _matmul_stream(h_all, w_in_t, n_cols=COL_GATES, b_rows_are_outputs=True, name="proj_main")
    z_gate = _matmul(h_all, w_in_t, n_cols=2 * M_HEADS, col_start=COL_GATES, tn=2 * M_HEADS,
                     b_rows_are_outputs=True, name="proj_gates")
    cos2, sin2 = _rope_tables()
    table_spec = pl.BlockSpec((MM_TM, A_HEAD_DIM), lambda j, i: (i, 0))
    z_aqkv = _matmul_stream(h_all, w_in_t, n_cols=W_AQKV, col_start=COL_REST, b_rows_are_outputs=True,
                            epilogue=_ep_rope, extras=(cos2, sin2), extra_specs=(table_spec, table_spec),
                            name="proj_attn")
    z_xq = _matmul(h_all, w_in_t, n_cols=W_XQ, col_start=COL_REST + W_AQKV, b_rows_are_outputs=True,
                   name="proj_cross_q")
    z_br = _matmul_stream(h_all, w_in_t, n_cols=3 * D_MODEL, col_start=COL_REST + W_AQKV + W_XQ,
                          b_rows_are_outputs=True, name="proj_branch_gates")

    gate_bias = jnp.stack([b_igate, b_fgate]).astype(F32)

    gp = z_gate[:N_PROMPT].reshape(N_PROMPT, 2, M_HEADS).transpose(2, 1, 0)
    a_p, c_p, n_p, m_p = _mlstm_prompt(z_main, gp, gate_bias)
    gs = z_gate[N_PROMPT:].reshape(DEC_BATCH, DEC_SEQ, 2, M_HEADS).transpose(0, 3, 2, 1)
    gs = jnp.pad(gs, ((0, 0), (0, 0), (0, 0), (0, SAMPLE_CHUNK - DEC_SEQ)))
    a_s, c_s, n_s, m_s = _mlstm_sample(
        z_main, gs, gate_bias, state_c, state_n, state_m.reshape(DEC_BATCH, M_HEADS, 1))
    a_ps = (a_p, a_s.astype(BF16))

    heads_per_step = (8, 1, 1)
    outs_p, lses_p, rows_p, rows_s = [], [], [], []
    rq = DEC_SEQ * A_HEADS
    qkv_s = z_aqkv[N_PROMPT:].reshape(DEC_BATCH, DEC_SEQ, 3 * N_GROUPS, A_HEADS, A_HEAD_DIM)
    qkv_s = qkv_s.transpose(0, 2, 1, 3, 4).reshape(DEC_BATCH, 3 * N_GROUPS, rq, A_HEAD_DIM)
    for g, (window, r) in enumerate(DIL_GROUPS):
        o, l = _dilated_prompt(z_aqkv, g, r, heads_per_step[g])
        outs_p.append(o)
        lses_p.append(l)
        c0 = 3 * g * W_AH
        keep = min(window, SEQ)
        for part in (1, 2):
            cs = c0 + part * W_AH
            kept = [z_aqkv[(b + 1) * SEQ - keep:(b + 1) * SEQ, cs:cs + W_AH] for b in range(BATCH)]
            rows_p.append(jnp.stack(kept).reshape(BATCH, keep, A_HEADS, A_HEAD_DIM))
            rows_s.append(qkv_s[:, 3 * g + part].reshape(DEC_BATCH, DEC_SEQ, A_HEADS, A_HEAD_DIM))
    yb_p = _combine_groups(outs_p, lses_p, rows_per_step=256)

    mem_h = _rmsnorm(mem_prompt, g_mem, BF16)
    mem_kv = _matmul(mem_h, w_mem_kv, n_cols=2 * W_XQ, tm=BATCH * MEM_LEN, name="mem_kv")
    yc_p = _cross_prompt(z_xq, mem_kv)

    xq_s = z_xq[N_PROMPT:].reshape(DEC_BATCH, DEC_SEQ * X_HEADS, X_HEAD_DIM)
    yb_s, yc_s = _sample_attention(
        qkv_s, caches, xq_s,
        cache_mem_k.reshape(DEC_BATCH, MEM_LEN * X_HEADS, X_HEAD_DIM),
        cache_mem_v.reshape(DEC_BATCH, MEM_LEN * X_HEADS, X_HEAD_DIM))
    yb_ps = (yb_p, yb_s.reshape(N_SAMPLE, W_AH))
    yc_ps = (yc_p, yc_s.reshape(N_SAMPLE, W_XQ))

    merged = _branch_merge(a_ps, yb_ps, yc_ps, w_br_a, w_br_b, w_br_c, z_br, b_branch.reshape(1, 3 * D_MODEL))
    x1 = _matmul_stream(merged, w_out, n_cols=D_MODEL, tn=MM_TN, epilogue=_ep_residual_two_groups, extras=(x_p, x_s),
                        extra_specs=_two_group_specs(MM_TM, MM_TN, lambda j: j, SUBLANES, False),
                        name="out_proj")

    h2 = _rmsnorm(x1, g_mlp, BF16)
    u = _matmul_stream(h2, w_up, n_cols=D_FF, out_dtype=BF16, epilogue=_ep_relu2, name="mlp_up")
    x2 = _matmul_ksplit_residual(u, w_down, x1)

    mem_k = mem_kv[:, :W_XQ].reshape(BATCH, MEM_LEN, X_HEADS, X_HEAD_DIM)
    mem_v = mem_kv[:, W_XQ:].reshape(BATCH, MEM_LEN, X_HEADS, X_HEAD_DIM)
    prompt_state = (c_p, n_p, m_p[:, :, 0])
    sample_state = (c_s, n_s, m_s[:, :, 0])
    return x2, prompt_state, sample_state, rows_p, rows_s, mem_k, mem_v


def kernel(x_prompt, x_sample, state_mlstm_C, state_mlstm_n, state_mlstm_m,
           cache_win_k_g0, cache_win_v_g0, cache_win_k_g1, cache_win_v_g1,
           cache_win_k_g2, cache_win_v_g2, cache_mem_k, cache_mem_v, mem_prompt,
           g_mix, w_in, b_igate, b_fgate, b_branch, g_mem, w_mem_kv,
           w_br_a, w_br_b, w_br_c, w_out, g_mlp, w_up, w_down, g_final):
    depth = g_mix.shape[0]
    assert depth == 1, "single-layer stack"
    caches = ((cache_win_k_g0[0], cache_win_v_g0[0]),
              (cache_win_k_g1[0], cache_win_v_g1[0]),
              (cache_win_k_g2[0], cache_win_v_g2[0]))
    x2, p_state, s_state, rows_p, rows_s, mem_k, mem_v = _layer(
        x_prompt.reshape(N_PROMPT, D_MODEL), x_sample.reshape(N_SAMPLE, D_MODEL),
        mem_prompt.reshape(BATCH * MEM_LEN, D_MODEL),
        state_mlstm_C, state_mlstm_n, state_mlstm_m[0], caches, cache_mem_k[0], cache_mem_v[0],
        g_mix[0], w_in[0], b_igate[0], b_fgate[0], b_branch[0], g_mem[0], w_mem_kv[0],
        w_br_a[0], w_br_b[0], w_br_c[0], w_out[0], g_mlp[0], w_up[0], w_down[0])
    y_prompt = _rmsnorm(x2, g_final, F32, row_start=0, n_rows=N_PROMPT).reshape(BATCH, SEQ, D_MODEL)
    y_sample = _rmsnorm(x2, g_final, F32, row_start=N_PROMPT, n_rows=N_SAMPLE).reshape(DEC_BATCH, DEC_SEQ, D_MODEL)
    lead = lambda a: a[None]
    return (y_prompt, y_sample,
            lead(p_state[0]), lead(p_state[1]), lead(p_state[2]),
            *[lead(r) for r in rows_p],
            lead(mem_k), lead(mem_v),
            lead(s_state[0]), lead(s_state[1]), lead(s_state[2]),
            *[lead(r) for r in rows_s])
```

```python
import functools
import math
from typing import Callable, NamedTuple

import numpy as np
import jax
import jax.numpy as jnp
from jax import lax
from jax.experimental import pallas as pl
from jax.experimental.pallas import tpu as pltpu

F32 = jnp.float32
BF16 = jnp.bfloat16
NEG_INF = float("-inf")

D_MODEL = 4096
BATCH = 2
SEQ = 4096
DEC_BATCH = 128
DEC_SEQ = 4
PAST_LEN = 2048
MEM_LEN = 256
M_HEADS = 8
M_DQK = D_MODEL // (2 * M_HEADS)
M_DV = D_MODEL // M_HEADS
A_HEAD_DIM = 128
A_HEADS = D_MODEL // 512
DIL_GROUPS = ((128, 1), (512, 4), (2048, 16))
N_GROUPS = 3
ROPE_THETA = 10000.0
X_HEADS = 4
X_HEAD_DIM = 128
D_FF = 4 * D_MODEL
EPS = 1e-6

W_MQK = M_HEADS * M_DQK
W_MV = M_HEADS * M_DV
W_AH = A_HEADS * A_HEAD_DIM
W_XQ = X_HEADS * X_HEAD_DIM
N_PROMPT = BATCH * SEQ
N_SAMPLE = DEC_BATCH * DEC_SEQ
N_ALL = N_PROMPT + N_SAMPLE

COL_GATES = 2 * W_MQK + 2 * W_MV
COL_REST = COL_GATES + 2 * M_HEADS
W_AQKV = 3 * N_GROUPS * W_AH

LANES = 128
SUBLANES = 8
VMEM_LIMIT_BYTES = 60 * 1024 * 1024

MM_TM = 1088
MM_TN = 512
MM_TN_WIDE = 1024
MERGE_TM = 544
HOST_TM = 544
DOWN_TM = 2176
DOWN_TK = 2048
NORM_ROWS = 256
CHUNK = 128
SAMPLE_CHUNK = 16
SAMPLE_SEQS_PER_BLOCK = 2
SPAN = 128


def _params(n_axes):
    return pltpu.CompilerParams(
        dimension_semantics=("arbitrary",) * n_axes,
        vmem_limit_bytes=VMEM_LIMIT_BYTES,
    )


def _rmsnorm_body(x_ref, g_ref, o_ref):
    x = x_ref[...]
    ms = jnp.mean(x * x, axis=-1, keepdims=True)
    o_ref[...] = ((x * lax.rsqrt(ms + EPS)) * g_ref[...]).astype(o_ref.dtype)


def _rmsnorm(x, g, out_dtype, *, row_start=0, n_rows=None):
    n_rows = x.shape[0] if n_rows is None else n_rows
    d = x.shape[1]
    off = row_start // NORM_ROWS
    return pl.pallas_call(
        _rmsnorm_body,
        grid=(n_rows // NORM_ROWS,),
        in_specs=[
            pl.BlockSpec((NORM_ROWS, d), lambda i: (i + off, 0)),
            pl.BlockSpec((1, d), lambda i: (0, 0)),
        ],
        out_specs=pl.BlockSpec((NORM_ROWS, d), lambda i: (i, 0)),
        out_shape=jax.ShapeDtypeStruct((n_rows, d), out_dtype),
        compiler_params=_params(1),
        name="rmsnorm",
    )(x, g.reshape(1, d))


def _rmsnorm_two_groups_body(xp_ref, xs_ref, g_ref, o_ref, *, prompt_steps):
    i = pl.program_id(0)

    @pl.when(i < prompt_steps)
    def _():
        _rmsnorm_body(xp_ref, g_ref, o_ref)

    @pl.when(i >= prompt_steps)
    def _():
        _rmsnorm_body(xs_ref, g_ref, o_ref)


def _rmsnorm_two_groups(xp, xs, g, out_dtype):
    d = xp.shape[1]
    ps, ss = xp.shape[0] // NORM_ROWS, xs.shape[0] // NORM_ROWS
    return pl.pallas_call(
        functools.partial(_rmsnorm_two_groups_body, prompt_steps=ps),
        grid=(ps + ss,),
        in_specs=[
            pl.BlockSpec((NORM_ROWS, d), lambda i: (jnp.minimum(i, ps - 1), 0)),
            pl.BlockSpec((NORM_ROWS, d), lambda i: (jnp.maximum(i - ps, 0), 0)),
            pl.BlockSpec((1, d), lambda i: (0, 0)),
        ],
        out_specs=pl.BlockSpec((NORM_ROWS, d), lambda i: (i, 0)),
        out_shape=jax.ShapeDtypeStruct((xp.shape[0] + xs.shape[0], d), out_dtype),
        compiler_params=_params(1),
        name="rmsnorm_two_groups",
    )(xp, xs, g.reshape(1, d))


def _ep_plain(acc, extra, o_ref, j):
    o_ref[...] = acc.astype(o_ref.dtype)


def _ep_residual_two_groups(acc, extra, o_ref, j):
    xp_ref, xs_ref = extra
    tm = acc.shape[0]
    n_full = N_PROMPT // tm
    rem = N_PROMPT - n_full * tm
    assert tm - rem == N_SAMPLE and rem % SUBLANES == 0
    i = pl.program_id(1)

    @pl.when(i < n_full)
    def _():
        o_ref[...] = xp_ref[...] + acc

    @pl.when(i >= n_full)
    def _():
        o_ref[0:rem, :] = xp_ref[tm - rem:tm, :] + acc[0:rem]
        o_ref[rem:tm, :] = xs_ref[...] + acc[rem:tm]


def _two_group_specs(tm, width, col_of_j, row_align, sample_block_is_constant):
    def p_map(j, i):
        row = jnp.minimum(i * tm, N_PROMPT - tm)
        col = col_of_j(j) * width
        return (pl.multiple_of(row, row_align), col if isinstance(col, int) else pl.multiple_of(col, LANES))
    p_spec = pl.BlockSpec((pl.Element(tm), pl.Element(width)), p_map)
    mode = pl.Buffered(1) if sample_block_is_constant else None
    s_spec = pl.BlockSpec((N_SAMPLE, width), lambda j, i: (0, col_of_j(j)), pipeline_mode=mode)
    return p_spec, s_spec


def _ep_relu2(acc, extra, o_ref, j):
    r = jnp.maximum(acc, 0.0)
    o_ref[...] = (r * r).astype(o_ref.dtype)


def _ep_rope(acc, extra, o_ref, j):
    cos_ref, sin_ref = extra
    tn = acc.shape[1]
    assert W_AH % tn == 0
    is_rope = (j // (W_AH // tn)) % 3 != 2
    cos2 = cos_ref[...]
    sin2 = sin_ref[...]
    for s in range(tn // A_HEAD_DIM):
        x = acc[:, s * A_HEAD_DIM:(s + 1) * A_HEAD_DIM]
        rotated = x * cos2 + pltpu.roll(x, A_HEAD_DIM // 2, axis=1) * sin2
        o_ref[:, s * A_HEAD_DIM:(s + 1) * A_HEAD_DIM] = jnp.where(is_rope, rotated, x)


def _mm_body(*refs, b_rows_are_outputs, epilogue, n_extra):
    a_ref, b_ref = refs[0], refs[1]
    extra = refs[2:2 + n_extra]
    o_ref = refs[2 + n_extra]
    b_scr = refs[3 + n_extra]

    @pl.when(pl.program_id(1) == 0)
    def _():
        b_scr[...] = b_ref[...].astype(BF16)

    if b_rows_are_outputs:
        acc = lax.dot_general(a_ref[...], b_scr[...], (((1,), (1,)), ((), ())),
                              preferred_element_type=F32)
    else:
        acc = jnp.dot(a_ref[...], b_scr[...], preferred_element_type=F32)
    epilogue(acc, extra, o_ref, pl.program_id(0))


def _matmul(a, b, *, n_cols, col_start=0, b_rows_are_outputs=False, tm=MM_TM, tn=MM_TN,
            out_dtype=F32, epilogue=_ep_plain, extras=(), extra_specs=(), name="matmul"):
    m, k = a.shape
    if b_rows_are_outputs:
        if col_start % tn == 0:
            b_spec = pl.BlockSpec((tn, k), lambda j, i: (j + col_start // tn, 0))
        else:
            assert col_start % SUBLANES == 0 and tn % SUBLANES == 0
            b_spec = pl.BlockSpec((pl.Element(tn), pl.Element(k)),
                                  lambda j, i: (pl.multiple_of(col_start + j * tn, SUBLANES), 0))
        b_block = (tn, k)
    else:
        assert col_start % tn == 0
        b_spec = pl.BlockSpec((k, tn), lambda j, i: (0, j + col_start // tn))
        b_block = (k, tn)
    return pl.pallas_call(
        functools.partial(_mm_body, b_rows_are_outputs=b_rows_are_outputs, epilogue=epilogue,
                          n_extra=len(extras)),
        grid=(n_cols // tn, m // tm),
        in_specs=[pl.BlockSpec((tm, k), lambda j, i: (i, 0)), b_spec, *extra_specs],
        out_specs=pl.BlockSpec((tm, tn), lambda j, i: (i, j)),
        out_shape=jax.ShapeDtypeStruct((m, n_cols), out_dtype),
        scratch_shapes=[pltpu.VMEM(b_block, BF16)],
        compiler_params=_params(2),
        name=name,
    )(a, b, *extras)


class _SideJob(NamedTuple):
    n_steps: int
    inputs: tuple
    in_specs: tuple
    out_shapes: tuple
    out_specs: tuple
    scratch_shapes: tuple
    init: Callable
    step: Callable


def _mm_stream_body(*refs, b_rows_are_outputs, epilogue, n_extra, slice_rows, split_cols, side):
    n_side_in = len(side.inputs) if side else 0
    n_side_out = len(side.out_shapes) if side else 0
    n_in = 2 + n_extra + n_side_in
    a_ref, bs_ref = refs[0], refs[1]
    extra = refs[2:2 + n_extra]
    side_in = refs[2 + n_extra:n_in]
    o_ref = refs[n_in]
    side_out = refs[n_in + 1:n_in + 1 + n_side_out]
    b_scr = refs[n_in + 1 + n_side_out]
    side_scr = refs[n_in + 2 + n_side_out:]
    p, i = pl.program_id(0), pl.program_id(1)

    r0 = pl.multiple_of(i * slice_rows, slice_rows)
    b_scr[p % 2, pl.ds(r0, slice_rows), :] = bs_ref[...].astype(BF16)

    if side is not None:
        @pl.when((p == 0) & (i == 0))
        def _():
            side.init(side_scr)

        t = (p - 1) * pl.num_programs(1) + i

        @pl.when((p > 0) & (t < side.n_steps))
        def _():
            side.step(t, side_in, side_out, side_scr)

    @pl.when(p > 0)
    def _():
        slot = (p - 1) % 2
        a = a_ref[...]
        tn = o_ref.shape[1]
        n_split = tn // split_cols
        for h in range(n_split):
            cols = pl.ds(h * split_cols, split_cols)
            if b_rows_are_outputs:
                acc = lax.dot_general(a, b_scr[slot, cols, :], (((1,), (1,)), ((), ())),
                                      preferred_element_type=F32)
            else:
                acc = jnp.dot(a, b_scr[slot, :, cols], preferred_element_type=F32)
            epilogue(acc, extra, o_ref.at[:, cols], (p - 1) * n_split + h)


def _matmul_stream(a, b, *, n_cols, col_start=0, b_rows_are_outputs=False, tm=MM_TM, tn=MM_TN_WIDE,
                   out_dtype=F32, epilogue=_ep_plain, extras=(), extra_specs=(), side=None, name="matmul"):
    m, k = a.shape
    ni, nj = m // tm, n_cols // tn
    assert col_start % SUBLANES == 0 and nj * tn == n_cols

    def side_spec(spec):
        if isinstance(spec, pl.BlockSpec):
            return spec
        block_shape, index_fn = spec
        return pl.BlockSpec(
            block_shape,
            lambda p, i: index_fn(jnp.clip((p - 1) * ni + i, 0, side.n_steps - 1)))

    side_in_specs = [side_spec(s) for s in side.in_specs] if side else []
    side_out_specs = [side_spec(s) for s in side.out_specs] if side else []
    assert side is None or side.n_steps <= nj * ni

    def staged(p):
        return jnp.minimum(p, nj - 1)

    def shifted(index_map):
        return lambda p, i: index_map(jnp.maximum(p - 1, 0), jnp.where(p == 0, 0, i))

    if b_rows_are_outputs:
        tile = (tn, k)
        slice_rows = tn // ni
        bs_spec = pl.BlockSpec(
            (pl.Element(slice_rows), pl.Element(k)),
            lambda p, i: (pl.multiple_of(col_start + staged(p) * tn + i * slice_rows, SUBLANES), 0))
    else:
        assert col_start % tn == 0
        tile = (k, tn)
        slice_rows = k // ni
        bs_spec = pl.BlockSpec((slice_rows, tn), lambda p, i: (i, col_start // tn + staged(p)))
    assert slice_rows * ni == tile[0] and slice_rows % 16 == 0
    extra_specs = [pl.BlockSpec(s.block_shape, shifted(s.index_map), pipeline_mode=s.pipeline_mode)
                   for s in extra_specs]
    outs = pl.pallas_call(
        functools.partial(_mm_stream_body, b_rows_are_outputs=b_rows_are_outputs, epilogue=epilogue,
                          n_extra=len(extras), slice_rows=slice_rows, split_cols=min(tn, MM_TN), side=side),
        grid=(nj + 1, ni),
        in_specs=[pl.BlockSpec((tm, k), shifted(lambda j, i: (i, 0))), bs_spec, *extra_specs,
                  *side_in_specs],
        out_specs=[pl.BlockSpec((tm, tn), shifted(lambda j, i: (i, j))), *side_out_specs],
        out_shape=[jax.ShapeDtypeStruct((m, n_cols), out_dtype), *(side.out_shapes if side else ())],
        scratch_shapes=[pltpu.VMEM((2,) + tile, BF16), *(side.scratch_shapes if side else ())],
        compiler_params=_params(2),
        name=name,
    )(a, b, *extras, *(side.inputs if side else ()))
    return outs if side else outs[0]


def _mm_ksplit_body(a_ref, b_ref, res_ref, o_ref, acc_ref):
    kk = pl.program_id(2)

    @pl.when(kk == 0)
    def _():
        acc_ref[...] = jnp.zeros_like(acc_ref)

    acc_ref[...] += jnp.dot(a_ref[...], b_ref[...].astype(BF16), preferred_element_type=F32)

    @pl.when(kk == pl.num_programs(2) - 1)
    def _():
        o_ref[...] = res_ref[...] + acc_ref[...]


def _matmul_ksplit_residual(a, b, res, *, tm=DOWN_TM, tn=MM_TN, tk=DOWN_TK):
    m, k = a.shape
    n = b.shape[1]
    return pl.pallas_call(
        _mm_ksplit_body,
        grid=(n // tn, m // tm, k // tk),
        in_specs=[
            pl.BlockSpec((tm, tk), lambda j, i, kk: (i, kk)),
            pl.BlockSpec((tk, tn), lambda j, i, kk: (kk, j)),
            pl.BlockSpec((tm, tn), lambda j, i, kk: (i, j)),
        ],
        out_specs=pl.BlockSpec((tm, tn), lambda j, i, kk: (i, j)),
        out_shape=jax.ShapeDtypeStruct((m, n), F32),
        scratch_shapes=[pltpu.VMEM((tm, tn), F32)],
        compiler_params=_params(3),
        name="mlp_down",
    )(a, b, res)


def _sigmoid(x):
    return 1.0 / (1.0 + jnp.exp(-x))


def _straddle_rows(p_ref, s_ref):
    tm = p_ref.shape[0]
    rem = tm - s_ref.shape[0]
    return jnp.concatenate([p_ref[tm - rem:tm, :], s_ref[...]], axis=0)


def _merge_body(ap_ref, as_ref, ybp_ref, ybs_ref, ycp_ref, ycs_ref,
                wa0_ref, wb0_ref, wc0_ref, was_ref, wbs_ref, wcs_ref,
                ga_ref, gb_ref, gc_ref, ba_ref, bb_ref, bc_ref,
                o_ref, wa_scr, wb_scr, wc_scr):
    j, i = pl.program_id(0), pl.program_id(1)
    cur = j % 2

    @pl.when((j == 0) & (i == 0))
    def _():
        wa_scr[0] = wa0_ref[...].astype(BF16)
        wb_scr[0] = wb0_ref[...].astype(BF16)
        wc_scr[0] = wc0_ref[...].astype(BF16)

    for scr, slice_ref in ((wa_scr, was_ref), (wb_scr, wbs_ref), (wc_scr, wcs_ref)):
        rows = slice_ref.shape[0]
        scr[1 - cur, pl.ds(pl.multiple_of(i * rows, rows), rows), :] = slice_ref[...].astype(BF16)

    def merge(a, yb, yc):
        y_a = jnp.dot(a, wa_scr[cur], preferred_element_type=F32)
        y_b = jnp.dot(yb, wb_scr[cur], preferred_element_type=F32)
        y_c = jnp.dot(yc, wc_scr[cur], preferred_element_type=F32)
        g_a = _sigmoid(ga_ref[...] + ba_ref[...])
        g_b = _sigmoid(gb_ref[...] + bb_ref[...])
        g_c = _sigmoid(gc_ref[...] + bc_ref[...])
        o_ref[...] = (g_a * y_a + g_b * y_b + g_c * y_c).astype(o_ref.dtype)

    is_prompt_tile = i < N_PROMPT // ap_ref.shape[0]

    @pl.when(is_prompt_tile)
    def _():
        merge(ap_ref[...], ybp_ref[...], ycp_ref[...])

    @pl.when(jnp.logical_not(is_prompt_tile))
    def _():
        merge(_straddle_rows(ap_ref, as_ref), _straddle_rows(ybp_ref, ybs_ref), _straddle_rows(ycp_ref, ycs_ref))


def _branch_merge(a_ps, yb_ps, yc_ps, w_a, w_b, w_c, z_br, b_branch, *, tm=MERGE_TM, tn=MM_TN):
    m = N_ALL
    assert (N_PROMPT // tm + 1) * tm == N_ALL
    nj = D_MODEL // tn

    def gate_spec(part):
        return pl.BlockSpec((tm, tn), lambda j, i: (i, part * nj + j))

    def bias_spec(part):
        return pl.BlockSpec((1, tn), lambda j, i: (0, part * nj + j))

    ni = m // tm
    widths = (W_MV, W_AH, W_XQ)
    assert all(k % (16 * ni) == 0 for k in widths)

    def first_tile_spec(k):
        return pl.BlockSpec((k, tn), lambda j, i: (0, 0), pipeline_mode=pl.Buffered(1))

    def next_slice_spec(k):
        return pl.BlockSpec((k // ni, tn), lambda j, i: (i, jnp.minimum(j + 1, nj - 1)))

    bf16_rows = 16
    return pl.pallas_call(
        _merge_body,
        grid=(nj, ni),
        in_specs=[
            *_two_group_specs(tm, W_MV, lambda j: 0, bf16_rows, True),
            *_two_group_specs(tm, W_AH, lambda j: 0, bf16_rows, True),
            *_two_group_specs(tm, W_XQ, lambda j: 0, bf16_rows, True),
            *[first_tile_spec(k) for k in widths],
            *[next_slice_spec(k) for k in widths],
            gate_spec(0), gate_spec(1), gate_spec(2),
            bias_spec(0), bias_spec(1), bias_spec(2),
        ],
        out_specs=pl.BlockSpec((tm, tn), lambda j, i: (i, j)),
        out_shape=jax.ShapeDtypeStruct((m, D_MODEL), BF16),
        scratch_shapes=[pltpu.VMEM((2, k, tn), BF16) for k in widths],
        compiler_params=_params(2),
        name="branch_merge",
    )(*a_ps, *yb_ps, *yc_ps, w_a, w_b, w_c, w_a, w_b, w_c,
      z_br, z_br, z_br, b_branch, b_branch, b_branch)


def _log_sigmoid(x):
    return jnp.minimum(x, 0.0) - jnp.log1p(jnp.exp(-jnp.abs(x)))


def _mlstm_chunk(q, k, v, irow, frow, c_state, n_state, m_state, n_valid):
    L = q.shape[0]
    ti = lax.broadcasted_iota(jnp.int32, (L, L), 0)
    si = lax.broadcasted_iota(jnp.int32, (L, L), 1)
    causal = si <= ti
    eye = si == ti
    f_b = jnp.broadcast_to(frow, (L, L))
    i_b = jnp.broadcast_to(irow, (L, L))
    bcol = jnp.sum(jnp.where(causal, f_b, 0.0), axis=1, keepdims=True)
    fcol = jnp.sum(jnp.where(eye, f_b, 0.0), axis=1, keepdims=True)
    icol = jnp.sum(jnp.where(eye, i_b, 0.0), axis=1, keepdims=True)
    brow = jnp.sum(jnp.where(ti <= si, jnp.broadcast_to(fcol, (L, L)), 0.0), axis=0, keepdims=True)

    acol = bcol + m_state
    logw = jnp.where(causal, bcol - brow + irow, NEG_INF)
    mt = jnp.maximum(acol, jnp.max(logw, axis=1, keepdims=True))
    w_inter = jnp.exp(acol - mt)
    w_intra = jnp.exp(logw - mt)

    qb = q.astype(BF16)
    kb = k.astype(BF16)
    nt = (((1,), (1,)), ((), ()))
    s = lax.dot_general(qb, kb, nt, preferred_element_type=F32) * w_intra
    inter = lax.dot_general(qb, c_state.astype(BF16), nt, preferred_element_type=F32)
    num = w_inter * inter + jnp.dot(s.astype(BF16), v.astype(BF16), preferred_element_type=F32)
    nq = w_inter * jnp.sum(q * n_state, axis=1, keepdims=True) + jnp.sum(s, axis=1, keepdims=True)
    h = num / jnp.maximum(jnp.abs(nq), jnp.exp(-mt))

    last = slice(n_valid - 1, n_valid)
    m_end = mt[last, :]
    w_c = jnp.exp(acol[last, :] - m_end)
    w_s = jnp.exp(bcol[last, :] - bcol + icol - m_end)
    if n_valid < L:
        w_s = jnp.where(lax.broadcasted_iota(jnp.int32, (L, 1), 0) < n_valid, w_s, 0.0)
    tn = (((0,), (0,)), ((), ()))
    c_new = w_c * c_state + lax.dot_general((v * w_s).astype(BF16), kb, tn, preferred_element_type=F32)
    n_new = w_c * n_state + jnp.sum(w_s * k, axis=0, keepdims=True)
    return h, c_new, n_new, m_end


def _gate_rows(g_ref_val, bias_ref, head):
    irow = g_ref_val[0:1, :] + bias_ref[0, head]
    frow = _log_sigmoid(g_ref_val[1:2, :] + bias_ref[1, head])
    return irow, frow


def _mlstm_prompt_body(bias_ref, q_ref, k_ref, v_ref, mo_ref, g_ref,
                       a_ref, c_ref, n_ref, m_ref):
    @pl.when(pl.program_id(1) == 0)
    def _():
        c_ref[...] = jnp.zeros_like(c_ref)
        n_ref[...] = jnp.zeros_like(n_ref)
        m_ref[...] = jnp.zeros_like(m_ref)

    for head in range(M_HEADS):
        qk = pl.ds(head * M_DQK, M_DQK)
        vo = pl.ds(head * M_DV, M_DV)
        one = pl.ds(head, 1)
        irow, frow = _gate_rows(g_ref[head], bias_ref, head)
        h, c_new, n_new, m_end = _mlstm_chunk(
            q_ref[:, qk], k_ref[:, qk] * (M_DQK ** -0.5), v_ref[:, vo], irow, frow,
            c_ref[0, head], n_ref[0, one, :], m_ref[0, one, 0:1], CHUNK)
        a_ref[:, vo] = (_sigmoid(mo_ref[:, vo]) * h).astype(a_ref.dtype)
        c_ref[0, head] = c_new
        n_ref[0, one, :] = n_new
        m_ref[0, one, :] = jnp.broadcast_to(m_end, (1, LANES))


def _mlstm_prompt(z_main, gates_rows, gate_bias):
    nc = SEQ // CHUNK
    return pl.pallas_call(
        _mlstm_prompt_body,
        grid=(BATCH, nc),
        in_specs=[
            pl.BlockSpec(memory_space=pltpu.SMEM),
            pl.BlockSpec((CHUNK, W_MQK), lambda b, c: (b * nc + c, 0)),
            pl.BlockSpec((CHUNK, W_MQK), lambda b, c: (b * nc + c, 1)),
            pl.BlockSpec((CHUNK, W_MV), lambda b, c: (b * nc + c, 2 * W_MQK // W_MV)),
            pl.BlockSpec((CHUNK, W_MV), lambda b, c: (b * nc + c, 2 * W_MQK // W_MV + 1)),
            pl.BlockSpec((M_HEADS, 2, CHUNK), lambda b, c: (0, 0, b * nc + c)),
        ],
        out_specs=[
            pl.BlockSpec((CHUNK, W_MV), lambda b, c: (b * nc + c, 0)),
            pl.BlockSpec((1, M_HEADS, M_DV, M_DQK), lambda b, c: (b, 0, 0, 0)),
            pl.BlockSpec((1, M_HEADS, M_DQK), lambda b, c: (b, 0, 0)),
            pl.BlockSpec((1, M_HEADS, LANES), lambda b, c: (b, 0, 0)),
        ],
        out_shape=[
            jax.ShapeDtypeStruct((N_PROMPT, W_MV), BF16),
            jax.ShapeDtypeStruct((BATCH, M_HEADS, M_DV, M_DQK), F32),
            jax.ShapeDtypeStruct((BATCH, M_HEADS, M_DQK), F32),
            jax.ShapeDtypeStruct((BATCH, M_HEADS, LANES), F32),
        ],
        compiler_params=_params(2),
        name="mlstm_prompt",
    )(gate_bias, z_main, z_main, z_main, z_main, gates_rows)


def _mlstm_sample_init(scratch):
    for scr in scratch:
        scr[...] = jnp.zeros_like(scr)


def _mlstm_sample_step(t, ins, outs, scratch):
    bias_ref, q_ref, k_ref, v_ref, mo_ref, g_ref, c0_ref, n0_ref, m0_ref = ins
    a_ref, c_ref, n_ref, m_ref = outs
    q_scr, k_scr, v_scr = scratch
    first_half = (t % SAMPLE_SEQS_PER_BLOCK) == 0

    def seq_rows(ref, cols=slice(None)):
        return jnp.where(first_half, ref[0:DEC_SEQ, cols], ref[DEC_SEQ:2 * DEC_SEQ, cols])

    q_scr[0:DEC_SEQ, :] = seq_rows(q_ref)
    k_scr[0:DEC_SEQ, :] = seq_rows(k_ref) * (M_DQK ** -0.5)
    v_scr[0:DEC_SEQ, :] = seq_rows(v_ref)
    for head in range(M_HEADS):
        qk = pl.ds(head * M_DQK, M_DQK)
        vo = pl.ds(head * M_DV, M_DV)
        one = pl.ds(head, 1)
        irow, frow = _gate_rows(g_ref[0, head], bias_ref, head)
        h, c_new, n_new, m_end = _mlstm_chunk(
            q_scr[:, qk], k_scr[:, qk], v_scr[:, vo], irow, frow,
            c0_ref[0, 0, head], n0_ref[0, 0, one, :], m0_ref[0, one, :], DEC_SEQ)
        a_ref[0, :, vo] = _sigmoid(seq_rows(mo_ref, vo)) * h[0:DEC_SEQ, :]
        c_ref[0, head] = c_new
        n_ref[0, one, :] = n_new
        m_ref[0, one, :] = jnp.broadcast_to(m_end, (1, LANES))


def _mlstm_sample_job(z_main, gates_rows, gate_bias, c0, n0, m0):
    per = SAMPLE_SEQS_PER_BLOCK
    rows = per * DEC_SEQ
    assert rows == SUBLANES and N_PROMPT % rows == 0
    r0 = N_PROMPT // rows
    v_blk = 2 * W_MQK // W_MV
    return _SideJob(
        n_steps=DEC_BATCH,
        inputs=(gate_bias, z_main, z_main, z_main, z_main, gates_rows, c0, n0, m0),
        in_specs=(
            pl.BlockSpec(memory_space=pltpu.SMEM),
            ((rows, W_MQK), lambda t: (r0 + t // per, 0)),
            ((rows, W_MQK), lambda t: (r0 + t // per, 1)),
            ((rows, W_MV), lambda t: (r0 + t // per, v_blk)),
            ((rows, W_MV), lambda t: (r0 + t // per, v_blk + 1)),
            ((1, M_HEADS, 2, SAMPLE_CHUNK), lambda t: (t, 0, 0, 0)),
            ((1, 1, M_HEADS, M_DV, M_DQK), lambda t: (0, t, 0, 0, 0)),
            ((1, 1, M_HEADS, M_DQK), lambda t: (0, t, 0, 0)),
            ((1, M_HEADS, 1), lambda t: (t, 0, 0)),
        ),
        out_shapes=(
            jax.ShapeDtypeStruct((DEC_BATCH, DEC_SEQ, W_MV), F32),
            jax.ShapeDtypeStruct((DEC_BATCH, M_HEADS, M_DV, M_DQK), F32),
            jax.ShapeDtypeStruct((DEC_BATCH, M_HEADS, M_DQK), F32),
            jax.ShapeDtypeStruct((DEC_BATCH, M_HEADS, LANES), F32),
        ),
        out_specs=(
            ((1, DEC_SEQ, W_MV), lambda t: (t, 0, 0)),
            ((1, M_HEADS, M_DV, M_DQK), lambda t: (t, 0, 0, 0)),
            ((1, M_HEADS, M_DQK), lambda t: (t, 0, 0)),
            ((1, M_HEADS, LANES), lambda t: (t, 0, 0)),
        ),
        scratch_shapes=(pltpu.VMEM((SAMPLE_CHUNK, W_MQK), F32), pltpu.VMEM((SAMPLE_CHUNK, W_MQK), F32),
                        pltpu.VMEM((SAMPLE_CHUNK, W_MV), F32)),
        init=_mlstm_sample_init,
        step=_mlstm_sample_step,
    )


def _dil_prompt_body(q_ref, kp_ref, kc_ref, vp_ref, vc_ref, o_ref, l_ref, *, r, hb):
    first_key = jnp.where(pl.program_id(2) > 0, 0, SPAN)
    qi = lax.broadcasted_iota(jnp.int32, (SPAN, 2 * SPAN), 0)
    ki = lax.broadcasted_iota(jnp.int32, (SPAN, 2 * SPAN), 1)
    ok = (ki >= qi) & (ki <= qi + SPAN) & (ki >= first_key)
    bias = jnp.where(ok, 0.0, NEG_INF)
    scale = A_HEAD_DIM ** -0.5
    nt = (((1,), (1,)), ((), ()))
    for c in range(r):
        rows = pl.ds(c, SPAN, stride=r) if r > 1 else pl.ds(0, SPAN)
        for hh in range(hb):
            cols = pl.ds(hh * A_HEAD_DIM, A_HEAD_DIM)
            q = q_ref[rows, cols].astype(BF16)
            kk = jnp.concatenate([kp_ref[rows, cols], kc_ref[rows, cols]], axis=0).astype(BF16)
            vv = jnp.concatenate([vp_ref[rows, cols], vc_ref[rows, cols]], axis=0).astype(BF16)
            s = lax.dot_general(q, kk, nt, preferred_element_type=F32) * scale + bias
            m = jnp.max(s, axis=1, keepdims=True)
            p = jnp.exp(s - m)
            den = jnp.sum(p, axis=1, keepdims=True)
            o = jnp.dot(p.astype(BF16), vv, preferred_element_type=F32) / den
            o_ref[rows, cols] = o
            l_ref[rows, cols] = jnp.broadcast_to(m + jnp.log(den), (SPAN, A_HEAD_DIM))


def _dilated_prompt(z_aqkv, g, r, hb):
    rows = SPAN * r
    nblk = SEQ // rows
    wcol = hb * A_HEAD_DIM
    per_part = W_AH // wcol

    def spec(part, prev):
        def imap(b, hg, n):
            nn = jnp.maximum(n - 1, 0) if prev else n
            return (b * nblk + nn, (3 * g + part) * per_part + hg)
        return pl.BlockSpec((rows, wcol), imap)

    out_spec = pl.BlockSpec((rows, wcol), lambda b, hg, n: (b * nblk + n, hg))
    return pl.pallas_call(
        functools.partial(_dil_prompt_body, r=r, hb=hb),
        grid=(BATCH, A_HEADS // hb, nblk),
        in_specs=[spec(0, False), spec(1, True), spec(1, False), spec(2, True), spec(2, False)],
        out_specs=[out_spec, out_spec],
        out_shape=[jax.ShapeDtypeStruct((N_PROMPT, W_AH), F32)] * 2,
        compiler_params=_params(3),
        name=f"dilated_prompt_g{g}",
    )(z_aqkv, z_aqkv, z_aqkv, z_aqkv, z_aqkv)


def _sample_attn_body(qkv_ref, k0_ref, v0_ref, k1_ref, v1_ref, k2_ref, v2_ref, b0_ref, b1_ref, b2_ref,
                      xq_ref, mk_ref, mv_ref, bx_ref, yb_ref, yc_ref):
    nt = (((1,), (1,)), ((), ()))
    outs, lses = [], []
    groups = ((k0_ref, v0_ref, b0_ref), (k1_ref, v1_ref, b1_ref), (k2_ref, v2_ref, b2_ref))
    for g, (kc_ref, vc_ref, bias_ref) in enumerate(groups):
        n_cached = kc_ref.shape[1] * kc_ref.shape[2]
        q = qkv_ref[0, 3 * g].astype(BF16)
        kk = jnp.concatenate([kc_ref[0].reshape(n_cached, A_HEAD_DIM), qkv_ref[0, 3 * g + 1]],
                             axis=0).astype(BF16)
        vv = jnp.concatenate([vc_ref[0].reshape(n_cached, A_HEAD_DIM), qkv_ref[0, 3 * g + 2]],
                             axis=0).astype(BF16)
        s = lax.dot_general(q, kk, nt, preferred_element_type=F32) * (A_HEAD_DIM ** -0.5) + bias_ref[...]
        m = jnp.max(s, axis=1, keepdims=True)
        p = jnp.exp(s - m)
        den = jnp.sum(p, axis=1, keepdims=True)
        outs.append(jnp.dot(p.astype(BF16), vv, preferred_element_type=F32) / den)
        lses.append(m + jnp.log(den))
    mx = jnp.maximum(jnp.maximum(lses[0], lses[1]), lses[2])
    e = [jnp.exp(l - mx) for l in lses]
    tot = e[0] + e[1] + e[2]
    yb = (e[0] / tot) * outs[0] + (e[1] / tot) * outs[1] + (e[2] / tot) * outs[2]
    yb_ref[0] = yb.astype(yb_ref.dtype)

    s = lax.dot_general(xq_ref[0].astype(BF16), mk_ref[0].astype(BF16), nt,
                        preferred_element_type=F32) * (X_HEAD_DIM ** -0.5) + bx_ref[...]
    m = jnp.max(s, axis=1, keepdims=True)
    ex = jnp.exp(s - m)
    p = ex / jnp.sum(ex, axis=1, keepdims=True)
    yc_ref[0] = jnp.dot(p.astype(BF16), mv_ref[0].astype(BF16),
                        preferred_element_type=F32).astype(yc_ref.dtype)


def _dilated_sample_bias(window, r, lb, rc):
    span = window // r
    n_c = (lb // r) * rc * A_HEADS
    rows = np.arange(DEC_SEQ * A_HEADS)
    s_q, h_q = rows // A_HEADS, rows % A_HEADS
    col = np.arange(n_c)
    m_k = col // (rc * A_HEADS)
    c_k = (col % (rc * A_HEADS)) // A_HEADS
    h_k = col % A_HEADS
    delta = (lb + s_q)[:, None] - (m_k * r + c_k)[None, :]
    ok_c = (h_q[:, None] == h_k[None, :]) & (delta % r == 0) & (delta // r <= span) & (delta >= 0)
    coln = np.arange(DEC_SEQ * A_HEADS)
    s_n, h_n = coln // A_HEADS, coln % A_HEADS
    dn = s_q[:, None] - s_n[None, :]
    ok_n = (h_q[:, None] == h_n[None, :]) & (dn >= 0) & (dn % r == 0) & (dn // r <= span)
    ok = np.concatenate([ok_c, ok_n], axis=1)
    return np.where(ok, 0.0, -np.inf).astype(np.float32)


def _sample_attention_job(qkv, caches, xq, mem_k, mem_v):
    rq = DEC_SEQ * A_HEADS
    rx = DEC_SEQ * X_HEADS
    nk = MEM_LEN * X_HEADS
    cache_args, cache_specs, biases = [], [], []
    for (window, r), (cache_k, cache_v) in zip(DIL_GROUPS, caches):
        lb = cache_k.shape[1]
        assert lb % r == 0 and window % r == 0
        rc = min(r, DEC_SEQ)
        nm = lb // r
        spec = ((1, nm, rc * A_HEADS, A_HEAD_DIM), lambda t: (t, 0, 0, 0))
        for c in (cache_k, cache_v):
            cache_args.append(c.reshape(DEC_BATCH, nm, r * A_HEADS, A_HEAD_DIM))
            cache_specs.append(spec)
        biases.append(jnp.asarray(_dilated_sample_bias(window, r, lb, rc)))
    ok = (np.arange(rx) % X_HEADS)[:, None] == (np.arange(nk) % X_HEADS)[None, :]
    bias_x = jnp.asarray(np.where(ok, 0.0, -np.inf).astype(np.float32))

    def const_spec(a):
        return pl.BlockSpec(a.shape, lambda p, i: (0, 0), pipeline_mode=pl.Buffered(1))

    return _SideJob(
        n_steps=DEC_BATCH,
        inputs=(qkv, *cache_args, *biases, xq, mem_k, mem_v, bias_x),
        in_specs=(
            ((1, 3 * N_GROUPS, rq, A_HEAD_DIM), lambda t: (t, 0, 0, 0)),
            *cache_specs,
            *[const_spec(a) for a in biases],
            ((1, rx, X_HEAD_DIM), lambda t: (t, 0, 0)),
            ((1, nk, X_HEAD_DIM), lambda t: (t, 0, 0)),
            ((1, nk, X_HEAD_DIM), lambda t: (t, 0, 0)),
            const_spec(bias_x),
        ),
        out_shapes=(jax.ShapeDtypeStruct((DEC_BATCH, rq, A_HEAD_DIM), BF16),
                    jax.ShapeDtypeStruct((DEC_BATCH, rx, X_HEAD_DIM), BF16)),
        out_specs=(((1, rq, A_HEAD_DIM), lambda t: (t, 0, 0)),
                   ((1, rx, X_HEAD_DIM), lambda t: (t, 0, 0))),
        scratch_shapes=(),
        init=lambda scratch: None,
        step=lambda t, ins, outs, scratch: _sample_attn_body(*ins, *outs),
    )


def _cross_prompt_body(q_ref, k_ref, v_ref, o_ref):
    scale = X_HEAD_DIM ** -0.5
    nt = (((1,), (1,)), ((), ()))
    for h in range(X_HEADS):
        cols = pl.ds(h * X_HEAD_DIM, X_HEAD_DIM)
        s = lax.dot_general(q_ref[:, cols].astype(BF16), k_ref[:, cols].astype(BF16), nt,
                            preferred_element_type=F32) * scale
        m = jnp.max(s, axis=1, keepdims=True)
        e = jnp.exp(s - m)
        p = e / jnp.sum(e, axis=1, keepdims=True)
        o_ref[:, cols] = jnp.dot(p.astype(BF16), v_ref[:, cols].astype(BF16),
                                 preferred_element_type=F32).astype(o_ref.dtype)


def _cross_prompt(z_xq, mem_kv, *, tq=512):
    nq = SEQ // tq
    return pl.pallas_call(
        _cross_prompt_body,
        grid=(BATCH, nq),
        in_specs=[
            pl.BlockSpec((tq, W_XQ), lambda b, i: (b * nq + i, 0)),
            pl.BlockSpec((MEM_LEN, W_XQ), lambda b, i: (b, 0)),
            pl.BlockSpec((MEM_LEN, W_XQ), lambda b, i: (b, 1)),
        ],
        out_specs=pl.BlockSpec((tq, W_XQ), lambda b, i: (b * nq + i, 0)),
        out_shape=jax.ShapeDtypeStruct((N_PROMPT, W_XQ), BF16),
        compiler_params=_params(2),
        name="cross_prompt",
    )(z_xq, mem_kv, mem_kv)


def _combine_body(o0, o1, o2, l0, l1, l2, y_ref):
    a0, a1, a2 = l0[...], l1[...], l2[...]
    mx = jnp.maximum(jnp.maximum(a0, a1), a2)
    e0, e1, e2 = jnp.exp(a0 - mx), jnp.exp(a1 - mx), jnp.exp(a2 - mx)
    tot = e0 + e1 + e2
    y = (e0 / tot) * o0[...] + (e1 / tot) * o1[...] + (e2 / tot) * o2[...]
    y_ref[...] = y.astype(y_ref.dtype)


def _combine_groups(outs, lses, *, rows_per_step):
    n, w = outs[0].shape
    spec = pl.BlockSpec((rows_per_step, w), lambda i: (i, 0))
    return pl.pallas_call(
        _combine_body,
        grid=(n // rows_per_step,),
        in_specs=[spec] * 6,
        out_specs=spec,
        out_shape=jax.ShapeDtypeStruct((n, w), BF16),
        compiler_params=_params(1),
        name="combine_groups",
    )(*outs, *lses)


def _rope_tables():
    pos = jnp.concatenate([
        jnp.tile(jnp.arange(SEQ, dtype=jnp.int32), BATCH),
        jnp.tile(PAST_LEN + jnp.arange(DEC_SEQ, dtype=jnp.int32), DEC_BATCH)])
    inv = ROPE_THETA ** (-jnp.arange(0, A_HEAD_DIM, 2, dtype=F32) / A_HEAD_DIM)
    ang = pos.astype(F32)[:, None] * inv[None, :]
    cos, sin = jnp.cos(ang), jnp.sin(ang)
    return jnp.concatenate([cos, cos], axis=1), jnp.concatenate([-sin, sin], axis=1)


def _layer(x_p, x_s, mem_prompt, state_c, state_n, state_m, caches, cache_mem_k, cache_mem_v,
           g_mix, w_in, b_igate, b_fgate, b_branch, g_mem, w_mem_kv,
           w_br_a, w_br_b, w_br_c, w_out, g_mlp, w_up, w_down):
    h_all = _rmsnorm_two_groups(x_p, x_s, g_mix, BF16)

    w_in_t = w_in.T
    z_main = _matmul_stream(h_all, w_in_t, n_cols=COL_GATES, b_rows_are_outputs=True, name="proj_main")
    z_gate = _matmul(h_all, w_in_t, n_cols=2 * M_HEADS, col_start=COL_GATES, tn=2 * M_HEADS,
                     b_rows_are_outputs=True, name="proj_gates")
    z_xq = _matmul(h_all, w_in_t, n_cols=W_XQ, col_start=COL_REST + W_AQKV, b_rows_are_outputs=True,
                   name="proj_cross_q")
    gate_bias = jnp.stack([b_igate, b_fgate]).astype(F32)

    gp = z_gate[:N_PROMPT].reshape(N_PROMPT, 2, M_HEADS).transpose(2, 1, 0)
    a_p, c_p, n_p, m_p = _mlstm_prompt(z_main, gp, gate_bias)
    gs = z_gate[N_PROMPT:].reshape(DEC_BATCH, DEC_SEQ, 2, M_HEADS).transpose(0, 3, 2, 1)
    gs = jnp.pad(gs, ((0, 0), (0, 0), (0, 0), (0, SAMPLE_CHUNK - DEC_SEQ)))
    mlstm_job = _mlstm_sample_job(
        z_main, gs, gate_bias, state_c, state_n, state_m.reshape(DEC_BATCH, M_HEADS, 1))
    cos2, sin2 = _rope_tables()
    table_spec = pl.BlockSpec((HOST_TM, A_HEAD_DIM), lambda j, i: (i, 0))
    z_aqkv, a_s, c_s, n_s, m_s = _matmul_stream(
        h_all, w_in_t, n_cols=W_AQKV, col_start=COL_REST, b_rows_are_outputs=True, tm=HOST_TM,
        epilogue=_ep_rope, extras=(cos2, sin2), extra_specs=(table_spec, table_spec),
        side=mlstm_job, name="proj_attn")
    a_ps = (a_p, a_s.reshape(N_SAMPLE, W_MV).astype(BF16))

    heads_per_step = (8, 1, 1)
    outs_p, lses_p, rows_p, rows_s = [], [], [], []
    rq = DEC_SEQ * A_HEADS
    qkv_s = z_aqkv[N_PROMPT:].reshape(DEC_BATCH, DEC_SEQ, 3 * N_GROUPS, A_HEADS, A_HEAD_DIM)
    qkv_s = qkv_s.transpose(0, 2, 1, 3, 4).reshape(DEC_BATCH, 3 * N_GROUPS, rq, A_HEAD_DIM)
    for g, (window, r) in enumerate(DIL_GROUPS):
        o, l = _dilated_prompt(z_aqkv, g, r, heads_per_step[g])
        outs_p.append(o)
        lses_p.append(l)
        c0 = 3 * g * W_AH
        keep = min(window, SEQ)
        for part in (1, 2):
            cs = c0 + part * W_AH
            kept = [z_aqkv[(b + 1) * SEQ - keep:(b + 1) * SEQ, cs:cs + W_AH] for b in range(BATCH)]
            rows_p.append(jnp.stack(kept).reshape(BATCH, keep, A_HEADS, A_HEAD_DIM))
            rows_s.append(qkv_s[:, 3 * g + part].reshape(DEC_BATCH, DEC_SEQ, A_HEADS, A_HEAD_DIM))
    yb_p = _combine_groups(outs_p, lses_p, rows_per_step=256)

    mem_h = _rmsnorm(mem_prompt, g_mem, BF16)
    mem_kv = _matmul(mem_h, w_mem_kv, n_cols=2 * W_XQ, tm=BATCH * MEM_LEN, name="mem_kv")
    yc_p = _cross_prompt(z_xq, mem_kv)

    xq_s = z_xq[N_PROMPT:].reshape(DEC_BATCH, DEC_SEQ * X_HEADS, X_HEAD_DIM)
    attention_job = _sample_attention_job(
        qkv_s, caches, xq_s,
        cache_mem_k.reshape(DEC_BATCH, MEM_LEN * X_HEADS, X_HEAD_DIM),
        cache_mem_v.reshape(DEC_BATCH, MEM_LEN * X_HEADS, X_HEAD_DIM))
    z_br, yb_s, yc_s = _matmul_stream(
        h_all, w_in_t, n_cols=3 * D_MODEL, col_start=COL_REST + W_AQKV + W_XQ, b_rows_are_outputs=True,
        tm=HOST_TM, side=attention_job, name="proj_branch_gates")
    yb_ps = (yb_p, yb_s.reshape(N_SAMPLE, W_AH))
    yc_ps = (yc_p, yc_s.reshape(N_SAMPLE, W_XQ))

    merged = _branch_merge(a_ps, yb_ps, yc_ps, w_br_a, w_br_b, w_br_c, z_br, b_branch.reshape(1, 3 * D_MODEL))
    x1 = _matmul_stream(merged, w_out, n_cols=D_MODEL, tn=MM_TN, epilogue=_ep_residual_two_groups, extras=(x_p, x_s),
                        extra_specs=_two_group_specs(MM_TM, MM_TN, lambda j: j, SUBLANES, False),
                        name="out_proj")

    h2 = _rmsnorm(x1, g_mlp, BF16)
    u = _matmul_stream(h2, w_up, n_cols=D_FF, out_dtype=BF16, epilogue=_ep_relu2, name="mlp_up")
    x2 = _matmul_ksplit_residual(u, w_down, x1)

    mem_k = mem_kv[:, :W_XQ].reshape(BATCH, MEM_LEN, X_HEADS, X_HEAD_DIM)
    mem_v = mem_kv[:, W_XQ:].reshape(BATCH, MEM_LEN, X_HEADS, X_HEAD_DIM)
    prompt_state = (c_p, n_p, m_p[:, :, 0])
    sample_state = (c_s, n_s, m_s[:, :, 0])
    return x2, prompt_state, sample_state, rows_p, rows_s, mem_k, mem_v


def kernel(x_prompt, x_sample, state_mlstm_C, state_mlstm_n, state_mlstm_m,
           cache_win_k_g0, cache_win_v_g0, cache_win_k_g1, cache_win_v_g1,
           cache_win_k_g2, cache_win_v_g2, cache_mem_k, cache_mem_v, mem_prompt,
           g_mix, w_in, b_igate, b_fgate, b_branch, g_mem, w_mem_kv,
           w_br_a, w_br_b, w_br_c, w_out, g_mlp, w_up, w_down, g_final):
    depth = g_mix.shape[0]
    assert depth == 1, "single-layer stack"
    caches = ((cache_win_k_g0[0], cache_win_v_g0[0]),
              (cache_win_k_g1[0], cache_win_v_g1[0]),
              (cache_win_k_g2[0], cache_win_v_g2[0]))
    x2, p_state, s_state, rows_p, rows_s, mem_k, mem_v = _layer(
        x_prompt.reshape(N_PROMPT, D_MODEL), x_sample.reshape(N_SAMPLE, D_MODEL),
        mem_prompt.reshape(BATCH * MEM_LEN, D_MODEL),
        state_mlstm_C, state_mlstm_n, state_mlstm_m[0], caches, cache_mem_k[0], cache_mem_v[0],
        g_mix[0], w_in[0], b_igate[0], b_fgate[0], b_branch[0], g_mem[0], w_mem_kv[0],
        w_br_a[0], w_br_b[0], w_br_c[0], w_out[0], g_mlp[0], w_up[0], w_down[0])
    y_prompt = _rmsnorm(x2, g_final, F32, row_start=0, n_rows=N_PROMPT).reshape(BATCH, SEQ, D_MODEL)
    y_sample = _rmsnorm(x2, g_final, F32, row_start=N_PROMPT, n_rows=N_SAMPLE).reshape(DEC_BATCH, DEC_SEQ, D_MODEL)
    lead = lambda a: a[None]
    return (y_prompt, y_sample,
            lead(p_state[0]), lead(p_state[1]), lead(p_state[2]),
            *[lead(r) for r in rows_p],
            lead(mem_k), lead(mem_v),
            lead(s_state[0]), lead(s_state[1]), lead(s_state[2]),
            *[lead(r) for r in rows_s])
```

```python
import functools
import math
from typing import Callable, NamedTuple

import numpy as np
import jax
import jax.numpy as jnp
from jax import lax
from jax.experimental import pallas as pl
from jax.experimental.pallas import tpu as pltpu

F32 = jnp.float32
BF16 = jnp.bfloat16
NEG_INF = float("-inf")

D_MODEL = 4096
BATCH = 2
SEQ = 4096
DEC_BATCH = 128
DEC_SEQ = 4
PAST_LEN = 2048
MEM_LEN = 256
M_HEADS = 8
M_DQK = D_MODEL // (2 * M_HEADS)
M_DV = D_MODEL // M_HEADS
A_HEAD_DIM = 128
A_HEADS = D_MODEL // 512
DIL_GROUPS = ((128, 1), (512, 4), (2048, 16))
N_GROUPS = 3
ROPE_THETA = 10000.0
X_HEADS = 4
X_HEAD_DIM = 128
D_FF = 4 * D_MODEL
EPS = 1e-6

W_MQK = M_HEADS * M_DQK
W_MV = M_HEADS * M_DV
W_AH = A_HEADS * A_HEAD_DIM
W_XQ = X_HEADS * X_HEAD_DIM
N_PROMPT = BATCH * SEQ
N_SAMPLE = DEC_BATCH * DEC_SEQ
N_ALL = N_PROMPT + N_SAMPLE

COL_GATES = 2 * W_MQK + 2 * W_MV
COL_REST = COL_GATES + 2 * M_HEADS
W_AQKV = 3 * N_GROUPS * W_AH

LANES = 128
SUBLANES = 8
VMEM_LIMIT_BYTES = 60 * 1024 * 1024

MM_TM = 1088
MM_TN = 512
MM_TN_WIDE = 1024
MERGE_TM = 544
HOST_TM = 544
HOST_SPLIT_COLS = 256
DOWN_TM = 2176
DOWN_TK = 2048
NORM_ROWS = 256
CHUNK = 128
SAMPLE_CHUNK = 16
SAMPLE_SEQS_PER_BLOCK = 2
SPAN = 128


def _params(n_axes):
    return pltpu.CompilerParams(
        dimension_semantics=("arbitrary",) * n_axes,
        vmem_limit_bytes=VMEM_LIMIT_BYTES,
    )


def _rmsnorm_body(x_ref, g_ref, o_ref):
    x = x_ref[...]
    ms = jnp.mean(x * x, axis=-1, keepdims=True)
    o_ref[...] = ((x * lax.rsqrt(ms + EPS)) * g_ref[...]).astype(o_ref.dtype)


def _rmsnorm(x, g, out_dtype, *, row_start=0, n_rows=None):
    n_rows = x.shape[0] if n_rows is None else n_rows
    d = x.shape[1]
    off = row_start // NORM_ROWS
    return pl.pallas_call(
        _rmsnorm_body,
        grid=(n_rows // NORM_ROWS,),
        in_specs=[
            pl.BlockSpec((NORM_ROWS, d), lambda i: (i + off, 0)),
            pl.BlockSpec((1, d), lambda i: (0, 0)),
        ],
        out_specs=pl.BlockSpec((NORM_ROWS, d), lambda i: (i, 0)),
        out_shape=jax.ShapeDtypeStruct((n_rows, d), out_dtype),
        compiler_params=_params(1),
        name="rmsnorm",
    )(x, g.reshape(1, d))


def _rmsnorm_two_groups_body(xp_ref, xs_ref, g_ref, o_ref, *, prompt_steps):
    i = pl.program_id(0)

    @pl.when(i < prompt_steps)
    def _():
        _rmsnorm_body(xp_ref, g_ref, o_ref)

    @pl.when(i >= prompt_steps)
    def _():
        _rmsnorm_body(xs_ref, g_ref, o_ref)


def _rmsnorm_two_groups(xp, xs, g, out_dtype):
    d = xp.shape[1]
    ps, ss = xp.shape[0] // NORM_ROWS, xs.shape[0] // NORM_ROWS
    return pl.pallas_call(
        functools.partial(_rmsnorm_two_groups_body, prompt_steps=ps),
        grid=(ps + ss,),
        in_specs=[
            pl.BlockSpec((NORM_ROWS, d), lambda i: (jnp.minimum(i, ps - 1), 0)),
            pl.BlockSpec((NORM_ROWS, d), lambda i: (jnp.maximum(i - ps, 0), 0)),
            pl.BlockSpec((1, d), lambda i: (0, 0)),
        ],
        out_specs=pl.BlockSpec((NORM_ROWS, d), lambda i: (i, 0)),
        out_shape=jax.ShapeDtypeStruct((xp.shape[0] + xs.shape[0], d), out_dtype),
        compiler_params=_params(1),
        name="rmsnorm_two_groups",
    )(xp, xs, g.reshape(1, d))


def _ep_plain(acc, extra, o_ref, j):
    o_ref[...] = acc.astype(o_ref.dtype)


def _ep_residual_two_groups(acc, extra, o_ref, j):
    xp_ref, xs_ref = extra
    tm = acc.shape[0]
    n_full = N_PROMPT // tm
    rem = N_PROMPT - n_full * tm
    assert tm - rem == N_SAMPLE and rem % SUBLANES == 0
    i = pl.program_id(1)

    @pl.when(i < n_full)
    def _():
        o_ref[...] = xp_ref[...] + acc

    @pl.when(i >= n_full)
    def _():
        o_ref[0:rem, :] = xp_ref[tm - rem:tm, :] + acc[0:rem]
        o_ref[rem:tm, :] = xs_ref[...] + acc[rem:tm]


def _two_group_specs(tm, width, col_of_j, row_align, sample_block_is_constant):
    def p_map(j, i):
        row = jnp.minimum(i * tm, N_PROMPT - tm)
        col = col_of_j(j) * width
        return (pl.multiple_of(row, row_align), col if isinstance(col, int) else pl.multiple_of(col, LANES))
    p_spec = pl.BlockSpec((pl.Element(tm), pl.Element(width)), p_map)
    mode = pl.Buffered(1) if sample_block_is_constant else None
    s_spec = pl.BlockSpec((N_SAMPLE, width), lambda j, i: (0, col_of_j(j)), pipeline_mode=mode)
    return p_spec, s_spec


def _ep_relu2(acc, extra, o_ref, j):
    r = jnp.maximum(acc, 0.0)
    o_ref[...] = (r * r).astype(o_ref.dtype)


def _ep_rope(acc, extra, o_ref, j):
    cos_ref, sin_ref = extra
    tn = acc.shape[1]
    assert W_AH % tn == 0
    is_rope = (j // (W_AH // tn)) % 3 != 2
    cos2 = cos_ref[...]
    sin2 = sin_ref[...]
    for s in range(tn // A_HEAD_DIM):
        x = acc[:, s * A_HEAD_DIM:(s + 1) * A_HEAD_DIM]
        rotated = x * cos2 + pltpu.roll(x, A_HEAD_DIM // 2, axis=1) * sin2
        o_ref[:, s * A_HEAD_DIM:(s + 1) * A_HEAD_DIM] = jnp.where(is_rope, rotated, x)


def _mm_body(*refs, b_rows_are_outputs, epilogue, n_extra):
    a_ref, b_ref = refs[0], refs[1]
    extra = refs[2:2 + n_extra]
    o_ref = refs[2 + n_extra]
    b_scr = refs[3 + n_extra]

    @pl.when(pl.program_id(1) == 0)
    def _():
        b_scr[...] = b_ref[...].astype(BF16)

    if b_rows_are_outputs:
        acc = lax.dot_general(a_ref[...], b_scr[...], (((1,), (1,)), ((), ())),
                              preferred_element_type=F32)
    else:
        acc = jnp.dot(a_ref[...], b_scr[...], preferred_element_type=F32)
    epilogue(acc, extra, o_ref, pl.program_id(0))


def _matmul(a, b, *, n_cols, col_start=0, b_rows_are_outputs=False, tm=MM_TM, tn=MM_TN,
            out_dtype=F32, epilogue=_ep_plain, extras=(), extra_specs=(), name="matmul"):
    m, k = a.shape
    if b_rows_are_outputs:
        if col_start % tn == 0:
            b_spec = pl.BlockSpec((tn, k), lambda j, i: (j + col_start // tn, 0))
        else:
            assert col_start % SUBLANES == 0 and tn % SUBLANES == 0
            b_spec = pl.BlockSpec((pl.Element(tn), pl.Element(k)),
                                  lambda j, i: (pl.multiple_of(col_start + j * tn, SUBLANES), 0))
        b_block = (tn, k)
    else:
        assert col_start % tn == 0
        b_spec = pl.BlockSpec((k, tn), lambda j, i: (0, j + col_start // tn))
        b_block = (k, tn)
    return pl.pallas_call(
        functools.partial(_mm_body, b_rows_are_outputs=b_rows_are_outputs, epilogue=epilogue,
                          n_extra=len(extras)),
        grid=(n_cols // tn, m // tm),
        in_specs=[pl.BlockSpec((tm, k), lambda j, i: (i, 0)), b_spec, *extra_specs],
        out_specs=pl.BlockSpec((tm, tn), lambda j, i: (i, j)),
        out_shape=jax.ShapeDtypeStruct((m, n_cols), out_dtype),
        scratch_shapes=[pltpu.VMEM(b_block, BF16)],
        compiler_params=_params(2),
        name=name,
    )(a, b, *extras)


class _SideJob(NamedTuple):
    n_steps: int
    inputs: tuple
    in_specs: tuple
    out_shapes: tuple
    out_specs: tuple
    scratch_shapes: tuple
    init: Callable
    pieces: Callable


def _mm_stream_body(*refs, b_rows_are_outputs, epilogue, n_extra, slice_rows, split_cols, side):
    n_side_in = len(side.inputs) if side else 0
    n_side_out = len(side.out_shapes) if side else 0
    n_in = 2 + n_extra + n_side_in
    a_ref, bs_ref = refs[0], refs[1]
    extra = refs[2:2 + n_extra]
    side_in = refs[2 + n_extra:n_in]
    o_ref = refs[n_in]
    side_out = refs[n_in + 1:n_in + 1 + n_side_out]
    b_scr = refs[n_in + 1 + n_side_out]
    side_scr = refs[n_in + 2 + n_side_out:]
    p, i = pl.program_id(0), pl.program_id(1)

    r0 = pl.multiple_of(i * slice_rows, slice_rows)
    b_scr[p % 2, pl.ds(r0, slice_rows), :] = bs_ref[...].astype(BF16)

    if side is not None:
        @pl.when((p == 0) & (i == 0))
        def _():
            side.init(side_scr)

    def multiply(side_pieces):
        slot = (p - 1) % 2
        a = a_ref[...]
        tn = o_ref.shape[1]
        n_split = tn // split_cols
        per_split = -(-len(side_pieces) // n_split)
        for h in range(n_split):
            cols = pl.ds(h * split_cols, split_cols)
            if b_rows_are_outputs:
                acc = lax.dot_general(a, b_scr[slot, cols, :], (((1,), (1,)), ((), ())),
                                      preferred_element_type=F32)
            else:
                acc = jnp.dot(a, b_scr[slot, :, cols], preferred_element_type=F32)
            epilogue(acc, extra, o_ref.at[:, cols], (p - 1) * n_split + h)
            for piece in side_pieces[h * per_split:(h + 1) * per_split]:
                piece()

    if side is None:
        pl.when(p > 0)(lambda: multiply([]))
    else:
        t = (p - 1) * pl.num_programs(1) + i
        pl.when((p > 0) & (t < side.n_steps))(
            lambda: multiply(side.pieces(t, side_in, side_out, side_scr)))
        pl.when((p > 0) & (t >= side.n_steps))(lambda: multiply([]))


def _matmul_stream(a, b, *, n_cols, col_start=0, b_rows_are_outputs=False, tm=MM_TM, tn=MM_TN_WIDE,
                   out_dtype=F32, epilogue=_ep_plain, extras=(), extra_specs=(), side=None,
                   split_cols=MM_TN, name="matmul"):
    m, k = a.shape
    ni, nj = m // tm, n_cols // tn
    assert col_start % SUBLANES == 0 and nj * tn == n_cols

    def side_spec(spec):
        if isinstance(spec, pl.BlockSpec):
            return spec
        block_shape, index_fn = spec
        return pl.BlockSpec(
            block_shape,
            lambda p, i: index_fn(jnp.clip((p - 1) * ni + i, 0, side.n_steps - 1)))

    side_in_specs = [side_spec(s) for s in side.in_specs] if side else []
    side_out_specs = [side_spec(s) for s in side.out_specs] if side else []
    assert side is None or side.n_steps <= nj * ni

    def staged(p):
        return jnp.minimum(p, nj - 1)

    def shifted(index_map):
        return lambda p, i: index_map(jnp.maximum(p - 1, 0), jnp.where(p == 0, 0, i))

    if b_rows_are_outputs:
        tile = (tn, k)
        slice_rows = tn // ni
        bs_spec = pl.BlockSpec(
            (pl.Element(slice_rows), pl.Element(k)),
            lambda p, i: (pl.multiple_of(col_start + staged(p) * tn + i * slice_rows, SUBLANES), 0))
    else:
        assert col_start % tn == 0
        tile = (k, tn)
        slice_rows = k // ni
        bs_spec = pl.BlockSpec((slice_rows, tn), lambda p, i: (i, col_start // tn + staged(p)))
    assert slice_rows * ni == tile[0] and slice_rows % 16 == 0
    extra_specs = [pl.BlockSpec(s.block_shape, shifted(s.index_map), pipeline_mode=s.pipeline_mode)
                   for s in extra_specs]
    outs = pl.pallas_call(
        functools.partial(_mm_stream_body, b_rows_are_outputs=b_rows_are_outputs, epilogue=epilogue,
                          n_extra=len(extras), slice_rows=slice_rows, split_cols=min(tn, split_cols), side=side),
        grid=(nj + 1, ni),
        in_specs=[pl.BlockSpec((tm, k), shifted(lambda j, i: (i, 0))), bs_spec, *extra_specs,
                  *side_in_specs],
        out_specs=[pl.BlockSpec((tm, tn), shifted(lambda j, i: (i, j))), *side_out_specs],
        out_shape=[jax.ShapeDtypeStruct((m, n_cols), out_dtype), *(side.out_shapes if side else ())],
        scratch_shapes=[pltpu.VMEM((2,) + tile, BF16), *(side.scratch_shapes if side else ())],
        compiler_params=_params(2),
        name=name,
    )(a, b, *extras, *(side.inputs if side else ()))
    return outs if side else outs[0]


def _mm_ksplit_body(a_ref, b_ref, res_ref, o_ref, acc_ref):
    kk = pl.program_id(2)

    @pl.when(kk == 0)
    def _():
        acc_ref[...] = jnp.zeros_like(acc_ref)

    acc_ref[...] += jnp.dot(a_ref[...], b_ref[...].astype(BF16), preferred_element_type=F32)

    @pl.when(kk == pl.num_programs(2) - 1)
    def _():
        o_ref[...] = res_ref[...] + acc_ref[...]


def _matmul_ksplit_residual(a, b, res, *, tm=DOWN_TM, tn=MM_TN, tk=DOWN_TK):
    m, k = a.shape
    n = b.shape[1]
    return pl.pallas_call(
        _mm_ksplit_body,
        grid=(n // tn, m // tm, k // tk),
        in_specs=[
            pl.BlockSpec((tm, tk), lambda j, i, kk: (i, kk)),
            pl.BlockSpec((tk, tn), lambda j, i, kk: (kk, j)),
            pl.BlockSpec((tm, tn), lambda j, i, kk: (i, j)),
        ],
        out_specs=pl.BlockSpec((tm, tn), lambda j, i, kk: (i, j)),
        out_shape=jax.ShapeDtypeStruct((m, n), F32),
        scratch_shapes=[pltpu.VMEM((tm, tn), F32)],
        compiler_params=_params(3),
        name="mlp_down",
    )(a, b, res)


def _sigmoid(x):
    return 1.0 / (1.0 + jnp.exp(-x))


def _straddle_rows(p_ref, s_ref):
    tm = p_ref.shape[0]
    rem = tm - s_ref.shape[0]
    return jnp.concatenate([p_ref[tm - rem:tm, :], s_ref[...]], axis=0)


def _merge_body(ap_ref, as_ref, ybp_ref, ybs_ref, ycp_ref, ycs_ref,
                wa0_ref, wb0_ref, wc0_ref, was_ref, wbs_ref, wcs_ref,
                ga_ref, gb_ref, gc_ref, ba_ref, bb_ref, bc_ref,
                o_ref, wa_scr, wb_scr, wc_scr):
    j, i = pl.program_id(0), pl.program_id(1)
    cur = j % 2

    @pl.when((j == 0) & (i == 0))
    def _():
        wa_scr[0] = wa0_ref[...].astype(BF16)
        wb_scr[0] = wb0_ref[...].astype(BF16)
        wc_scr[0] = wc0_ref[...].astype(BF16)

    for scr, slice_ref in ((wa_scr, was_ref), (wb_scr, wbs_ref), (wc_scr, wcs_ref)):
        rows = slice_ref.shape[0]
        scr[1 - cur, pl.ds(pl.multiple_of(i * rows, rows), rows), :] = slice_ref[...].astype(BF16)

    def merge(a, yb, yc):
        y_a = jnp.dot(a, wa_scr[cur], preferred_element_type=F32)
        y_b = jnp.dot(yb, wb_scr[cur], preferred_element_type=F32)
        y_c = jnp.dot(yc, wc_scr[cur], preferred_element_type=F32)
        g_a = _sigmoid(ga_ref[...] + ba_ref[...])
        g_b = _sigmoid(gb_ref[...] + bb_ref[...])
        g_c = _sigmoid(gc_ref[...] + bc_ref[...])
        o_ref[...] = (g_a * y_a + g_b * y_b + g_c * y_c).astype(o_ref.dtype)

    is_prompt_tile = i < N_PROMPT // ap_ref.shape[0]

    @pl.when(is_prompt_tile)
    def _():
        merge(ap_ref[...], ybp_ref[...], ycp_ref[...])

    @pl.when(jnp.logical_not(is_prompt_tile))
    def _():
        merge(_straddle_rows(ap_ref, as_ref), _straddle_rows(ybp_ref, ybs_ref), _straddle_rows(ycp_ref, ycs_ref))


def _branch_merge(a_ps, yb_ps, yc_ps, w_a, w_b, w_c, z_br, b_branch, *, tm=MERGE_TM, tn=MM_TN):
    m = N_ALL
    assert (N_PROMPT // tm + 1) * tm == N_ALL
    nj = D_MODEL // tn

    def gate_spec(part):
        return pl.BlockSpec((tm, tn), lambda j, i: (i, part * nj + j))

    def bias_spec(part):
        return pl.BlockSpec((1, tn), lambda j, i: (0, part * nj + j))

    ni = m // tm
    widths = (W_MV, W_AH, W_XQ)
    assert all(k % (16 * ni) == 0 for k in widths)

    def first_tile_spec(k):
        return pl.BlockSpec((k, tn), lambda j, i: (0, 0), pipeline_mode=pl.Buffered(1))

    def next_slice_spec(k):
        return pl.BlockSpec((k // ni, tn), lambda j, i: (i, jnp.minimum(j + 1, nj - 1)))

    bf16_rows = 16
    return pl.pallas_call(
        _merge_body,
        grid=(nj, ni),
        in_specs=[
            *_two_group_specs(tm, W_MV, lambda j: 0, bf16_rows, True),
            *_two_group_specs(tm, W_AH, lambda j: 0, bf16_rows, True),
            *_two_group_specs(tm, W_XQ, lambda j: 0, bf16_rows, True),
            *[first_tile_spec(k) for k in widths],
            *[next_slice_spec(k) for k in widths],
            gate_spec(0), gate_spec(1), gate_spec(2),
            bias_spec(0), bias_spec(1), bias_spec(2),
        ],
        out_specs=pl.BlockSpec((tm, tn), lambda j, i: (i, j)),
        out_shape=jax.ShapeDtypeStruct((m, D_MODEL), BF16),
        scratch_shapes=[pltpu.VMEM((2, k, tn), BF16) for k in widths],
        compiler_params=_params(2),
        name="branch_merge",
    )(*a_ps, *yb_ps, *yc_ps, w_a, w_b, w_c, w_a, w_b, w_c,
      z_br, z_br, z_br, b_branch, b_branch, b_branch)


def _log_sigmoid(x):
    return jnp.minimum(x, 0.0) - jnp.log1p(jnp.exp(-jnp.abs(x)))


def _mlstm_chunk(q, k, v, irow, frow, c_state, n_state, m_state, n_valid):
    L = q.shape[0]
    ti = lax.broadcasted_iota(jnp.int32, (L, L), 0)
    si = lax.broadcasted_iota(jnp.int32, (L, L), 1)
    causal = si <= ti
    eye = si == ti
    f_b = jnp.broadcast_to(frow, (L, L))
    i_b = jnp.broadcast_to(irow, (L, L))
    bcol = jnp.sum(jnp.where(causal, f_b, 0.0), axis=1, keepdims=True)
    fcol = jnp.sum(jnp.where(eye, f_b, 0.0), axis=1, keepdims=True)
    icol = jnp.sum(jnp.where(eye, i_b, 0.0), axis=1, keepdims=True)
    brow = jnp.sum(jnp.where(ti <= si, jnp.broadcast_to(fcol, (L, L)), 0.0), axis=0, keepdims=True)

    acol = bcol + m_state
    logw = jnp.where(causal, bcol - brow + irow, NEG_INF)
    mt = jnp.maximum(acol, jnp.max(logw, axis=1, keepdims=True))
    w_inter = jnp.exp(acol - mt)
    w_intra = jnp.exp(logw - mt)

    qb = q.astype(BF16)
    kb = k.astype(BF16)
    nt = (((1,), (1,)), ((), ()))
    s = lax.dot_general(qb, kb, nt, preferred_element_type=F32) * w_intra
    inter = lax.dot_general(qb, c_state.astype(BF16), nt, preferred_element_type=F32)
    num = w_inter * inter + jnp.dot(s.astype(BF16), v.astype(BF16), preferred_element_type=F32)
    nq = w_inter * jnp.sum(q * n_state, axis=1, keepdims=True) + jnp.sum(s, axis=1, keepdims=True)
    h = num / jnp.maximum(jnp.abs(nq), jnp.exp(-mt))

    last = slice(n_valid - 1, n_valid)
    m_end = mt[last, :]
    w_c = jnp.exp(acol[last, :] - m_end)
    w_s = jnp.exp(bcol[last, :] - bcol + icol - m_end)
    if n_valid < L:
        w_s = jnp.where(lax.broadcasted_iota(jnp.int32, (L, 1), 0) < n_valid, w_s, 0.0)
    tn = (((0,), (0,)), ((), ()))
    c_new = w_c * c_state + lax.dot_general((v * w_s).astype(BF16), kb, tn, preferred_element_type=F32)
    n_new = w_c * n_state + jnp.sum(w_s * k, axis=0, keepdims=True)
    return h, c_new, n_new, m_end


def _gate_rows(g_ref_val, bias_ref, head):
    irow = g_ref_val[0:1, :] + bias_ref[0, head]
    frow = _log_sigmoid(g_ref_val[1:2, :] + bias_ref[1, head])
    return irow, frow


def _mlstm_prompt_body(bias_ref, q_ref, k_ref, v_ref, mo_ref, g_ref,
                       a_ref, c_ref, n_ref, m_ref):
    @pl.when(pl.program_id(1) == 0)
    def _():
        c_ref[...] = jnp.zeros_like(c_ref)
        n_ref[...] = jnp.zeros_like(n_ref)
        m_ref[...] = jnp.zeros_like(m_ref)

    for head in range(M_HEADS):
        qk = pl.ds(head * M_DQK, M_DQK)
        vo = pl.ds(head * M_DV, M_DV)
        one = pl.ds(head, 1)
        irow, frow = _gate_rows(g_ref[head], bias_ref, head)
        h, c_new, n_new, m_end = _mlstm_chunk(
            q_ref[:, qk], k_ref[:, qk] * (M_DQK ** -0.5), v_ref[:, vo], irow, frow,
            c_ref[0, head], n_ref[0, one, :], m_ref[0, one, 0:1], CHUNK)
        a_ref[:, vo] = (_sigmoid(mo_ref[:, vo]) * h).astype(a_ref.dtype)
        c_ref[0, head] = c_new
        n_ref[0, one, :] = n_new
        m_ref[0, one, :] = jnp.broadcast_to(m_end, (1, LANES))


def _mlstm_prompt(z_main, gates_rows, gate_bias):
    nc = SEQ // CHUNK
    return pl.pallas_call(
        _mlstm_prompt_body,
        grid=(BATCH, nc),
        in_specs=[
            pl.BlockSpec(memory_space=pltpu.SMEM),
            pl.BlockSpec((CHUNK, W_MQK), lambda b, c: (b * nc + c, 0)),
            pl.BlockSpec((CHUNK, W_MQK), lambda b, c: (b * nc + c, 1)),
            pl.BlockSpec((CHUNK, W_MV), lambda b, c: (b * nc + c, 2 * W_MQK // W_MV)),
            pl.BlockSpec((CHUNK, W_MV), lambda b, c: (b * nc + c, 2 * W_MQK // W_MV + 1)),
            pl.BlockSpec((M_HEADS, 2, CHUNK), lambda b, c: (0, 0, b * nc + c)),
        ],
        out_specs=[
            pl.BlockSpec((CHUNK, W_MV), lambda b, c: (b * nc + c, 0)),
            pl.BlockSpec((1, M_HEADS, M_DV, M_DQK), lambda b, c: (b, 0, 0, 0)),
            pl.BlockSpec((1, M_HEADS, M_DQK), lambda b, c: (b, 0, 0)),
            pl.BlockSpec((1, M_HEADS, LANES), lambda b, c: (b, 0, 0)),
        ],
        out_shape=[
            jax.ShapeDtypeStruct((N_PROMPT, W_MV), BF16),
            jax.ShapeDtypeStruct((BATCH, M_HEADS, M_DV, M_DQK), F32),
            jax.ShapeDtypeStruct((BATCH, M_HEADS, M_DQK), F32),
            jax.ShapeDtypeStruct((BATCH, M_HEADS, LANES), F32),
        ],
        compiler_params=_params(2),
        name="mlstm_prompt",
    )(gate_bias, z_main, z_main, z_main, z_main, gates_rows)


def _mlstm_sample_init(scratch):
    for scr in scratch:
        scr[...] = jnp.zeros_like(scr)


def _mlstm_sample_pieces(t, ins, outs, scratch):
    bias_ref, q_ref, k_ref, v_ref, mo_ref, g_ref, c0_ref, n0_ref, m0_ref = ins
    a_ref, c_ref, n_ref, m_ref = outs
    q_scr, k_scr, v_scr = scratch
    first_half = (t % SAMPLE_SEQS_PER_BLOCK) == 0

    def seq_rows(ref, cols):
        return jnp.where(first_half, ref[0:DEC_SEQ, cols], ref[DEC_SEQ:2 * DEC_SEQ, cols])

    def one_head(head):
        qk = pl.ds(head * M_DQK, M_DQK)
        vo = pl.ds(head * M_DV, M_DV)
        one = pl.ds(head, 1)
        q_scr[0:DEC_SEQ, qk] = seq_rows(q_ref, qk)
        k_scr[0:DEC_SEQ, qk] = seq_rows(k_ref, qk) * (M_DQK ** -0.5)
        v_scr[0:DEC_SEQ, vo] = seq_rows(v_ref, vo)
        irow, frow = _gate_rows(g_ref[0, head], bias_ref, head)
        h, c_new, n_new, m_end = _mlstm_chunk(
            q_scr[:, qk], k_scr[:, qk], v_scr[:, vo], irow, frow,
            c0_ref[0, 0, head], n0_ref[0, 0, one, :], m0_ref[0, one, :], DEC_SEQ)
        a_ref[0, :, vo] = _sigmoid(seq_rows(mo_ref, vo)) * h[0:DEC_SEQ, :]
        c_ref[0, head] = c_new
        n_ref[0, one, :] = n_new
        m_ref[0, one, :] = jnp.broadcast_to(m_end, (1, LANES))

    def all_heads():
        for head in range(M_HEADS):
            one_head(head)

    return [all_heads]


def _mlstm_sample_job(z_main, gates_rows, gate_bias, c0, n0, m0):
    per = SAMPLE_SEQS_PER_BLOCK
    rows = per * DEC_SEQ
    assert rows == SUBLANES and N_PROMPT % rows == 0
    r0 = N_PROMPT // rows
    v_blk = 2 * W_MQK // W_MV
    return _SideJob(
        n_steps=DEC_BATCH,
        inputs=(gate_bias, z_main, z_main, z_main, z_main, gates_rows, c0, n0, m0),
        in_specs=(
            pl.BlockSpec(memory_space=pltpu.SMEM),
            ((rows, W_MQK), lambda t: (r0 + t // per, 0)),
            ((rows, W_MQK), lambda t: (r0 + t // per, 1)),
            ((rows, W_MV), lambda t: (r0 + t // per, v_blk)),
            ((rows, W_MV), lambda t: (r0 + t // per, v_blk + 1)),
            ((1, M_HEADS, 2, SAMPLE_CHUNK), lambda t: (t, 0, 0, 0)),
            ((1, 1, M_HEADS, M_DV, M_DQK), lambda t: (0, t, 0, 0, 0)),
            ((1, 1, M_HEADS, M_DQK), lambda t: (0, t, 0, 0)),
            ((1, M_HEADS, 1), lambda t: (t, 0, 0)),
        ),
        out_shapes=(
            jax.ShapeDtypeStruct((DEC_BATCH, DEC_SEQ, W_MV), F32),
            jax.ShapeDtypeStruct((DEC_BATCH, M_HEADS, M_DV, M_DQK), F32),
            jax.ShapeDtypeStruct((DEC_BATCH, M_HEADS, M_DQK), F32),
            jax.ShapeDtypeStruct((DEC_BATCH, M_HEADS, LANES), F32),
        ),
        out_specs=(
            ((1, DEC_SEQ, W_MV), lambda t: (t, 0, 0)),
            ((1, M_HEADS, M_DV, M_DQK), lambda t: (t, 0, 0, 0)),
            ((1, M_HEADS, M_DQK), lambda t: (t, 0, 0)),
            ((1, M_HEADS, LANES), lambda t: (t, 0, 0)),
        ),
        scratch_shapes=(pltpu.VMEM((SAMPLE_CHUNK, W_MQK), F32), pltpu.VMEM((SAMPLE_CHUNK, W_MQK), F32),
                        pltpu.VMEM((SAMPLE_CHUNK, W_MV), F32)),
        init=_mlstm_sample_init,
        pieces=_mlstm_sample_pieces,
    )


def _dil_prompt_body(q_ref, kp_ref, kc_ref, vp_ref, vc_ref, o_ref, l_ref, *, r, hb):
    first_key = jnp.where(pl.program_id(2) > 0, 0, SPAN)
    qi = lax.broadcasted_iota(jnp.int32, (SPAN, 2 * SPAN), 0)
    ki = lax.broadcasted_iota(jnp.int32, (SPAN, 2 * SPAN), 1)
    ok = (ki >= qi) & (ki <= qi + SPAN) & (ki >= first_key)
    bias = jnp.where(ok, 0.0, NEG_INF)
    scale = A_HEAD_DIM ** -0.5
    nt = (((1,), (1,)), ((), ()))
    for c in range(r):
        rows = pl.ds(c, SPAN, stride=r) if r > 1 else pl.ds(0, SPAN)
        for hh in range(hb):
            cols = pl.ds(hh * A_HEAD_DIM, A_HEAD_DIM)
            q = q_ref[rows, cols].astype(BF16)
            kk = jnp.concatenate([kp_ref[rows, cols], kc_ref[rows, cols]], axis=0).astype(BF16)
            vv = jnp.concatenate([vp_ref[rows, cols], vc_ref[rows, cols]], axis=0).astype(BF16)
            s = lax.dot_general(q, kk, nt, preferred_element_type=F32) * scale + bias
            m = jnp.max(s, axis=1, keepdims=True)
            p = jnp.exp(s - m)
            den = jnp.sum(p, axis=1, keepdims=True)
            o = jnp.dot(p.astype(BF16), vv, preferred_element_type=F32) / den
            o_ref[rows, cols] = o
            l_ref[rows, cols] = jnp.broadcast_to(m + jnp.log(den), (SPAN, A_HEAD_DIM))


def _dilated_prompt(z_aqkv, g, r, hb):
    rows = SPAN * r
    nblk = SEQ // rows
    wcol = hb * A_HEAD_DIM
    per_part = W_AH // wcol

    def spec(part, prev):
        def imap(b, hg, n):
            nn = jnp.maximum(n - 1, 0) if prev else n
            return (b * nblk + nn, (3 * g + part) * per_part + hg)
        return pl.BlockSpec((rows, wcol), imap)

    out_spec = pl.BlockSpec((rows, wcol), lambda b, hg, n: (b * nblk + n, hg))
    return pl.pallas_call(
        functools.partial(_dil_prompt_body, r=r, hb=hb),
        grid=(BATCH, A_HEADS // hb, nblk),
        in_specs=[spec(0, False), spec(1, True), spec(1, False), spec(2, True), spec(2, False)],
        out_specs=[out_spec, out_spec],
        out_shape=[jax.ShapeDtypeStruct((N_PROMPT, W_AH), F32)] * 2,
        compiler_params=_params(3),
        name=f"dilated_prompt_g{g}",
    )(z_aqkv, z_aqkv, z_aqkv, z_aqkv, z_aqkv)


def _sample_attn_pieces(t, ins, out_refs, scratch):
    (qkv_ref, k0_ref, v0_ref, k1_ref, v1_ref, k2_ref, v2_ref, b0_ref, b1_ref, b2_ref,
     xq_ref, mk_ref, mv_ref, bx_ref) = ins
    yb_ref, yc_ref = out_refs
    nt = (((1,), (1,)), ((), ()))
    outs, lses = [], []
    groups = ((k0_ref, v0_ref, b0_ref), (k1_ref, v1_ref, b1_ref), (k2_ref, v2_ref, b2_ref))

    def cross():
        s = lax.dot_general(xq_ref[0].astype(BF16), mk_ref[0].astype(BF16), nt,
                            preferred_element_type=F32) * (X_HEAD_DIM ** -0.5) + bx_ref[...]
        m = jnp.max(s, axis=1, keepdims=True)
        ex = jnp.exp(s - m)
        p = ex / jnp.sum(ex, axis=1, keepdims=True)
        yc_ref[0] = jnp.dot(p.astype(BF16), mv_ref[0].astype(BF16),
                            preferred_element_type=F32).astype(yc_ref.dtype)

    def group(g):
        kc_ref, vc_ref, bias_ref = groups[g]
        n_cached = kc_ref.shape[1] * kc_ref.shape[2]
        q = qkv_ref[0, 3 * g].astype(BF16)
        kk = jnp.concatenate([kc_ref[0].reshape(n_cached, A_HEAD_DIM), qkv_ref[0, 3 * g + 1]],
                             axis=0).astype(BF16)
        vv = jnp.concatenate([vc_ref[0].reshape(n_cached, A_HEAD_DIM), qkv_ref[0, 3 * g + 2]],
                             axis=0).astype(BF16)
        s = lax.dot_general(q, kk, nt, preferred_element_type=F32) * (A_HEAD_DIM ** -0.5) + bias_ref[...]
        m = jnp.max(s, axis=1, keepdims=True)
        p = jnp.exp(s - m)
        den = jnp.sum(p, axis=1, keepdims=True)
        outs.append(jnp.dot(p.astype(BF16), vv, preferred_element_type=F32) / den)
        lses.append(m + jnp.log(den))
        if g < N_GROUPS - 1:
            return
        mx = jnp.maximum(jnp.maximum(lses[0], lses[1]), lses[2])
        e = [jnp.exp(l - mx) for l in lses]
        tot = e[0] + e[1] + e[2]
        yb = (e[0] / tot) * outs[0] + (e[1] / tot) * outs[1] + (e[2] / tot) * outs[2]
        yb_ref[0] = yb.astype(yb_ref.dtype)

    return [cross] + [functools.partial(group, g) for g in range(N_GROUPS)]


def _dilated_sample_bias(window, r, lb, rc):
    span = window // r
    n_c = (lb // r) * rc * A_HEADS
    rows = np.arange(DEC_SEQ * A_HEADS)
    s_q, h_q = rows // A_HEADS, rows % A_HEADS
    col = np.arange(n_c)
    m_k = col // (rc * A_HEADS)
    c_k = (col % (rc * A_HEADS)) // A_HEADS
    h_k = col % A_HEADS
    delta = (lb + s_q)[:, None] - (m_k * r + c_k)[None, :]
    ok_c = (h_q[:, None] == h_k[None, :]) & (delta % r == 0) & (delta // r <= span) & (delta >= 0)
    coln = np.arange(DEC_SEQ * A_HEADS)
    s_n, h_n = coln // A_HEADS, coln % A_HEADS
    dn = s_q[:, None] - s_n[None, :]
    ok_n = (h_q[:, None] == h_n[None, :]) & (dn >= 0) & (dn % r == 0) & (dn // r <= span)
    ok = np.concatenate([ok_c, ok_n], axis=1)
    return np.where(ok, 0.0, -np.inf).astype(np.float32)


def _sample_attention_job(qkv, caches, xq, mem_k, mem_v):
    rq = DEC_SEQ * A_HEADS
    rx = DEC_SEQ * X_HEADS
    nk = MEM_LEN * X_HEADS
    cache_args, cache_specs, biases = [], [], []
    for (window, r), (cache_k, cache_v) in zip(DIL_GROUPS, caches):
        lb = cache_k.shape[1]
        assert lb % r == 0 and window % r == 0
        rc = min(r, DEC_SEQ)
        nm = lb // r
        spec = ((1, nm, rc * A_HEADS, A_HEAD_DIM), lambda t: (t, 0, 0, 0))
        for c in (cache_k, cache_v):
            cache_args.append(c.reshape(DEC_BATCH, nm, r * A_HEADS, A_HEAD_DIM))
            cache_specs.append(spec)
        biases.append(jnp.asarray(_dilated_sample_bias(window, r, lb, rc)))
    ok = (np.arange(rx) % X_HEADS)[:, None] == (np.arange(nk) % X_HEADS)[None, :]
    bias_x = jnp.asarray(np.where(ok, 0.0, -np.inf).astype(np.float32))

    def const_spec(a):
        return pl.BlockSpec(a.shape, lambda p, i: (0, 0), pipeline_mode=pl.Buffered(1))

    return _SideJob(
        n_steps=DEC_BATCH,
        inputs=(qkv, *cache_args, *biases, xq, mem_k, mem_v, bias_x),
        in_specs=(
            ((1, 3 * N_GROUPS, rq, A_HEAD_DIM), lambda t: (t, 0, 0, 0)),
            *cache_specs,
            *[const_spec(a) for a in biases],
            ((1, rx, X_HEAD_DIM), lambda t: (t, 0, 0)),
            ((1, nk, X_HEAD_DIM), lambda t: (t, 0, 0)),
            ((1, nk, X_HEAD_DIM), lambda t: (t, 0, 0)),
            const_spec(bias_x),
        ),
        out_shapes=(jax.ShapeDtypeStruct((DEC_BATCH, rq, A_HEAD_DIM), BF16),
                    jax.ShapeDtypeStruct((DEC_BATCH, rx, X_HEAD_DIM), BF16)),
        out_specs=(((1, rq, A_HEAD_DIM), lambda t: (t, 0, 0)),
                   ((1, rx, X_HEAD_DIM), lambda t: (t, 0, 0))),
        scratch_shapes=(),
        init=lambda scratch: None,
        pieces=_sample_attn_pieces,
    )


def _cross_prompt_body(q_ref, k_ref, v_ref, o_ref):
    scale = X_HEAD_DIM ** -0.5
    nt = (((1,), (1,)), ((), ()))
    for h in range(X_HEADS):
        cols = pl.ds(h * X_HEAD_DIM, X_HEAD_DIM)
        s = lax.dot_general(q_ref[:, cols].astype(BF16), k_ref[:, cols].astype(BF16), nt,
                            preferred_element_type=F32) * scale
        m = jnp.max(s, axis=1, keepdims=True)
        e = jnp.exp(s - m)
        p = e / jnp.sum(e, axis=1, keepdims=True)
        o_ref[:, cols] = jnp.dot(p.astype(BF16), v_ref[:, cols].astype(BF16),
                                 preferred_element_type=F32).astype(o_ref.dtype)


def _cross_prompt(z_xq, mem_kv, *, tq=512):
    nq = SEQ // tq
    return pl.pallas_call(
        _cross_prompt_body,
        grid=(BATCH, nq),
        in_specs=[
            pl.BlockSpec((tq, W_XQ), lambda b, i: (b * nq + i, 0)),
            pl.BlockSpec((MEM_LEN, W_XQ), lambda b, i: (b, 0)),
            pl.BlockSpec((MEM_LEN, W_XQ), lambda b, i: (b, 1)),
        ],
        out_specs=pl.BlockSpec((tq, W_XQ), lambda b, i: (b * nq + i, 0)),
        out_shape=jax.ShapeDtypeStruct((N_PROMPT, W_XQ), BF16),
        compiler_params=_params(2),
        name="cross_prompt",
    )(z_xq, mem_kv, mem_kv)


def _combine_body(o0, o1, o2, l0, l1, l2, y_ref):
    a0, a1, a2 = l0[...], l1[...], l2[...]
    mx = jnp.maximum(jnp.maximum(a0, a1), a2)
    e0, e1, e2 = jnp.exp(a0 - mx), jnp.exp(a1 - mx), jnp.exp(a2 - mx)
    tot = e0 + e1 + e2
    y = (e0 / tot) * o0[...] + (e1 / tot) * o1[...] + (e2 / tot) * o2[...]
    y_ref[...] = y.astype(y_ref.dtype)


def _combine_groups(outs, lses, *, rows_per_step):
    n, w = outs[0].shape
    spec = pl.BlockSpec((rows_per_step, w), lambda i: (i, 0))
    return pl.pallas_call(
        _combine_body,
        grid=(n // rows_per_step,),
        in_specs=[spec] * 6,
        out_specs=spec,
        out_shape=jax.ShapeDtypeStruct((n, w), BF16),
        compiler_params=_params(1),
        name="combine_groups",
    )(*outs, *lses)


def _rope_tables():
    pos = jnp.concatenate([
        jnp.tile(jnp.arange(SEQ, dtype=jnp.int32), BATCH),
        jnp.tile(PAST_LEN + jnp.arange(DEC_SEQ, dtype=jnp.int32), DEC_BATCH)])
    inv = ROPE_THETA ** (-jnp.arange(0, A_HEAD_DIM, 2, dtype=F32) / A_HEAD_DIM)
    ang = pos.astype(F32)[:, None] * inv[None, :]
    cos, sin = jnp.cos(ang), jnp.sin(ang)
    return jnp.concatenate([cos, cos], axis=1), jnp.concatenate([-sin, sin], axis=1)


def _layer(x_p, x_s, mem_prompt, state_c, state_n, state_m, caches, cache_mem_k, cache_mem_v,
           g_mix, w_in, b_igate, b_fgate, b_branch, g_mem, w_mem_kv,
           w_br_a, w_br_b, w_br_c, w_out, g_mlp, w_up, w_down):
    h_all = _rmsnorm_two_groups(x_p, x_s, g_mix, BF16)

    w_in_t = w_in.T
    z_main = _matmul_stream(h_all, w_in_t, n_cols=COL_GATES, b_rows_are_outputs=True, name="proj_main")
    z_gate = _matmul(h_all, w_in_t, n_cols=2 * M_HEADS, col_start=COL_GATES, tn=2 * M_HEADS,
                     b_rows_are_outputs=True, name="proj_gates")
    z_xq = _matmul(h_all, w_in_t, n_cols=W_XQ, col_start=COL_REST + W_AQKV, b_rows_are_outputs=True,
                   name="proj_cross_q")
    gate_bias = jnp.stack([b_igate, b_fgate]).astype(F32)

    gp = z_gate[:N_PROMPT].reshape(N_PROMPT, 2, M_HEADS).transpose(2, 1, 0)
    a_p, c_p, n_p, m_p = _mlstm_prompt(z_main, gp, gate_bias)
    gs = z_gate[N_PROMPT:].reshape(DEC_BATCH, DEC_SEQ, 2, M_HEADS).transpose(0, 3, 2, 1)
    gs = jnp.pad(gs, ((0, 0), (0, 0), (0, 0), (0, SAMPLE_CHUNK - DEC_SEQ)))
    mlstm_job = _mlstm_sample_job(
        z_main, gs, gate_bias, state_c, state_n, state_m.reshape(DEC_BATCH, M_HEADS, 1))
    cos2, sin2 = _rope_tables()
    table_spec = pl.BlockSpec((HOST_TM, A_HEAD_DIM), lambda j, i: (i, 0))
    z_aqkv, a_s, c_s, n_s, m_s = _matmul_stream(
        h_all, w_in_t, n_cols=W_AQKV, col_start=COL_REST, b_rows_are_outputs=True, tm=HOST_TM,
        epilogue=_ep_rope, extras=(cos2, sin2), extra_specs=(table_spec, table_spec),
        side=mlstm_job, name="proj_attn")
    a_ps = (a_p, a_s.reshape(N_SAMPLE, W_MV).astype(BF16))

    heads_per_step = (8, 1, 1)
    outs_p, lses_p, rows_p, rows_s = [], [], [], []
    rq = DEC_SEQ * A_HEADS
    qkv_s = z_aqkv[N_PROMPT:].reshape(DEC_BATCH, DEC_SEQ, 3 * N_GROUPS, A_HEADS, A_HEAD_DIM)
    qkv_s = qkv_s.transpose(0, 2, 1, 3, 4).reshape(DEC_BATCH, 3 * N_GROUPS, rq, A_HEAD_DIM)
    for g, (window, r) in enumerate(DIL_GROUPS):
        o, l = _dilated_prompt(z_aqkv, g, r, heads_per_step[g])
        outs_p.append(o)
        lses_p.append(l)
        c0 = 3 * g * W_AH
        keep = min(window, SEQ)
        for part in (1, 2):
            cs = c0 + part * W_AH
            kept = [z_aqkv[(b + 1) * SEQ - keep:(b + 1) * SEQ, cs:cs + W_AH] for b in range(BATCH)]
            rows_p.append(jnp.stack(kept).reshape(BATCH, keep, A_HEADS, A_HEAD_DIM))
            rows_s.append(qkv_s[:, 3 * g + part].reshape(DEC_BATCH, DEC_SEQ, A_HEADS, A_HEAD_DIM))
    yb_p = _combine_groups(outs_p, lses_p, rows_per_step=256)

    mem_h = _rmsnorm(mem_prompt, g_mem, BF16)
    mem_kv = _matmul(mem_h, w_mem_kv, n_cols=2 * W_XQ, tm=BATCH * MEM_LEN, name="mem_kv")
    yc_p = _cross_prompt(z_xq, mem_kv)

    xq_s = z_xq[N_PROMPT:].reshape(DEC_BATCH, DEC_SEQ * X_HEADS, X_HEAD_DIM)
    attention_job = _sample_attention_job(
        qkv_s, caches, xq_s,
        cache_mem_k.reshape(DEC_BATCH, MEM_LEN * X_HEADS, X_HEAD_DIM),
        cache_mem_v.reshape(DEC_BATCH, MEM_LEN * X_HEADS, X_HEAD_DIM))
    z_br, yb_s, yc_s = _matmul_stream(
        h_all, w_in_t, n_cols=3 * D_MODEL, col_start=COL_REST + W_AQKV + W_XQ, b_rows_are_outputs=True,
        tm=HOST_TM, side=attention_job, split_cols=HOST_SPLIT_COLS, name="proj_branch_gates")
    yb_ps = (yb_p, yb_s.reshape(N_SAMPLE, W_AH))
    yc_ps = (yc_p, yc_s.reshape(N_SAMPLE, W_XQ))

    merged = _branch_merge(a_ps, yb_ps, yc_ps, w_br_a, w_br_b, w_br_c, z_br, b_branch.reshape(1, 3 * D_MODEL))
    x1 = _matmul_stream(merged, w_out, n_cols=D_MODEL, tn=MM_TN, epilogue=_ep_residual_two_groups, extras=(x_p, x_s),
                        extra_specs=_two_group_specs(MM_TM, MM_TN, lambda j: j, SUBLANES, False),
                        name="out_proj")

    h2 = _rmsnorm(x1, g_mlp, BF16)
    u = _matmul_stream(h2, w_up, n_cols=D_FF, out_dtype=BF16, epilogue=_ep_relu2, name="mlp_up")
    x2 = _matmul_ksplit_residual(u, w_down, x1)

    mem_k = mem_kv[:, :W_XQ].reshape(BATCH, MEM_LEN, X_HEADS, X_HEAD_DIM)
    mem_v = mem_kv[:, W_XQ:].reshape(BATCH, MEM_LEN, X_HEADS, X_HEAD_DIM)
    prompt_state = (c_p, n_p, m_p[:, :, 0])
    sample_state = (c_s, n_s, m_s[:, :, 0])
    return x2, prompt_state, sample_state, rows_p, rows_s, mem_k, mem_v


def kernel(x_prompt, x_sample, state_mlstm_C, state_mlstm_n, state_mlstm_m,
           cache_win_k_g0, cache_win_v_g0, cache_win_k_g1, cache_win_v_g1,
           cache_win_k_g2, cache_win_v_g2, cache_mem_k, cache_mem_v, mem_prompt,
           g_mix, w_in, b_igate, b_fgate, b_branch, g_mem, w_mem_kv,
           w_br_a, w_br_b, w_br_c, w_out, g_mlp, w_up, w_down, g_final):
    depth = g_mix.shape[0]
    assert depth == 1, "single-layer stack"
    caches = ((cache_win_k_g0[0], cache_win_v_g0[0]),
              (cache_win_k_g1[0], cache_win_v_g1[0]),
              (cache_win_k_g2[0], cache_win_v_g2[0]))
    x2, p_state, s_state, rows_p, rows_s, mem_k, mem_v = _layer(
        x_prompt.reshape(N_PROMPT, D_MODEL), x_sample.reshape(N_SAMPLE, D_MODEL),
        mem_prompt.reshape(BATCH * MEM_LEN, D_MODEL),
        state_mlstm_C, state_mlstm_n, state_mlstm_m[0], caches, cache_mem_k[0], cache_mem_v[0],
        g_mix[0], w_in[0], b_igate[0], b_fgate[0], b_branch[0], g_mem[0], w_mem_kv[0],
        w_br_a[0], w_br_b[0], w_br_c[0], w_out[0], g_mlp[0], w_up[0], w_down[0])
    y_prompt = _rmsnorm(x2, g_final, F32, row_start=0, n_rows=N_PROMPT).reshape(BATCH, SEQ, D_MODEL)
    y_sample = _rmsnorm(x2, g_final, F32, row_start=N_PROMPT, n_rows=N_SAMPLE).reshape(DEC_BATCH, DEC_SEQ, D_MODEL)
    lead = lambda a: a[None]
    return (y_prompt, y_sample,
            lead(p_state[0]), lead(p_state[1]), lead(p_state[2]),
            *[lead(r) for r in rows_p],
            lead(mem_k), lead(mem_v),
            lead(s_state[0]), lead(s_state[1]), lead(s_state[2]),
            *[lead(r) for r in rows_s])
```

```python
import functools
import math
from typing import Callable, NamedTuple

import numpy as np
import jax
import jax.numpy as jnp
from jax import lax
from jax.experimental import pallas as pl
from jax.experimental.pallas import tpu as pltpu

F32 = jnp.float32
BF16 = jnp.bfloat16
NEG_INF = float("-inf")

D_MODEL = 4096
BATCH = 2
SEQ = 4096
DEC_BATCH = 128
DEC_SEQ = 4
PAST_LEN = 2048
MEM_LEN = 256
M_HEADS = 8
M_DQK = D_MODEL // (2 * M_HEADS)
M_DV = D_MODEL // M_HEADS
A_HEAD_DIM = 128
A_HEADS = D_MODEL // 512
DIL_GROUPS = ((128, 1), (512, 4), (2048, 16))
N_GROUPS = 3
ROPE_THETA = 10000.0
X_HEADS = 4
X_HEAD_DIM = 128
D_FF = 4 * D_MODEL
EPS = 1e-6

W_MQK = M_HEADS * M_DQK
W_MV = M_HEADS * M_DV
W_AH = A_HEADS * A_HEAD_DIM
W_XQ = X_HEADS * X_HEAD_DIM
N_PROMPT = BATCH * SEQ
N_SAMPLE = DEC_BATCH * DEC_SEQ
N_ALL = N_PROMPT + N_SAMPLE

COL_GATES = 2 * W_MQK + 2 * W_MV
COL_REST = COL_GATES + 2 * M_HEADS
W_AQKV = 3 * N_GROUPS * W_AH

LANES = 128
SUBLANES = 8
VMEM_LIMIT_BYTES = 60 * 1024 * 1024

MM_TM = 1088
MM_TN = 512
MM_TN_WIDE = 1024
MERGE_TM = 544
HOST_TM = 544
HOST_SPLIT_COLS = 256
DOWN_TM = 2176
DOWN_TK = 2048
NORM_ROWS = 256
CHUNK = 128
SAMPLE_CHUNK = 16
SAMPLE_SEQS_PER_BLOCK = 2
SPAN = 128


def _params(n_axes):
    return pltpu.CompilerParams(
        dimension_semantics=("arbitrary",) * n_axes,
        vmem_limit_bytes=VMEM_LIMIT_BYTES,
    )


def _rmsnorm_body(x_ref, g_ref, o_ref):
    x = x_ref[...]
    ms = jnp.mean(x * x, axis=-1, keepdims=True)
    o_ref[...] = ((x * lax.rsqrt(ms + EPS)) * g_ref[...]).astype(o_ref.dtype)


def _rmsnorm(x, g, out_dtype, *, row_start=0, n_rows=None):
    n_rows = x.shape[0] if n_rows is None else n_rows
    d = x.shape[1]
    off = row_start // NORM_ROWS
    return pl.pallas_call(
        _rmsnorm_body,
        grid=(n_rows // NORM_ROWS,),
        in_specs=[
            pl.BlockSpec((NORM_ROWS, d), lambda i: (i + off, 0)),
            pl.BlockSpec((1, d), lambda i: (0, 0)),
        ],
        out_specs=pl.BlockSpec((NORM_ROWS, d), lambda i: (i, 0)),
        out_shape=jax.ShapeDtypeStruct((n_rows, d), out_dtype),
        compiler_params=_params(1),
        name="rmsnorm",
    )(x, g.reshape(1, d))


def _rmsnorm_two_groups_body(xp_ref, xs_ref, g_ref, o_ref, *, prompt_steps):
    i = pl.program_id(0)

    @pl.when(i < prompt_steps)
    def _():
        _rmsnorm_body(xp_ref, g_ref, o_ref)

    @pl.when(i >= prompt_steps)
    def _():
        _rmsnorm_body(xs_ref, g_ref, o_ref)


def _rmsnorm_two_groups(xp, xs, g, out_dtype):
    d = xp.shape[1]
    ps, ss = xp.shape[0] // NORM_ROWS, xs.shape[0] // NORM_ROWS
    return pl.pallas_call(
        functools.partial(_rmsnorm_two_groups_body, prompt_steps=ps),
        grid=(ps + ss,),
        in_specs=[
            pl.BlockSpec((NORM_ROWS, d), lambda i: (jnp.minimum(i, ps - 1), 0)),
            pl.BlockSpec((NORM_ROWS, d), lambda i: (jnp.maximum(i - ps, 0), 0)),
            pl.BlockSpec((1, d), lambda i: (0, 0)),
        ],
        out_specs=pl.BlockSpec((NORM_ROWS, d), lambda i: (i, 0)),
        out_shape=jax.ShapeDtypeStruct((xp.shape[0] + xs.shape[0], d), out_dtype),
        compiler_params=_params(1),
        name="rmsnorm_two_groups",
    )(xp, xs, g.reshape(1, d))


def _ep_plain(acc, extra, o_ref, j):
    o_ref[...] = acc.astype(o_ref.dtype)


def _ep_residual_two_groups(acc, extra, o_ref, j):
    xp_ref, xs_ref = extra
    tm = acc.shape[0]
    n_full = N_PROMPT // tm
    rem = N_PROMPT - n_full * tm
    assert tm - rem == N_SAMPLE and rem % SUBLANES == 0
    i = pl.program_id(1)

    @pl.when(i < n_full)
    def _():
        o_ref[...] = xp_ref[...] + acc

    @pl.when(i >= n_full)
    def _():
        o_ref[0:rem, :] = xp_ref[tm - rem:tm, :] + acc[0:rem]
        o_ref[rem:tm, :] = xs_ref[...] + acc[rem:tm]


def _two_group_specs(tm, width, col_of_j, row_align, sample_block_is_constant):
    def p_map(j, i):
        row = jnp.minimum(i * tm, N_PROMPT - tm)
        col = col_of_j(j) * width
        return (pl.multiple_of(row, row_align), col if isinstance(col, int) else pl.multiple_of(col, LANES))
    p_spec = pl.BlockSpec((pl.Element(tm), pl.Element(width)), p_map)
    mode = pl.Buffered(1) if sample_block_is_constant else None
    s_spec = pl.BlockSpec((N_SAMPLE, width), lambda j, i: (0, col_of_j(j)), pipeline_mode=mode)
    return p_spec, s_spec


def _ep_relu2(acc, extra, o_ref, j):
    r = jnp.maximum(acc, 0.0)
    o_ref[...] = (r * r).astype(o_ref.dtype)


def _ep_rope(acc, extra, o_ref, j):
    cos_ref, sin_ref = extra
    tn = acc.shape[1]
    assert W_AH % tn == 0
    is_rope = (j // (W_AH // tn)) % 3 != 2
    cos2 = cos_ref[...]
    sin2 = sin_ref[...]
    for s in range(tn // A_HEAD_DIM):
        x = acc[:, s * A_HEAD_DIM:(s + 1) * A_HEAD_DIM]
        rotated = x * cos2 + pltpu.roll(x, A_HEAD_DIM // 2, axis=1) * sin2
        o_ref[:, s * A_HEAD_DIM:(s + 1) * A_HEAD_DIM] = jnp.where(is_rope, rotated, x)


def _mm_body(*refs, b_rows_are_outputs, epilogue, n_extra):
    a_ref, b_ref = refs[0], refs[1]
    extra = refs[2:2 + n_extra]
    o_ref = refs[2 + n_extra]
    b_scr = refs[3 + n_extra]

    @pl.when(pl.program_id(1) == 0)
    def _():
        b_scr[...] = b_ref[...].astype(BF16)

    if b_rows_are_outputs:
        acc = lax.dot_general(a_ref[...], b_scr[...], (((1,), (1,)), ((), ())),
                              preferred_element_type=F32)
    else:
        acc = jnp.dot(a_ref[...], b_scr[...], preferred_element_type=F32)
    epilogue(acc, extra, o_ref, pl.program_id(0))


def _matmul(a, b, *, n_cols, col_start=0, b_rows_are_outputs=False, tm=MM_TM, tn=MM_TN,
            out_dtype=F32, epilogue=_ep_plain, extras=(), extra_specs=(), name="matmul"):
    m, k = a.shape
    if b_rows_are_outputs:
        if col_start % tn == 0:
            b_spec = pl.BlockSpec((tn, k), lambda j, i: (j + col_start // tn, 0))
        else:
            assert col_start % SUBLANES == 0 and tn % SUBLANES == 0
            b_spec = pl.BlockSpec((pl.Element(tn), pl.Element(k)),
                                  lambda j, i: (pl.multiple_of(col_start + j * tn, SUBLANES), 0))
        b_block = (tn, k)
    else:
        assert col_start % tn == 0
        b_spec = pl.BlockSpec((k, tn), lambda j, i: (0, j + col_start // tn))
        b_block = (k, tn)
    return pl.pallas_call(
        functools.partial(_mm_body, b_rows_are_outputs=b_rows_are_outputs, epilogue=epilogue,
                          n_extra=len(extras)),
        grid=(n_cols // tn, m // tm),
        in_specs=[pl.BlockSpec((tm, k), lambda j, i: (i, 0)), b_spec, *extra_specs],
        out_specs=pl.BlockSpec((tm, tn), lambda j, i: (i, j)),
        out_shape=jax.ShapeDtypeStruct((m, n_cols), out_dtype),
        scratch_shapes=[pltpu.VMEM(b_block, BF16)],
        compiler_params=_params(2),
        name=name,
    )(a, b, *extras)


def _narrow_pair_body(a_ref, w0_ref, w1_ref, o0_ref, o1_ref, w0_scr, w1_scr):
    @pl.when(pl.program_id(0) == 0)
    def _():
        w0_scr[...] = w0_ref[...].astype(BF16)
        w1_scr[...] = w1_ref[...].astype(BF16)

    a = a_ref[...]
    nt = (((1,), (1,)), ((), ()))
    o0_ref[...] = lax.dot_general(a, w0_scr[...], nt, preferred_element_type=F32)
    o1_ref[...] = lax.dot_general(a, w1_scr[...], nt, preferred_element_type=F32)


def _narrow_projection_pair(a, w_t, start0, n0, start1, n1, *, tm=MM_TM, name="narrow_pair"):
    m, k = a.shape
    assert start0 % SUBLANES == 0 and start1 % SUBLANES == 0

    def w_spec(start, n):
        return pl.BlockSpec((pl.Element(n), pl.Element(k)), lambda i: (start, 0), pipeline_mode=pl.Buffered(1))

    return pl.pallas_call(
        _narrow_pair_body,
        grid=(m // tm,),
        in_specs=[pl.BlockSpec((tm, k), lambda i: (i, 0)), w_spec(start0, n0), w_spec(start1, n1)],
        out_specs=[pl.BlockSpec((tm, n0), lambda i: (i, 0)), pl.BlockSpec((tm, n1), lambda i: (i, 0))],
        out_shape=[jax.ShapeDtypeStruct((m, n0), F32), jax.ShapeDtypeStruct((m, n1), F32)],
        scratch_shapes=[pltpu.VMEM((n0, k), BF16), pltpu.VMEM((n1, k), BF16)],
        compiler_params=_params(1),
        name=name,
    )(a, w_t, w_t)


class _SideJob(NamedTuple):
    n_steps: int
    inputs: tuple
    in_specs: tuple
    out_shapes: tuple
    out_specs: tuple
    scratch_shapes: tuple
    init: Callable
    pieces: Callable


def _mm_stream_body(*refs, b_rows_are_outputs, epilogue, n_extra, slice_rows, split_cols, side):
    n_side_in = len(side.inputs) if side else 0
    n_side_out = len(side.out_shapes) if side else 0
    n_in = 2 + n_extra + n_side_in
    a_ref, bs_ref = refs[0], refs[1]
    extra = refs[2:2 + n_extra]
    side_in = refs[2 + n_extra:n_in]
    o_ref = refs[n_in]
    side_out = refs[n_in + 1:n_in + 1 + n_side_out]
    b_scr = refs[n_in + 1 + n_side_out]
    side_scr = refs[n_in + 2 + n_side_out:]
    p, i = pl.program_id(0), pl.program_id(1)

    r0 = pl.multiple_of(i * slice_rows, slice_rows)
    b_scr[p % 2, pl.ds(r0, slice_rows), :] = bs_ref[...].astype(BF16)

    if side is not None:
        @pl.when((p == 0) & (i == 0))
        def _():
            side.init(side_scr)

    def multiply(side_pieces):
        slot = (p - 1) % 2
        a = a_ref[...]
        tn = o_ref.shape[1]
        n_split = tn // split_cols
        per_split = -(-len(side_pieces) // n_split)
        for h in range(n_split):
            cols = pl.ds(h * split_cols, split_cols)
            if b_rows_are_outputs:
                acc = lax.dot_general(a, b_scr[slot, cols, :], (((1,), (1,)), ((), ())),
                                      preferred_element_type=F32)
            else:
                acc = jnp.dot(a, b_scr[slot, :, cols], preferred_element_type=F32)
            epilogue(acc, extra, o_ref.at[:, cols], (p - 1) * n_split + h)
            for piece in side_pieces[h * per_split:(h + 1) * per_split]:
                piece()

    if side is None:
        pl.when(p > 0)(lambda: multiply([]))
    else:
        t = (p - 1) * pl.num_programs(1) + i
        pl.when((p > 0) & (t < side.n_steps))(
            lambda: multiply(side.pieces(t, side_in, side_out, side_scr)))
        pl.when((p > 0) & (t >= side.n_steps))(lambda: multiply([]))


def _matmul_stream(a, b, *, n_cols, col_start=0, b_rows_are_outputs=False, tm=MM_TM, tn=MM_TN_WIDE,
                   out_dtype=F32, epilogue=_ep_plain, extras=(), extra_specs=(), side=None,
                   split_cols=MM_TN, name="matmul"):
    m, k = a.shape
    ni, nj = m // tm, n_cols // tn
    assert col_start % SUBLANES == 0 and nj * tn == n_cols

    def side_spec(spec):
        if isinstance(spec, pl.BlockSpec):
            return spec
        block_shape, index_fn = spec
        return pl.BlockSpec(
            block_shape,
            lambda p, i: index_fn(jnp.clip((p - 1) * ni + i, 0, side.n_steps - 1)))

    side_in_specs = [side_spec(s) for s in side.in_specs] if side else []
    side_out_specs = [side_spec(s) for s in side.out_specs] if side else []
    assert side is None or side.n_steps <= nj * ni

    def staged(p):
        return jnp.minimum(p, nj - 1)

    def shifted(index_map):
        return lambda p, i: index_map(jnp.maximum(p - 1, 0), jnp.where(p == 0, 0, i))

    if b_rows_are_outputs:
        tile = (tn, k)
        slice_rows = tn // ni
        bs_spec = pl.BlockSpec(
            (pl.Element(slice_rows), pl.Element(k)),
            lambda p, i: (pl.multiple_of(col_start + staged(p) * tn + i * slice_rows, SUBLANES), 0))
    else:
        assert col_start % tn == 0
        tile = (k, tn)
        slice_rows = k // ni
        bs_spec = pl.BlockSpec((slice_rows, tn), lambda p, i: (i, col_start // tn + staged(p)))
    assert slice_rows * ni == tile[0] and slice_rows % 16 == 0
    extra_specs = [pl.BlockSpec(s.block_shape, shifted(s.index_map), pipeline_mode=s.pipeline_mode)
                   for s in extra_specs]
    outs = pl.pallas_call(
        functools.partial(_mm_stream_body, b_rows_are_outputs=b_rows_are_outputs, epilogue=epilogue,
                          n_extra=len(extras), slice_rows=slice_rows, split_cols=min(tn, split_cols), side=side),
        grid=(nj + 1, ni),
        in_specs=[pl.BlockSpec((tm, k), shifted(lambda j, i: (i, 0))), bs_spec, *extra_specs,
                  *side_in_specs],
        out_specs=[pl.BlockSpec((tm, tn), shifted(lambda j, i: (i, j))), *side_out_specs],
        out_shape=[jax.ShapeDtypeStruct((m, n_cols), out_dtype), *(side.out_shapes if side else ())],
        scratch_shapes=[pltpu.VMEM((2,) + tile, BF16), *(side.scratch_shapes if side else ())],
        compiler_params=_params(2),
        name=name,
    )(a, b, *extras, *(side.inputs if side else ()))
    return outs if side else outs[0]


def _mm_ksplit_body(a_ref, b_ref, res_ref, o_ref, acc_ref):
    kk = pl.program_id(2)
    last = pl.num_programs(2) - 1

    def product():
        return jnp.dot(a_ref[...], b_ref[...].astype(BF16), preferred_element_type=F32)

    @pl.when(kk == 0)
    def _():
        acc_ref[...] = product()

    @pl.when((kk > 0) & (kk < last))
    def _():
        acc_ref[...] += product()

    @pl.when(kk == last)
    def _():
        o_ref[...] = res_ref[...] + (acc_ref[...] + product())


def _matmul_ksplit_residual(a, b, res, *, tm=DOWN_TM, tn=MM_TN, tk=DOWN_TK):
    m, k = a.shape
    n = b.shape[1]
    return pl.pallas_call(
        _mm_ksplit_body,
        grid=(n // tn, m // tm, k // tk),
        in_specs=[
            pl.BlockSpec((tm, tk), lambda j, i, kk: (i, kk)),
            pl.BlockSpec((tk, tn), lambda j, i, kk: (kk, j)),
            pl.BlockSpec((tm, tn), lambda j, i, kk: (i, j)),
        ],
        out_specs=pl.BlockSpec((tm, tn), lambda j, i, kk: (i, j)),
        out_shape=jax.ShapeDtypeStruct((m, n), F32),
        scratch_shapes=[pltpu.VMEM((tm, tn), F32)],
        compiler_params=_params(3),
        name="mlp_down",
    )(a, b, res)


def _sigmoid(x):
    return 1.0 / (1.0 + jnp.exp(-x))


def _straddle_rows(p_ref, s_ref):
    tm = p_ref.shape[0]
    rem = tm - s_ref.shape[0]
    return jnp.concatenate([p_ref[tm - rem:tm, :], s_ref[...]], axis=0)


def _merge_body(ap_ref, as_ref, ybp_ref, ybs_ref, ycp_ref, ycs_ref,
                wa0_ref, wb0_ref, wc0_ref, was_ref, wbs_ref, wcs_ref,
                ga_ref, gb_ref, gc_ref, ba_ref, bb_ref, bc_ref,
                o_ref, wa_scr, wb_scr, wc_scr):
    j, i = pl.program_id(0), pl.program_id(1)
    cur = j % 2

    @pl.when((j == 0) & (i == 0))
    def _():
        wa_scr[0] = wa0_ref[...].astype(BF16)
        wb_scr[0] = wb0_ref[...].astype(BF16)
        wc_scr[0] = wc0_ref[...].astype(BF16)

    for scr, slice_ref in ((wa_scr, was_ref), (wb_scr, wbs_ref), (wc_scr, wcs_ref)):
        rows = slice_ref.shape[0]
        scr[1 - cur, pl.ds(pl.multiple_of(i * rows, rows), rows), :] = slice_ref[...].astype(BF16)

    def merge(a, yb, yc):
        for h in range(o_ref.shape[1] // HOST_SPLIT_COLS):
            cols = pl.ds(h * HOST_SPLIT_COLS, HOST_SPLIT_COLS)
            y_a = jnp.dot(a, wa_scr[cur, :, cols], preferred_element_type=F32)
            y_b = jnp.dot(yb, wb_scr[cur, :, cols], preferred_element_type=F32)
            y_c = jnp.dot(yc, wc_scr[cur, :, cols], preferred_element_type=F32)
            g_a = _sigmoid(ga_ref[:, cols] + ba_ref[:, cols])
            g_b = _sigmoid(gb_ref[:, cols] + bb_ref[:, cols])
            g_c = _sigmoid(gc_ref[:, cols] + bc_ref[:, cols])
            o_ref[:, cols] = (g_a * y_a + g_b * y_b + g_c * y_c).astype(o_ref.dtype)

    is_prompt_tile = i < N_PROMPT // ap_ref.shape[0]

    @pl.when(is_prompt_tile)
    def _():
        merge(ap_ref[...], ybp_ref[...], ycp_ref[...])

    @pl.when(jnp.logical_not(is_prompt_tile))
    def _():
        merge(_straddle_rows(ap_ref, as_ref), _straddle_rows(ybp_ref, ybs_ref), _straddle_rows(ycp_ref, ycs_ref))


def _branch_merge(a_ps, yb_ps, yc_ps, w_a, w_b, w_c, z_br, b_branch, *, tm=MERGE_TM, tn=MM_TN):
    m = N_ALL
    assert (N_PROMPT // tm + 1) * tm == N_ALL
    nj = D_MODEL // tn

    def gate_spec(part):
        return pl.BlockSpec((tm, tn), lambda j, i: (i, part * nj + j))

    def bias_spec(part):
        return pl.BlockSpec((1, tn), lambda j, i: (0, part * nj + j))

    ni = m // tm
    widths = (W_MV, W_AH, W_XQ)
    assert all(k % (16 * ni) == 0 for k in widths)

    def first_tile_spec(k):
        return pl.BlockSpec((k, tn), lambda j, i: (0, 0), pipeline_mode=pl.Buffered(1))

    def next_slice_spec(k):
        return pl.BlockSpec((k // ni, tn), lambda j, i: (i, jnp.minimum(j + 1, nj - 1)))

    bf16_rows = 16
    return pl.pallas_call(
        _merge_body,
        grid=(nj, ni),
        in_specs=[
            *_two_group_specs(tm, W_MV, lambda j: 0, bf16_rows, True),
            *_two_group_specs(tm, W_AH, lambda j: 0, bf16_rows, True),
            *_two_group_specs(tm, W_XQ, lambda j: 0, bf16_rows, True),
            *[first_tile_spec(k) for k in widths],
            *[next_slice_spec(k) for k in widths],
            gate_spec(0), gate_spec(1), gate_spec(2),
            bias_spec(0), bias_spec(1), bias_spec(2),
        ],
        out_specs=pl.BlockSpec((tm, tn), lambda j, i: (i, j)),
        out_shape=jax.ShapeDtypeStruct((m, D_MODEL), BF16),
        scratch_shapes=[pltpu.VMEM((2, k, tn), BF16) for k in widths],
        compiler_params=_params(2),
        name="branch_merge",
    )(*a_ps, *yb_ps, *yc_ps, w_a, w_b, w_c, w_a, w_b, w_c,
      z_br, z_br, z_br, b_branch, b_branch, b_branch)


def _log_sigmoid(x):
    return jnp.minimum(x, 0.0) - jnp.log1p(jnp.exp(-jnp.abs(x)))


def _mlstm_chunk(q, k, v, irow, frow, c_state, n_state, m_state, n_valid):
    L = q.shape[0]
    ti = lax.broadcasted_iota(jnp.int32, (L, L), 0)
    si = lax.broadcasted_iota(jnp.int32, (L, L), 1)
    causal = si <= ti
    eye = si == ti
    f_b = jnp.broadcast_to(frow, (L, L))
    i_b = jnp.broadcast_to(irow, (L, L))
    bcol = jnp.sum(jnp.where(causal, f_b, 0.0), axis=1, keepdims=True)
    fcol = jnp.sum(jnp.where(eye, f_b, 0.0), axis=1, keepdims=True)
    icol = jnp.sum(jnp.where(eye, i_b, 0.0), axis=1, keepdims=True)
    brow = jnp.sum(jnp.where(ti <= si, jnp.broadcast_to(fcol, (L, L)), 0.0), axis=0, keepdims=True)

    acol = bcol + m_state
    logw = jnp.where(causal, bcol - brow + irow, NEG_INF)
    mt = jnp.maximum(acol, jnp.max(logw, axis=1, keepdims=True))
    w_inter = jnp.exp(acol - mt)
    w_intra = jnp.exp(logw - mt)

    qb = q.astype(BF16)
    kb = k.astype(BF16)
    nt = (((1,), (1,)), ((), ()))
    s = lax.dot_general(qb, kb, nt, preferred_element_type=F32) * w_intra
    inter = lax.dot_general(qb, c_state.astype(BF16), nt, preferred_element_type=F32)
    num = w_inter * inter + jnp.dot(s.astype(BF16), v.astype(BF16), preferred_element_type=F32)
    nq = w_inter * jnp.sum(q * n_state, axis=1, keepdims=True) + jnp.sum(s, axis=1, keepdims=True)
    h = num / jnp.maximum(jnp.abs(nq), jnp.exp(-mt))

    last = slice(n_valid - 1, n_valid)
    m_end = mt[last, :]
    w_c = jnp.exp(acol[last, :] - m_end)
    w_s = jnp.exp(bcol[last, :] - bcol + icol - m_end)
    if n_valid < L:
        w_s = jnp.where(lax.broadcasted_iota(jnp.int32, (L, 1), 0) < n_valid, w_s, 0.0)
    tn = (((0,), (0,)), ((), ()))
    c_new = w_c * c_state + lax.dot_general((v * w_s).astype(BF16), kb, tn, preferred_element_type=F32)
    n_new = w_c * n_state + jnp.sum(w_s * k, axis=0, keepdims=True)
    return h, c_new, n_new, m_end


def _gate_rows(g_ref_val, bias_ref, head):
    irow = g_ref_val[0:1, :] + bias_ref[0, head]
    frow = _log_sigmoid(g_ref_val[1:2, :] + bias_ref[1, head])
    return irow, frow


def _mlstm_prompt_body(bias_ref, q_ref, k_ref, v_ref, mo_ref, g_ref,
                       a_ref, c_ref, n_ref, m_ref):
    @pl.when(pl.program_id(1) == 0)
    def _():
        c_ref[...] = jnp.zeros_like(c_ref)
        n_ref[...] = jnp.zeros_like(n_ref)
        m_ref[...] = jnp.zeros_like(m_ref)

    for head in range(M_HEADS):
        qk = pl.ds(head * M_DQK, M_DQK)
        vo = pl.ds(head * M_DV, M_DV)
        one = pl.ds(head, 1)
        irow, frow = _gate_rows(g_ref[head], bias_ref, head)
        h, c_new, n_new, m_end = _mlstm_chunk(
            q_ref[:, qk], k_ref[:, qk] * (M_DQK ** -0.5), v_ref[:, vo], irow, frow,
            c_ref[0, head], n_ref[0, one, :], m_ref[0, one, 0:1], CHUNK)
        a_ref[:, vo] = (_sigmoid(mo_ref[:, vo]) * h).astype(a_ref.dtype)
        c_ref[0, head] = c_new
        n_ref[0, one, :] = n_new
        m_ref[0, one, :] = jnp.broadcast_to(m_end, (1, LANES))


def _mlstm_prompt(z_main, gates_rows, gate_bias):
    nc = SEQ // CHUNK
    return pl.pallas_call(
        _mlstm_prompt_body,
        grid=(BATCH, nc),
        in_specs=[
            pl.BlockSpec(memory_space=pltpu.SMEM),
            pl.BlockSpec((CHUNK, W_MQK), lambda b, c: (b * nc + c, 0)),
            pl.BlockSpec((CHUNK, W_MQK), lambda b, c: (b * nc + c, 1)),
            pl.BlockSpec((CHUNK, W_MV), lambda b, c: (b * nc + c, 2 * W_MQK // W_MV)),
            pl.BlockSpec((CHUNK, W_MV), lambda b, c: (b * nc + c, 2 * W_MQK // W_MV + 1)),
            pl.BlockSpec((M_HEADS, 2, CHUNK), lambda b, c: (0, 0, b * nc + c)),
        ],
        out_specs=[
            pl.BlockSpec((CHUNK, W_MV), lambda b, c: (b * nc + c, 0)),
            pl.BlockSpec((1, M_HEADS, M_DV, M_DQK), lambda b, c: (b, 0, 0, 0)),
            pl.BlockSpec((1, M_HEADS, M_DQK), lambda b, c: (b, 0, 0)),
            pl.BlockSpec((1, M_HEADS, LANES), lambda b, c: (b, 0, 0)),
        ],
        out_shape=[
            jax.ShapeDtypeStruct((N_PROMPT, W_MV), BF16),
            jax.ShapeDtypeStruct((BATCH, M_HEADS, M_DV, M_DQK), F32),
            jax.ShapeDtypeStruct((BATCH, M_HEADS, M_DQK), F32),
            jax.ShapeDtypeStruct((BATCH, M_HEADS, LANES), F32),
        ],
        compiler_params=_params(2),
        name="mlstm_prompt",
    )(gate_bias, z_main, z_main, z_main, z_main, gates_rows)


def _mlstm_sample_init(scratch):
    for scr in scratch:
        scr[...] = jnp.zeros_like(scr)


def _mlstm_sample_pieces(t, ins, outs, scratch):
    bias_ref, q_ref, k_ref, v_ref, mo_ref, g_ref, c0_ref, n0_ref, m0_ref = ins
    a_ref, c_ref, n_ref, m_ref = outs
    q_scr, k_scr, v_scr = scratch
    first_half = (t % SAMPLE_SEQS_PER_BLOCK) == 0

    def seq_rows(ref, cols):
        return jnp.where(first_half, ref[0:DEC_SEQ, cols], ref[DEC_SEQ:2 * DEC_SEQ, cols])

    def one_head(head):
        qk = pl.ds(head * M_DQK, M_DQK)
        vo = pl.ds(head * M_DV, M_DV)
        one = pl.ds(head, 1)
        q_scr[0:DEC_SEQ, qk] = seq_rows(q_ref, qk)
        k_scr[0:DEC_SEQ, qk] = seq_rows(k_ref, qk) * (M_DQK ** -0.5)
        v_scr[0:DEC_SEQ, vo] = seq_rows(v_ref, vo)
        irow, frow = _gate_rows(g_ref[0, head], bias_ref, head)
        h, c_new, n_new, m_end = _mlstm_chunk(
            q_scr[:, qk], k_scr[:, qk], v_scr[:, vo], irow, frow,
            c0_ref[0, 0, head], n0_ref[0, 0, one, :], m0_ref[0, one, :], DEC_SEQ)
        a_ref[0, :, vo] = _sigmoid(seq_rows(mo_ref, vo)) * h[0:DEC_SEQ, :]
        c_ref[0, head] = c_new
        n_ref[0, one, :] = n_new
        m_ref[0, one, :] = jnp.broadcast_to(m_end, (1, LANES))

    def all_heads():
        for head in range(M_HEADS):
            one_head(head)

    return [all_heads]


def _mlstm_sample_job(z_main, gates_rows, gate_bias, c0, n0, m0):
    per = SAMPLE_SEQS_PER_BLOCK
    rows = per * DEC_SEQ
    assert rows == SUBLANES and N_PROMPT % rows == 0
    r0 = N_PROMPT // rows
    v_blk = 2 * W_MQK // W_MV
    return _SideJob(
        n_steps=DEC_BATCH,
        inputs=(gate_bias, z_main, z_main, z_main, z_main, gates_rows, c0, n0, m0),
        in_specs=(
            pl.BlockSpec(memory_space=pltpu.SMEM),
            ((rows, W_MQK), lambda t: (r0 + t // per, 0)),
            ((rows, W_MQK), lambda t: (r0 + t // per, 1)),
            ((rows, W_MV), lambda t: (r0 + t // per, v_blk)),
            ((rows, W_MV), lambda t: (r0 + t // per, v_blk + 1)),
            ((1, M_HEADS, 2, SAMPLE_CHUNK), lambda t: (t, 0, 0, 0)),
            ((1, 1, M_HEADS, M_DV, M_DQK), lambda t: (0, t, 0, 0, 0)),
            ((1, 1, M_HEADS, M_DQK), lambda t: (0, t, 0, 0)),
            ((1, M_HEADS, 1), lambda t: (t, 0, 0)),
        ),
        out_shapes=(
            jax.ShapeDtypeStruct((DEC_BATCH, DEC_SEQ, W_MV), F32),
            jax.ShapeDtypeStruct((DEC_BATCH, M_HEADS, M_DV, M_DQK), F32),
            jax.ShapeDtypeStruct((DEC_BATCH, M_HEADS, M_DQK), F32),
            jax.ShapeDtypeStruct((DEC_BATCH, M_HEADS, LANES), F32),
        ),
        out_specs=(
            ((1, DEC_SEQ, W_MV), lambda t: (t, 0, 0)),
            ((1, M_HEADS, M_DV, M_DQK), lambda t: (t, 0, 0, 0)),
            ((1, M_HEADS, M_DQK), lambda t: (t, 0, 0)),
            ((1, M_HEADS, LANES), lambda t: (t, 0, 0)),
        ),
        scratch_shapes=(pltpu.VMEM((SAMPLE_CHUNK, W_MQK), F32), pltpu.VMEM((SAMPLE_CHUNK, W_MQK), F32),
                        pltpu.VMEM((SAMPLE_CHUNK, W_MV), F32)),
        init=_mlstm_sample_init,
        pieces=_mlstm_sample_pieces,
    )


def _dil_prompt_body(q_ref, kp_ref, kc_ref, vp_ref, vc_ref, o_ref, l_ref, *, r, hb):
    first_key = jnp.where(pl.program_id(2) > 0, 0, SPAN)
    qi = lax.broadcasted_iota(jnp.int32, (SPAN, 2 * SPAN), 0)
    ki = lax.broadcasted_iota(jnp.int32, (SPAN, 2 * SPAN), 1)
    ok = (ki >= qi) & (ki <= qi + SPAN) & (ki >= first_key)
    bias = jnp.where(ok, 0.0, NEG_INF)
    scale = A_HEAD_DIM ** -0.5
    nt = (((1,), (1,)), ((), ()))
    for c in range(r):
        rows = pl.ds(c, SPAN, stride=r) if r > 1 else pl.ds(0, SPAN)
        for hh in range(hb):
            cols = pl.ds(hh * A_HEAD_DIM, A_HEAD_DIM)
            q = q_ref[rows, cols].astype(BF16)
            kk = jnp.concatenate([kp_ref[rows, cols], kc_ref[rows, cols]], axis=0).astype(BF16)
            vv = jnp.concatenate([vp_ref[rows, cols], vc_ref[rows, cols]], axis=0).astype(BF16)
            s = lax.dot_general(q, kk, nt, preferred_element_type=F32) * scale + bias
            m = jnp.max(s, axis=1, keepdims=True)
            p = jnp.exp(s - m)
            den = jnp.sum(p, axis=1, keepdims=True)
            o = jnp.dot(p.astype(BF16), vv, preferred_element_type=F32) / den
            o_ref[rows, cols] = o
            l_ref[rows, cols] = jnp.broadcast_to(m + jnp.log(den), (SPAN, A_HEAD_DIM))


def _dilated_prompt(z_aqkv, g, r, hb):
    rows = SPAN * r
    nblk = SEQ // rows
    wcol = hb * A_HEAD_DIM
    per_part = W_AH // wcol

    def spec(part, prev):
        def imap(b, hg, n):
            nn = jnp.maximum(n - 1, 0) if prev else n
            return (b * nblk + nn, (3 * g + part) * per_part + hg)
        return pl.BlockSpec((rows, wcol), imap)

    out_spec = pl.BlockSpec((rows, wcol), lambda b, hg, n: (b * nblk + n, hg))
    return pl.pallas_call(
        functools.partial(_dil_prompt_body, r=r, hb=hb),
        grid=(BATCH, A_HEADS // hb, nblk),
        in_specs=[spec(0, False), spec(1, True), spec(1, False), spec(2, True), spec(2, False)],
        out_specs=[out_spec, out_spec],
        out_shape=[jax.ShapeDtypeStruct((N_PROMPT, W_AH), F32)] * 2,
        compiler_params=_params(3),
        name=f"dilated_prompt_g{g}",
    )(z_aqkv, z_aqkv, z_aqkv, z_aqkv, z_aqkv)


def _sample_attn_pieces(t, ins, out_refs, scratch):
    (qkv_ref, k0_ref, v0_ref, k1_ref, v1_ref, k2_ref, v2_ref, b0_ref, b1_ref, b2_ref,
     xq_ref, mk_ref, mv_ref, bx_ref) = ins
    yb_ref, yc_ref = out_refs
    nt = (((1,), (1,)), ((), ()))
    outs, lses = [], []
    groups = ((k0_ref, v0_ref, b0_ref), (k1_ref, v1_ref, b1_ref), (k2_ref, v2_ref, b2_ref))

    def cross():
        s = lax.dot_general(xq_ref[0].astype(BF16), mk_ref[0].astype(BF16), nt,
                            preferred_element_type=F32) * (X_HEAD_DIM ** -0.5) + bx_ref[...]
        m = jnp.max(s, axis=1, keepdims=True)
        ex = jnp.exp(s - m)
        p = ex / jnp.sum(ex, axis=1, keepdims=True)
        yc_ref[0] = jnp.dot(p.astype(BF16), mv_ref[0].astype(BF16),
                            preferred_element_type=F32).astype(yc_ref.dtype)

    def group(g):
        kc_ref, vc_ref, bias_ref = groups[g]
        n_cached = kc_ref.shape[1] * kc_ref.shape[2]
        q = qkv_ref[0, 3 * g].astype(BF16)
        kk = jnp.concatenate([kc_ref[0].reshape(n_cached, A_HEAD_DIM), qkv_ref[0, 3 * g + 1]],
                             axis=0).astype(BF16)
        vv = jnp.concatenate([vc_ref[0].reshape(n_cached, A_HEAD_DIM), qkv_ref[0, 3 * g + 2]],
                             axis=0).astype(BF16)
        s = lax.dot_general(q, kk, nt, preferred_element_type=F32) * (A_HEAD_DIM ** -0.5) + bias_ref[...]
        m = jnp.max(s, axis=1, keepdims=True)
        p = jnp.exp(s - m)
        den = jnp.sum(p, axis=1, keepdims=True)
        outs.append(jnp.dot(p.astype(BF16), vv, preferred_element_type=F32) / den)
        lses.append(m + jnp.log(den))
        if g < N_GROUPS - 1:
            return
        mx = jnp.maximum(jnp.maximum(lses[0], lses[1]), lses[2])
        e = [jnp.exp(l - mx) for l in lses]
        tot = e[0] + e[1] + e[2]
        yb = (e[0] / tot) * outs[0] + (e[1] / tot) * outs[1] + (e[2] / tot) * outs[2]
        yb_ref[0] = yb.astype(yb_ref.dtype)

    return [cross] + [functools.partial(group, g) for g in range(N_GROUPS)]


def _dilated_sample_bias(window, r, lb, rc):
    span = window // r
    n_c = (lb // r) * rc * A_HEADS
    rows = np.arange(DEC_SEQ * A_HEADS)
    s_q, h_q = rows // A_HEADS, rows % A_HEADS
    col = np.arange(n_c)
    m_k = col // (rc * A_HEADS)
    c_k = (col % (rc * A_HEADS)) // A_HEADS
    h_k = col % A_HEADS
    delta = (lb + s_q)[:, None] - (m_k * r + c_k)[None, :]
    ok_c = (h_q[:, None] == h_k[None, :]) & (delta % r == 0) & (delta // r <= span) & (delta >= 0)
    coln = np.arange(DEC_SEQ * A_HEADS)
    s_n, h_n = coln // A_HEADS, coln % A_HEADS
    dn = s_q[:, None] - s_n[None, :]
    ok_n = (h_q[:, None] == h_n[None, :]) & (dn >= 0) & (dn % r == 0) & (dn // r <= span)
    ok = np.concatenate([ok_c, ok_n], axis=1)
    return np.where(ok, 0.0, -np.inf).astype(np.float32)


def _sample_attention_job(qkv, caches, xq, mem_k, mem_v):
    rq = DEC_SEQ * A_HEADS
    rx = DEC_SEQ * X_HEADS
    nk = MEM_LEN * X_HEADS
    cache_args, cache_specs, biases = [], [], []
    for (window, r), (cache_k, cache_v) in zip(DIL_GROUPS, caches):
        lb = cache_k.shape[1]
        assert lb % r == 0 and window % r == 0
        rc = min(r, DEC_SEQ)
        nm = lb // r
        spec = ((1, nm, rc * A_HEADS, A_HEAD_DIM), lambda t: (t, 0, 0, 0))
        for c in (cache_k, cache_v):
            cache_args.append(c.reshape(DEC_BATCH, nm, r * A_HEADS, A_HEAD_DIM))
            cache_specs.append(spec)
        biases.append(jnp.asarray(_dilated_sample_bias(window, r, lb, rc)))
    ok = (np.arange(rx) % X_HEADS)[:, None] == (np.arange(nk) % X_HEADS)[None, :]
    bias_x = jnp.asarray(np.where(ok, 0.0, -np.inf).astype(np.float32))

    def const_spec(a):
        return pl.BlockSpec(a.shape, lambda p, i: (0, 0), pipeline_mode=pl.Buffered(1))

    return _SideJob(
        n_steps=DEC_BATCH,
        inputs=(qkv, *cache_args, *biases, xq, mem_k, mem_v, bias_x),
        in_specs=(
            ((1, 3 * N_GROUPS, rq, A_HEAD_DIM), lambda t: (t, 0, 0, 0)),
            *cache_specs,
            *[const_spec(a) for a in biases],
            ((1, rx, X_HEAD_DIM), lambda t: (t, 0, 0)),
            ((1, nk, X_HEAD_DIM), lambda t: (t, 0, 0)),
            ((1, nk, X_HEAD_DIM), lambda t: (t, 0, 0)),
            const_spec(bias_x),
        ),
        out_shapes=(jax.ShapeDtypeStruct((DEC_BATCH, rq, A_HEAD_DIM), BF16),
                    jax.ShapeDtypeStruct((DEC_BATCH, rx, X_HEAD_DIM), BF16)),
        out_specs=(((1, rq, A_HEAD_DIM), lambda t: (t, 0, 0)),
                   ((1, rx, X_HEAD_DIM), lambda t: (t, 0, 0))),
        scratch_shapes=(),
        init=lambda scratch: None,
        pieces=_sample_attn_pieces,
    )


def _cross_prompt_body(q_ref, k_ref, v_ref, o_ref):
    scale = X_HEAD_DIM ** -0.5
    nt = (((1,), (1,)), ((), ()))
    for h in range(X_HEADS):
        cols = pl.ds(h * X_HEAD_DIM, X_HEAD_DIM)
        s = lax.dot_general(q_ref[:, cols].astype(BF16), k_ref[:, cols].astype(BF16), nt,
                            preferred_element_type=F32) * scale
        m = jnp.max(s, axis=1, keepdims=True)
        e = jnp.exp(s - m)
        p = e / jnp.sum(e, axis=1, keepdims=True)
        o_ref[:, cols] = jnp.dot(p.astype(BF16), v_ref[:, cols].astype(BF16),
                                 preferred_element_type=F32).astype(o_ref.dtype)


def _cross_prompt(z_xq, mem_kv, *, tq=512):
    nq = SEQ // tq
    return pl.pallas_call(
        _cross_prompt_body,
        grid=(BATCH, nq),
        in_specs=[
            pl.BlockSpec((tq, W_XQ), lambda b, i: (b * nq + i, 0)),
            pl.BlockSpec((MEM_LEN, W_XQ), lambda b, i: (b, 0)),
            pl.BlockSpec((MEM_LEN, W_XQ), lambda b, i: (b, 1)),
        ],
        out_specs=pl.BlockSpec((tq, W_XQ), lambda b, i: (b * nq + i, 0)),
        out_shape=jax.ShapeDtypeStruct((N_PROMPT, W_XQ), BF16),
        compiler_params=_params(2),
        name="cross_prompt",
    )(z_xq, mem_kv, mem_kv)


def _combine_body(o0, o1, o2, l0, l1, l2, y_ref):
    a0, a1, a2 = l0[...], l1[...], l2[...]
    mx = jnp.maximum(jnp.maximum(a0, a1), a2)
    e0, e1, e2 = jnp.exp(a0 - mx), jnp.exp(a1 - mx), jnp.exp(a2 - mx)
    tot = e0 + e1 + e2
    y = (e0 / tot) * o0[...] + (e1 / tot) * o1[...] + (e2 / tot) * o2[...]
    y_ref[...] = y.astype(y_ref.dtype)


def _combine_groups(outs, lses, *, rows_per_step):
    n, w = outs[0].shape
    spec = pl.BlockSpec((rows_per_step, w), lambda i: (i, 0))
    return pl.pallas_call(
        _combine_body,
        grid=(n // rows_per_step,),
        in_specs=[spec] * 6,
        out_specs=spec,
        out_shape=jax.ShapeDtypeStruct((n, w), BF16),
        compiler_params=_params(1),
        name="combine_groups",
    )(*outs, *lses)


def _rope_tables():
    pos = jnp.concatenate([
        jnp.tile(jnp.arange(SEQ, dtype=jnp.int32), BATCH),
        jnp.tile(PAST_LEN + jnp.arange(DEC_SEQ, dtype=jnp.int32), DEC_BATCH)])
    inv = ROPE_THETA ** (-jnp.arange(0, A_HEAD_DIM, 2, dtype=F32) / A_HEAD_DIM)
    ang = pos.astype(F32)[:, None] * inv[None, :]
    cos, sin = jnp.cos(ang), jnp.sin(ang)
    return jnp.concatenate([cos, cos], axis=1), jnp.concatenate([-sin, sin], axis=1)


def _layer(x_p, x_s, mem_prompt, state_c, state_n, state_m, caches, cache_mem_k, cache_mem_v,
           g_mix, w_in, b_igate, b_fgate, b_branch, g_mem, w_mem_kv,
           w_br_a, w_br_b, w_br_c, w_out, g_mlp, w_up, w_down):
    h_all = _rmsnorm_two_groups(x_p, x_s, g_mix, BF16)

    w_in_t = w_in.T
    z_main = _matmul_stream(h_all, w_in_t, n_cols=COL_GATES, b_rows_are_outputs=True, name="proj_main")
    z_gate, z_xq = _narrow_projection_pair(h_all, w_in_t, COL_GATES, 2 * M_HEADS, COL_REST + W_AQKV, W_XQ,
                                           name="proj_gates_cross_q")
    gate_bias = jnp.stack([b_igate, b_fgate]).astype(F32)

    gp = z_gate[:N_PROMPT].reshape(N_PROMPT, 2, M_HEADS).transpose(2, 1, 0)
    a_p, c_p, n_p, m_p = _mlstm_prompt(z_main, gp, gate_bias)
    gs = z_gate[N_PROMPT:].reshape(DEC_BATCH, DEC_SEQ, 2, M_HEADS).transpose(0, 3, 2, 1)
    gs = jnp.pad(gs, ((0, 0), (0, 0), (0, 0), (0, SAMPLE_CHUNK - DEC_SEQ)))
    mlstm_job = _mlstm_sample_job(
        z_main, gs, gate_bias, state_c, state_n, state_m.reshape(DEC_BATCH, M_HEADS, 1))
    cos2, sin2 = _rope_tables()
    table_spec = pl.BlockSpec((HOST_TM, A_HEAD_DIM), lambda j, i: (i, 0))
    z_aqkv, a_s, c_s, n_s, m_s = _matmul_stream(
        h_all, w_in_t, n_cols=W_AQKV, col_start=COL_REST, b_rows_are_outputs=True, tm=HOST_TM,
        epilogue=_ep_rope, extras=(cos2, sin2), extra_specs=(table_spec, table_spec),
        side=mlstm_job, name="proj_attn")
    a_ps = (a_p, a_s.reshape(N_SAMPLE, W_MV).astype(BF16))

    heads_per_step = (8, 1, 1)
    outs_p, lses_p, rows_p, rows_s = [], [], [], []
    rq = DEC_SEQ * A_HEADS
    qkv_s = z_aqkv[N_PROMPT:].reshape(DEC_BATCH, DEC_SEQ, 3 * N_GROUPS, A_HEADS, A_HEAD_DIM)
    qkv_s = qkv_s.transpose(0, 2, 1, 3, 4).reshape(DEC_BATCH, 3 * N_GROUPS, rq, A_HEAD_DIM)
    for g, (window, r) in enumerate(DIL_GROUPS):
        o, l = _dilated_prompt(z_aqkv, g, r, heads_per_step[g])
        outs_p.append(o)
        lses_p.append(l)
        c0 = 3 * g * W_AH
        keep = min(window, SEQ)
        for part in (1, 2):
            cs = c0 + part * W_AH
            kept = [z_aqkv[(b + 1) * SEQ - keep:(b + 1) * SEQ, cs:cs + W_AH] for b in range(BATCH)]
            rows_p.append(jnp.stack(kept).reshape(BATCH, keep, A_HEADS, A_HEAD_DIM))
            rows_s.append(qkv_s[:, 3 * g + part].reshape(DEC_BATCH, DEC_SEQ, A_HEADS, A_HEAD_DIM))
    yb_p = _combine_groups(outs_p, lses_p, rows_per_step=256)

    mem_h = _rmsnorm(mem_prompt, g_mem, BF16)
    mem_kv = _matmul(mem_h, w_mem_kv, n_cols=2 * W_XQ, tm=BATCH * MEM_LEN, name="mem_kv")
    yc_p = _cross_prompt(z_xq, mem_kv)

    xq_s = z_xq[N_PROMPT:].reshape(DEC_BATCH, DEC_SEQ * X_HEADS, X_HEAD_DIM)
    attention_job = _sample_attention_job(
        qkv_s, caches, xq_s,
        cache_mem_k.reshape(DEC_BATCH, MEM_LEN * X_HEADS, X_HEAD_DIM),
        cache_mem_v.reshape(DEC_BATCH, MEM_LEN * X_HEADS, X_HEAD_DIM))
    z_br, yb_s, yc_s = _matmul_stream(
        h_all, w_in_t, n_cols=3 * D_MODEL, col_start=COL_REST + W_AQKV + W_XQ, b_rows_are_outputs=True,
        tm=HOST_TM, side=attention_job, split_cols=HOST_SPLIT_COLS, name="proj_branch_gates")
    yb_ps = (yb_p, yb_s.reshape(N_SAMPLE, W_AH))
    yc_ps = (yc_p, yc_s.reshape(N_SAMPLE, W_XQ))

    merged = _branch_merge(a_ps, yb_ps, yc_ps, w_br_a, w_br_b, w_br_c, z_br, b_branch.reshape(1, 3 * D_MODEL))
    x1 = _matmul_stream(merged, w_out, n_cols=D_MODEL, tn=MM_TN, epilogue=_ep_residual_two_groups, extras=(x_p, x_s),
                        extra_specs=_two_group_specs(MM_TM, MM_TN, lambda j: j, SUBLANES, False),
                        name="out_proj")

    h2 = _rmsnorm(x1, g_mlp, BF16)
    u = _matmul_stream(h2, w_up, n_cols=D_FF, out_dtype=BF16, epilogue=_ep_relu2, name="mlp_up")
    x2 = _matmul_ksplit_residual(u, w_down, x1)

    mem_k = mem_kv[:, :W_XQ].reshape(BATCH, MEM_LEN, X_HEADS, X_HEAD_DIM)
    mem_v = mem_kv[:, W_XQ:].reshape(BATCH, MEM_LEN, X_HEADS, X_HEAD_DIM)
    prompt_state = (c_p, n_p, m_p[:, :, 0])
    sample_state = (c_s, n_s, m_s[:, :, 0])
    return x2, prompt_state, sample_state, rows_p, rows_s, mem_k, mem_v


def kernel(x_prompt, x_sample, state_mlstm_C, state_mlstm_n, state_mlstm_m,
           cache_win_k_g0, cache_win_v_g0, cache_win_k_g1, cache_win_v_g1,
           cache_win_k_g2, cache_win_v_g2, cache_mem_k, cache_mem_v, mem_prompt,
           g_mix, w_in, b_igate, b_fgate, b_branch, g_mem, w_mem_kv,
           w_br_a, w_br_b, w_br_c, w_out, g_mlp, w_up, w_down, g_final):
    depth = g_mix.shape[0]
    assert depth == 1, "single-layer stack"
    caches = ((cache_win_k_g0[0], cache_win_v_g0[0]),
              (cache_win_k_g1[0], cache_win_v_g1[0]),
              (cache_win_k_g2[0], cache_win_v_g2[0]))
    x2, p_state, s_state, rows_p, rows_s, mem_k, mem_v = _layer(
        x_prompt.reshape(N_PROMPT, D_MODEL), x_sample.reshape(N_SAMPLE, D_MODEL),
        mem_prompt.reshape(BATCH * MEM_LEN, D_MODEL),
        state_mlstm_C, state_mlstm_n, state_mlstm_m[0], caches, cache_mem_k[0], cache_mem_v[0],
        g_mix[0], w_in[0], b_igate[0], b_fgate[0], b_branch[0], g_mem[0], w_mem_kv[0],
        w_br_a[0], w_br_b[0], w_br_c[0], w_out[0], g_mlp[0], w_up[0], w_down[0])
    y_prompt = _rmsnorm(x2, g_final, F32, row_start=0, n_rows=N_PROMPT).reshape(BATCH, SEQ, D_MODEL)
    y_sample = _rmsnorm(x2, g_final, F32, row_start=N_PROMPT, n_rows=N_SAMPLE).reshape(DEC_BATCH, DEC_SEQ, D_MODEL)
    lead = lambda a: a[None]
    return (y_prompt, y_sample,
            lead(p_state[0]), lead(p_state[1]), lead(p_state[2]),
            *[lead(r) for r in rows_p],
            lead(mem_k), lead(mem_v),
            lead(s_state[0]), lead(s_state[1]), lead(s_state[2]),
            *[lead(r) for r in rows_s])
```

```python
import functools
import math
from typing import Callable, NamedTuple

import numpy as np
import jax
import jax.numpy as jnp
from jax import lax
from jax.experimental import pallas as pl
from jax.experimental.pallas import tpu as pltpu

F32 = jnp.float32
BF16 = jnp.bfloat16
NEG_INF = float("-inf")

D_MODEL = 4096
BATCH = 2
SEQ = 4096
DEC_BATCH = 128
DEC_SEQ = 4
PAST_LEN = 2048
MEM_LEN = 256
M_HEADS = 8
M_DQK = D_MODEL // (2 * M_HEADS)
M_DV = D_MODEL // M_HEADS
A_HEAD_DIM = 128
A_HEADS = D_MODEL // 512
DIL_GROUPS = ((128, 1), (512, 4), (2048, 16))
N_GROUPS = 3
ROPE_THETA = 10000.0
X_HEADS = 4
X_HEAD_DIM = 128
D_FF = 4 * D_MODEL
EPS = 1e-6

W_MQK = M_HEADS * M_DQK
W_MV = M_HEADS * M_DV
W_AH = A_HEADS * A_HEAD_DIM
W_XQ = X_HEADS * X_HEAD_DIM
N_PROMPT = BATCH * SEQ
N_SAMPLE = DEC_BATCH * DEC_SEQ
N_ALL = N_PROMPT + N_SAMPLE

COL_GATES = 2 * W_MQK + 2 * W_MV
COL_REST = COL_GATES + 2 * M_HEADS
W_AQKV = 3 * N_GROUPS * W_AH

LANES = 128
SUBLANES = 8
VMEM_LIMIT_BYTES = 60 * 1024 * 1024

MM_TM = 1088
MM_TN = 512
MM_TN_WIDE = 1024
MERGE_TM = 544
HOST_TM = 544
HOST_SPLIT_COLS = 256
DOWN_TM = 2176
DOWN_TK = 2048
NORM_ROWS = 256
CHUNK = 128
SAMPLE_CHUNK = 16
SAMPLE_SEQS_PER_BLOCK = 2
SPAN = 128


def _params(n_axes):
    return pltpu.CompilerParams(
        dimension_semantics=("arbitrary",) * n_axes,
        vmem_limit_bytes=VMEM_LIMIT_BYTES,
    )


def _rmsnorm_body(x_ref, g_ref, o_ref):
    x = x_ref[...]
    ms = jnp.mean(x * x, axis=-1, keepdims=True)
    o_ref[...] = ((x * lax.rsqrt(ms + EPS)) * g_ref[...]).astype(o_ref.dtype)


def _rmsnorm(x, g, out_dtype, *, row_start=0, n_rows=None):
    n_rows = x.shape[0] if n_rows is None else n_rows
    d = x.shape[1]
    off = row_start // NORM_ROWS
    return pl.pallas_call(
        _rmsnorm_body,
        grid=(n_rows // NORM_ROWS,),
        in_specs=[
            pl.BlockSpec((NORM_ROWS, d), lambda i: (i + off, 0)),
            pl.BlockSpec((1, d), lambda i: (0, 0)),
        ],
        out_specs=pl.BlockSpec((NORM_ROWS, d), lambda i: (i, 0)),
        out_shape=jax.ShapeDtypeStruct((n_rows, d), out_dtype),
        compiler_params=_params(1),
        name="rmsnorm",
    )(x, g.reshape(1, d))


def _rmsnorm_two_groups_body(xp_ref, xs_ref, g_ref, o_ref, *, prompt_steps):
    i = pl.program_id(0)

    @pl.when(i < prompt_steps)
    def _():
        _rmsnorm_body(xp_ref, g_ref, o_ref)

    @pl.when(i >= prompt_steps)
    def _():
        _rmsnorm_body(xs_ref, g_ref, o_ref)


def _rmsnorm_two_groups(xp, xs, g, out_dtype):
    d = xp.shape[1]
    ps, ss = xp.shape[0] // NORM_ROWS, xs.shape[0] // NORM_ROWS
    return pl.pallas_call(
        functools.partial(_rmsnorm_two_groups_body, prompt_steps=ps),
        grid=(ps + ss,),
        in_specs=[
            pl.BlockSpec((NORM_ROWS, d), lambda i: (jnp.minimum(i, ps - 1), 0)),
            pl.BlockSpec((NORM_ROWS, d), lambda i: (jnp.maximum(i - ps, 0), 0)),
            pl.BlockSpec((1, d), lambda i: (0, 0)),
        ],
        out_specs=pl.BlockSpec((NORM_ROWS, d), lambda i: (i, 0)),
        out_shape=jax.ShapeDtypeStruct((xp.shape[0] + xs.shape[0], d), out_dtype),
        compiler_params=_params(1),
        name="rmsnorm_two_groups",
    )(xp, xs, g.reshape(1, d))


def _ep_plain(acc, extra, o_ref, j):
    o_ref[...] = acc.astype(o_ref.dtype)


def _ep_residual_two_groups(acc, extra, o_ref, j):
    xp_ref, xs_ref = extra
    tm = acc.shape[0]
    n_full = N_PROMPT // tm
    rem = N_PROMPT - n_full * tm
    assert tm - rem == N_SAMPLE and rem % SUBLANES == 0
    i = pl.program_id(1)

    @pl.when(i < n_full)
    def _():
        o_ref[...] = xp_ref[...] + acc

    @pl.when(i >= n_full)
    def _():
        o_ref[0:rem, :] = xp_ref[tm - rem:tm, :] + acc[0:rem]
        o_ref[rem:tm, :] = xs_ref[...] + acc[rem:tm]


def _two_group_specs(tm, width, col_of_j, row_align, sample_block_is_constant):
    def p_map(j, i):
        row = jnp.minimum(i * tm, N_PROMPT - tm)
        col = col_of_j(j) * width
        return (pl.multiple_of(row, row_align), col if isinstance(col, int) else pl.multiple_of(col, LANES))
    p_spec = pl.BlockSpec((pl.Element(tm), pl.Element(width)), p_map)
    mode = pl.Buffered(1) if sample_block_is_constant else None
    s_spec = pl.BlockSpec((N_SAMPLE, width), lambda j, i: (0, col_of_j(j)), pipeline_mode=mode)
    return p_spec, s_spec


def _ep_relu2(acc, extra, o_ref, j):
    r = jnp.maximum(acc, 0.0)
    o_ref[...] = (r * r).astype(o_ref.dtype)


def _ep_rope(acc, extra, o_ref, j):
    cos_ref, sin_ref = extra
    tn = acc.shape[1]
    assert W_AH % tn == 0
    is_rope = (j // (W_AH // tn)) % 3 != 2
    cos2 = cos_ref[...]
    sin2 = sin_ref[...]
    for s in range(tn // A_HEAD_DIM):
        x = acc[:, s * A_HEAD_DIM:(s + 1) * A_HEAD_DIM]
        rotated = x * cos2 + pltpu.roll(x, A_HEAD_DIM // 2, axis=1) * sin2
        o_ref[:, s * A_HEAD_DIM:(s + 1) * A_HEAD_DIM] = jnp.where(is_rope, rotated, x)


def _mm_body(*refs, b_rows_are_outputs, epilogue, n_extra):
    a_ref, b_ref = refs[0], refs[1]
    extra = refs[2:2 + n_extra]
    o_ref = refs[2 + n_extra]
    b_scr = refs[3 + n_extra]

    @pl.when(pl.program_id(1) == 0)
    def _():
        b_scr[...] = b_ref[...].astype(BF16)

    if b_rows_are_outputs:
        acc = lax.dot_general(a_ref[...], b_scr[...], (((1,), (1,)), ((), ())),
                              preferred_element_type=F32)
    else:
        acc = jnp.dot(a_ref[...], b_scr[...], preferred_element_type=F32)
    epilogue(acc, extra, o_ref, pl.program_id(0))


def _matmul(a, b, *, n_cols, col_start=0, b_rows_are_outputs=False, tm=MM_TM, tn=MM_TN,
            out_dtype=F32, epilogue=_ep_plain, extras=(), extra_specs=(), name="matmul"):
    m, k = a.shape
    if b_rows_are_outputs:
        if col_start % tn == 0:
            b_spec = pl.BlockSpec((tn, k), lambda j, i: (j + col_start // tn, 0))
        else:
            assert col_start % SUBLANES == 0 and tn % SUBLANES == 0
            b_spec = pl.BlockSpec((pl.Element(tn), pl.Element(k)),
                                  lambda j, i: (pl.multiple_of(col_start + j * tn, SUBLANES), 0))
        b_block = (tn, k)
    else:
        assert col_start % tn == 0
        b_spec = pl.BlockSpec((k, tn), lambda j, i: (0, j + col_start // tn))
        b_block = (k, tn)
    return pl.pallas_call(
        functools.partial(_mm_body, b_rows_are_outputs=b_rows_are_outputs, epilogue=epilogue,
                          n_extra=len(extras)),
        grid=(n_cols // tn, m // tm),
        in_specs=[pl.BlockSpec((tm, k), lambda j, i: (i, 0)), b_spec, *extra_specs],
        out_specs=pl.BlockSpec((tm, tn), lambda j, i: (i, j)),
        out_shape=jax.ShapeDtypeStruct((m, n_cols), out_dtype),
        scratch_shapes=[pltpu.VMEM(b_block, BF16)],
        compiler_params=_params(2),
        name=name,
    )(a, b, *extras)


def _narrow_pair_body(a_ref, w0_ref, w1_ref, o0_ref, o1_ref, w0_scr, w1_scr):
    @pl.when(pl.program_id(0) == 0)
    def _():
        w0_scr[...] = w0_ref[...].astype(BF16)
        w1_scr[...] = w1_ref[...].astype(BF16)

    a = a_ref[...]
    nt = (((1,), (1,)), ((), ()))
    o0_ref[...] = lax.dot_general(a, w0_scr[...], nt, preferred_element_type=F32)
    o1_ref[...] = lax.dot_general(a, w1_scr[...], nt, preferred_element_type=F32)


def _narrow_projection_pair(a, w_t, start0, n0, start1, n1, *, tm=MM_TM, name="narrow_pair"):
    m, k = a.shape
    assert start0 % SUBLANES == 0 and start1 % SUBLANES == 0

    def w_spec(start, n):
        return pl.BlockSpec((pl.Element(n), pl.Element(k)), lambda i: (start, 0), pipeline_mode=pl.Buffered(1))

    return pl.pallas_call(
        _narrow_pair_body,
        grid=(m // tm,),
        in_specs=[pl.BlockSpec((tm, k), lambda i: (i, 0)), w_spec(start0, n0), w_spec(start1, n1)],
        out_specs=[pl.BlockSpec((tm, n0), lambda i: (i, 0)), pl.BlockSpec((tm, n1), lambda i: (i, 0))],
        out_shape=[jax.ShapeDtypeStruct((m, n0), F32), jax.ShapeDtypeStruct((m, n1), F32)],
        scratch_shapes=[pltpu.VMEM((n0, k), BF16), pltpu.VMEM((n1, k), BF16)],
        compiler_params=_params(1),
        name=name,
    )(a, w_t, w_t)


class _SideJob(NamedTuple):
    n_steps: int
    inputs: tuple
    in_specs: tuple
    out_shapes: tuple
    out_specs: tuple
    scratch_shapes: tuple
    init: Callable
    pieces: Callable


def _mm_stream_body(*refs, b_rows_are_outputs, epilogue, n_extra, slice_rows, split_cols, side):
    n_side_in = len(side.inputs) if side else 0
    n_side_out = len(side.out_shapes) if side else 0
    n_in = 2 + n_extra + n_side_in
    a_ref, bs_ref = refs[0], refs[1]
    extra = refs[2:2 + n_extra]
    side_in = refs[2 + n_extra:n_in]
    o_ref = refs[n_in]
    side_out = refs[n_in + 1:n_in + 1 + n_side_out]
    b_scr = refs[n_in + 1 + n_side_out]
    side_scr = refs[n_in + 2 + n_side_out:]
    p, i = pl.program_id(0), pl.program_id(1)

    r0 = pl.multiple_of(i * slice_rows, slice_rows)
    b_scr[p % 2, pl.ds(r0, slice_rows), :] = bs_ref[...].astype(BF16)

    if side is not None:
        @pl.when((p == 0) & (i == 0))
        def _():
            side.init(side_scr)

    def multiply(side_pieces):
        slot = (p - 1) % 2
        a = a_ref[...]
        tn = o_ref.shape[1]
        n_split = tn // split_cols
        per_split = -(-len(side_pieces) // n_split)
        for h in range(n_split):
            cols = pl.ds(h * split_cols, split_cols)
            if b_rows_are_outputs:
                acc = lax.dot_general(a, b_scr[slot, cols, :], (((1,), (1,)), ((), ())),
                                      preferred_element_type=F32)
            else:
                acc = jnp.dot(a, b_scr[slot, :, cols], preferred_element_type=F32)
            epilogue(acc, extra, o_ref.at[:, cols], (p - 1) * n_split + h)
            for piece in side_pieces[h * per_split:(h + 1) * per_split]:
                piece()

    if side is None:
        pl.when(p > 0)(lambda: multiply([]))
    else:
        t = (p - 1) * pl.num_programs(1) + i
        pl.when((p > 0) & (t < side.n_steps))(
            lambda: multiply(side.pieces(t, side_in, side_out, side_scr)))
        pl.when((p > 0) & (t >= side.n_steps))(lambda: multiply([]))


def _matmul_stream(a, b, *, n_cols, col_start=0, b_rows_are_outputs=False, tm=MM_TM, tn=MM_TN_WIDE,
                   out_dtype=F32, epilogue=_ep_plain, extras=(), extra_specs=(), side=None,
                   split_cols=MM_TN, name="matmul"):
    m, k = a.shape
    ni, nj = m // tm, n_cols // tn
    assert col_start % SUBLANES == 0 and nj * tn == n_cols

    def side_spec(spec):
        if isinstance(spec, pl.BlockSpec):
            return spec
        block_shape, index_fn = spec
        return pl.BlockSpec(
            block_shape,
            lambda p, i: index_fn(jnp.clip((p - 1) * ni + i, 0, side.n_steps - 1)))

    side_in_specs = [side_spec(s) for s in side.in_specs] if side else []
    side_out_specs = [side_spec(s) for s in side.out_specs] if side else []
    assert side is None or side.n_steps <= nj * ni

    def staged(p):
        return jnp.minimum(p, nj - 1)

    def shifted(index_map):
        return lambda p, i: index_map(jnp.maximum(p - 1, 0), jnp.where(p == 0, 0, i))

    if b_rows_are_outputs:
        tile = (tn, k)
        slice_rows = tn // ni
        bs_spec = pl.BlockSpec(
            (pl.Element(slice_rows), pl.Element(k)),
            lambda p, i: (pl.multiple_of(col_start + staged(p) * tn + i * slice_rows, SUBLANES), 0))
    else:
        assert col_start % tn == 0
        tile = (k, tn)
        slice_rows = k // ni
        bs_spec = pl.BlockSpec((slice_rows, tn), lambda p, i: (i, col_start // tn + staged(p)))
    assert slice_rows * ni == tile[0] and slice_rows % 16 == 0
    extra_specs = [pl.BlockSpec(s.block_shape, shifted(s.index_map), pipeline_mode=s.pipeline_mode)
                   for s in extra_specs]
    outs = pl.pallas_call(
        functools.partial(_mm_stream_body, b_rows_are_outputs=b_rows_are_outputs, epilogue=epilogue,
                          n_extra=len(extras), slice_rows=slice_rows, split_cols=min(tn, split_cols), side=side),
        grid=(nj + 1, ni),
        in_specs=[pl.BlockSpec((tm, k), shifted(lambda j, i: (i, 0))), bs_spec, *extra_specs,
                  *side_in_specs],
        out_specs=[pl.BlockSpec((tm, tn), shifted(lambda j, i: (i, j))), *side_out_specs],
        out_shape=[jax.ShapeDtypeStruct((m, n_cols), out_dtype), *(side.out_shapes if side else ())],
        scratch_shapes=[pltpu.VMEM((2,) + tile, BF16), *(side.scratch_shapes if side else ())],
        compiler_params=_params(2),
        name=name,
    )(a, b, *extras, *(side.inputs if side else ()))
    return outs if side else outs[0]


def _mm_ksplit_body(a_ref, b_ref, res_ref, o_ref, acc_ref):
    kk = pl.program_id(2)
    last = pl.num_programs(2) - 1

    def product():
        return jnp.dot(a_ref[...], b_ref[...].astype(BF16), preferred_element_type=F32)

    @pl.when(kk == 0)
    def _():
        acc_ref[...] = product()

    @pl.when((kk > 0) & (kk < last))
    def _():
        acc_ref[...] += product()

    @pl.when(kk == last)
    def _():
        o_ref[...] = res_ref[...] + (acc_ref[...] + product())


def _matmul_ksplit_residual(a, b, res, *, tm=DOWN_TM, tn=MM_TN, tk=DOWN_TK):
    m, k = a.shape
    n = b.shape[1]
    return pl.pallas_call(
        _mm_ksplit_body,
        grid=(n // tn, m // tm, k // tk),
        in_specs=[
            pl.BlockSpec((tm, tk), lambda j, i, kk: (i, kk)),
            pl.BlockSpec((tk, tn), lambda j, i, kk: (kk, j)),
            pl.BlockSpec((tm, tn), lambda j, i, kk: (i, j)),
        ],
        out_specs=pl.BlockSpec((tm, tn), lambda j, i, kk: (i, j)),
        out_shape=jax.ShapeDtypeStruct((m, n), F32),
        scratch_shapes=[pltpu.VMEM((tm, tn), F32)],
        compiler_params=_params(3),
        name="mlp_down",
    )(a, b, res)


def _sigmoid(x):
    return 1.0 / (1.0 + jnp.exp(-x))


def _straddle_rows(p_ref, s_ref):
    tm = p_ref.shape[0]
    rem = tm - s_ref.shape[0]
    return jnp.concatenate([p_ref[tm - rem:tm, :], s_ref[...]], axis=0)


def _merge_body(ap_ref, as_ref, ybp_ref, ybs_ref, ycp_ref, ycs_ref,
                wa0_ref, wb0_ref, wc0_ref, was_ref, wbs_ref, wcs_ref,
                ga_ref, gb_ref, gc_ref, ba_ref, bb_ref, bc_ref,
                o_ref, wa_scr, wb_scr, wc_scr):
    j, i = pl.program_id(0), pl.program_id(1)
    cur = j % 2

    @pl.when((j == 0) & (i == 0))
    def _():
        wa_scr[0] = wa0_ref[...].astype(BF16)
        wb_scr[0] = wb0_ref[...].astype(BF16)
        wc_scr[0] = wc0_ref[...].astype(BF16)

    for scr, slice_ref in ((wa_scr, was_ref), (wb_scr, wbs_ref), (wc_scr, wcs_ref)):
        rows = slice_ref.shape[0]
        scr[1 - cur, pl.ds(pl.multiple_of(i * rows, rows), rows), :] = slice_ref[...].astype(BF16)

    def merge(a, yb, yc):
        for h in range(o_ref.shape[1] // HOST_SPLIT_COLS):
            cols = pl.ds(h * HOST_SPLIT_COLS, HOST_SPLIT_COLS)
            y_a = jnp.dot(a, wa_scr[cur, :, cols], preferred_element_type=F32)
            y_b = jnp.dot(yb, wb_scr[cur, :, cols], preferred_element_type=F32)
            y_c = jnp.dot(yc, wc_scr[cur, :, cols], preferred_element_type=F32)
            g_a = _sigmoid(ga_ref[:, cols] + ba_ref[:, cols])
            g_b = _sigmoid(gb_ref[:, cols] + bb_ref[:, cols])
            g_c = _sigmoid(gc_ref[:, cols] + bc_ref[:, cols])
            o_ref[:, cols] = (g_a * y_a + g_b * y_b + g_c * y_c).astype(o_ref.dtype)

    is_prompt_tile = i < N_PROMPT // ap_ref.shape[0]

    @pl.when(is_prompt_tile)
    def _():
        merge(ap_ref[...], ybp_ref[...], ycp_ref[...])

    @pl.when(jnp.logical_not(is_prompt_tile))
    def _():
        merge(_straddle_rows(ap_ref, as_ref), _straddle_rows(ybp_ref, ybs_ref), _straddle_rows(ycp_ref, ycs_ref))


def _branch_merge(a_ps, yb_ps, yc_ps, w_a, w_b, w_c, z_br, b_branch, *, tm=MERGE_TM, tn=MM_TN):
    m = N_ALL
    assert (N_PROMPT // tm + 1) * tm == N_ALL
    nj = D_MODEL // tn

    def gate_spec(part):
        return pl.BlockSpec((tm, tn), lambda j, i: (i, part * nj + j))

    def bias_spec(part):
        return pl.BlockSpec((1, tn), lambda j, i: (0, part * nj + j))

    ni = m // tm
    widths = (W_MV, W_AH, W_XQ)
    assert all(k % (16 * ni) == 0 for k in widths)

    def first_tile_spec(k):
        return pl.BlockSpec((k, tn), lambda j, i: (0, 0), pipeline_mode=pl.Buffered(1))

    def next_slice_spec(k):
        return pl.BlockSpec((k // ni, tn), lambda j, i: (i, jnp.minimum(j + 1, nj - 1)))

    bf16_rows = 16
    return pl.pallas_call(
        _merge_body,
        grid=(nj, ni),
        in_specs=[
            *_two_group_specs(tm, W_MV, lambda j: 0, bf16_rows, True),
            *_two_group_specs(tm, W_AH, lambda j: 0, bf16_rows, True),
            *_two_group_specs(tm, W_XQ, lambda j: 0, bf16_rows, True),
            *[first_tile_spec(k) for k in widths],
            *[next_slice_spec(k) for k in widths],
            gate_spec(0), gate_spec(1), gate_spec(2),
            bias_spec(0), bias_spec(1), bias_spec(2),
        ],
        out_specs=pl.BlockSpec((tm, tn), lambda j, i: (i, j)),
        out_shape=jax.ShapeDtypeStruct((m, D_MODEL), BF16),
        scratch_shapes=[pltpu.VMEM((2, k, tn), BF16) for k in widths],
        compiler_params=_params(2),
        name="branch_merge",
    )(*a_ps, *yb_ps, *yc_ps, w_a, w_b, w_c, w_a, w_b, w_c,
      z_br, z_br, z_br, b_branch, b_branch, b_branch)


def _log_sigmoid(x):
    return jnp.minimum(x, 0.0) - jnp.log1p(jnp.exp(-jnp.abs(x)))


def _mlstm_chunk(q, k, v, irow, frow, c_state, n_state, m_state, n_valid):
    L = q.shape[0]
    ti = lax.broadcasted_iota(jnp.int32, (L, L), 0)
    si = lax.broadcasted_iota(jnp.int32, (L, L), 1)
    causal = si <= ti
    eye = si == ti
    f_b = jnp.broadcast_to(frow, (L, L))
    i_b = jnp.broadcast_to(irow, (L, L))
    bcol = jnp.sum(jnp.where(causal, f_b, 0.0), axis=1, keepdims=True)
    fcol = jnp.sum(jnp.where(eye, f_b, 0.0), axis=1, keepdims=True)
    icol = jnp.sum(jnp.where(eye, i_b, 0.0), axis=1, keepdims=True)
    brow = jnp.sum(jnp.where(ti <= si, jnp.broadcast_to(fcol, (L, L)), 0.0), axis=0, keepdims=True)

    acol = bcol + m_state
    logw = jnp.where(causal, bcol - brow + irow, NEG_INF)
    mt = jnp.maximum(acol, jnp.max(logw, axis=1, keepdims=True))
    w_inter = jnp.exp(acol - mt)
    w_intra = jnp.exp(logw - mt)

    qb = q.astype(BF16)
    kb = k.astype(BF16)
    nt = (((1,), (1,)), ((), ()))
    s = lax.dot_general(qb, kb, nt, preferred_element_type=F32) * w_intra
    inter = lax.dot_general(qb, c_state.astype(BF16), nt, preferred_element_type=F32)
    num = w_inter * inter + jnp.dot(s.astype(BF16), v.astype(BF16), preferred_element_type=F32)
    nq = w_inter * jnp.sum(q * n_state, axis=1, keepdims=True) + jnp.sum(s, axis=1, keepdims=True)
    h = num / jnp.maximum(jnp.abs(nq), jnp.exp(-mt))

    last = slice(n_valid - 1, n_valid)
    m_end = mt[last, :]
    w_c = jnp.exp(acol[last, :] - m_end)
    w_s = jnp.exp(bcol[last, :] - bcol + icol - m_end)
    if n_valid < L:
        w_s = jnp.where(lax.broadcasted_iota(jnp.int32, (L, 1), 0) < n_valid, w_s, 0.0)
    tn = (((0,), (0,)), ((), ()))
    c_new = w_c * c_state + lax.dot_general((v * w_s).astype(BF16), kb, tn, preferred_element_type=F32)
    n_new = w_c * n_state + jnp.sum(w_s * k, axis=0, keepdims=True)
    return h, c_new, n_new, m_end


def _gate_rows(g_ref_val, bias_ref, head):
    irow = g_ref_val[0:1, :] + bias_ref[0, head]
    frow = _log_sigmoid(g_ref_val[1:2, :] + bias_ref[1, head])
    return irow, frow


def _mlstm_prompt_body(bias_ref, q_ref, k_ref, v_ref, mo_ref, g_ref,
                       a_ref, c_ref, n_ref, m_ref):
    @pl.when(pl.program_id(1) == 0)
    def _():
        c_ref[...] = jnp.zeros_like(c_ref)
        n_ref[...] = jnp.zeros_like(n_ref)
        m_ref[...] = jnp.zeros_like(m_ref)

    for head in range(M_HEADS):
        qk = pl.ds(head * M_DQK, M_DQK)
        vo = pl.ds(head * M_DV, M_DV)
        one = pl.ds(head, 1)
        irow, frow = _gate_rows(g_ref[head], bias_ref, head)
        h, c_new, n_new, m_end = _mlstm_chunk(
            q_ref[:, qk], k_ref[:, qk] * (M_DQK ** -0.5), v_ref[:, vo], irow, frow,
            c_ref[0, head], n_ref[0, one, :], m_ref[0, one, 0:1], CHUNK)
        a_ref[:, vo] = (_sigmoid(mo_ref[:, vo]) * h).astype(a_ref.dtype)
        c_ref[0, head] = c_new
        n_ref[0, one, :] = n_new
        m_ref[0, one, :] = jnp.broadcast_to(m_end, (1, LANES))


def _mlstm_prompt(z_main, gates_rows, gate_bias):
    nc = SEQ // CHUNK
    return pl.pallas_call(
        _mlstm_prompt_body,
        grid=(BATCH, nc),
        in_specs=[
            pl.BlockSpec(memory_space=pltpu.SMEM),
            pl.BlockSpec((CHUNK, W_MQK), lambda b, c: (b * nc + c, 0)),
            pl.BlockSpec((CHUNK, W_MQK), lambda b, c: (b * nc + c, 1)),
            pl.BlockSpec((CHUNK, W_MV), lambda b, c: (b * nc + c, 2 * W_MQK // W_MV)),
            pl.BlockSpec((CHUNK, W_MV), lambda b, c: (b * nc + c, 2 * W_MQK // W_MV + 1)),
            pl.BlockSpec((M_HEADS, 2, CHUNK), lambda b, c: (0, 0, b * nc + c)),
        ],
        out_specs=[
            pl.BlockSpec((CHUNK, W_MV), lambda b, c: (b * nc + c, 0)),
            pl.BlockSpec((1, M_HEADS, M_DV, M_DQK), lambda b, c: (b, 0, 0, 0)),
            pl.BlockSpec((1, M_HEADS, M_DQK), lambda b, c: (b, 0, 0)),
            pl.BlockSpec((1, M_HEADS, LANES), lambda b, c: (b, 0, 0)),
        ],
        out_shape=[
            jax.ShapeDtypeStruct((N_PROMPT, W_MV), BF16),
            jax.ShapeDtypeStruct((BATCH, M_HEADS, M_DV, M_DQK), F32),
            jax.ShapeDtypeStruct((BATCH, M_HEADS, M_DQK), F32),
            jax.ShapeDtypeStruct((BATCH, M_HEADS, LANES), F32),
        ],
        compiler_params=_params(2),
        name="mlstm_prompt",
    )(gate_bias, z_main, z_main, z_main, z_main, gates_rows)


def _mlstm_sample_init(scratch):
    for scr in scratch:
        scr[...] = jnp.zeros_like(scr)


def _mlstm_sample_pieces(t, ins, outs, scratch):
    bias_ref, q_ref, k_ref, v_ref, mo_ref, g_ref, c0_ref, n0_ref, m0_ref = ins
    a_ref, c_ref, n_ref, m_ref = outs
    q_scr, k_scr, v_scr = scratch
    first_half = (t % SAMPLE_SEQS_PER_BLOCK) == 0

    def seq_rows(ref, cols):
        return jnp.where(first_half, ref[0:DEC_SEQ, cols], ref[DEC_SEQ:2 * DEC_SEQ, cols])

    def one_head(head):
        qk = pl.ds(head * M_DQK, M_DQK)
        vo = pl.ds(head * M_DV, M_DV)
        one = pl.ds(head, 1)
        q_scr[0:DEC_SEQ, qk] = seq_rows(q_ref, qk)
        k_scr[0:DEC_SEQ, qk] = seq_rows(k_ref, qk) * (M_DQK ** -0.5)
        v_scr[0:DEC_SEQ, vo] = seq_rows(v_ref, vo)
        irow, frow = _gate_rows(g_ref[0, head], bias_ref, head)
        h, c_new, n_new, m_end = _mlstm_chunk(
            q_scr[:, qk], k_scr[:, qk], v_scr[:, vo], irow, frow,
            c0_ref[0, 0, head], n0_ref[0, 0, one, :], m0_ref[0, one, :], DEC_SEQ)
        a_ref[0, :, vo] = _sigmoid(seq_rows(mo_ref, vo)) * h[0:DEC_SEQ, :]
        c_ref[0, head] = c_new
        n_ref[0, one, :] = n_new
        m_ref[0, one, :] = jnp.broadcast_to(m_end, (1, LANES))

    def all_heads():
        for head in range(M_HEADS):
            one_head(head)

    return [all_heads]


def _mlstm_sample_job(z_main, gates_rows, gate_bias, c0, n0, m0):
    per = SAMPLE_SEQS_PER_BLOCK
    rows = per * DEC_SEQ
    assert rows == SUBLANES and N_PROMPT % rows == 0
    r0 = N_PROMPT // rows
    v_blk = 2 * W_MQK // W_MV
    return _SideJob(
        n_steps=DEC_BATCH,
        inputs=(gate_bias, z_main, z_main, z_main, z_main, gates_rows, c0, n0, m0),
        in_specs=(
            pl.BlockSpec(memory_space=pltpu.SMEM),
            ((rows, W_MQK), lambda t: (r0 + t // per, 0)),
            ((rows, W_MQK), lambda t: (r0 + t // per, 1)),
            ((rows, W_MV), lambda t: (r0 + t // per, v_blk)),
            ((rows, W_MV), lambda t: (r0 + t // per, v_blk + 1)),
            ((1, M_HEADS, 2, SAMPLE_CHUNK), lambda t: (t, 0, 0, 0)),
            ((1, 1, M_HEADS, M_DV, M_DQK), lambda t: (0, t, 0, 0, 0)),
            ((1, 1, M_HEADS, M_DQK), lambda t: (0, t, 0, 0)),
            ((1, M_HEADS, 1), lambda t: (t, 0, 0)),
        ),
        out_shapes=(
            jax.ShapeDtypeStruct((DEC_BATCH, DEC_SEQ, W_MV), F32),
            jax.ShapeDtypeStruct((DEC_BATCH, M_HEADS, M_DV, M_DQK), F32),
            jax.ShapeDtypeStruct((DEC_BATCH, M_HEADS, M_DQK), F32),
            jax.ShapeDtypeStruct((DEC_BATCH, M_HEADS, LANES), F32),
        ),
        out_specs=(
            ((1, DEC_SEQ, W_MV), lambda t: (t, 0, 0)),
            ((1, M_HEADS, M_DV, M_DQK), lambda t: (t, 0, 0, 0)),
            ((1, M_HEADS, M_DQK), lambda t: (t, 0, 0)),
            ((1, M_HEADS, LANES), lambda t: (t, 0, 0)),
        ),
        scratch_shapes=(pltpu.VMEM((SAMPLE_CHUNK, W_MQK), F32), pltpu.VMEM((SAMPLE_CHUNK, W_MQK), F32),
                        pltpu.VMEM((SAMPLE_CHUNK, W_MV), F32)),
        init=_mlstm_sample_init,
        pieces=_mlstm_sample_pieces,
    )


def _dil_prompt_combine_body(q_ref, kp_ref, kc_ref, vp_ref, vc_ref, o0_ref, l0_ref, o1_ref, l1_ref,
                             y_ref, o_scr, l_scr, *, r, hb):
    _dil_prompt_body(q_ref, kp_ref, kc_ref, vp_ref, vc_ref, o_scr, l_scr, r=r, hb=hb)
    _combine_body(o0_ref, o1_ref, o_scr, l0_ref, l1_ref, l_scr, y_ref)


def _dil_prompt_body(q_ref, kp_ref, kc_ref, vp_ref, vc_ref, o_ref, l_ref, *, r, hb):
    first_key = jnp.where(pl.program_id(2) > 0, 0, SPAN)
    qi = lax.broadcasted_iota(jnp.int32, (SPAN, 2 * SPAN), 0)
    ki = lax.broadcasted_iota(jnp.int32, (SPAN, 2 * SPAN), 1)
    ok = (ki >= qi) & (ki <= qi + SPAN) & (ki >= first_key)
    bias = jnp.where(ok, 0.0, NEG_INF)
    scale = A_HEAD_DIM ** -0.5
    nt = (((1,), (1,)), ((), ()))
    for c in range(r):
        rows = pl.ds(c, SPAN, stride=r) if r > 1 else pl.ds(0, SPAN)
        for hh in range(hb):
            cols = pl.ds(hh * A_HEAD_DIM, A_HEAD_DIM)
            q = q_ref[rows, cols].astype(BF16)
            kk = jnp.concatenate([kp_ref[rows, cols], kc_ref[rows, cols]], axis=0).astype(BF16)
            vv = jnp.concatenate([vp_ref[rows, cols], vc_ref[rows, cols]], axis=0).astype(BF16)
            s = lax.dot_general(q, kk, nt, preferred_element_type=F32) * scale + bias
            m = jnp.max(s, axis=1, keepdims=True)
            p = jnp.exp(s - m)
            den = jnp.sum(p, axis=1, keepdims=True)
            o = jnp.dot(p.astype(BF16), vv, preferred_element_type=F32) / den
            o_ref[rows, cols] = o
            l_ref[rows, cols] = jnp.broadcast_to(m + jnp.log(den), (SPAN, A_HEAD_DIM))


def _dilated_prompt(z_aqkv, g, r, hb, combine_with=None):
    rows = SPAN * r
    nblk = SEQ // rows
    wcol = hb * A_HEAD_DIM
    per_part = W_AH // wcol

    def spec(part, prev):
        def imap(b, hg, n):
            nn = jnp.maximum(n - 1, 0) if prev else n
            return (b * nblk + nn, (3 * g + part) * per_part + hg)
        return pl.BlockSpec((rows, wcol), imap)

    out_spec = pl.BlockSpec((rows, wcol), lambda b, hg, n: (b * nblk + n, hg))
    qkv_specs = [spec(0, False), spec(1, True), spec(1, False), spec(2, True), spec(2, False)]
    if combine_with is None:
        return pl.pallas_call(
            functools.partial(_dil_prompt_body, r=r, hb=hb),
            grid=(BATCH, A_HEADS // hb, nblk),
            in_specs=qkv_specs,
            out_specs=[out_spec, out_spec],
            out_shape=[jax.ShapeDtypeStruct((N_PROMPT, W_AH), F32)] * 2,
            compiler_params=_params(3),
            name=f"dilated_prompt_g{g}",
        )(z_aqkv, z_aqkv, z_aqkv, z_aqkv, z_aqkv)
    return pl.pallas_call(
        functools.partial(_dil_prompt_combine_body, r=r, hb=hb),
        grid=(BATCH, A_HEADS // hb, nblk),
        in_specs=qkv_specs + [out_spec] * 4,
        out_specs=out_spec,
        out_shape=jax.ShapeDtypeStruct((N_PROMPT, W_AH), BF16),
        scratch_shapes=[pltpu.VMEM((rows, wcol), F32)] * 2,
        compiler_params=_params(3),
        name=f"dilated_prompt_g{g}_combine",
    )(z_aqkv, z_aqkv, z_aqkv, z_aqkv, z_aqkv, *combine_with)


def _sample_attn_pieces(t, ins, out_refs, scratch):
    (qkv_ref, k0_ref, v0_ref, k1_ref, v1_ref, k2_ref, v2_ref, b0_ref, b1_ref, b2_ref,
     xq_ref, mk_ref, mv_ref, bx_ref) = ins
    yb_ref, yc_ref = out_refs
    nt = (((1,), (1,)), ((), ()))
    outs, lses = [], []
    groups = ((k0_ref, v0_ref, b0_ref), (k1_ref, v1_ref, b1_ref), (k2_ref, v2_ref, b2_ref))

    def cross():
        s = lax.dot_general(xq_ref[0].astype(BF16), mk_ref[0].astype(BF16), nt,
                            preferred_element_type=F32) * (X_HEAD_DIM ** -0.5) + bx_ref[...]
        m = jnp.max(s, axis=1, keepdims=True)
        ex = jnp.exp(s - m)
        p = ex / jnp.sum(ex, axis=1, keepdims=True)
        yc_ref[0] = jnp.dot(p.astype(BF16), mv_ref[0].astype(BF16),
                            preferred_element_type=F32).astype(yc_ref.dtype)

    def group(g):
        kc_ref, vc_ref, bias_ref = groups[g]
        n_cached = kc_ref.shape[1] * kc_ref.shape[2]
        q = qkv_ref[0, 3 * g].astype(BF16)
        kk = jnp.concatenate([kc_ref[0].reshape(n_cached, A_HEAD_DIM), qkv_ref[0, 3 * g + 1]],
                             axis=0).astype(BF16)
        vv = jnp.concatenate([vc_ref[0].reshape(n_cached, A_HEAD_DIM), qkv_ref[0, 3 * g + 2]],
                             axis=0).astype(BF16)
        s = lax.dot_general(q, kk, nt, preferred_element_type=F32) * (A_HEAD_DIM ** -0.5) + bias_ref[...]
        m = jnp.max(s, axis=1, keepdims=True)
        p = jnp.exp(s - m)
        den = jnp.sum(p, axis=1, keepdims=True)
        outs.append(jnp.dot(p.astype(BF16), vv, preferred_element_type=F32) / den)
        lses.append(m + jnp.log(den))
        if g < N_GROUPS - 1:
            return
        mx = jnp.maximum(jnp.maximum(lses[0], lses[1]), lses[2])
        e = [jnp.exp(l - mx) for l in lses]
        tot = e[0] + e[1] + e[2]
        yb = (e[0] / tot) * outs[0] + (e[1] / tot) * outs[1] + (e[2] / tot) * outs[2]
        yb_ref[0] = yb.astype(yb_ref.dtype)

    return [cross] + [functools.partial(group, g) for g in range(N_GROUPS)]


def _dilated_sample_bias(window, r, lb, rc):
    span = window // r
    n_c = (lb // r) * rc * A_HEADS
    rows = np.arange(DEC_SEQ * A_HEADS)
    s_q, h_q = rows // A_HEADS, rows % A_HEADS
    col = np.arange(n_c)
    m_k = col // (rc * A_HEADS)
    c_k = (col % (rc * A_HEADS)) // A_HEADS
    h_k = col % A_HEADS
    delta = (lb + s_q)[:, None] - (m_k * r + c_k)[None, :]
    ok_c = (h_q[:, None] == h_k[None, :]) & (delta % r == 0) & (delta // r <= span) & (delta >= 0)
    coln = np.arange(DEC_SEQ * A_HEADS)
    s_n, h_n = coln // A_HEADS, coln % A_HEADS
    dn = s_q[:, None] - s_n[None, :]
    ok_n = (h_q[:, None] == h_n[None, :]) & (dn >= 0) & (dn % r == 0) & (dn // r <= span)
    ok = np.concatenate([ok_c, ok_n], axis=1)
    return np.where(ok, 0.0, -np.inf).astype(np.float32)


def _sample_attention_job(qkv, caches, xq, mem_k, mem_v):
    rq = DEC_SEQ * A_HEADS
    rx = DEC_SEQ * X_HEADS
    nk = MEM_LEN * X_HEADS
    cache_args, cache_specs, biases = [], [], []
    for (window, r), (cache_k, cache_v) in zip(DIL_GROUPS, caches):
        lb = cache_k.shape[1]
        assert lb % r == 0 and window % r == 0
        rc = min(r, DEC_SEQ)
        nm = lb // r
        spec = ((1, nm, rc * A_HEADS, A_HEAD_DIM), lambda t: (t, 0, 0, 0))
        for c in (cache_k, cache_v):
            cache_args.append(c.reshape(DEC_BATCH, nm, r * A_HEADS, A_HEAD_DIM))
            cache_specs.append(spec)
        biases.append(jnp.asarray(_dilated_sample_bias(window, r, lb, rc)))
    ok = (np.arange(rx) % X_HEADS)[:, None] == (np.arange(nk) % X_HEADS)[None, :]
    bias_x = jnp.asarray(np.where(ok, 0.0, -np.inf).astype(np.float32))

    def const_spec(a):
        return pl.BlockSpec(a.shape, lambda p, i: (0, 0), pipeline_mode=pl.Buffered(1))

    return _SideJob(
        n_steps=DEC_BATCH,
        inputs=(qkv, *cache_args, *biases, xq, mem_k, mem_v, bias_x),
        in_specs=(
            ((1, 3 * N_GROUPS, rq, A_HEAD_DIM), lambda t: (t, 0, 0, 0)),
            *cache_specs,
            *[const_spec(a) for a in biases],
            ((1, rx, X_HEAD_DIM), lambda t: (t, 0, 0)),
            ((1, nk, X_HEAD_DIM), lambda t: (t, 0, 0)),
            ((1, nk, X_HEAD_DIM), lambda t: (t, 0, 0)),
            const_spec(bias_x),
        ),
        out_shapes=(jax.ShapeDtypeStruct((DEC_BATCH, rq, A_HEAD_DIM), BF16),
                    jax.ShapeDtypeStruct((DEC_BATCH, rx, X_HEAD_DIM), BF16)),
        out_specs=(((1, rq, A_HEAD_DIM), lambda t: (t, 0, 0)),
                   ((1, rx, X_HEAD_DIM), lambda t: (t, 0, 0))),
        scratch_shapes=(),
        init=lambda scratch: None,
        pieces=_sample_attn_pieces,
    )


def _cross_prompt_body(q_ref, k_ref, v_ref, o_ref):
    scale = X_HEAD_DIM ** -0.5
    nt = (((1,), (1,)), ((), ()))
    for h in range(X_HEADS):
        cols = pl.ds(h * X_HEAD_DIM, X_HEAD_DIM)
        s = lax.dot_general(q_ref[:, cols].astype(BF16), k_ref[:, cols].astype(BF16), nt,
                            preferred_element_type=F32) * scale
        m = jnp.max(s, axis=1, keepdims=True)
        e = jnp.exp(s - m)
        p = e / jnp.sum(e, axis=1, keepdims=True)
        o_ref[:, cols] = jnp.dot(p.astype(BF16), v_ref[:, cols].astype(BF16),
                                 preferred_element_type=F32).astype(o_ref.dtype)


def _cross_prompt(z_xq, mem_kv, *, tq=512):
    nq = SEQ // tq
    return pl.pallas_call(
        _cross_prompt_body,
        grid=(BATCH, nq),
        in_specs=[
            pl.BlockSpec((tq, W_XQ), lambda b, i: (b * nq + i, 0)),
            pl.BlockSpec((MEM_LEN, W_XQ), lambda b, i: (b, 0)),
            pl.BlockSpec((MEM_LEN, W_XQ), lambda b, i: (b, 1)),
        ],
        out_specs=pl.BlockSpec((tq, W_XQ), lambda b, i: (b * nq + i, 0)),
        out_shape=jax.ShapeDtypeStruct((N_PROMPT, W_XQ), BF16),
        compiler_params=_params(2),
        name="cross_prompt",
    )(z_xq, mem_kv, mem_kv)


def _combine_body(o0, o1, o2, l0, l1, l2, y_ref):
    a0, a1, a2 = l0[...], l1[...], l2[...]
    mx = jnp.maximum(jnp.maximum(a0, a1), a2)
    e0, e1, e2 = jnp.exp(a0 - mx), jnp.exp(a1 - mx), jnp.exp(a2 - mx)
    tot = e0 + e1 + e2
    y = (e0 / tot) * o0[...] + (e1 / tot) * o1[...] + (e2 / tot) * o2[...]
    y_ref[...] = y.astype(y_ref.dtype)


def _rope_tables():
    pos = jnp.concatenate([
        jnp.tile(jnp.arange(SEQ, dtype=jnp.int32), BATCH),
        jnp.tile(PAST_LEN + jnp.arange(DEC_SEQ, dtype=jnp.int32), DEC_BATCH)])
    inv = ROPE_THETA ** (-jnp.arange(0, A_HEAD_DIM, 2, dtype=F32) / A_HEAD_DIM)
    ang = pos.astype(F32)[:, None] * inv[None, :]
    cos, sin = jnp.cos(ang), jnp.sin(ang)
    return jnp.concatenate([cos, cos], axis=1), jnp.concatenate([-sin, sin], axis=1)


def _layer(x_p, x_s, mem_prompt, state_c, state_n, state_m, caches, cache_mem_k, cache_mem_v,
           g_mix, w_in, b_igate, b_fgate, b_branch, g_mem, w_mem_kv,
           w_br_a, w_br_b, w_br_c, w_out, g_mlp, w_up, w_down):
    h_all = _rmsnorm_two_groups(x_p, x_s, g_mix, BF16)

    w_in_t = w_in.T
    z_main = _matmul_stream(h_all, w_in_t, n_cols=COL_GATES, b_rows_are_outputs=True, name="proj_main")
    z_gate, z_xq = _narrow_projection_pair(h_all, w_in_t, COL_GATES, 2 * M_HEADS, COL_REST + W_AQKV, W_XQ,
                                           name="proj_gates_cross_q")
    gate_bias = jnp.stack([b_igate, b_fgate]).astype(F32)

    gp = z_gate[:N_PROMPT].reshape(N_PROMPT, 2, M_HEADS).transpose(2, 1, 0)
    a_p, c_p, n_p, m_p = _mlstm_prompt(z_main, gp, gate_bias)
    gs = z_gate[N_PROMPT:].reshape(DEC_BATCH, DEC_SEQ, 2, M_HEADS).transpose(0, 3, 2, 1)
    gs = jnp.pad(gs, ((0, 0), (0, 0), (0, 0), (0, SAMPLE_CHUNK - DEC_SEQ)))
    mlstm_job = _mlstm_sample_job(
        z_main, gs, gate_bias, state_c, state_n, state_m.reshape(DEC_BATCH, M_HEADS, 1))
    cos2, sin2 = _rope_tables()
    table_spec = pl.BlockSpec((HOST_TM, A_HEAD_DIM), lambda j, i: (i, 0))
    z_aqkv, a_s, c_s, n_s, m_s = _matmul_stream(
        h_all, w_in_t, n_cols=W_AQKV, col_start=COL_REST, b_rows_are_outputs=True, tm=HOST_TM,
        epilogue=_ep_rope, extras=(cos2, sin2), extra_specs=(table_spec, table_spec),
        side=mlstm_job, name="proj_attn")
    a_ps = (a_p, a_s.reshape(N_SAMPLE, W_MV).astype(BF16))

    heads_per_step = (8, 1, 1)
    partial_p, rows_p, rows_s = [], [], []
    yb_p = None
    rq = DEC_SEQ * A_HEADS
    qkv_s = z_aqkv[N_PROMPT:].reshape(DEC_BATCH, DEC_SEQ, 3 * N_GROUPS, A_HEADS, A_HEAD_DIM)
    qkv_s = qkv_s.transpose(0, 2, 1, 3, 4).reshape(DEC_BATCH, 3 * N_GROUPS, rq, A_HEAD_DIM)
    for g, (window, r) in enumerate(DIL_GROUPS):
        if g < N_GROUPS - 1:
            partial_p += _dilated_prompt(z_aqkv, g, r, heads_per_step[g])
        else:
            yb_p = _dilated_prompt(z_aqkv, g, r, heads_per_step[g], combine_with=partial_p)
        c0 = 3 * g * W_AH
        keep = min(window, SEQ)
        for part in (1, 2):
            cs = c0 + part * W_AH
            kept = [z_aqkv[(b + 1) * SEQ - keep:(b + 1) * SEQ, cs:cs + W_AH] for b in range(BATCH)]
            rows_p.append(jnp.stack(kept).reshape(BATCH, keep, A_HEADS, A_HEAD_DIM))
            rows_s.append(qkv_s[:, 3 * g + part].reshape(DEC_BATCH, DEC_SEQ, A_HEADS, A_HEAD_DIM))

    mem_h = _rmsnorm(mem_prompt, g_mem, BF16)
    mem_kv = _matmul(mem_h, w_mem_kv, n_cols=2 * W_XQ, tm=BATCH * MEM_LEN, name="mem_kv")
    yc_p = _cross_prompt(z_xq, mem_kv)

    xq_s = z_xq[N_PROMPT:].reshape(DEC_BATCH, DEC_SEQ * X_HEADS, X_HEAD_DIM)
    attention_job = _sample_attention_job(
        qkv_s, caches, xq_s,
        cache_mem_k.reshape(DEC_BATCH, MEM_LEN * X_HEADS, X_HEAD_DIM),
        cache_mem_v.reshape(DEC_BATCH, MEM_LEN * X_HEADS, X_HEAD_DIM))
    z_br, yb_s, yc_s = _matmul_stream(
        h_all, w_in_t, n_cols=3 * D_MODEL, col_start=COL_REST + W_AQKV + W_XQ, b_rows_are_outputs=True,
        tm=HOST_TM, side=attention_job, split_cols=HOST_SPLIT_COLS, name="proj_branch_gates")
    yb_ps = (yb_p, yb_s.reshape(N_SAMPLE, W_AH))
    yc_ps = (yc_p, yc_s.reshape(N_SAMPLE, W_XQ))

    merged = _branch_merge(a_ps, yb_ps, yc_ps, w_br_a, w_br_b, w_br_c, z_br, b_branch.reshape(1, 3 * D_MODEL))
    x1 = _matmul_stream(merged, w_out, n_cols=D_MODEL, tn=MM_TN, epilogue=_ep_residual_two_groups, extras=(x_p, x_s),
                        extra_specs=_two_group_specs(MM_TM, MM_TN, lambda j: j, SUBLANES, False),
                        name="out_proj")

    h2 = _rmsnorm(x1, g_mlp, BF16)
    u = _matmul_stream(h2, w_up, n_cols=D_FF, out_dtype=BF16, epilogue=_ep_relu2, name="mlp_up")
    x2 = _matmul_ksplit_residual(u, w_down, x1)

    mem_k = mem_kv[:, :W_XQ].reshape(BATCH, MEM_LEN, X_HEADS, X_HEAD_DIM)
    mem_v = mem_kv[:, W_XQ:].reshape(BATCH, MEM_LEN, X_HEADS, X_HEAD_DIM)
    prompt_state = (c_p, n_p, m_p[:, :, 0])
    sample_state = (c_s, n_s, m_s[:, :, 0])
    return x2, prompt_state, sample_state, rows_p, rows_s, mem_k, mem_v


def kernel(x_prompt, x_sample, state_mlstm_C, state_mlstm_n, state_mlstm_m,
           cache_win_k_g0, cache_win_v_g0, cache_win_k_g1, cache_win_v_g1,
           cache_win_k_g2, cache_win_v_g2, cache_mem_k, cache_mem_v, mem_prompt,
           g_mix, w_in, b_igate, b_fgate, b_branch, g_mem, w_mem_kv,
           w_br_a, w_br_b, w_br_c, w_out, g_mlp, w_up, w_down, g_final):
    depth = g_mix.shape[0]
    assert depth == 1, "single-layer stack"
    caches = ((cache_win_k_g0[0], cache_win_v_g0[0]),
              (cache_win_k_g1[0], cache_win_v_g1[0]),
              (cache_win_k_g2[0], cache_win_v_g2[0]))
    x2, p_state, s_state, rows_p, rows_s, mem_k, mem_v = _layer(
        x_prompt.reshape(N_PROMPT, D_MODEL), x_sample.reshape(N_SAMPLE, D_MODEL),
        mem_prompt.reshape(BATCH * MEM_LEN, D_MODEL),
        state_mlstm_C, state_mlstm_n, state_mlstm_m[0], caches, cache_mem_k[0], cache_mem_v[0],
        g_mix[0], w_in[0], b_igate[0], b_fgate[0], b_branch[0], g_mem[0], w_mem_kv[0],
        w_br_a[0], w_br_b[0], w_br_c[0], w_out[0], g_mlp[0], w_up[0], w_down[0])
    y_prompt = _rmsnorm(x2, g_final, F32, row_start=0, n_rows=N_PROMPT).reshape(BATCH, SEQ, D_MODEL)
    y_sample = _rmsnorm(x2, g_final, F32, row_start=N_PROMPT, n_rows=N_SAMPLE).reshape(DEC_BATCH, DEC_SEQ, D_MODEL)
    lead = lambda a: a[None]
    return (y_prompt, y_sample,
            lead(p_state[0]), lead(p_state[1]), lead(p_state[2]),
            *[lead(r) for r in rows_p],
            lead(mem_k), lead(mem_v),
            lead(s_state[0]), lead(s_state[1]), lead(s_state[2]),
            *[lead(r) for r in rows_s])
```

```python
import functools
import math
from typing import Callable, NamedTuple

import numpy as np
import jax
import jax.numpy as jnp
from jax import lax
from jax.experimental import pallas as pl
from jax.experimental.pallas import tpu as pltpu

F32 = jnp.float32
BF16 = jnp.bfloat16
NEG_INF = float("-inf")

D_MODEL = 4096
BATCH = 2
SEQ = 4096
DEC_BATCH = 128
DEC_SEQ = 4
PAST_LEN = 2048
MEM_LEN = 256
M_HEADS = 8
M_DQK = D_MODEL // (2 * M_HEADS)
M_DV = D_MODEL // M_HEADS
A_HEAD_DIM = 128
A_HEADS = D_MODEL // 512
DIL_GROUPS = ((128, 1), (512, 4), (2048, 16))
N_GROUPS = 3
ROPE_THETA = 10000.0
X_HEADS = 4
X_HEAD_DIM = 128
D_FF = 4 * D_MODEL
EPS = 1e-6

W_MQK = M_HEADS * M_DQK
W_MV = M_HEADS * M_DV
W_AH = A_HEADS * A_HEAD_DIM
W_XQ = X_HEADS * X_HEAD_DIM
N_PROMPT = BATCH * SEQ
N_SAMPLE = DEC_BATCH * DEC_SEQ
N_ALL = N_PROMPT + N_SAMPLE

COL_GATES = 2 * W_MQK + 2 * W_MV
COL_REST = COL_GATES + 2 * M_HEADS
W_AQKV = 3 * N_GROUPS * W_AH

LANES = 128
SUBLANES = 8
VMEM_LIMIT_BYTES = 60 * 1024 * 1024

MM_TM = 1088
MM_TN = 512
MM_TN_WIDE = 1024
MERGE_TM = 544
HOST_TM = 544
HOST_SPLIT_COLS = 256
DOWN_TM = 2176
DOWN_TK = 2048
NORM_ROWS = 256
NARROW_TM = 256
CHUNK = 128
SAMPLE_CHUNK = 16
SAMPLE_SEQS_PER_BLOCK = 2
SPAN = 128


def _params(n_axes):
    return pltpu.CompilerParams(
        dimension_semantics=("arbitrary",) * n_axes,
        vmem_limit_bytes=VMEM_LIMIT_BYTES,
    )


def _rmsnorm_body(x_ref, g_ref, o_ref):
    x = x_ref[...]
    ms = jnp.mean(x * x, axis=-1, keepdims=True)
    o_ref[...] = ((x * lax.rsqrt(ms + EPS)) * g_ref[...]).astype(o_ref.dtype)


def _rmsnorm(x, g, out_dtype, *, row_start=0, n_rows=None):
    n_rows = x.shape[0] if n_rows is None else n_rows
    d = x.shape[1]
    off = row_start // NORM_ROWS
    return pl.pallas_call(
        _rmsnorm_body,
        grid=(n_rows // NORM_ROWS,),
        in_specs=[
            pl.BlockSpec((NORM_ROWS, d), lambda i: (i + off, 0)),
            pl.BlockSpec((1, d), lambda i: (0, 0)),
        ],
        out_specs=pl.BlockSpec((NORM_ROWS, d), lambda i: (i, 0)),
        out_shape=jax.ShapeDtypeStruct((n_rows, d), out_dtype),
        compiler_params=_params(1),
        name="rmsnorm",
    )(x, g.reshape(1, d))


def _rmsnorm_two_groups_body(xp_ref, xs_ref, g_ref, o_ref, *, prompt_steps):
    i = pl.program_id(0)

    @pl.when(i < prompt_steps)
    def _():
        _rmsnorm_body(xp_ref, g_ref, o_ref)

    @pl.when(i >= prompt_steps)
    def _():
        _rmsnorm_body(xs_ref, g_ref, o_ref)


def _rmsnorm_two_groups(xp, xs, g, out_dtype):
    d = xp.shape[1]
    ps, ss = xp.shape[0] // NORM_ROWS, xs.shape[0] // NORM_ROWS
    return pl.pallas_call(
        functools.partial(_rmsnorm_two_groups_body, prompt_steps=ps),
        grid=(ps + ss,),
        in_specs=[
            pl.BlockSpec((NORM_ROWS, d), lambda i: (jnp.minimum(i, ps - 1), 0)),
            pl.BlockSpec((NORM_ROWS, d), lambda i: (jnp.maximum(i - ps, 0), 0)),
            pl.BlockSpec((1, d), lambda i: (0, 0)),
        ],
        out_specs=pl.BlockSpec((NORM_ROWS, d), lambda i: (i, 0)),
        out_shape=jax.ShapeDtypeStruct((xp.shape[0] + xs.shape[0], d), out_dtype),
        compiler_params=_params(1),
        name="rmsnorm_two_groups",
    )(xp, xs, g.reshape(1, d))


def _ep_plain(acc, extra, o_ref, j):
    o_ref[...] = acc.astype(o_ref.dtype)


def _ep_residual_two_groups(acc, extra, o_ref, j):
    xp_ref, xs_ref = extra
    tm = acc.shape[0]
    n_full = N_PROMPT // tm
    rem = N_PROMPT - n_full * tm
    assert tm - rem == N_SAMPLE and rem % SUBLANES == 0
    i = pl.program_id(1)

    @pl.when(i < n_full)
    def _():
        o_ref[...] = xp_ref[...] + acc

    @pl.when(i >= n_full)
    def _():
        o_ref[0:rem, :] = xp_ref[tm - rem:tm, :] + acc[0:rem]
        o_ref[rem:tm, :] = xs_ref[...] + acc[rem:tm]


def _two_group_specs(tm, width, col_of_j, row_align, sample_block_is_constant):
    def p_map(j, i):
        row = jnp.minimum(i * tm, N_PROMPT - tm)
        col = col_of_j(j) * width
        return (pl.multiple_of(row, row_align), col if isinstance(col, int) else pl.multiple_of(col, LANES))
    p_spec = pl.BlockSpec((pl.Element(tm), pl.Element(width)), p_map)
    mode = pl.Buffered(1) if sample_block_is_constant else None
    s_spec = pl.BlockSpec((N_SAMPLE, width), lambda j, i: (0, col_of_j(j)), pipeline_mode=mode)
    return p_spec, s_spec


def _ep_relu2(acc, extra, o_ref, j):
    r = jnp.maximum(acc, 0.0)
    o_ref[...] = (r * r).astype(o_ref.dtype)


def _ep_rope(acc, extra, o_ref, j):
    cos_ref, sin_ref = extra
    tn = acc.shape[1]
    assert W_AH % tn == 0
    is_rope = (j // (W_AH // tn)) % 3 != 2
    cos2 = cos_ref[...]
    sin2 = sin_ref[...]
    for s in range(tn // A_HEAD_DIM):
        x = acc[:, s * A_HEAD_DIM:(s + 1) * A_HEAD_DIM]
        rotated = x * cos2 + pltpu.roll(x, A_HEAD_DIM // 2, axis=1) * sin2
        o_ref[:, s * A_HEAD_DIM:(s + 1) * A_HEAD_DIM] = jnp.where(is_rope, rotated, x)


def _mm_body(*refs, b_rows_are_outputs, epilogue, n_extra):
    a_ref, b_ref = refs[0], refs[1]
    extra = refs[2:2 + n_extra]
    o_ref = refs[2 + n_extra]
    b_scr = refs[3 + n_extra]

    @pl.when(pl.program_id(1) == 0)
    def _():
        b_scr[...] = b_ref[...].astype(BF16)

    if b_rows_are_outputs:
        acc = lax.dot_general(a_ref[...], b_scr[...], (((1,), (1,)), ((), ())),
                              preferred_element_type=F32)
    else:
        acc = jnp.dot(a_ref[...], b_scr[...], preferred_element_type=F32)
    epilogue(acc, extra, o_ref, pl.program_id(0))


def _matmul(a, b, *, n_cols, col_start=0, b_rows_are_outputs=False, tm=MM_TM, tn=MM_TN,
            out_dtype=F32, epilogue=_ep_plain, extras=(), extra_specs=(), name="matmul"):
    m, k = a.shape
    if b_rows_are_outputs:
        if col_start % tn == 0:
            b_spec = pl.BlockSpec((tn, k), lambda j, i: (j + col_start // tn, 0))
        else:
            assert col_start % SUBLANES == 0 and tn % SUBLANES == 0
            b_spec = pl.BlockSpec((pl.Element(tn), pl.Element(k)),
                                  lambda j, i: (pl.multiple_of(col_start + j * tn, SUBLANES), 0))
        b_block = (tn, k)
    else:
        assert col_start % tn == 0
        b_spec = pl.BlockSpec((k, tn), lambda j, i: (0, j + col_start // tn))
        b_block = (k, tn)
    return pl.pallas_call(
        functools.partial(_mm_body, b_rows_are_outputs=b_rows_are_outputs, epilogue=epilogue,
                          n_extra=len(extras)),
        grid=(n_cols // tn, m // tm),
        in_specs=[pl.BlockSpec((tm, k), lambda j, i: (i, 0)), b_spec, *extra_specs],
        out_specs=pl.BlockSpec((tm, tn), lambda j, i: (i, j)),
        out_shape=jax.ShapeDtypeStruct((m, n_cols), out_dtype),
        scratch_shapes=[pltpu.VMEM(b_block, BF16)],
        compiler_params=_params(2),
        name=name,
    )(a, b, *extras)


def _narrow_pair_body(a_ref, w0_ref, w1_ref, o0_ref, o1_ref, w0_scr, w1_scr):
    @pl.when(pl.program_id(0) == 0)
    def _():
        w0_scr[...] = w0_ref[...].astype(BF16)
        w1_scr[...] = w1_ref[...].astype(BF16)

    a = a_ref[...]
    nt = (((1,), (1,)), ((), ()))
    o0_ref[...] = lax.dot_general(a, w0_scr[...], nt, preferred_element_type=F32)
    o1_ref[...] = lax.dot_general(a, w1_scr[...], nt, preferred_element_type=F32)


def _narrow_projection_pair(a, w_t, start0, n0, start1, n1, *, tm=MM_TM, name="narrow_pair"):
    m, k = a.shape
    assert start0 % SUBLANES == 0 and start1 % SUBLANES == 0

    def w_spec(start, n):
        return pl.BlockSpec((pl.Element(n), pl.Element(k)), lambda i: (start, 0), pipeline_mode=pl.Buffered(1))

    return pl.pallas_call(
        _narrow_pair_body,
        grid=(m // tm,),
        in_specs=[pl.BlockSpec((tm, k), lambda i: (i, 0)), w_spec(start0, n0), w_spec(start1, n1)],
        out_specs=[pl.BlockSpec((tm, n0), lambda i: (i, 0)), pl.BlockSpec((tm, n1), lambda i: (i, 0))],
        out_shape=[jax.ShapeDtypeStruct((m, n0), F32), jax.ShapeDtypeStruct((m, n1), F32)],
        scratch_shapes=[pltpu.VMEM((n0, k), BF16), pltpu.VMEM((n1, k), BF16)],
        compiler_params=_params(1),
        name=name,
    )(a, w_t, w_t)


class _SideJob(NamedTuple):
    n_steps: int
    inputs: tuple
    in_specs: tuple
    out_shapes: tuple
    out_specs: tuple
    scratch_shapes: tuple
    init: Callable
    pieces: Callable


def _mm_stream_body(*refs, b_rows_are_outputs, epilogue, n_extra, slice_rows, split_cols, side):
    n_side_in = len(side.inputs) if side else 0
    n_side_out = len(side.out_shapes) if side else 0
    n_in = 2 + n_extra + n_side_in
    a_ref, bs_ref = refs[0], refs[1]
    extra = refs[2:2 + n_extra]
    side_in = refs[2 + n_extra:n_in]
    o_ref = refs[n_in]
    side_out = refs[n_in + 1:n_in + 1 + n_side_out]
    b_scr = refs[n_in + 1 + n_side_out]
    side_scr = refs[n_in + 2 + n_side_out:]
    p, i = pl.program_id(0), pl.program_id(1)

    r0 = pl.multiple_of(i * slice_rows, slice_rows)
    b_scr[p % 2, pl.ds(r0, slice_rows), :] = bs_ref[...].astype(BF16)

    if side is not None:
        @pl.when((p == 0) & (i == 0))
        def _():
            side.init(side_scr)

    def multiply(side_pieces):
        slot = (p - 1) % 2
        a = a_ref[...]
        tn = o_ref.shape[1]
        n_split = tn // split_cols
        per_split = -(-len(side_pieces) // n_split)
        for h in range(n_split):
            cols = pl.ds(h * split_cols, split_cols)
            if b_rows_are_outputs:
                acc = lax.dot_general(a, b_scr[slot, cols, :], (((1,), (1,)), ((), ())),
                                      preferred_element_type=F32)
            else:
                acc = jnp.dot(a, b_scr[slot, :, cols], preferred_element_type=F32)
            epilogue(acc, extra, o_ref.at[:, cols], (p - 1) * n_split + h)
            for piece in side_pieces[h * per_split:(h + 1) * per_split]:
                piece()

    if side is None:
        pl.when(p > 0)(lambda: multiply([]))
    else:
        t = (p - 1) * pl.num_programs(1) + i
        pl.when((p > 0) & (t < side.n_steps))(
            lambda: multiply(side.pieces(t, side_in, side_out, side_scr)))
        pl.when((p > 0) & (t >= side.n_steps))(lambda: multiply([]))


def _matmul_stream(a, b, *, n_cols, col_start=0, b_rows_are_outputs=False, tm=MM_TM, tn=MM_TN_WIDE,
                   out_dtype=F32, epilogue=_ep_plain, extras=(), extra_specs=(), side=None,
                   split_cols=MM_TN, name="matmul"):
    m, k = a.shape
    ni, nj = m // tm, n_cols // tn
    assert col_start % SUBLANES == 0 and nj * tn == n_cols

    def side_spec(spec):
        if isinstance(spec, pl.BlockSpec):
            return spec
        block_shape, index_fn = spec
        return pl.BlockSpec(
            block_shape,
            lambda p, i: index_fn(jnp.clip((p - 1) * ni + i, 0, side.n_steps - 1)))

    side_in_specs = [side_spec(s) for s in side.in_specs] if side else []
    side_out_specs = [side_spec(s) for s in side.out_specs] if side else []
    assert side is None or side.n_steps <= nj * ni

    def staged(p):
        return jnp.minimum(p, nj - 1)

    def shifted(index_map):
        return lambda p, i: index_map(jnp.maximum(p - 1, 0), jnp.where(p == 0, 0, i))

    if b_rows_are_outputs:
        tile = (tn, k)
        slice_rows = tn // ni
        bs_spec = pl.BlockSpec(
            (pl.Element(slice_rows), pl.Element(k)),
            lambda p, i: (pl.multiple_of(col_start + staged(p) * tn + i * slice_rows, SUBLANES), 0))
    else:
        assert col_start % tn == 0
        tile = (k, tn)
        slice_rows = k // ni
        bs_spec = pl.BlockSpec((slice_rows, tn), lambda p, i: (i, col_start // tn + staged(p)))
    assert slice_rows * ni == tile[0] and slice_rows % 16 == 0
    extra_specs = [pl.BlockSpec(s.block_shape, shifted(s.index_map), pipeline_mode=s.pipeline_mode)
                   for s in extra_specs]
    outs = pl.pallas_call(
        functools.partial(_mm_stream_body, b_rows_are_outputs=b_rows_are_outputs, epilogue=epilogue,
                          n_extra=len(extras), slice_rows=slice_rows, split_cols=min(tn, split_cols), side=side),
        grid=(nj + 1, ni),
        in_specs=[pl.BlockSpec((tm, k), shifted(lambda j, i: (i, 0))), bs_spec, *extra_specs,
                  *side_in_specs],
        out_specs=[pl.BlockSpec((tm, tn), shifted(lambda j, i: (i, j))), *side_out_specs],
        out_shape=[jax.ShapeDtypeStruct((m, n_cols), out_dtype), *(side.out_shapes if side else ())],
        scratch_shapes=[pltpu.VMEM((2,) + tile, BF16), *(side.scratch_shapes if side else ())],
        compiler_params=_params(2),
        name=name,
    )(a, b, *extras, *(side.inputs if side else ()))
    return outs if side else outs[0]


def _mm_ksplit_body(a_ref, b_ref, res_ref, o_ref, acc_ref):
    kk = pl.program_id(2)
    last = pl.num_programs(2) - 1

    def product():
        return jnp.dot(a_ref[...], b_ref[...].astype(BF16), preferred_element_type=F32)

    @pl.when(kk == 0)
    def _():
        acc_ref[...] = product()

    @pl.when((kk > 0) & (kk < last))
    def _():
        acc_ref[...] += product()

    @pl.when(kk == last)
    def _():
        o_ref[...] = res_ref[...] + (acc_ref[...] + product())


def _matmul_ksplit_residual(a, b, res, *, tm=DOWN_TM, tn=MM_TN, tk=DOWN_TK):
    m, k = a.shape
    n = b.shape[1]
    return pl.pallas_call(
        _mm_ksplit_body,
        grid=(n // tn, m // tm, k // tk),
        in_specs=[
            pl.BlockSpec((tm, tk), lambda j, i, kk: (i, kk)),
            pl.BlockSpec((tk, tn), lambda j, i, kk: (kk, j)),
            pl.BlockSpec((tm, tn), lambda j, i, kk: (i, j)),
        ],
        out_specs=pl.BlockSpec((tm, tn), lambda j, i, kk: (i, j)),
        out_shape=jax.ShapeDtypeStruct((m, n), F32),
        scratch_shapes=[pltpu.VMEM((tm, tn), F32)],
        compiler_params=_params(3),
        name="mlp_down",
    )(a, b, res)


def _sigmoid(x):
    return 1.0 / (1.0 + jnp.exp(-x))


def _straddle_rows(p_ref, s_ref):
    tm = p_ref.shape[0]
    rem = tm - s_ref.shape[0]
    return jnp.concatenate([p_ref[tm - rem:tm, :], s_ref[...]], axis=0)


def _merge_body(ap_ref, as_ref, ybp_ref, ybs_ref, ycp_ref, ycs_ref,
                wa0_ref, wb0_ref, wc0_ref, was_ref, wbs_ref, wcs_ref,
                ga_ref, gb_ref, gc_ref, ba_ref, bb_ref, bc_ref,
                o_ref, wa_scr, wb_scr, wc_scr):
    j, i = pl.program_id(0), pl.program_id(1)
    cur = j % 2

    @pl.when((j == 0) & (i == 0))
    def _():
        wa_scr[0] = wa0_ref[...].astype(BF16)
        wb_scr[0] = wb0_ref[...].astype(BF16)
        wc_scr[0] = wc0_ref[...].astype(BF16)

    for scr, slice_ref in ((wa_scr, was_ref), (wb_scr, wbs_ref), (wc_scr, wcs_ref)):
        rows = slice_ref.shape[0]
        scr[1 - cur, pl.ds(pl.multiple_of(i * rows, rows), rows), :] = slice_ref[...].astype(BF16)

    def merge(a, yb, yc):
        for h in range(o_ref.shape[1] // HOST_SPLIT_COLS):
            cols = pl.ds(h * HOST_SPLIT_COLS, HOST_SPLIT_COLS)
            y_a = jnp.dot(a, wa_scr[cur, :, cols], preferred_element_type=F32)
            y_b = jnp.dot(yb, wb_scr[cur, :, cols], preferred_element_type=F32)
            y_c = jnp.dot(yc, wc_scr[cur, :, cols], preferred_element_type=F32)
            g_a = _sigmoid(ga_ref[:, cols] + ba_ref[:, cols])
            g_b = _sigmoid(gb_ref[:, cols] + bb_ref[:, cols])
            g_c = _sigmoid(gc_ref[:, cols] + bc_ref[:, cols])
            o_ref[:, cols] = (g_a * y_a + g_b * y_b + g_c * y_c).astype(o_ref.dtype)

    is_prompt_tile = i < N_PROMPT // ap_ref.shape[0]

    @pl.when(is_prompt_tile)
    def _():
        merge(ap_ref[...], ybp_ref[...], ycp_ref[...])

    @pl.when(jnp.logical_not(is_prompt_tile))
    def _():
        merge(_straddle_rows(ap_ref, as_ref), _straddle_rows(ybp_ref, ybs_ref), _straddle_rows(ycp_ref, ycs_ref))


def _branch_merge(a_ps, yb_ps, yc_ps, w_a, w_b, w_c, z_br, b_branch, *, tm=MERGE_TM, tn=MM_TN):
    m = N_ALL
    assert (N_PROMPT // tm + 1) * tm == N_ALL
    nj = D_MODEL // tn

    def gate_spec(part):
        return pl.BlockSpec((tm, tn), lambda j, i: (i, part * nj + j))

    def bias_spec(part):
        return pl.BlockSpec((1, tn), lambda j, i: (0, part * nj + j))

    ni = m // tm
    widths = (W_MV, W_AH, W_XQ)
    assert all(k % (16 * ni) == 0 for k in widths)

    def first_tile_spec(k):
        return pl.BlockSpec((k, tn), lambda j, i: (0, 0), pipeline_mode=pl.Buffered(1))

    def next_slice_spec(k):
        return pl.BlockSpec((k // ni, tn), lambda j, i: (i, jnp.minimum(j + 1, nj - 1)))

    bf16_rows = 16
    return pl.pallas_call(
        _merge_body,
        grid=(nj, ni),
        in_specs=[
            *_two_group_specs(tm, W_MV, lambda j: 0, bf16_rows, True),
            *_two_group_specs(tm, W_AH, lambda j: 0, bf16_rows, True),
            *_two_group_specs(tm, W_XQ, lambda j: 0, bf16_rows, True),
            *[first_tile_spec(k) for k in widths],
            *[next_slice_spec(k) for k in widths],
            gate_spec(0), gate_spec(1), gate_spec(2),
            bias_spec(0), bias_spec(1), bias_spec(2),
        ],
        out_specs=pl.BlockSpec((tm, tn), lambda j, i: (i, j)),
        out_shape=jax.ShapeDtypeStruct((m, D_MODEL), BF16),
        scratch_shapes=[pltpu.VMEM((2, k, tn), BF16) for k in widths],
        compiler_params=_params(2),
        name="branch_merge",
    )(*a_ps, *yb_ps, *yc_ps, w_a, w_b, w_c, w_a, w_b, w_c,
      z_br, z_br, z_br, b_branch, b_branch, b_branch)


def _log_sigmoid(x):
    return jnp.minimum(x, 0.0) - jnp.log1p(jnp.exp(-jnp.abs(x)))


def _mlstm_chunk(q, k, v, irow, frow, c_state, n_state, m_state, n_valid):
    L = q.shape[0]
    ti = lax.broadcasted_iota(jnp.int32, (L, L), 0)
    si = lax.broadcasted_iota(jnp.int32, (L, L), 1)
    causal = si <= ti
    eye = si == ti
    f_b = jnp.broadcast_to(frow, (L, L))
    i_b = jnp.broadcast_to(irow, (L, L))
    bcol = jnp.sum(jnp.where(causal, f_b, 0.0), axis=1, keepdims=True)
    fcol = jnp.sum(jnp.where(eye, f_b, 0.0), axis=1, keepdims=True)
    icol = jnp.sum(jnp.where(eye, i_b, 0.0), axis=1, keepdims=True)
    brow = jnp.sum(jnp.where(ti <= si, jnp.broadcast_to(fcol, (L, L)), 0.0), axis=0, keepdims=True)

    acol = bcol + m_state
    logw = jnp.where(causal, bcol - brow + irow, NEG_INF)
    mt = jnp.maximum(acol, jnp.max(logw, axis=1, keepdims=True))
    w_inter = jnp.exp(acol - mt)
    w_intra = jnp.exp(logw - mt)

    qb = q.astype(BF16)
    kb = k.astype(BF16)
    nt = (((1,), (1,)), ((), ()))
    s = lax.dot_general(qb, kb, nt, preferred_element_type=F32) * w_intra
    inter = lax.dot_general(qb, c_state.astype(BF16), nt, preferred_element_type=F32)
    num = w_inter * inter + jnp.dot(s.astype(BF16), v.astype(BF16), preferred_element_type=F32)
    nq = w_inter * jnp.sum(q * n_state, axis=1, keepdims=True) + jnp.sum(s, axis=1, keepdims=True)
    h = num / jnp.maximum(jnp.abs(nq), jnp.exp(-mt))

    last = slice(n_valid - 1, n_valid)
    m_end = mt[last, :]
    w_c = jnp.exp(acol[last, :] - m_end)
    w_s = jnp.exp(bcol[last, :] - bcol + icol - m_end)
    if n_valid < L:
        w_s = jnp.where(lax.broadcasted_iota(jnp.int32, (L, 1), 0) < n_valid, w_s, 0.0)
    tn = (((0,), (0,)), ((), ()))
    c_new = w_c * c_state + lax.dot_general((v * w_s).astype(BF16), kb, tn, preferred_element_type=F32)
    n_new = w_c * n_state + jnp.sum(w_s * k, axis=0, keepdims=True)
    return h, c_new, n_new, m_end


def _gate_rows(g_ref_val, bias_ref, head):
    irow = g_ref_val[0:1, :] + bias_ref[0, head]
    frow = _log_sigmoid(g_ref_val[1:2, :] + bias_ref[1, head])
    return irow, frow


def _mlstm_prompt_body(bias_ref, q_ref, k_ref, v_ref, mo_ref, g_ref,
                       a_ref, c_ref, n_ref, m_ref):
    @pl.when(pl.program_id(1) == 0)
    def _():
        c_ref[...] = jnp.zeros_like(c_ref)
        n_ref[...] = jnp.zeros_like(n_ref)
        m_ref[...] = jnp.zeros_like(m_ref)

    for head in range(M_HEADS):
        qk = pl.ds(head * M_DQK, M_DQK)
        vo = pl.ds(head * M_DV, M_DV)
        one = pl.ds(head, 1)
        irow, frow = _gate_rows(g_ref[head], bias_ref, head)
        h, c_new, n_new, m_end = _mlstm_chunk(
            q_ref[:, qk], k_ref[:, qk] * (M_DQK ** -0.5), v_ref[:, vo], irow, frow,
            c_ref[0, head], n_ref[0, one, :], m_ref[0, one, 0:1], CHUNK)
        a_ref[:, vo] = (_sigmoid(mo_ref[:, vo]) * h).astype(a_ref.dtype)
        c_ref[0, head] = c_new
        n_ref[0, one, :] = n_new
        m_ref[0, one, :] = jnp.broadcast_to(m_end, (1, LANES))


def _mlstm_prompt(z_main, gates_rows, gate_bias):
    nc = SEQ // CHUNK
    return pl.pallas_call(
        _mlstm_prompt_body,
        grid=(BATCH, nc),
        in_specs=[
            pl.BlockSpec(memory_space=pltpu.SMEM),
            pl.BlockSpec((CHUNK, W_MQK), lambda b, c: (b * nc + c, 0)),
            pl.BlockSpec((CHUNK, W_MQK), lambda b, c: (b * nc + c, 1)),
            pl.BlockSpec((CHUNK, W_MV), lambda b, c: (b * nc + c, 2 * W_MQK // W_MV)),
            pl.BlockSpec((CHUNK, W_MV), lambda b, c: (b * nc + c, 2 * W_MQK // W_MV + 1)),
            pl.BlockSpec((M_HEADS, 2, CHUNK), lambda b, c: (0, 0, b * nc + c)),
        ],
        out_specs=[
            pl.BlockSpec((CHUNK, W_MV), lambda b, c: (b * nc + c, 0)),
            pl.BlockSpec((1, M_HEADS, M_DV, M_DQK), lambda b, c: (b, 0, 0, 0)),
            pl.BlockSpec((1, M_HEADS, M_DQK), lambda b, c: (b, 0, 0)),
            pl.BlockSpec((1, M_HEADS, LANES), lambda b, c: (b, 0, 0)),
        ],
        out_shape=[
            jax.ShapeDtypeStruct((N_PROMPT, W_MV), BF16),
            jax.ShapeDtypeStruct((BATCH, M_HEADS, M_DV, M_DQK), F32),
            jax.ShapeDtypeStruct((BATCH, M_HEADS, M_DQK), F32),
            jax.ShapeDtypeStruct((BATCH, M_HEADS, LANES), F32),
        ],
        compiler_params=_params(2),
        name="mlstm_prompt",
    )(gate_bias, z_main, z_main, z_main, z_main, gates_rows)


def _mlstm_sample_init(scratch):
    for scr in scratch:
        scr[...] = jnp.zeros_like(scr)


def _mlstm_sample_pieces(t, ins, outs, scratch):
    bias_ref, q_ref, k_ref, v_ref, mo_ref, g_ref, c0_ref, n0_ref, m0_ref = ins
    a_ref, c_ref, n_ref, m_ref = outs
    q_scr, k_scr, v_scr = scratch
    first_half = (t % SAMPLE_SEQS_PER_BLOCK) == 0

    def seq_rows(ref, cols):
        return jnp.where(first_half, ref[0:DEC_SEQ, cols], ref[DEC_SEQ:2 * DEC_SEQ, cols])

    def one_head(head):
        qk = pl.ds(head * M_DQK, M_DQK)
        vo = pl.ds(head * M_DV, M_DV)
        one = pl.ds(head, 1)
        q_scr[0:DEC_SEQ, qk] = seq_rows(q_ref, qk)
        k_scr[0:DEC_SEQ, qk] = seq_rows(k_ref, qk) * (M_DQK ** -0.5)
        v_scr[0:DEC_SEQ, vo] = seq_rows(v_ref, vo)
        irow, frow = _gate_rows(g_ref[0, head], bias_ref, head)
        h, c_new, n_new, m_end = _mlstm_chunk(
            q_scr[:, qk], k_scr[:, qk], v_scr[:, vo], irow, frow,
            c0_ref[0, 0, head], n0_ref[0, 0, one, :], m0_ref[0, one, :], DEC_SEQ)
        a_ref[0, :, vo] = _sigmoid(seq_rows(mo_ref, vo)) * h[0:DEC_SEQ, :]
        c_ref[0, head] = c_new
        n_ref[0, one, :] = n_new
        m_ref[0, one, :] = jnp.broadcast_to(m_end, (1, LANES))

    def all_heads():
        for head in range(M_HEADS):
            one_head(head)

    return [all_heads]


def _mlstm_sample_job(z_main, gates_rows, gate_bias, c0, n0, m0):
    per = SAMPLE_SEQS_PER_BLOCK
    rows = per * DEC_SEQ
    assert rows == SUBLANES and N_PROMPT % rows == 0
    r0 = N_PROMPT // rows
    v_blk = 2 * W_MQK // W_MV
    return _SideJob(
        n_steps=DEC_BATCH,
        inputs=(gate_bias, z_main, z_main, z_main, z_main, gates_rows, c0, n0, m0),
        in_specs=(
            pl.BlockSpec(memory_space=pltpu.SMEM),
            ((rows, W_MQK), lambda t: (r0 + t // per, 0)),
            ((rows, W_MQK), lambda t: (r0 + t // per, 1)),
            ((rows, W_MV), lambda t: (r0 + t // per, v_blk)),
            ((rows, W_MV), lambda t: (r0 + t // per, v_blk + 1)),
            ((1, M_HEADS, 2, SAMPLE_CHUNK), lambda t: (t, 0, 0, 0)),
            ((1, 1, M_HEADS, M_DV, M_DQK), lambda t: (0, t, 0, 0, 0)),
            ((1, 1, M_HEADS, M_DQK), lambda t: (0, t, 0, 0)),
            ((1, M_HEADS, 1), lambda t: (t, 0, 0)),
        ),
        out_shapes=(
            jax.ShapeDtypeStruct((DEC_BATCH, DEC_SEQ, W_MV), F32),
            jax.ShapeDtypeStruct((DEC_BATCH, M_HEADS, M_DV, M_DQK), F32),
            jax.ShapeDtypeStruct((DEC_BATCH, M_HEADS, M_DQK), F32),
            jax.ShapeDtypeStruct((DEC_BATCH, M_HEADS, LANES), F32),
        ),
        out_specs=(
            ((1, DEC_SEQ, W_MV), lambda t: (t, 0, 0)),
            ((1, M_HEADS, M_DV, M_DQK), lambda t: (t, 0, 0, 0)),
            ((1, M_HEADS, M_DQK), lambda t: (t, 0, 0)),
            ((1, M_HEADS, LANES), lambda t: (t, 0, 0)),
        ),
        scratch_shapes=(pltpu.VMEM((SAMPLE_CHUNK, W_MQK), F32), pltpu.VMEM((SAMPLE_CHUNK, W_MQK), F32),
                        pltpu.VMEM((SAMPLE_CHUNK, W_MV), F32)),
        init=_mlstm_sample_init,
        pieces=_mlstm_sample_pieces,
    )


def _dil_prompt_combine_body(q_ref, kp_ref, kc_ref, vp_ref, vc_ref, o0_ref, l0_ref, o1_ref, l1_ref,
                             y_ref, o_scr, l_scr, *, r, hb):
    _dil_prompt_body(q_ref, kp_ref, kc_ref, vp_ref, vc_ref, o_scr, l_scr, r=r, hb=hb)
    _combine_body(o0_ref, o1_ref, o_scr, l0_ref, l1_ref, l_scr, y_ref)


def _dil_prompt_heads_body(*refs, r, hb):
    ins = refs[:5 * hb]
    o_ref, l_ref = refs[5 * hb], refs[5 * hb + 1]
    scr = refs[5 * hb + 2:]
    for hh in range(hb):
        o_scr, l_scr = scr[2 * hh], scr[2 * hh + 1]
        _dil_prompt_body(*ins[5 * hh:5 * hh + 5], o_scr, l_scr, r=r, hb=1)
        cols = pl.ds(hh * A_HEAD_DIM, A_HEAD_DIM)
        o_ref[:, cols] = o_scr[...]
        l_ref[:, cols] = l_scr[...]


def _dil_prompt_body(q_ref, kp_ref, kc_ref, vp_ref, vc_ref, o_ref, l_ref, *, r, hb):
    first_key = jnp.where(pl.program_id(2) > 0, 0, SPAN)
    qi = lax.broadcasted_iota(jnp.int32, (SPAN, 2 * SPAN), 0)
    ki = lax.broadcasted_iota(jnp.int32, (SPAN, 2 * SPAN), 1)
    ok = (ki >= qi) & (ki <= qi + SPAN) & (ki >= first_key)
    bias = jnp.where(ok, 0.0, NEG_INF)
    scale = A_HEAD_DIM ** -0.5
    nt = (((1,), (1,)), ((), ()))
    for c in range(r):
        rows = pl.ds(c, SPAN, stride=r) if r > 1 else pl.ds(0, SPAN)
        for hh in range(hb):
            cols = pl.ds(hh * A_HEAD_DIM, A_HEAD_DIM)
            q = q_ref[rows, cols].astype(BF16)
            kk = jnp.concatenate([kp_ref[rows, cols], kc_ref[rows, cols]], axis=0).astype(BF16)
            vv = jnp.concatenate([vp_ref[rows, cols], vc_ref[rows, cols]], axis=0).astype(BF16)
            s = lax.dot_general(q, kk, nt, preferred_element_type=F32) * scale + bias
            m = jnp.max(s, axis=1, keepdims=True)
            p = jnp.exp(s - m)
            den = jnp.sum(p, axis=1, keepdims=True)
            o = jnp.dot(p.astype(BF16), vv, preferred_element_type=F32) / den
            o_ref[rows, cols] = o
            l_ref[rows, cols] = jnp.broadcast_to(m + jnp.log(den), (SPAN, A_HEAD_DIM))


def _dilated_prompt(z_aqkv, g, r, hb, combine_with=None):
    rows = SPAN * r
    nblk = SEQ // rows
    wcol = hb * A_HEAD_DIM
    per_part = W_AH // wcol

    def spec(part, prev):
        def imap(b, hg, n):
            nn = jnp.maximum(n - 1, 0) if prev else n
            return (b * nblk + nn, (3 * g + part) * per_part + hg)
        return pl.BlockSpec((rows, wcol), imap)

    out_spec = pl.BlockSpec((rows, wcol), lambda b, hg, n: (b * nblk + n, hg))
    qkv_specs = [spec(0, False), spec(1, True), spec(1, False), spec(2, True), spec(2, False)]
    if r > 1 and hb > 1:
        assert combine_with is None

        def head_spec(part, prev, hh):
            def imap(b, hg, n):
                nn = jnp.maximum(n - 1, 0) if prev else n
                return (b * nblk + nn, (3 * g + part) * A_HEADS + hg * hb + hh)
            return pl.BlockSpec((rows, A_HEAD_DIM), imap)

        head_specs = [head_spec(part, prev, hh) for hh in range(hb)
                      for part, prev in ((0, False), (1, True), (1, False), (2, True), (2, False))]
        return pl.pallas_call(
            functools.partial(_dil_prompt_heads_body, r=r, hb=hb),
            grid=(BATCH, A_HEADS // hb, nblk),
            in_specs=head_specs,
            out_specs=[out_spec, out_spec],
            out_shape=[jax.ShapeDtypeStruct((N_PROMPT, W_AH), F32)] * 2,
            scratch_shapes=[pltpu.VMEM((rows, A_HEAD_DIM), F32)] * (2 * hb),
            compiler_params=_params(3),
            name=f"dilated_prompt_g{g}",
        )(*([z_aqkv] * (5 * hb)))
    if combine_with is None:
        return pl.pallas_call(
            functools.partial(_dil_prompt_body, r=r, hb=hb),
            grid=(BATCH, A_HEADS // hb, nblk),
            in_specs=qkv_specs,
            out_specs=[out_spec, out_spec],
            out_shape=[jax.ShapeDtypeStruct((N_PROMPT, W_AH), F32)] * 2,
            compiler_params=_params(3),
            name=f"dilated_prompt_g{g}",
        )(z_aqkv, z_aqkv, z_aqkv, z_aqkv, z_aqkv)
    return pl.pallas_call(
        functools.partial(_dil_prompt_combine_body, r=r, hb=hb),
        grid=(BATCH, A_HEADS // hb, nblk),
        in_specs=qkv_specs + [out_spec] * 4,
        out_specs=out_spec,
        out_shape=jax.ShapeDtypeStruct((N_PROMPT, W_AH), BF16),
        scratch_shapes=[pltpu.VMEM((rows, wcol), F32)] * 2,
        compiler_params=_params(3),
        name=f"dilated_prompt_g{g}_combine",
    )(z_aqkv, z_aqkv, z_aqkv, z_aqkv, z_aqkv, *combine_with)


def _sample_attn_pieces(t, ins, out_refs, scratch):
    (qkv_ref, k0_ref, v0_ref, k1_ref, v1_ref, k2_ref, v2_ref, b0_ref, b1_ref, b2_ref,
     xq_ref, mk_ref, mv_ref, bx_ref) = ins
    yb_ref, yc_ref = out_refs
    nt = (((1,), (1,)), ((), ()))
    outs, lses = [], []
    groups = ((k0_ref, v0_ref, b0_ref), (k1_ref, v1_ref, b1_ref), (k2_ref, v2_ref, b2_ref))

    def cross():
        s = lax.dot_general(xq_ref[0].astype(BF16), mk_ref[0].astype(BF16), nt,
                            preferred_element_type=F32) * (X_HEAD_DIM ** -0.5) + bx_ref[...]
        m = jnp.max(s, axis=1, keepdims=True)
        ex = jnp.exp(s - m)
        p = ex / jnp.sum(ex, axis=1, keepdims=True)
        yc_ref[0] = jnp.dot(p.astype(BF16), mv_ref[0].astype(BF16),
                            preferred_element_type=F32).astype(yc_ref.dtype)

    def group(g):
        kc_ref, vc_ref, bias_ref = groups[g]
        n_cached = kc_ref.shape[1] * kc_ref.shape[2]
        q = qkv_ref[0, 3 * g].astype(BF16)
        kk = jnp.concatenate([kc_ref[0].reshape(n_cached, A_HEAD_DIM), qkv_ref[0, 3 * g + 1]],
                             axis=0).astype(BF16)
        vv = jnp.concatenate([vc_ref[0].reshape(n_cached, A_HEAD_DIM), qkv_ref[0, 3 * g + 2]],
                             axis=0).astype(BF16)
        s = lax.dot_general(q, kk, nt, preferred_element_type=F32) * (A_HEAD_DIM ** -0.5) + bias_ref[...]
        m = jnp.max(s, axis=1, keepdims=True)
        p = jnp.exp(s - m)
        den = jnp.sum(p, axis=1, keepdims=True)
        outs.append(jnp.dot(p.astype(BF16), vv, preferred_element_type=F32) / den)
        lses.append(m + jnp.log(den))
        if g < N_GROUPS - 1:
            return
        mx = jnp.maximum(jnp.maximum(lses[0], lses[1]), lses[2])
        e = [jnp.exp(l - mx) for l in lses]
        tot = e[0] + e[1] + e[2]
        yb = (e[0] / tot) * outs[0] + (e[1] / tot) * outs[1] + (e[2] / tot) * outs[2]
        yb_ref[0] = yb.astype(yb_ref.dtype)

    return [cross] + [functools.partial(group, g) for g in range(N_GROUPS)]


def _dilated_sample_bias(window, r, lb, rc):
    span = window // r
    n_c = (lb // r) * rc * A_HEADS
    rows = np.arange(DEC_SEQ * A_HEADS)
    s_q, h_q = rows // A_HEADS, rows % A_HEADS
    col = np.arange(n_c)
    m_k = col // (rc * A_HEADS)
    c_k = (col % (rc * A_HEADS)) // A_HEADS
    h_k = col % A_HEADS
    delta = (lb + s_q)[:, None] - (m_k * r + c_k)[None, :]
    ok_c = (h_q[:, None] == h_k[None, :]) & (delta % r == 0) & (delta // r <= span) & (delta >= 0)
    coln = np.arange(DEC_SEQ * A_HEADS)
    s_n, h_n = coln // A_HEADS, coln % A_HEADS
    dn = s_q[:, None] - s_n[None, :]
    ok_n = (h_q[:, None] == h_n[None, :]) & (dn >= 0) & (dn % r == 0) & (dn // r <= span)
    ok = np.concatenate([ok_c, ok_n], axis=1)
    return np.where(ok, 0.0, -np.inf).astype(np.float32)


def _sample_attention_job(qkv, caches, xq, mem_k, mem_v):
    rq = DEC_SEQ * A_HEADS
    rx = DEC_SEQ * X_HEADS
    nk = MEM_LEN * X_HEADS
    cache_args, cache_specs, biases = [], [], []
    for (window, r), (cache_k, cache_v) in zip(DIL_GROUPS, caches):
        lb = cache_k.shape[1]
        assert lb % r == 0 and window % r == 0
        rc = min(r, DEC_SEQ)
        nm = lb // r
        spec = ((1, nm, rc * A_HEADS, A_HEAD_DIM), lambda t: (t, 0, 0, 0))
        for c in (cache_k, cache_v):
            cache_args.append(c.reshape(DEC_BATCH, nm, r * A_HEADS, A_HEAD_DIM))
            cache_specs.append(spec)
        biases.append(jnp.asarray(_dilated_sample_bias(window, r, lb, rc)))
    ok = (np.arange(rx) % X_HEADS)[:, None] == (np.arange(nk) % X_HEADS)[None, :]
    bias_x = jnp.asarray(np.where(ok, 0.0, -np.inf).astype(np.float32))

    def const_spec(a):
        return pl.BlockSpec(a.shape, lambda p, i: (0, 0), pipeline_mode=pl.Buffered(1))

    return _SideJob(
        n_steps=DEC_BATCH,
        inputs=(qkv, *cache_args, *biases, xq, mem_k, mem_v, bias_x),
        in_specs=(
            ((1, 3 * N_GROUPS, rq, A_HEAD_DIM), lambda t: (t, 0, 0, 0)),
            *cache_specs,
            *[const_spec(a) for a in biases],
            ((1, rx, X_HEAD_DIM), lambda t: (t, 0, 0)),
            ((1, nk, X_HEAD_DIM), lambda t: (t, 0, 0)),
            ((1, nk, X_HEAD_DIM), lambda t: (t, 0, 0)),
            const_spec(bias_x),
        ),
        out_shapes=(jax.ShapeDtypeStruct((DEC_BATCH, rq, A_HEAD_DIM), BF16),
                    jax.ShapeDtypeStruct((DEC_BATCH, rx, X_HEAD_DIM), BF16)),
        out_specs=(((1, rq, A_HEAD_DIM), lambda t: (t, 0, 0)),
                   ((1, rx, X_HEAD_DIM), lambda t: (t, 0, 0))),
        scratch_shapes=(),
        init=lambda scratch: None,
        pieces=_sample_attn_pieces,
    )


def _cross_prompt_body(q_ref, k_ref, v_ref, o_ref):
    scale = X_HEAD_DIM ** -0.5
    nt = (((1,), (1,)), ((), ()))
    for h in range(X_HEADS):
        cols = pl.ds(h * X_HEAD_DIM, X_HEAD_DIM)
        s = lax.dot_general(q_ref[:, cols].astype(BF16), k_ref[:, cols].astype(BF16), nt,
                            preferred_element_type=F32) * scale
        m = jnp.max(s, axis=1, keepdims=True)
        e = jnp.exp(s - m)
        p = e / jnp.sum(e, axis=1, keepdims=True)
        o_ref[:, cols] = jnp.dot(p.astype(BF16), v_ref[:, cols].astype(BF16),
                                 preferred_element_type=F32).astype(o_ref.dtype)


def _cross_prompt(z_xq, mem_kv, *, tq=512):
    nq = SEQ // tq
    return pl.pallas_call(
        _cross_prompt_body,
        grid=(BATCH, nq),
        in_specs=[
            pl.BlockSpec((tq, W_XQ), lambda b, i: (b * nq + i, 0)),
            pl.BlockSpec((MEM_LEN, W_XQ), lambda b, i: (b, 0)),
            pl.BlockSpec((MEM_LEN, W_XQ), lambda b, i: (b, 1)),
        ],
        out_specs=pl.BlockSpec((tq, W_XQ), lambda b, i: (b * nq + i, 0)),
        out_shape=jax.ShapeDtypeStruct((N_PROMPT, W_XQ), BF16),
        compiler_params=_params(2),
        name="cross_prompt",
    )(z_xq, mem_kv, mem_kv)


def _combine_body(o0, o1, o2, l0, l1, l2, y_ref):
    a0, a1, a2 = l0[...], l1[...], l2[...]
    mx = jnp.maximum(jnp.maximum(a0, a1), a2)
    e0, e1, e2 = jnp.exp(a0 - mx), jnp.exp(a1 - mx), jnp.exp(a2 - mx)
    tot = e0 + e1 + e2
    y = (e0 / tot) * o0[...] + (e1 / tot) * o1[...] + (e2 / tot) * o2[...]
    y_ref[...] = y.astype(y_ref.dtype)


def _rope_tables():
    pos = jnp.concatenate([
        jnp.tile(jnp.arange(SEQ, dtype=jnp.int32), BATCH),
        jnp.tile(PAST_LEN + jnp.arange(DEC_SEQ, dtype=jnp.int32), DEC_BATCH)])
    inv = ROPE_THETA ** (-jnp.arange(0, A_HEAD_DIM, 2, dtype=F32) / A_HEAD_DIM)
    ang = pos.astype(F32)[:, None] * inv[None, :]
    cos, sin = jnp.cos(ang), jnp.sin(ang)
    return jnp.concatenate([cos, cos], axis=1), jnp.concatenate([-sin, sin], axis=1)


def _layer(x_p, x_s, mem_prompt, state_c, state_n, state_m, caches, cache_mem_k, cache_mem_v,
           g_mix, w_in, b_igate, b_fgate, b_branch, g_mem, w_mem_kv,
           w_br_a, w_br_b, w_br_c, w_out, g_mlp, w_up, w_down):
    h_all = _rmsnorm_two_groups(x_p, x_s, g_mix, BF16)

    w_in_t = w_in.T
    z_main = _matmul_stream(h_all, w_in_t, n_cols=COL_GATES, b_rows_are_outputs=True, name="proj_main")
    z_gate, z_xq = _narrow_projection_pair(h_all, w_in_t, COL_GATES, 2 * M_HEADS, COL_REST + W_AQKV, W_XQ,
                                           tm=NARROW_TM, name="proj_gates_cross_q")
    gate_bias = jnp.stack([b_igate, b_fgate]).astype(F32)

    gp = z_gate[:N_PROMPT].reshape(N_PROMPT, 2, M_HEADS).transpose(2, 1, 0)
    a_p, c_p, n_p, m_p = _mlstm_prompt(z_main, gp, gate_bias)
    gs = z_gate[N_PROMPT:].reshape(DEC_BATCH, DEC_SEQ, 2, M_HEADS).transpose(0, 3, 2, 1)
    gs = jnp.pad(gs, ((0, 0), (0, 0), (0, 0), (0, SAMPLE_CHUNK - DEC_SEQ)))
    mlstm_job = _mlstm_sample_job(
        z_main, gs, gate_bias, state_c, state_n, state_m.reshape(DEC_BATCH, M_HEADS, 1))
    cos2, sin2 = _rope_tables()
    table_spec = pl.BlockSpec((HOST_TM, A_HEAD_DIM), lambda j, i: (i, 0))
    z_aqkv, a_s, c_s, n_s, m_s = _matmul_stream(
        h_all, w_in_t, n_cols=W_AQKV, col_start=COL_REST, b_rows_are_outputs=True, tm=HOST_TM,
        epilogue=_ep_rope, extras=(cos2, sin2), extra_specs=(table_spec, table_spec),
        side=mlstm_job, name="proj_attn")
    a_ps = (a_p, a_s.reshape(N_SAMPLE, W_MV).astype(BF16))

    heads_per_step = (8, 4, 1)
    partial_p, rows_p, rows_s = [], [], []
    yb_p = None
    rq = DEC_SEQ * A_HEADS
    qkv_s = z_aqkv[N_PROMPT:].reshape(DEC_BATCH, DEC_SEQ, 3 * N_GROUPS, A_HEADS, A_HEAD_DIM)
    qkv_s = qkv_s.transpose(0, 2, 1, 3, 4).reshape(DEC_BATCH, 3 * N_GROUPS, rq, A_HEAD_DIM)
    for g, (window, r) in enumerate(DIL_GROUPS):
        if g < N_GROUPS - 1:
            partial_p += _dilated_prompt(z_aqkv, g, r, heads_per_step[g])
        else:
            yb_p = _dilated_prompt(z_aqkv, g, r, heads_per_step[g], combine_with=partial_p)
        c0 = 3 * g * W_AH
        keep = min(window, SEQ)
        for part in (1, 2):
            cs = c0 + part * W_AH
            kept = [z_aqkv[(b + 1) * SEQ - keep:(b + 1) * SEQ, cs:cs + W_AH] for b in range(BATCH)]
            rows_p.append(jnp.stack(kept).reshape(BATCH, keep, A_HEADS, A_HEAD_DIM))
            rows_s.append(qkv_s[:, 3 * g + part].reshape(DEC_BATCH, DEC_SEQ, A_HEADS, A_HEAD_DIM))

    mem_h = _rmsnorm(mem_prompt, g_mem, BF16)
    mem_kv = _matmul(mem_h, w_mem_kv, n_cols=2 * W_XQ, tm=BATCH * MEM_LEN, name="mem_kv")
    yc_p = _cross_prompt(z_xq, mem_kv)

    xq_s = z_xq[N_PROMPT:].reshape(DEC_BATCH, DEC_SEQ * X_HEADS, X_HEAD_DIM)
    attention_job = _sample_attention_job(
        qkv_s, caches, xq_s,
        cache_mem_k.reshape(DEC_BATCH, MEM_LEN * X_HEADS, X_HEAD_DIM),
        cache_mem_v.reshape(DEC_BATCH, MEM_LEN * X_HEADS, X_HEAD_DIM))
    z_br, yb_s, yc_s = _matmul_stream(
        h_all, w_in_t, n_cols=3 * D_MODEL, col_start=COL_REST + W_AQKV + W_XQ, b_rows_are_outputs=True,
        tm=HOST_TM, side=attention_job, split_cols=HOST_SPLIT_COLS, name="proj_branch_gates")
    yb_ps = (yb_p, yb_s.reshape(N_SAMPLE, W_AH))
    yc_ps = (yc_p, yc_s.reshape(N_SAMPLE, W_XQ))

    merged = _branch_merge(a_ps, yb_ps, yc_ps, w_br_a, w_br_b, w_br_c, z_br, b_branch.reshape(1, 3 * D_MODEL))
    x1 = _matmul_stream(merged, w_out, n_cols=D_MODEL, tn=MM_TN, epilogue=_ep_residual_two_groups, extras=(x_p, x_s),
                        extra_specs=_two_group_specs(MM_TM, MM_TN, lambda j: j, SUBLANES, False),
                        name="out_proj")

    h2 = _rmsnorm(x1, g_mlp, BF16)
    u = _matmul_stream(h2, w_up, n_cols=D_FF, out_dtype=BF16, epilogue=_ep_relu2, name="mlp_up")
    x2 = _matmul_ksplit_residual(u, w_down, x1)

    mem_k = mem_kv[:, :W_XQ].reshape(BATCH, MEM_LEN, X_HEADS, X_HEAD_DIM)
    mem_v = mem_kv[:, W_XQ:].reshape(BATCH, MEM_LEN, X_HEADS, X_HEAD_DIM)
    prompt_state = (c_p, n_p, m_p[:, :, 0])
    sample_state = (c_s, n_s, m_s[:, :, 0])
    return x2, prompt_state, sample_state, rows_p, rows_s, mem_k, mem_v


def kernel(x_prompt, x_sample, state_mlstm_C, state_mlstm_n, state_mlstm_m,
           cache_win_k_g0, cache_win_v_g0, cache_win_k_g1, cache_win_v_g1,
           cache_win_k_g2, cache_win_v_g2, cache_mem_k, cache_mem_v, mem_prompt,
           g_mix, w_in, b_igate, b_fgate, b_branch, g_mem, w_mem_kv,
           w_br_a, w_br_b, w_br_c, w_out, g_mlp, w_up, w_down, g_final):
    depth = g_mix.shape[0]
    assert depth == 1, "single-layer stack"
    caches = ((cache_win_k_g0[0], cache_win_v_g0[0]),
              (cache_win_k_g1[0], cache_win_v_g1[0]),
              (cache_win_k_g2[0], cache_win_v_g2[0]))
    x2, p_state, s_state, rows_p, rows_s, mem_k, mem_v = _layer(
        x_prompt.reshape(N_PROMPT, D_MODEL), x_sample.reshape(N_SAMPLE, D_MODEL),
        mem_prompt.reshape(BATCH * MEM_LEN, D_MODEL),
        state_mlstm_C, state_mlstm_n, state_mlstm_m[0], caches, cache_mem_k[0], cache_mem_v[0],
        g_mix[0], w_in[0], b_igate[0], b_fgate[0], b_branch[0], g_mem[0], w_mem_kv[0],
        w_br_a[0], w_br_b[0], w_br_c[0], w_out[0], g_mlp[0], w_up[0], w_down[0])
    y_prompt = _rmsnorm(x2, g_final, F32, row_start=0, n_rows=N_PROMPT).reshape(BATCH, SEQ, D_MODEL)
    y_sample = _rmsnorm(x2, g_final, F32, row_start=N_PROMPT, n_rows=N_SAMPLE).reshape(DEC_BATCH, DEC_SEQ, D_MODEL)
    lead = lambda a: a[None]
    return (y_prompt, y_sample,
            lead(p_state[0]), lead(p_state[1]), lead(p_state[2]),
            *[lead(r) for r in rows_p],
            lead(mem_k), lead(mem_v),
            lead(s_state[0]), lead(s_state[1]), lead(s_state[2]),
            *[lead(r) for r in rows_s])
```

```python
import functools
from typing import Callable, NamedTuple

import numpy as np
import jax
import jax.numpy as jnp
from jax import lax
from jax.experimental import pallas as pl
from jax.experimental.pallas import tpu as pltpu

F32 = jnp.float32
BF16 = jnp.bfloat16
NEG_INF = float("-inf")

D_MODEL = 4096
BATCH = 2
SEQ = 4096
DEC_BATCH = 128
DEC_SEQ = 4
PAST_LEN = 2048
MEM_LEN = 256
M_HEADS = 8
M_DQK = D_MODEL // (2 * M_HEADS)
M_DV = D_MODEL // M_HEADS
A_HEAD_DIM = 128
A_HEADS = D_MODEL // 512
DIL_GROUPS = ((128, 1), (512, 4), (2048, 16))
N_GROUPS = 3
ROPE_THETA = 10000.0
X_HEADS = 4
X_HEAD_DIM = 128
D_FF = 4 * D_MODEL
EPS = 1e-6

W_MQK = M_HEADS * M_DQK
W_MV = M_HEADS * M_DV
W_AH = A_HEADS * A_HEAD_DIM
W_XQ = X_HEADS * X_HEAD_DIM
N_PROMPT = BATCH * SEQ
N_SAMPLE = DEC_BATCH * DEC_SEQ
N_ALL = N_PROMPT + N_SAMPLE

COL_GATES = 2 * W_MQK + 2 * W_MV
COL_REST = COL_GATES + 2 * M_HEADS
W_AQKV = 3 * N_GROUPS * W_AH

LANES = 128
SUBLANES = 8
V7X_VMEM_BYTES = 64 * 1024 * 1024
VMEM_LIMIT_BYTES = V7X_VMEM_BYTES - 4 * 1024 * 1024

MM_TM = 1088
MM_TN = 512
MM_TN_WIDE = 1024
MERGE_TM = 544
HOST_TM = 544
HOST_SPLIT_COLS = 256
DOWN_TM = 2176
DOWN_TK = 2048
NORM_ROWS = 256
CHUNK = 128
SAMPLE_CHUNK = 16
SAMPLE_SEQS_PER_BLOCK = 2
SPAN = 128


def _params(n_axes):
    return pltpu.CompilerParams(
        dimension_semantics=("arbitrary",) * n_axes,
        vmem_limit_bytes=VMEM_LIMIT_BYTES,
    )


def _rmsnorm_body(x_ref, g_ref, o_ref):
    x = x_ref[...]
    ms = jnp.mean(x * x, axis=-1, keepdims=True)
    o_ref[...] = ((x * lax.rsqrt(ms + EPS)) * g_ref[...]).astype(o_ref.dtype)


def _rmsnorm(x, g, out_dtype, *, row_start=0, n_rows=None):
    n_rows = x.shape[0] if n_rows is None else n_rows
    d = x.shape[1]
    off = row_start // NORM_ROWS
    return pl.pallas_call(
        _rmsnorm_body,
        grid=(n_rows // NORM_ROWS,),
        in_specs=[
            pl.BlockSpec((NORM_ROWS, d), lambda i: (i + off, 0)),
            pl.BlockSpec((1, d), lambda i: (0, 0)),
        ],
        out_specs=pl.BlockSpec((NORM_ROWS, d), lambda i: (i, 0)),
        out_shape=jax.ShapeDtypeStruct((n_rows, d), out_dtype),
        compiler_params=_params(1),
        name="rmsnorm",
    )(x, g.reshape(1, d))


def _rmsnorm_two_groups_body(xp_ref, xs_ref, g_ref, o_ref, *, prompt_steps):
    i = pl.program_id(0)

    @pl.when(i < prompt_steps)
    def _():
        _rmsnorm_body(xp_ref, g_ref, o_ref)

    @pl.when(i >= prompt_steps)
    def _():
        _rmsnorm_body(xs_ref, g_ref, o_ref)


def _rmsnorm_two_groups(xp, xs, g, out_dtype):
    d = xp.shape[1]
    ps, ss = xp.shape[0] // NORM_ROWS, xs.shape[0] // NORM_ROWS
    return pl.pallas_call(
        functools.partial(_rmsnorm_two_groups_body, prompt_steps=ps),
        grid=(ps + ss,),
        in_specs=[
            pl.BlockSpec((NORM_ROWS, d), lambda i: (jnp.minimum(i, ps - 1), 0)),
            pl.BlockSpec((NORM_ROWS, d), lambda i: (jnp.maximum(i - ps, 0), 0)),
            pl.BlockSpec((1, d), lambda i: (0, 0)),
        ],
        out_specs=pl.BlockSpec((NORM_ROWS, d), lambda i: (i, 0)),
        out_shape=jax.ShapeDtypeStruct((xp.shape[0] + xs.shape[0], d), out_dtype),
        compiler_params=_params(1),
        name="rmsnorm_two_groups",
    )(xp, xs, g.reshape(1, d))


def _ep_plain(acc, extra, o_ref, j):
    o_ref[...] = acc.astype(o_ref.dtype)


def _ep_residual_two_groups(acc, extra, o_ref, j):
    xp_ref, xs_ref = extra
    tm = acc.shape[0]
    n_full = N_PROMPT // tm
    rem = N_PROMPT - n_full * tm
    assert tm - rem == N_SAMPLE and rem % SUBLANES == 0
    i = pl.program_id(1)

    @pl.when(i < n_full)
    def _():
        o_ref[...] = xp_ref[...] + acc

    @pl.when(i >= n_full)
    def _():
        o_ref[0:rem, :] = xp_ref[tm - rem:tm, :] + acc[0:rem]
        o_ref[rem:tm, :] = xs_ref[...] + acc[rem:tm]


def _two_group_specs(tm, width, col_of_j, row_align, sample_block_is_constant):
    def p_map(j, i):
        row = jnp.minimum(i * tm, N_PROMPT - tm)
        col = col_of_j(j) * width
        return (pl.multiple_of(row, row_align), col if isinstance(col, int) else pl.multiple_of(col, LANES))
    p_spec = pl.BlockSpec((pl.Element(tm), pl.Element(width)), p_map)
    mode = pl.Buffered(1) if sample_block_is_constant else None
    s_spec = pl.BlockSpec((N_SAMPLE, width), lambda j, i: (0, col_of_j(j)), pipeline_mode=mode)
    return p_spec, s_spec


def _ep_relu2(acc, extra, o_ref, j):
    r = jnp.maximum(acc, 0.0)
    o_ref[...] = (r * r).astype(o_ref.dtype)


def _ep_rope(acc, extra, o_ref, j):
    cos_ref, sin_ref = extra
    tn = acc.shape[1]
    assert W_AH % tn == 0
    is_rope = (j // (W_AH // tn)) % 3 != 2
    cos2 = cos_ref[...]
    sin2 = sin_ref[...]
    for s in range(tn // A_HEAD_DIM):
        x = acc[:, s * A_HEAD_DIM:(s + 1) * A_HEAD_DIM]
        rotated = x * cos2 + pltpu.roll(x, A_HEAD_DIM // 2, axis=1) * sin2
        o_ref[:, s * A_HEAD_DIM:(s + 1) * A_HEAD_DIM] = jnp.where(is_rope, rotated, x)


def _mm_body(*refs, b_rows_are_outputs, epilogue, n_extra):
    a_ref, b_ref = refs[0], refs[1]
    extra = refs[2:2 + n_extra]
    o_ref = refs[2 + n_extra]
    b_scr = refs[3 + n_extra]

    @pl.when(pl.program_id(1) == 0)
    def _():
        b_scr[...] = b_ref[...].astype(BF16)

    if b_rows_are_outputs:
        acc = lax.dot_general(a_ref[...], b_scr[...], (((1,), (1,)), ((), ())),
                              preferred_element_type=F32)
    else:
        acc = jnp.dot(a_ref[...], b_scr[...], preferred_element_type=F32)
    epilogue(acc, extra, o_ref, pl.program_id(0))


def _matmul(a, b, *, n_cols, col_start=0, b_rows_are_outputs=False, tm=MM_TM, tn=MM_TN,
            out_dtype=F32, epilogue=_ep_plain, extras=(), extra_specs=(), name="matmul"):
    m, k = a.shape
    if b_rows_are_outputs:
        if col_start % tn == 0:
            b_spec = pl.BlockSpec((tn, k), lambda j, i: (j + col_start // tn, 0))
        else:
            assert col_start % SUBLANES == 0 and tn % SUBLANES == 0
            b_spec = pl.BlockSpec((pl.Element(tn), pl.Element(k)),
                                  lambda j, i: (pl.multiple_of(col_start + j * tn, SUBLANES), 0))
        b_block = (tn, k)
    else:
        assert col_start % tn == 0
        b_spec = pl.BlockSpec((k, tn), lambda j, i: (0, j + col_start // tn))
        b_block = (k, tn)
    return pl.pallas_call(
        functools.partial(_mm_body, b_rows_are_outputs=b_rows_are_outputs, epilogue=epilogue,
                          n_extra=len(extras)),
        grid=(n_cols // tn, m // tm),
        in_specs=[pl.BlockSpec((tm, k), lambda j, i: (i, 0)), b_spec, *extra_specs],
        out_specs=pl.BlockSpec((tm, tn), lambda j, i: (i, j)),
        out_shape=jax.ShapeDtypeStruct((m, n_cols), out_dtype),
        scratch_shapes=[pltpu.VMEM(b_block, BF16)],
        compiler_params=_params(2),
        name=name,
    )(a, b, *extras)


def _narrow_pair_body(a_ref, w0_ref, w1_ref, o0_ref, o1_ref, w0_scr, w1_scr):
    @pl.when(pl.program_id(0) == 0)
    def _():
        w0_scr[...] = w0_ref[...].astype(BF16)
        w1_scr[...] = w1_ref[...].astype(BF16)

    a = a_ref[...]
    nt = (((1,), (1,)), ((), ()))
    o0_ref[...] = lax.dot_general(a, w0_scr[...], nt, preferred_element_type=F32)
    o1_ref[...] = lax.dot_general(a, w1_scr[...], nt, preferred_element_type=F32)


def _narrow_projection_pair(a, w_t, start0, n0, start1, n1, *, tm=MM_TM, name="narrow_pair"):
    m, k = a.shape
    assert start0 % SUBLANES == 0 and start1 % SUBLANES == 0

    def w_spec(start, n):
        return pl.BlockSpec((pl.Element(n), pl.Element(k)), lambda i: (start, 0), pipeline_mode=pl.Buffered(1))

    return pl.pallas_call(
        _narrow_pair_body,
        grid=(m // tm,),
        in_specs=[pl.BlockSpec((tm, k), lambda i: (i, 0)), w_spec(start0, n0), w_spec(start1, n1)],
        out_specs=[pl.BlockSpec((tm, n0), lambda i: (i, 0)), pl.BlockSpec((tm, n1), lambda i: (i, 0))],
        out_shape=[jax.ShapeDtypeStruct((m, n0), F32), jax.ShapeDtypeStruct((m, n1), F32)],
        scratch_shapes=[pltpu.VMEM((n0, k), BF16), pltpu.VMEM((n1, k), BF16)],
        compiler_params=_params(1),
        name=name,
    )(a, w_t, w_t)


class _SideJob(NamedTuple):
    n_steps: int
    inputs: tuple
    in_specs: tuple
    out_shapes: tuple
    out_specs: tuple
    scratch_shapes: tuple
    init: Callable
    pieces: Callable


def _mm_stream_body(*refs, b_rows_are_outputs, epilogue, n_extra, slice_rows, split_cols, side):
    n_side_in = len(side.inputs) if side else 0
    n_side_out = len(side.out_shapes) if side else 0
    n_in = 2 + n_extra + n_side_in
    a_ref, bs_ref = refs[0], refs[1]
    extra = refs[2:2 + n_extra]
    side_in = refs[2 + n_extra:n_in]
    o_ref = refs[n_in]
    side_out = refs[n_in + 1:n_in + 1 + n_side_out]
    b_scr = refs[n_in + 1 + n_side_out]
    side_scr = refs[n_in + 2 + n_side_out:]
    p, i = pl.program_id(0), pl.program_id(1)

    r0 = pl.multiple_of(i * slice_rows, slice_rows)
    b_scr[p % 2, pl.ds(r0, slice_rows), :] = bs_ref[...].astype(BF16)

    if side is not None:
        @pl.when((p == 0) & (i == 0))
        def _():
            side.init(side_scr)

    def multiply(side_pieces):
        slot = (p - 1) % 2
        a = a_ref[...]
        tn = o_ref.shape[1]
        n_split = tn // split_cols
        per_split = -(-len(side_pieces) // n_split)
        for h in range(n_split):
            cols = pl.ds(h * split_cols, split_cols)
            if b_rows_are_outputs:
                acc = lax.dot_general(a, b_scr[slot, cols, :], (((1,), (1,)), ((), ())),
                                      preferred_element_type=F32)
            else:
                acc = jnp.dot(a, b_scr[slot, :, cols], preferred_element_type=F32)
            epilogue(acc, extra, o_ref.at[:, cols], (p - 1) * n_split + h)
            for piece in side_pieces[h * per_split:(h + 1) * per_split]:
                piece()

    if side is None:
        pl.when(p > 0)(lambda: multiply([]))
    else:
        t = (p - 1) * pl.num_programs(1) + i
        pl.when((p > 0) & (t < side.n_steps))(
            lambda: multiply(side.pieces(t, side_in, side_out, side_scr)))
        pl.when((p > 0) & (t >= side.n_steps))(lambda: multiply([]))


def _matmul_stream(a, b, *, n_cols, col_start=0, b_rows_are_outputs=False, tm=MM_TM, tn=MM_TN_WIDE,
                   out_dtype=F32, epilogue=_ep_plain, extras=(), extra_specs=(), side=None,
                   split_cols=MM_TN, name="matmul"):
    m, k = a.shape
    ni, nj = m // tm, n_cols // tn
    assert col_start % SUBLANES == 0 and nj * tn == n_cols

    def side_spec(spec):
        if isinstance(spec, pl.BlockSpec):
            return spec
        block_shape, index_fn = spec
        return pl.BlockSpec(
            block_shape,
            lambda p, i: index_fn(jnp.clip((p - 1) * ni + i, 0, side.n_steps - 1)))

    side_in_specs = [side_spec(s) for s in side.in_specs] if side else []
    side_out_specs = [side_spec(s) for s in side.out_specs] if side else []
    assert side is None or side.n_steps <= nj * ni

    def staged(p):
        return jnp.minimum(p, nj - 1)

    def shifted(index_map):
        return lambda p, i: index_map(jnp.maximum(p - 1, 0), jnp.where(p == 0, 0, i))

    if b_rows_are_outputs:
        tile = (tn, k)
        slice_rows = tn // ni
        bs_spec = pl.BlockSpec(
            (pl.Element(slice_rows), pl.Element(k)),
            lambda p, i: (pl.multiple_of(col_start + staged(p) * tn + i * slice_rows, SUBLANES), 0))
    else:
        assert col_start % tn == 0
        tile = (k, tn)
        slice_rows = k // ni
        bs_spec = pl.BlockSpec((slice_rows, tn), lambda p, i: (i, col_start // tn + staged(p)))
    assert slice_rows * ni == tile[0] and slice_rows % 16 == 0
    extra_specs = [pl.BlockSpec(s.block_shape, shifted(s.index_map), pipeline_mode=s.pipeline_mode)
                   for s in extra_specs]
    outs = pl.pallas_call(
        functools.partial(_mm_stream_body, b_rows_are_outputs=b_rows_are_outputs, epilogue=epilogue,
                          n_extra=len(extras), slice_rows=slice_rows, split_cols=min(tn, split_cols), side=side),
        grid=(nj + 1, ni),
        in_specs=[pl.BlockSpec((tm, k), shifted(lambda j, i: (i, 0))), bs_spec, *extra_specs,
                  *side_in_specs],
        out_specs=[pl.BlockSpec((tm, tn), shifted(lambda j, i: (i, j))), *side_out_specs],
        out_shape=[jax.ShapeDtypeStruct((m, n_cols), out_dtype), *(side.out_shapes if side else ())],
        scratch_shapes=[pltpu.VMEM((2,) + tile, BF16), *(side.scratch_shapes if side else ())],
        compiler_params=_params(2),
        name=name,
    )(a, b, *extras, *(side.inputs if side else ()))
    return outs if side else outs[0]


def _mm_ksplit_body(a_ref, b_ref, res_ref, o_ref, acc_ref):
    kk = pl.program_id(2)
    last = pl.num_programs(2) - 1

    def product():
        return jnp.dot(a_ref[...], b_ref[...].astype(BF16), preferred_element_type=F32)

    @pl.when(kk == 0)
    def _():
        acc_ref[...] = product()

    @pl.when((kk > 0) & (kk < last))
    def _():
        acc_ref[...] += product()

    @pl.when(kk == last)
    def _():
        o_ref[...] = res_ref[...] + (acc_ref[...] + product())


def _matmul_ksplit_residual(a, b, res, *, tm=DOWN_TM, tn=MM_TN, tk=DOWN_TK):
    m, k = a.shape
    n = b.shape[1]
    return pl.pallas_call(
        _mm_ksplit_body,
        grid=(n // tn, m // tm, k // tk),
        in_specs=[
            pl.BlockSpec((tm, tk), lambda j, i, kk: (i, kk)),
            pl.BlockSpec((tk, tn), lambda j, i, kk: (kk, j)),
            pl.BlockSpec((tm, tn), lambda j, i, kk: (i, j)),
        ],
        out_specs=pl.BlockSpec((tm, tn), lambda j, i, kk: (i, j)),
        out_shape=jax.ShapeDtypeStruct((m, n), F32),
        scratch_shapes=[pltpu.VMEM((tm, tn), F32)],
        compiler_params=_params(3),
        name="mlp_down",
    )(a, b, res)


def _sigmoid(x):
    return 1.0 / (1.0 + jnp.exp(-x))


def _straddle_rows(p_ref, s_ref):
    tm = p_ref.shape[0]
    rem = tm - s_ref.shape[0]
    return jnp.concatenate([p_ref[tm - rem:tm, :], s_ref[...]], axis=0)


def _merge_body(ap_ref, as_ref, ybp_ref, ybs_ref, ycp_ref, ycs_ref,
                wa0_ref, wb0_ref, wc0_ref, was_ref, wbs_ref, wcs_ref,
                ga_ref, gb_ref, gc_ref, ba_ref, bb_ref, bc_ref,
                o_ref, wa_scr, wb_scr, wc_scr):
    j, i = pl.program_id(0), pl.program_id(1)
    cur = j % 2

    @pl.when((j == 0) & (i == 0))
    def _():
        wa_scr[0] = wa0_ref[...].astype(BF16)
        wb_scr[0] = wb0_ref[...].astype(BF16)
        wc_scr[0] = wc0_ref[...].astype(BF16)

    for scr, slice_ref in ((wa_scr, was_ref), (wb_scr, wbs_ref), (wc_scr, wcs_ref)):
        rows = slice_ref.shape[0]
        scr[1 - cur, pl.ds(pl.multiple_of(i * rows, rows), rows), :] = slice_ref[...].astype(BF16)

    def merge(a, yb, yc):
        for h in range(o_ref.shape[1] // HOST_SPLIT_COLS):
            cols = pl.ds(h * HOST_SPLIT_COLS, HOST_SPLIT_COLS)
            y_a = jnp.dot(a, wa_scr[cur, :, cols], preferred_element_type=F32)
            y_b = jnp.dot(yb, wb_scr[cur, :, cols], preferred_element_type=F32)
            y_c = jnp.dot(yc, wc_scr[cur, :, cols], preferred_element_type=F32)
            g_a = _sigmoid(ga_ref[:, cols] + ba_ref[:, cols])
            g_b = _sigmoid(gb_ref[:, cols] + bb_ref[:, cols])
            g_c = _sigmoid(gc_ref[:, cols] + bc_ref[:, cols])
            o_ref[:, cols] = (g_a * y_a + g_b * y_b + g_c * y_c).astype(o_ref.dtype)

    is_prompt_tile = i < N_PROMPT // ap_ref.shape[0]

    @pl.when(is_prompt_tile)
    def _():
        merge(ap_ref[...], ybp_ref[...], ycp_ref[...])

    @pl.when(jnp.logical_not(is_prompt_tile))
    def _():
        merge(_straddle_rows(ap_ref, as_ref), _straddle_rows(ybp_ref, ybs_ref), _straddle_rows(ycp_ref, ycs_ref))


def _branch_merge(a_ps, yb_ps, yc_ps, w_a, w_b, w_c, z_br, b_branch, *, tm=MERGE_TM, tn=MM_TN):
    m = N_ALL
    assert (N_PROMPT // tm + 1) * tm == N_ALL
    nj = D_MODEL // tn

    def gate_spec(part):
        return pl.BlockSpec((tm, tn), lambda j, i: (i, part * nj + j))

    def bias_spec(part):
        return pl.BlockSpec((1, tn), lambda j, i: (0, part * nj + j))

    ni = m // tm
    widths = (W_MV, W_AH, W_XQ)
    assert all(k % (16 * ni) == 0 for k in widths)

    def first_tile_spec(k):
        return pl.BlockSpec((k, tn), lambda j, i: (0, 0), pipeline_mode=pl.Buffered(1))

    def next_slice_spec(k):
        return pl.BlockSpec((k // ni, tn), lambda j, i: (i, jnp.minimum(j + 1, nj - 1)))

    bf16_rows = 16
    return pl.pallas_call(
        _merge_body,
        grid=(nj, ni),
        in_specs=[
            *_two_group_specs(tm, W_MV, lambda j: 0, bf16_rows, True),
            *_two_group_specs(tm, W_AH, lambda j: 0, bf16_rows, True),
            *_two_group_specs(tm, W_XQ, lambda j: 0, bf16_rows, True),
            *[first_tile_spec(k) for k in widths],
            *[next_slice_spec(k) for k in widths],
            gate_spec(0), gate_spec(1), gate_spec(2),
            bias_spec(0), bias_spec(1), bias_spec(2),
        ],
        out_specs=pl.BlockSpec((tm, tn), lambda j, i: (i, j)),
        out_shape=jax.ShapeDtypeStruct((m, D_MODEL), BF16),
        scratch_shapes=[pltpu.VMEM((2, k, tn), BF16) for k in widths],
        compiler_params=_params(2),
        name="branch_merge",
    )(*a_ps, *yb_ps, *yc_ps, w_a, w_b, w_c, w_a, w_b, w_c,
      z_br, z_br, z_br, b_branch, b_branch, b_branch)


def _log_sigmoid(x):
    return jnp.minimum(x, 0.0) - jnp.log1p(jnp.exp(-jnp.abs(x)))


def _mlstm_chunk(q, k, v, irow, frow, c_state, n_state, m_state, n_valid):
    L = q.shape[0]
    ti = lax.broadcasted_iota(jnp.int32, (L, L), 0)
    si = lax.broadcasted_iota(jnp.int32, (L, L), 1)
    causal = si <= ti
    eye = si == ti
    f_b = jnp.broadcast_to(frow, (L, L))
    i_b = jnp.broadcast_to(irow, (L, L))
    bcol = jnp.sum(jnp.where(causal, f_b, 0.0), axis=1, keepdims=True)
    fcol = jnp.sum(jnp.where(eye, f_b, 0.0), axis=1, keepdims=True)
    icol = jnp.sum(jnp.where(eye, i_b, 0.0), axis=1, keepdims=True)
    brow = jnp.sum(jnp.where(ti <= si, jnp.broadcast_to(fcol, (L, L)), 0.0), axis=0, keepdims=True)

    acol = bcol + m_state
    logw = jnp.where(causal, bcol - brow + irow, NEG_INF)
    mt = jnp.maximum(acol, jnp.max(logw, axis=1, keepdims=True))
    w_inter = jnp.exp(acol - mt)
    w_intra = jnp.exp(logw - mt)

    qb = q.astype(BF16)
    kb = k.astype(BF16)
    nt = (((1,), (1,)), ((), ()))
    s = lax.dot_general(qb, kb, nt, preferred_element_type=F32) * w_intra
    inter = lax.dot_general(qb, c_state.astype(BF16), nt, preferred_element_type=F32)
    num = w_inter * inter + jnp.dot(s.astype(BF16), v.astype(BF16), preferred_element_type=F32)
    nq = w_inter * jnp.sum(q * n_state, axis=1, keepdims=True) + jnp.sum(s, axis=1, keepdims=True)
    h = num / jnp.maximum(jnp.abs(nq), jnp.exp(-mt))

    last = slice(n_valid - 1, n_valid)
    m_end = mt[last, :]
    w_c = jnp.exp(acol[last, :] - m_end)
    w_s = jnp.exp(bcol[last, :] - bcol + icol - m_end)
    if n_valid < L:
        w_s = jnp.where(lax.broadcasted_iota(jnp.int32, (L, 1), 0) < n_valid, w_s, 0.0)
    tn = (((0,), (0,)), ((), ()))
    c_new = w_c * c_state + lax.dot_general((v * w_s).astype(BF16), kb, tn, preferred_element_type=F32)
    n_new = w_c * n_state + jnp.sum(w_s * k, axis=0, keepdims=True)
    return h, c_new, n_new, m_end


def _gate_rows(g_ref_val, bias_ref, head):
    irow = g_ref_val[0:1, :] + bias_ref[0, head]
    frow = _log_sigmoid(g_ref_val[1:2, :] + bias_ref[1, head])
    return irow, frow


def _mlstm_prompt_body(bias_ref, q_ref, k_ref, v_ref, mo_ref, g_ref,
                       a_ref, c_ref, n_ref, m_ref):
    @pl.when(pl.program_id(1) == 0)
    def _():
        c_ref[...] = jnp.zeros_like(c_ref)
        n_ref[...] = jnp.zeros_like(n_ref)
        m_ref[...] = jnp.zeros_like(m_ref)

    for head in range(M_HEADS):
        qk = pl.ds(head * M_DQK, M_DQK)
        vo = pl.ds(head * M_DV, M_DV)
        one = pl.ds(head, 1)
        irow, frow = _gate_rows(g_ref[head], bias_ref, head)
        h, c_new, n_new, m_end = _mlstm_chunk(
            q_ref[:, qk], k_ref[:, qk] * (M_DQK ** -0.5), v_ref[:, vo], irow, frow,
            c_ref[0, head], n_ref[0, one, :], m_ref[0, one, 0:1], CHUNK)
        a_ref[:, vo] = (_sigmoid(mo_ref[:, vo]) * h).astype(a_ref.dtype)
        c_ref[0, head] = c_new
        n_ref[0, one, :] = n_new
        m_ref[0, one, :] = jnp.broadcast_to(m_end, (1, LANES))


def _mlstm_prompt(z_main, gates_rows, gate_bias):
    nc = SEQ // CHUNK
    return pl.pallas_call(
        _mlstm_prompt_body,
        grid=(BATCH, nc),
        in_specs=[
            pl.BlockSpec(memory_space=pltpu.SMEM),
            pl.BlockSpec((CHUNK, W_MQK), lambda b, c: (b * nc + c, 0)),
            pl.BlockSpec((CHUNK, W_MQK), lambda b, c: (b * nc + c, 1)),
            pl.BlockSpec((CHUNK, W_MV), lambda b, c: (b * nc + c, 2 * W_MQK // W_MV)),
            pl.BlockSpec((CHUNK, W_MV), lambda b, c: (b * nc + c, 2 * W_MQK // W_MV + 1)),
            pl.BlockSpec((M_HEADS, 2, CHUNK), lambda b, c: (0, 0, b * nc + c)),
        ],
        out_specs=[
            pl.BlockSpec((CHUNK, W_MV), lambda b, c: (b * nc + c, 0)),
            pl.BlockSpec((1, M_HEADS, M_DV, M_DQK), lambda b, c: (b, 0, 0, 0)),
            pl.BlockSpec((1, M_HEADS, M_DQK), lambda b, c: (b, 0, 0)),
            pl.BlockSpec((1, M_HEADS, LANES), lambda b, c: (b, 0, 0)),
        ],
        out_shape=[
            jax.ShapeDtypeStruct((N_PROMPT, W_MV), BF16),
            jax.ShapeDtypeStruct((BATCH, M_HEADS, M_DV, M_DQK), F32),
            jax.ShapeDtypeStruct((BATCH, M_HEADS, M_DQK), F32),
            jax.ShapeDtypeStruct((BATCH, M_HEADS, LANES), F32),
        ],
        compiler_params=_params(2),
        name="mlstm_prompt",
    )(gate_bias, z_main, z_main, z_main, z_main, gates_rows)


def _mlstm_sample_init(scratch):
    for scr in scratch:
        scr[...] = jnp.zeros_like(scr)


def _mlstm_sample_pieces(t, ins, outs, scratch):
    bias_ref, q_ref, k_ref, v_ref, mo_ref, g_ref, c0_ref, n0_ref, m0_ref = ins
    a_ref, c_ref, n_ref, m_ref = outs
    q_scr, k_scr, v_scr = scratch
    first_half = (t % SAMPLE_SEQS_PER_BLOCK) == 0

    def seq_rows(ref, cols):
        return jnp.where(first_half, ref[0:DEC_SEQ, cols], ref[DEC_SEQ:2 * DEC_SEQ, cols])

    def one_head(head):
        qk = pl.ds(head * M_DQK, M_DQK)
        vo = pl.ds(head * M_DV, M_DV)
        one = pl.ds(head, 1)
        q_scr[0:DEC_SEQ, qk] = seq_rows(q_ref, qk)
        k_scr[0:DEC_SEQ, qk] = seq_rows(k_ref, qk) * (M_DQK ** -0.5)
        v_scr[0:DEC_SEQ, vo] = seq_rows(v_ref, vo)
        irow, frow = _gate_rows(g_ref[0, head], bias_ref, head)
        h, c_new, n_new, m_end = _mlstm_chunk(
            q_scr[:, qk], k_scr[:, qk], v_scr[:, vo], irow, frow,
            c0_ref[0, 0, head], n0_ref[0, 0, one, :], m0_ref[0, one, :], DEC_SEQ)
        a_ref[0, :, vo] = _sigmoid(seq_rows(mo_ref, vo)) * h[0:DEC_SEQ, :]
        c_ref[0, head] = c_new
        n_ref[0, one, :] = n_new
        m_ref[0, one, :] = jnp.broadcast_to(m_end, (1, LANES))

    def all_heads():
        for head in range(M_HEADS):
            one_head(head)

    return [all_heads]


def _mlstm_sample_job(z_main, gates_rows, gate_bias, c0, n0, m0):
    per = SAMPLE_SEQS_PER_BLOCK
    rows = per * DEC_SEQ
    assert rows == SUBLANES and N_PROMPT % rows == 0
    r0 = N_PROMPT // rows
    v_blk = 2 * W_MQK // W_MV
    return _SideJob(
        n_steps=DEC_BATCH,
        inputs=(gate_bias, z_main, z_main, z_main, z_main, gates_rows, c0, n0, m0),
        in_specs=(
            pl.BlockSpec(memory_space=pltpu.SMEM),
            ((rows, W_MQK), lambda t: (r0 + t // per, 0)),
            ((rows, W_MQK), lambda t: (r0 + t // per, 1)),
            ((rows, W_MV), lambda t: (r0 + t // per, v_blk)),
            ((rows, W_MV), lambda t: (r0 + t // per, v_blk + 1)),
            ((1, M_HEADS, 2, SAMPLE_CHUNK), lambda t: (t, 0, 0, 0)),
            ((1, 1, M_HEADS, M_DV, M_DQK), lambda t: (0, t, 0, 0, 0)),
            ((1, 1, M_HEADS, M_DQK), lambda t: (0, t, 0, 0)),
            ((1, M_HEADS, 1), lambda t: (t, 0, 0)),
        ),
        out_shapes=(
            jax.ShapeDtypeStruct((DEC_BATCH, DEC_SEQ, W_MV), F32),
            jax.ShapeDtypeStruct((DEC_BATCH, M_HEADS, M_DV, M_DQK), F32),
            jax.ShapeDtypeStruct((DEC_BATCH, M_HEADS, M_DQK), F32),
            jax.ShapeDtypeStruct((DEC_BATCH, M_HEADS, LANES), F32),
        ),
        out_specs=(
            ((1, DEC_SEQ, W_MV), lambda t: (t, 0, 0)),
            ((1, M_HEADS, M_DV, M_DQK), lambda t: (t, 0, 0, 0)),
            ((1, M_HEADS, M_DQK), lambda t: (t, 0, 0)),
            ((1, M_HEADS, LANES), lambda t: (t, 0, 0)),
        ),
        scratch_shapes=(pltpu.VMEM((SAMPLE_CHUNK, W_MQK), F32), pltpu.VMEM((SAMPLE_CHUNK, W_MQK), F32),
                        pltpu.VMEM((SAMPLE_CHUNK, W_MV), F32)),
        init=_mlstm_sample_init,
        pieces=_mlstm_sample_pieces,
    )


def _dil_prompt_combine_body(q_ref, kp_ref, kc_ref, vp_ref, vc_ref, o0_ref, l0_ref, o1_ref, l1_ref,
                             y_ref, o_scr, l_scr, *, r, hb):
    _dil_prompt_body(q_ref, kp_ref, kc_ref, vp_ref, vc_ref, o_scr, l_scr, r=r, hb=hb)
    _combine_body(o0_ref, o1_ref, o_scr, l0_ref, l1_ref, l_scr, y_ref)


def _dil_prompt_heads_body(*refs, r, hb):
    ins = refs[:5 * hb]
    o_ref, l_ref = refs[5 * hb], refs[5 * hb + 1]
    scr = refs[5 * hb + 2:]
    for hh in range(hb):
        o_scr, l_scr = scr[2 * hh], scr[2 * hh + 1]
        _dil_prompt_body(*ins[5 * hh:5 * hh + 5], o_scr, l_scr, r=r, hb=1)
        cols = pl.ds(hh * A_HEAD_DIM, A_HEAD_DIM)
        o_ref[:, cols] = o_scr[...]
        l_ref[:, cols] = l_scr[...]


def _dil_prompt_body(q_ref, kp_ref, kc_ref, vp_ref, vc_ref, o_ref, l_ref, *, r, hb):
    first_key = jnp.where(pl.program_id(2) > 0, 0, SPAN)
    qi = lax.broadcasted_iota(jnp.int32, (SPAN, 2 * SPAN), 0)
    ki = lax.broadcasted_iota(jnp.int32, (SPAN, 2 * SPAN), 1)
    ok = (ki >= qi) & (ki <= qi + SPAN) & (ki >= first_key)
    bias = jnp.where(ok, 0.0, NEG_INF)
    scale = A_HEAD_DIM ** -0.5
    nt = (((1,), (1,)), ((), ()))
    for c in range(r):
        rows = pl.ds(c, SPAN, stride=r) if r > 1 else pl.ds(0, SPAN)
        for hh in range(hb):
            cols = pl.ds(hh * A_HEAD_DIM, A_HEAD_DIM)
            q = q_ref[rows, cols].astype(BF16)
            kk = jnp.concatenate([kp_ref[rows, cols], kc_ref[rows, cols]], axis=0).astype(BF16)
            vv = jnp.concatenate([vp_ref[rows, cols], vc_ref[rows, cols]], axis=0).astype(BF16)
            s = lax.dot_general(q, kk, nt, preferred_element_type=F32) * scale + bias
            m = jnp.max(s, axis=1, keepdims=True)
            p = jnp.exp(s - m)
            den = jnp.sum(p, axis=1, keepdims=True)
            o = jnp.dot(p.astype(BF16), vv, preferred_element_type=F32) / den
            o_ref[rows, cols] = o
            l_ref[rows, cols] = jnp.broadcast_to(m + jnp.log(den), (SPAN, A_HEAD_DIM))


def _dilated_prompt(z_aqkv, g, r, hb, combine_with=None):
    rows = SPAN * r
    nblk = SEQ // rows
    wcol = hb * A_HEAD_DIM
    per_part = W_AH // wcol

    def spec(part, prev):
        def imap(b, hg, n):
            nn = jnp.maximum(n - 1, 0) if prev else n
            return (b * nblk + nn, (3 * g + part) * per_part + hg)
        return pl.BlockSpec((rows, wcol), imap)

    out_spec = pl.BlockSpec((rows, wcol), lambda b, hg, n: (b * nblk + n, hg))
    qkv_specs = [spec(0, False), spec(1, True), spec(1, False), spec(2, True), spec(2, False)]
    if r > 1 and hb > 1:
        assert combine_with is None

        def head_spec(part, prev, hh):
            def imap(b, hg, n):
                nn = jnp.maximum(n - 1, 0) if prev else n
                return (b * nblk + nn, (3 * g + part) * A_HEADS + hg * hb + hh)
            return pl.BlockSpec((rows, A_HEAD_DIM), imap)

        head_specs = [head_spec(part, prev, hh) for hh in range(hb)
                      for part, prev in ((0, False), (1, True), (1, False), (2, True), (2, False))]
        return pl.pallas_call(
            functools.partial(_dil_prompt_heads_body, r=r, hb=hb),
            grid=(BATCH, A_HEADS // hb, nblk),
            in_specs=head_specs,
            out_specs=[out_spec, out_spec],
            out_shape=[jax.ShapeDtypeStruct((N_PROMPT, W_AH), F32)] * 2,
            scratch_shapes=[pltpu.VMEM((rows, A_HEAD_DIM), F32)] * (2 * hb),
            compiler_params=_params(3),
            name=f"dilated_prompt_g{g}",
        )(*([z_aqkv] * (5 * hb)))
    if combine_with is None:
        return pl.pallas_call(
            functools.partial(_dil_prompt_body, r=r, hb=hb),
            grid=(BATCH, A_HEADS // hb, nblk),
            in_specs=qkv_specs,
            out_specs=[out_spec, out_spec],
            out_shape=[jax.ShapeDtypeStruct((N_PROMPT, W_AH), F32)] * 2,
            compiler_params=_params(3),
            name=f"dilated_prompt_g{g}",
        )(z_aqkv, z_aqkv, z_aqkv, z_aqkv, z_aqkv)
    return pl.pallas_call(
        functools.partial(_dil_prompt_combine_body, r=r, hb=hb),
        grid=(BATCH, A_HEADS // hb, nblk),
        in_specs=qkv_specs + [out_spec] * 4,
        out_specs=out_spec,
        out_shape=jax.ShapeDtypeStruct((N_PROMPT, W_AH), BF16),
        scratch_shapes=[pltpu.VMEM((rows, wcol), F32)] * 2,
        compiler_params=_params(3),
        name=f"dilated_prompt_g{g}_combine",
    )(z_aqkv, z_aqkv, z_aqkv, z_aqkv, z_aqkv, *combine_with)


def _sample_attn_pieces(t, ins, out_refs, scratch):
    (qkv_ref, k0_ref, v0_ref, k1_ref, v1_ref, k2_ref, v2_ref, b0_ref, b1_ref, b2_ref,
     xq_ref, mk_ref, mv_ref, bx_ref) = ins
    yb_ref, yc_ref = out_refs
    nt = (((1,), (1,)), ((), ()))
    outs, lses = [], []
    groups = ((k0_ref, v0_ref, b0_ref), (k1_ref, v1_ref, b1_ref), (k2_ref, v2_ref, b2_ref))

    def cross():
        s = lax.dot_general(xq_ref[0].astype(BF16), mk_ref[0].astype(BF16), nt,
                            preferred_element_type=F32) * (X_HEAD_DIM ** -0.5) + bx_ref[...]
        m = jnp.max(s, axis=1, keepdims=True)
        ex = jnp.exp(s - m)
        p = ex / jnp.sum(ex, axis=1, keepdims=True)
        yc_ref[0] = jnp.dot(p.astype(BF16), mv_ref[0].astype(BF16),
                            preferred_element_type=F32).astype(yc_ref.dtype)

    def group(g):
        kc_ref, vc_ref, bias_ref = groups[g]
        n_cached = kc_ref.shape[1] * kc_ref.shape[2]
        q = qkv_ref[0, 3 * g].astype(BF16)
        kk = jnp.concatenate([kc_ref[0].reshape(n_cached, A_HEAD_DIM), qkv_ref[0, 3 * g + 1]],
                             axis=0).astype(BF16)
        vv = jnp.concatenate([vc_ref[0].reshape(n_cached, A_HEAD_DIM), qkv_ref[0, 3 * g + 2]],
                             axis=0).astype(BF16)
        s = lax.dot_general(q, kk, nt, preferred_element_type=F32) * (A_HEAD_DIM ** -0.5) + bias_ref[...]
        m = jnp.max(s, axis=1, keepdims=True)
        p = jnp.exp(s - m)
        den = jnp.sum(p, axis=1, keepdims=True)
        outs.append(jnp.dot(p.astype(BF16), vv, preferred_element_type=F32) / den)
        lses.append(m + jnp.log(den))
        if g < N_GROUPS - 1:
            return
        mx = jnp.maximum(jnp.maximum(lses[0], lses[1]), lses[2])
        e = [jnp.exp(l - mx) for l in lses]
        tot = e[0] + e[1] + e[2]
        yb = (e[0] / tot) * outs[0] + (e[1] / tot) * outs[1] + (e[2] / tot) * outs[2]
        yb_ref[0] = yb.astype(yb_ref.dtype)

    return [cross] + [functools.partial(group, g) for g in range(N_GROUPS)]


def _dilated_sample_bias(window, r, lb, rc):
    span = window // r
    n_c = (lb // r) * rc * A_HEADS
    rows = np.arange(DEC_SEQ * A_HEADS)
    s_q, h_q = rows // A_HEADS, rows % A_HEADS
    col = np.arange(n_c)
    m_k = col // (rc * A_HEADS)
    c_k = (col % (rc * A_HEADS)) // A_HEADS
    h_k = col % A_HEADS
    delta = (lb + s_q)[:, None] - (m_k * r + c_k)[None, :]
    ok_c = (h_q[:, None] == h_k[None, :]) & (delta % r == 0) & (delta // r <= span) & (delta >= 0)
    coln = np.arange(DEC_SEQ * A_HEADS)
    s_n, h_n = coln // A_HEADS, coln % A_HEADS
    dn = s_q[:, None] - s_n[None, :]
    ok_n = (h_q[:, None] == h_n[None, :]) & (dn >= 0) & (dn % r == 0) & (dn // r <= span)
    ok = np.concatenate([ok_c, ok_n], axis=1)
    return np.where(ok, 0.0, -np.inf).astype(np.float32)


def _sample_attention_job(qkv, caches, xq, mem_k, mem_v):
    rq = DEC_SEQ * A_HEADS
    rx = DEC_SEQ * X_HEADS
    nk = MEM_LEN * X_HEADS
    cache_args, cache_specs, biases = [], [], []
    for (window, r), (cache_k, cache_v) in zip(DIL_GROUPS, caches):
        lb = cache_k.shape[1]
        assert lb % r == 0 and window % r == 0
        rc = min(r, DEC_SEQ)
        nm = lb // r
        spec = ((1, nm, rc * A_HEADS, A_HEAD_DIM), lambda t: (t, 0, 0, 0))
        for c in (cache_k, cache_v):
            cache_args.append(c.reshape(DEC_BATCH, nm, r * A_HEADS, A_HEAD_DIM))
            cache_specs.append(spec)
        biases.append(jnp.asarray(_dilated_sample_bias(window, r, lb, rc)))
    ok = (np.arange(rx) % X_HEADS)[:, None] == (np.arange(nk) % X_HEADS)[None, :]
    bias_x = jnp.asarray(np.where(ok, 0.0, -np.inf).astype(np.float32))

    def const_spec(a):
        return pl.BlockSpec(a.shape, lambda p, i: (0, 0), pipeline_mode=pl.Buffered(1))

    return _SideJob(
        n_steps=DEC_BATCH,
        inputs=(qkv, *cache_args, *biases, xq, mem_k, mem_v, bias_x),
        in_specs=(
            ((1, 3 * N_GROUPS, rq, A_HEAD_DIM), lambda t: (t, 0, 0, 0)),
            *cache_specs,
            *[const_spec(a) for a in biases],
            ((1, rx, X_HEAD_DIM), lambda t: (t, 0, 0)),
            ((1, nk, X_HEAD_DIM), lambda t: (t, 0, 0)),
            ((1, nk, X_HEAD_DIM), lambda t: (t, 0, 0)),
            const_spec(bias_x),
        ),
        out_shapes=(jax.ShapeDtypeStruct((DEC_BATCH, rq, A_HEAD_DIM), BF16),
                    jax.ShapeDtypeStruct((DEC_BATCH, rx, X_HEAD_DIM), BF16)),
        out_specs=(((1, rq, A_HEAD_DIM), lambda t: (t, 0, 0)),
                   ((1, rx, X_HEAD_DIM), lambda t: (t, 0, 0))),
        scratch_shapes=(),
        init=lambda scratch: None,
        pieces=_sample_attn_pieces,
    )


def _cross_prompt_body(q_ref, k_ref, v_ref, o_ref):
    scale = X_HEAD_DIM ** -0.5
    nt = (((1,), (1,)), ((), ()))
    for h in range(X_HEADS):
        cols = pl.ds(h * X_HEAD_DIM, X_HEAD_DIM)
        s = lax.dot_general(q_ref[:, cols].astype(BF16), k_ref[:, cols].astype(BF16), nt,
                            preferred_element_type=F32) * scale
        m = jnp.max(s, axis=1, keepdims=True)
        e = jnp.exp(s - m)
        p = e / jnp.sum(e, axis=1, keepdims=True)
        o_ref[:, cols] = jnp.dot(p.astype(BF16), v_ref[:, cols].astype(BF16),
                                 preferred_element_type=F32).astype(o_ref.dtype)


def _cross_prompt(z_xq, mem_kv, *, tq=512):
    nq = SEQ // tq
    return pl.pallas_call(
        _cross_prompt_body,
        grid=(BATCH, nq),
        in_specs=[
            pl.BlockSpec((tq, W_XQ), lambda b, i: (b * nq + i, 0)),
            pl.BlockSpec((MEM_LEN, W_XQ), lambda b, i: (b, 0)),
            pl.BlockSpec((MEM_LEN, W_XQ), lambda b, i: (b, 1)),
        ],
        out_specs=pl.BlockSpec((tq, W_XQ), lambda b, i: (b * nq + i, 0)),
        out_shape=jax.ShapeDtypeStruct((N_PROMPT, W_XQ), BF16),
        compiler_params=_params(2),
        name="cross_prompt",
    )(z_xq, mem_kv, mem_kv)


def _combine_body(o0, o1, o2, l0, l1, l2, y_ref):
    a0, a1, a2 = l0[...], l1[...], l2[...]
    mx = jnp.maximum(jnp.maximum(a0, a1), a2)
    e0, e1, e2 = jnp.exp(a0 - mx), jnp.exp(a1 - mx), jnp.exp(a2 - mx)
    tot = e0 + e1 + e2
    y = (e0 / tot) * o0[...] + (e1 / tot) * o1[...] + (e2 / tot) * o2[...]
    y_ref[...] = y.astype(y_ref.dtype)


def _rope_tables():
    pos = jnp.concatenate([
        jnp.tile(jnp.arange(SEQ, dtype=jnp.int32), BATCH),
        jnp.tile(PAST_LEN + jnp.arange(DEC_SEQ, dtype=jnp.int32), DEC_BATCH)])
    inv = ROPE_THETA ** (-jnp.arange(0, A_HEAD_DIM, 2, dtype=F32) / A_HEAD_DIM)
    ang = pos.astype(F32)[:, None] * inv[None, :]
    cos, sin = jnp.cos(ang), jnp.sin(ang)
    return jnp.concatenate([cos, cos], axis=1), jnp.concatenate([-sin, sin], axis=1)


def _layer(x_p, x_s, mem_prompt, state_c, state_n, state_m, caches, cache_mem_k, cache_mem_v,
           g_mix, w_in, b_igate, b_fgate, b_branch, g_mem, w_mem_kv,
           w_br_a, w_br_b, w_br_c, w_out, g_mlp, w_up, w_down):
    h_all = _rmsnorm_two_groups(x_p, x_s, g_mix, BF16)

    w_in_t = w_in.T
    z_main = _matmul_stream(h_all, w_in_t, n_cols=COL_GATES, b_rows_are_outputs=True, name="proj_main")
    z_gate, z_xq = _narrow_projection_pair(h_all, w_in_t, COL_GATES, 2 * M_HEADS, COL_REST + W_AQKV, W_XQ,
                                           name="proj_gates_cross_q")
    gate_bias = jnp.stack([b_igate, b_fgate]).astype(F32)

    gp = z_gate[:N_PROMPT].reshape(N_PROMPT, 2, M_HEADS).transpose(2, 1, 0)
    a_p, c_p, n_p, m_p = _mlstm_prompt(z_main, gp, gate_bias)
    gs = z_gate[N_PROMPT:].reshape(DEC_BATCH, DEC_SEQ, 2, M_HEADS).transpose(0, 3, 2, 1)
    gs = jnp.pad(gs, ((0, 0), (0, 0), (0, 0), (0, SAMPLE_CHUNK - DEC_SEQ)))
    mlstm_job = _mlstm_sample_job(
        z_main, gs, gate_bias, state_c, state_n, state_m.reshape(DEC_BATCH, M_HEADS, 1))
    cos2, sin2 = _rope_tables()
    table_spec = pl.BlockSpec((HOST_TM, A_HEAD_DIM), lambda j, i: (i, 0))
    z_aqkv, a_s, c_s, n_s, m_s = _matmul_stream(
        h_all, w_in_t, n_cols=W_AQKV, col_start=COL_REST, b_rows_are_outputs=True, tm=HOST_TM,
        epilogue=_ep_rope, extras=(cos2, sin2), extra_specs=(table_spec, table_spec),
        side=mlstm_job, name="proj_attn")
    a_ps = (a_p, a_s.reshape(N_SAMPLE, W_MV).astype(BF16))

    heads_per_step = (8, 4, 1)
    partial_p, rows_p, rows_s = [], [], []
    yb_p = None
    rq = DEC_SEQ * A_HEADS
    qkv_s = z_aqkv[N_PROMPT:].reshape(DEC_BATCH, DEC_SEQ, 3 * N_GROUPS, A_HEADS, A_HEAD_DIM)
    qkv_s = qkv_s.transpose(0, 2, 1, 3, 4).reshape(DEC_BATCH, 3 * N_GROUPS, rq, A_HEAD_DIM)
    for g, (window, r) in enumerate(DIL_GROUPS):
        if g < N_GROUPS - 1:
            partial_p += _dilated_prompt(z_aqkv, g, r, heads_per_step[g])
        else:
            yb_p = _dilated_prompt(z_aqkv, g, r, heads_per_step[g], combine_with=partial_p)
        c0 = 3 * g * W_AH
        keep = min(window, SEQ)
        for part in (1, 2):
            cs = c0 + part * W_AH
            kept = [z_aqkv[(b + 1) * SEQ - keep:(b + 1) * SEQ, cs:cs + W_AH] for b in range(BATCH)]
            rows_p.append(jnp.stack(kept).reshape(BATCH, keep, A_HEADS, A_HEAD_DIM))
            rows_s.append(qkv_s[:, 3 * g + part].reshape(DEC_BATCH, DEC_SEQ, A_HEADS, A_HEAD_DIM))

    mem_h = _rmsnorm(mem_prompt, g_mem, BF16)
    mem_kv = _matmul(mem_h, w_mem_kv, n_cols=2 * W_XQ, tm=BATCH * MEM_LEN, name="mem_kv")
    yc_p = _cross_prompt(z_xq, mem_kv)

    xq_s = z_xq[N_PROMPT:].reshape(DEC_BATCH, DEC_SEQ * X_HEADS, X_HEAD_DIM)
    attention_job = _sample_attention_job(
        qkv_s, caches, xq_s,
        cache_mem_k.reshape(DEC_BATCH, MEM_LEN * X_HEADS, X_HEAD_DIM),
        cache_mem_v.reshape(DEC_BATCH, MEM_LEN * X_HEADS, X_HEAD_DIM))
    z_br, yb_s, yc_s = _matmul_stream(
        h_all, w_in_t, n_cols=3 * D_MODEL, col_start=COL_REST + W_AQKV + W_XQ, b_rows_are_outputs=True,
        tm=HOST_TM, side=attention_job, split_cols=HOST_SPLIT_COLS, name="proj_branch_gates")
    yb_ps = (yb_p, yb_s.reshape(N_SAMPLE, W_AH))
    yc_ps = (yc_p, yc_s.reshape(N_SAMPLE, W_XQ))

    merged = _branch_merge(a_ps, yb_ps, yc_ps, w_br_a, w_br_b, w_br_c, z_br, b_branch.reshape(1, 3 * D_MODEL))
    x1 = _matmul_stream(merged, w_out, n_cols=D_MODEL, tn=MM_TN, epilogue=_ep_residual_two_groups, extras=(x_p, x_s),
                        extra_specs=_two_group_specs(MM_TM, MM_TN, lambda j: j, SUBLANES, False),
                        name="out_proj")

    h2 = _rmsnorm(x1, g_mlp, BF16)
    u = _matmul_stream(h2, w_up, n_cols=D_FF, out_dtype=BF16, epilogue=_ep_relu2, name="mlp_up")
    x2 = _matmul_ksplit_residual(u, w_down, x1)

    mem_k = mem_kv[:, :W_XQ].reshape(BATCH, MEM_LEN, X_HEADS, X_HEAD_DIM)
    mem_v = mem_kv[:, W_XQ:].reshape(BATCH, MEM_LEN, X_HEADS, X_HEAD_DIM)
    prompt_state = (c_p, n_p, m_p[:, :, 0])
    sample_state = (c_s, n_s, m_s[:, :, 0])
    return x2, prompt_state, sample_state, rows_p, rows_s, mem_k, mem_v


def kernel(x_prompt, x_sample, state_mlstm_C, state_mlstm_n, state_mlstm_m,
           cache_win_k_g0, cache_win_v_g0, cache_win_k_g1, cache_win_v_g1,
           cache_win_k_g2, cache_win_v_g2, cache_mem_k, cache_mem_v, mem_prompt,
           g_mix, w_in, b_igate, b_fgate, b_branch, g_mem, w_mem_kv,
           w_br_a, w_br_b, w_br_c, w_out, g_mlp, w_up, w_down, g_final):
    depth = g_mix.shape[0]
    assert depth == 1, "single-layer stack"
    caches = ((cache_win_k_g0[0], cache_win_v_g0[0]),
              (cache_win_k_g1[0], cache_win_v_g1[0]),
              (cache_win_k_g2[0], cache_win_v_g2[0]))
    x2, p_state, s_state, rows_p, rows_s, mem_k, mem_v = _layer(
        x_prompt.reshape(N_PROMPT, D_MODEL), x_sample.reshape(N_SAMPLE, D_MODEL),
        mem_prompt.reshape(BATCH * MEM_LEN, D_MODEL),
        state_mlstm_C, state_mlstm_n, state_mlstm_m[0], caches, cache_mem_k[0], cache_mem_v[0],
        g_mix[0], w_in[0], b_igate[0], b_fgate[0], b_branch[0], g_mem[0], w_mem_kv[0],
        w_br_a[0], w_br_b[0], w_br_c[0], w_out[0], g_mlp[0], w_up[0], w_down[0])
    y_prompt = _rmsnorm(x2, g_final, F32, row_start=0, n_rows=N_PROMPT).reshape(BATCH, SEQ, D_MODEL)
    y_sample = _rmsnorm(x2, g_final, F32, row_start=N_PROMPT, n_rows=N_SAMPLE).reshape(DEC_BATCH, DEC_SEQ, D_MODEL)
    lead = lambda a: a[None]
    return (y_prompt, y_sample,
            lead(p_state[0]), lead(p_state[1]), lead(p_state[2]),
            *[lead(r) for r in rows_p],
            lead(mem_k), lead(mem_v),
            lead(s_state[0]), lead(s_state[1]), lead(s_state[2]),
            *[lead(r) for r in rows_s])
```

```python
import functools
from typing import Callable, NamedTuple

import numpy as np
import jax
import jax.numpy as jnp
from jax import lax
from jax.experimental import pallas as pl
from jax.experimental.pallas import tpu as pltpu

F32 = jnp.float32
BF16 = jnp.bfloat16
NEG_INF = float("-inf")

D_MODEL = 4096
BATCH = 2
SEQ = 4096
DEC_BATCH = 128
DEC_SEQ = 4
PAST_LEN = 2048
MEM_LEN = 256
M_HEADS = 8
M_DQK = D_MODEL // (2 * M_HEADS)
M_DV = D_MODEL // M_HEADS
A_HEAD_DIM = 128
A_HEADS = D_MODEL // 512
DIL_GROUPS = ((128, 1), (512, 4), (2048, 16))
N_GROUPS = 3
ROPE_THETA = 10000.0
X_HEADS = 4
X_HEAD_DIM = 128
D_FF = 4 * D_MODEL
EPS = 1e-6

W_MQK = M_HEADS * M_DQK
W_MV = M_HEADS * M_DV
W_AH = A_HEADS * A_HEAD_DIM
W_XQ = X_HEADS * X_HEAD_DIM
N_PROMPT = BATCH * SEQ
N_SAMPLE = DEC_BATCH * DEC_SEQ
N_ALL = N_PROMPT + N_SAMPLE

COL_GATES = 2 * W_MQK + 2 * W_MV
COL_REST = COL_GATES + 2 * M_HEADS
W_AQKV = 3 * N_GROUPS * W_AH

LANES = 128
SUBLANES = 8
V7X_VMEM_BYTES = 64 * 1024 * 1024
VMEM_LIMIT_BYTES = V7X_VMEM_BYTES - 4 * 1024 * 1024

MM_TM = 1088
MM_TN = 512
MM_TN_WIDE = 1024
MERGE_TM = 544
HOST_TM = 544
HOST_SPLIT_COLS = 256
DOWN_TM = 2176
DOWN_TK = 2048
NORM_ROWS = 256
CHUNK = 128
SAMPLE_CHUNK = 16
SAMPLE_SEQS_PER_BLOCK = 2
SPAN = 128
UNDILATED_BLOCKS_PER_STEP = 4


def _params(n_axes):
    return pltpu.CompilerParams(
        dimension_semantics=("arbitrary",) * n_axes,
        vmem_limit_bytes=VMEM_LIMIT_BYTES,
    )


def _rmsnorm_body(x_ref, g_ref, o_ref):
    x = x_ref[...]
    ms = jnp.mean(x * x, axis=-1, keepdims=True)
    o_ref[...] = ((x * lax.rsqrt(ms + EPS)) * g_ref[...]).astype(o_ref.dtype)


def _rmsnorm(x, g, out_dtype, *, row_start=0, n_rows=None):
    n_rows = x.shape[0] if n_rows is None else n_rows
    d = x.shape[1]
    off = row_start // NORM_ROWS
    return pl.pallas_call(
        _rmsnorm_body,
        grid=(n_rows // NORM_ROWS,),
        in_specs=[
            pl.BlockSpec((NORM_ROWS, d), lambda i: (i + off, 0)),
            pl.BlockSpec((1, d), lambda i: (0, 0)),
        ],
        out_specs=pl.BlockSpec((NORM_ROWS, d), lambda i: (i, 0)),
        out_shape=jax.ShapeDtypeStruct((n_rows, d), out_dtype),
        compiler_params=_params(1),
        name="rmsnorm",
    )(x, g.reshape(1, d))


def _rmsnorm_two_groups_body(xp_ref, xs_ref, g_ref, o_ref, *, prompt_steps):
    i = pl.program_id(0)

    @pl.when(i < prompt_steps)
    def _():
        _rmsnorm_body(xp_ref, g_ref, o_ref)

    @pl.when(i >= prompt_steps)
    def _():
        _rmsnorm_body(xs_ref, g_ref, o_ref)


def _rmsnorm_two_groups(xp, xs, g, out_dtype):
    d = xp.shape[1]
    ps, ss = xp.shape[0] // NORM_ROWS, xs.shape[0] // NORM_ROWS
    return pl.pallas_call(
        functools.partial(_rmsnorm_two_groups_body, prompt_steps=ps),
        grid=(ps + ss,),
        in_specs=[
            pl.BlockSpec((NORM_ROWS, d), lambda i: (jnp.minimum(i, ps - 1), 0)),
            pl.BlockSpec((NORM_ROWS, d), lambda i: (jnp.maximum(i - ps, 0), 0)),
            pl.BlockSpec((1, d), lambda i: (0, 0)),
        ],
        out_specs=pl.BlockSpec((NORM_ROWS, d), lambda i: (i, 0)),
        out_shape=jax.ShapeDtypeStruct((xp.shape[0] + xs.shape[0], d), out_dtype),
        compiler_params=_params(1),
        name="rmsnorm_two_groups",
    )(xp, xs, g.reshape(1, d))


def _ep_plain(acc, extra, o_ref, j):
    o_ref[...] = acc.astype(o_ref.dtype)


def _ep_residual_two_groups(acc, extra, o_ref, j):
    xp_ref, xs_ref = extra
    tm = acc.shape[0]
    n_full = N_PROMPT // tm
    rem = N_PROMPT - n_full * tm
    assert tm - rem == N_SAMPLE and rem % SUBLANES == 0
    i = pl.program_id(1)

    @pl.when(i < n_full)
    def _():
        o_ref[...] = xp_ref[...] + acc

    @pl.when(i >= n_full)
    def _():
        o_ref[0:rem, :] = xp_ref[tm - rem:tm, :] + acc[0:rem]
        o_ref[rem:tm, :] = xs_ref[...] + acc[rem:tm]


def _two_group_specs(tm, width, col_of_j, row_align, sample_block_is_constant):
    def p_map(j, i):
        row = jnp.minimum(i * tm, N_PROMPT - tm)
        col = col_of_j(j) * width
        return (pl.multiple_of(row, row_align), col if isinstance(col, int) else pl.multiple_of(col, LANES))
    p_spec = pl.BlockSpec((pl.Element(tm), pl.Element(width)), p_map)
    mode = pl.Buffered(1) if sample_block_is_constant else None
    s_spec = pl.BlockSpec((N_SAMPLE, width), lambda j, i: (0, col_of_j(j)), pipeline_mode=mode)
    return p_spec, s_spec


def _ep_relu2(acc, extra, o_ref, j):
    r = jnp.maximum(acc, 0.0)
    o_ref[...] = (r * r).astype(o_ref.dtype)


def _ep_rope(acc, extra, o_ref, j):
    cos_ref, sin_ref = extra
    tn = acc.shape[1]
    assert W_AH % tn == 0
    is_rope = (j // (W_AH // tn)) % 3 != 2
    cos2 = cos_ref[...]
    sin2 = sin_ref[...]
    for s in range(tn // A_HEAD_DIM):
        x = acc[:, s * A_HEAD_DIM:(s + 1) * A_HEAD_DIM]
        rotated = x * cos2 + pltpu.roll(x, A_HEAD_DIM // 2, axis=1) * sin2
        o_ref[:, s * A_HEAD_DIM:(s + 1) * A_HEAD_DIM] = jnp.where(is_rope, rotated, x)


def _mm_body(*refs, b_rows_are_outputs, epilogue, n_extra):
    a_ref, b_ref = refs[0], refs[1]
    extra = refs[2:2 + n_extra]
    o_ref = refs[2 + n_extra]
    b_scr = refs[3 + n_extra]

    @pl.when(pl.program_id(1) == 0)
    def _():
        b_scr[...] = b_ref[...].astype(BF16)

    if b_rows_are_outputs:
        acc = lax.dot_general(a_ref[...], b_scr[...], (((1,), (1,)), ((), ())),
                              preferred_element_type=F32)
    else:
        acc = jnp.dot(a_ref[...], b_scr[...], preferred_element_type=F32)
    epilogue(acc, extra, o_ref, pl.program_id(0))


def _matmul(a, b, *, n_cols, col_start=0, b_rows_are_outputs=False, tm=MM_TM, tn=MM_TN,
            out_dtype=F32, epilogue=_ep_plain, extras=(), extra_specs=(), name="matmul"):
    m, k = a.shape
    if b_rows_are_outputs:
        if col_start % tn == 0:
            b_spec = pl.BlockSpec((tn, k), lambda j, i: (j + col_start // tn, 0))
        else:
            assert col_start % SUBLANES == 0 and tn % SUBLANES == 0
            b_spec = pl.BlockSpec((pl.Element(tn), pl.Element(k)),
                                  lambda j, i: (pl.multiple_of(col_start + j * tn, SUBLANES), 0))
        b_block = (tn, k)
    else:
        assert col_start % tn == 0
        b_spec = pl.BlockSpec((k, tn), lambda j, i: (0, j + col_start // tn))
        b_block = (k, tn)
    return pl.pallas_call(
        functools.partial(_mm_body, b_rows_are_outputs=b_rows_are_outputs, epilogue=epilogue,
                          n_extra=len(extras)),
        grid=(n_cols // tn, m // tm),
        in_specs=[pl.BlockSpec((tm, k), lambda j, i: (i, 0)), b_spec, *extra_specs],
        out_specs=pl.BlockSpec((tm, tn), lambda j, i: (i, j)),
        out_shape=jax.ShapeDtypeStruct((m, n_cols), out_dtype),
        scratch_shapes=[pltpu.VMEM(b_block, BF16)],
        compiler_params=_params(2),
        name=name,
    )(a, b, *extras)


def _narrow_pair_body(a_ref, w0_ref, w1_ref, o0_ref, o1_ref, w0_scr, w1_scr):
    @pl.when(pl.program_id(0) == 0)
    def _():
        w0_scr[...] = w0_ref[...].astype(BF16)
        w1_scr[...] = w1_ref[...].astype(BF16)

    a = a_ref[...]
    nt = (((1,), (1,)), ((), ()))
    o0_ref[...] = lax.dot_general(a, w0_scr[...], nt, preferred_element_type=F32)
    o1_ref[...] = lax.dot_general(a, w1_scr[...], nt, preferred_element_type=F32)


def _narrow_projection_pair(a, w_t, start0, n0, start1, n1, *, tm=MM_TM, name="narrow_pair"):
    m, k = a.shape
    assert start0 % SUBLANES == 0 and start1 % SUBLANES == 0

    def w_spec(start, n):
        return pl.BlockSpec((pl.Element(n), pl.Element(k)), lambda i: (start, 0), pipeline_mode=pl.Buffered(1))

    return pl.pallas_call(
        _narrow_pair_body,
        grid=(m // tm,),
        in_specs=[pl.BlockSpec((tm, k), lambda i: (i, 0)), w_spec(start0, n0), w_spec(start1, n1)],
        out_specs=[pl.BlockSpec((tm, n0), lambda i: (i, 0)), pl.BlockSpec((tm, n1), lambda i: (i, 0))],
        out_shape=[jax.ShapeDtypeStruct((m, n0), F32), jax.ShapeDtypeStruct((m, n1), F32)],
        scratch_shapes=[pltpu.VMEM((n0, k), BF16), pltpu.VMEM((n1, k), BF16)],
        compiler_params=_params(1),
        name=name,
    )(a, w_t, w_t)


class _SideJob(NamedTuple):
    n_steps: int
    inputs: tuple
    in_specs: tuple
    out_shapes: tuple
    out_specs: tuple
    scratch_shapes: tuple
    init: Callable
    pieces: Callable


def _mm_stream_body(*refs, b_rows_are_outputs, epilogue, n_extra, slice_rows, split_cols, side):
    n_side_in = len(side.inputs) if side else 0
    n_side_out = len(side.out_shapes) if side else 0
    n_in = 2 + n_extra + n_side_in
    a_ref, bs_ref = refs[0], refs[1]
    extra = refs[2:2 + n_extra]
    side_in = refs[2 + n_extra:n_in]
    o_ref = refs[n_in]
    side_out = refs[n_in + 1:n_in + 1 + n_side_out]
    b_scr = refs[n_in + 1 + n_side_out]
    side_scr = refs[n_in + 2 + n_side_out:]
    p, i = pl.program_id(0), pl.program_id(1)

    r0 = pl.multiple_of(i * slice_rows, slice_rows)
    b_scr[p % 2, pl.ds(r0, slice_rows), :] = bs_ref[...].astype(BF16)

    if side is not None:
        @pl.when((p == 0) & (i == 0))
        def _():
            side.init(side_scr)

    def multiply(side_pieces):
        slot = (p - 1) % 2
        a = a_ref[...]
        tn = o_ref.shape[1]
        n_split = tn // split_cols
        per_split = -(-len(side_pieces) // n_split)
        for h in range(n_split):
            cols = pl.ds(h * split_cols, split_cols)
            if b_rows_are_outputs:
                acc = lax.dot_general(a, b_scr[slot, cols, :], (((1,), (1,)), ((), ())),
                                      preferred_element_type=F32)
            else:
                acc = jnp.dot(a, b_scr[slot, :, cols], preferred_element_type=F32)
            epilogue(acc, extra, o_ref.at[:, cols], (p - 1) * n_split + h)
            for piece in side_pieces[h * per_split:(h + 1) * per_split]:
                piece()

    if side is None:
        pl.when(p > 0)(lambda: multiply([]))
    else:
        t = (p - 1) * pl.num_programs(1) + i
        pl.when((p > 0) & (t < side.n_steps))(
            lambda: multiply(side.pieces(t, side_in, side_out, side_scr)))
        pl.when((p > 0) & (t >= side.n_steps))(lambda: multiply([]))


def _matmul_stream(a, b, *, n_cols, col_start=0, b_rows_are_outputs=False, tm=MM_TM, tn=MM_TN_WIDE,
                   out_dtype=F32, epilogue=_ep_plain, extras=(), extra_specs=(), side=None,
                   split_cols=MM_TN, name="matmul"):
    m, k = a.shape
    ni, nj = m // tm, n_cols // tn
    assert col_start % SUBLANES == 0 and nj * tn == n_cols

    def side_spec(spec):
        if isinstance(spec, pl.BlockSpec):
            return spec
        block_shape, index_fn = spec
        return pl.BlockSpec(
            block_shape,
            lambda p, i: index_fn(jnp.clip((p - 1) * ni + i, 0, side.n_steps - 1)))

    side_in_specs = [side_spec(s) for s in side.in_specs] if side else []
    side_out_specs = [side_spec(s) for s in side.out_specs] if side else []
    assert side is None or side.n_steps <= nj * ni

    def staged(p):
        return jnp.minimum(p, nj - 1)

    def shifted(index_map):
        return lambda p, i: index_map(jnp.maximum(p - 1, 0), jnp.where(p == 0, 0, i))

    if b_rows_are_outputs:
        tile = (tn, k)
        slice_rows = tn // ni
        bs_spec = pl.BlockSpec(
            (pl.Element(slice_rows), pl.Element(k)),
            lambda p, i: (pl.multiple_of(col_start + staged(p) * tn + i * slice_rows, SUBLANES), 0))
    else:
        assert col_start % tn == 0
        tile = (k, tn)
        slice_rows = k // ni
        bs_spec = pl.BlockSpec((slice_rows, tn), lambda p, i: (i, col_start // tn + staged(p)))
    assert slice_rows * ni == tile[0] and slice_rows % 16 == 0
    extra_specs = [pl.BlockSpec(s.block_shape, shifted(s.index_map), pipeline_mode=s.pipeline_mode)
                   for s in extra_specs]
    outs = pl.pallas_call(
        functools.partial(_mm_stream_body, b_rows_are_outputs=b_rows_are_outputs, epilogue=epilogue,
                          n_extra=len(extras), slice_rows=slice_rows, split_cols=min(tn, split_cols), side=side),
        grid=(nj + 1, ni),
        in_specs=[pl.BlockSpec((tm, k), shifted(lambda j, i: (i, 0))), bs_spec, *extra_specs,
                  *side_in_specs],
        out_specs=[pl.BlockSpec((tm, tn), shifted(lambda j, i: (i, j))), *side_out_specs],
        out_shape=[jax.ShapeDtypeStruct((m, n_cols), out_dtype), *(side.out_shapes if side else ())],
        scratch_shapes=[pltpu.VMEM((2,) + tile, BF16), *(side.scratch_shapes if side else ())],
        compiler_params=_params(2),
        name=name,
    )(a, b, *extras, *(side.inputs if side else ()))
    return outs if side else outs[0]


def _mm_ksplit_body(a_ref, b_ref, res_ref, o_ref, acc_ref):
    kk = pl.program_id(2)
    last = pl.num_programs(2) - 1

    def product():
        return jnp.dot(a_ref[...], b_ref[...].astype(BF16), preferred_element_type=F32)

    @pl.when(kk == 0)
    def _():
        acc_ref[...] = product()

    @pl.when((kk > 0) & (kk < last))
    def _():
        acc_ref[...] += product()

    @pl.when(kk == last)
    def _():
        o_ref[...] = res_ref[...] + (acc_ref[...] + product())


def _matmul_ksplit_residual(a, b, res, *, tm=DOWN_TM, tn=MM_TN, tk=DOWN_TK):
    m, k = a.shape
    n = b.shape[1]
    return pl.pallas_call(
        _mm_ksplit_body,
        grid=(n // tn, m // tm, k // tk),
        in_specs=[
            pl.BlockSpec((tm, tk), lambda j, i, kk: (i, kk)),
            pl.BlockSpec((tk, tn), lambda j, i, kk: (kk, j)),
            pl.BlockSpec((tm, tn), lambda j, i, kk: (i, j)),
        ],
        out_specs=pl.BlockSpec((tm, tn), lambda j, i, kk: (i, j)),
        out_shape=jax.ShapeDtypeStruct((m, n), F32),
        scratch_shapes=[pltpu.VMEM((tm, tn), F32)],
        compiler_params=_params(3),
        name="mlp_down",
    )(a, b, res)


def _sigmoid(x):
    return 1.0 / (1.0 + jnp.exp(-x))


def _straddle_rows(p_ref, s_ref):
    tm = p_ref.shape[0]
    rem = tm - s_ref.shape[0]
    return jnp.concatenate([p_ref[tm - rem:tm, :], s_ref[...]], axis=0)


def _merge_body(ap_ref, as_ref, ybp_ref, ybs_ref, ycp_ref, ycs_ref,
                wa0_ref, wb0_ref, wc0_ref, was_ref, wbs_ref, wcs_ref,
                ga_ref, gb_ref, gc_ref, ba_ref, bb_ref, bc_ref,
                o_ref, wa_scr, wb_scr, wc_scr):
    j, i = pl.program_id(0), pl.program_id(1)
    cur = j % 2

    @pl.when((j == 0) & (i == 0))
    def _():
        wa_scr[0] = wa0_ref[...].astype(BF16)
        wb_scr[0] = wb0_ref[...].astype(BF16)
        wc_scr[0] = wc0_ref[...].astype(BF16)

    for scr, slice_ref in ((wa_scr, was_ref), (wb_scr, wbs_ref), (wc_scr, wcs_ref)):
        rows = slice_ref.shape[0]
        scr[1 - cur, pl.ds(pl.multiple_of(i * rows, rows), rows), :] = slice_ref[...].astype(BF16)

    def merge(a, yb, yc):
        for h in range(o_ref.shape[1] // HOST_SPLIT_COLS):
            cols = pl.ds(h * HOST_SPLIT_COLS, HOST_SPLIT_COLS)
            y_a = jnp.dot(a, wa_scr[cur, :, cols], preferred_element_type=F32)
            y_b = jnp.dot(yb, wb_scr[cur, :, cols], preferred_element_type=F32)
            y_c = jnp.dot(yc, wc_scr[cur, :, cols], preferred_element_type=F32)
            g_a = _sigmoid(ga_ref[:, cols] + ba_ref[:, cols])
            g_b = _sigmoid(gb_ref[:, cols] + bb_ref[:, cols])
            g_c = _sigmoid(gc_ref[:, cols] + bc_ref[:, cols])
            o_ref[:, cols] = (g_a * y_a + g_b * y_b + g_c * y_c).astype(o_ref.dtype)

    is_prompt_tile = i < N_PROMPT // ap_ref.shape[0]

    @pl.when(is_prompt_tile)
    def _():
        merge(ap_ref[...], ybp_ref[...], ycp_ref[...])

    @pl.when(jnp.logical_not(is_prompt_tile))
    def _():
        merge(_straddle_rows(ap_ref, as_ref), _straddle_rows(ybp_ref, ybs_ref), _straddle_rows(ycp_ref, ycs_ref))


def _branch_merge(a_ps, yb_ps, yc_ps, w_a, w_b, w_c, z_br, b_branch, *, tm=MERGE_TM, tn=MM_TN):
    m = N_ALL
    assert (N_PROMPT // tm + 1) * tm == N_ALL
    nj = D_MODEL // tn

    def gate_spec(part):
        return pl.BlockSpec((tm, tn), lambda j, i: (i, part * nj + j))

    def bias_spec(part):
        return pl.BlockSpec((1, tn), lambda j, i: (0, part * nj + j))

    ni = m // tm
    widths = (W_MV, W_AH, W_XQ)
    assert all(k % (16 * ni) == 0 for k in widths)

    def first_tile_spec(k):
        return pl.BlockSpec((k, tn), lambda j, i: (0, 0), pipeline_mode=pl.Buffered(1))

    def next_slice_spec(k):
        return pl.BlockSpec((k // ni, tn), lambda j, i: (i, jnp.minimum(j + 1, nj - 1)))

    bf16_rows = 16
    return pl.pallas_call(
        _merge_body,
        grid=(nj, ni),
        in_specs=[
            *_two_group_specs(tm, W_MV, lambda j: 0, bf16_rows, True),
            *_two_group_specs(tm, W_AH, lambda j: 0, bf16_rows, True),
            *_two_group_specs(tm, W_XQ, lambda j: 0, bf16_rows, True),
            *[first_tile_spec(k) for k in widths],
            *[next_slice_spec(k) for k in widths],
            gate_spec(0), gate_spec(1), gate_spec(2),
            bias_spec(0), bias_spec(1), bias_spec(2),
        ],
        out_specs=pl.BlockSpec((tm, tn), lambda j, i: (i, j)),
        out_shape=jax.ShapeDtypeStruct((m, D_MODEL), BF16),
        scratch_shapes=[pltpu.VMEM((2, k, tn), BF16) for k in widths],
        compiler_params=_params(2),
        name="branch_merge",
    )(*a_ps, *yb_ps, *yc_ps, w_a, w_b, w_c, w_a, w_b, w_c,
      z_br, z_br, z_br, b_branch, b_branch, b_branch)


def _log_sigmoid(x):
    return jnp.minimum(x, 0.0) - jnp.log1p(jnp.exp(-jnp.abs(x)))


def _mlstm_chunk(q, k, v, irow, frow, c_state, n_state, m_state, n_valid):
    L = q.shape[0]
    ti = lax.broadcasted_iota(jnp.int32, (L, L), 0)
    si = lax.broadcasted_iota(jnp.int32, (L, L), 1)
    causal = si <= ti
    eye = si == ti
    f_b = jnp.broadcast_to(frow, (L, L))
    i_b = jnp.broadcast_to(irow, (L, L))
    bcol = jnp.sum(jnp.where(causal, f_b, 0.0), axis=1, keepdims=True)
    fcol = jnp.sum(jnp.where(eye, f_b, 0.0), axis=1, keepdims=True)
    icol = jnp.sum(jnp.where(eye, i_b, 0.0), axis=1, keepdims=True)
    brow = jnp.sum(jnp.where(ti <= si, jnp.broadcast_to(fcol, (L, L)), 0.0), axis=0, keepdims=True)

    acol = bcol + m_state
    logw = jnp.where(causal, bcol - brow + irow, NEG_INF)
    mt = jnp.maximum(acol, jnp.max(logw, axis=1, keepdims=True))
    w_inter = jnp.exp(acol - mt)
    w_intra = jnp.exp(logw - mt)

    qb = q.astype(BF16)
    kb = k.astype(BF16)
    nt = (((1,), (1,)), ((), ()))
    s = lax.dot_general(qb, kb, nt, preferred_element_type=F32) * w_intra
    inter = lax.dot_general(qb, c_state.astype(BF16), nt, preferred_element_type=F32)
    num = w_inter * inter + jnp.dot(s.astype(BF16), v.astype(BF16), preferred_element_type=F32)
    nq = w_inter * jnp.sum(q * n_state, axis=1, keepdims=True) + jnp.sum(s, axis=1, keepdims=True)
    h = num / jnp.maximum(jnp.abs(nq), jnp.exp(-mt))

    last = slice(n_valid - 1, n_valid)
    m_end = mt[last, :]
    w_c = jnp.exp(acol[last, :] - m_end)
    w_s = jnp.exp(bcol[last, :] - bcol + icol - m_end)
    if n_valid < L:
        w_s = jnp.where(lax.broadcasted_iota(jnp.int32, (L, 1), 0) < n_valid, w_s, 0.0)
    tn = (((0,), (0,)), ((), ()))
    c_new = w_c * c_state + lax.dot_general((v * w_s).astype(BF16), kb, tn, preferred_element_type=F32)
    n_new = w_c * n_state + jnp.sum(w_s * k, axis=0, keepdims=True)
    return h, c_new, n_new, m_end


def _gate_rows(g_ref_val, bias_ref, head):
    irow = g_ref_val[0:1, :] + bias_ref[0, head]
    frow = _log_sigmoid(g_ref_val[1:2, :] + bias_ref[1, head])
    return irow, frow


def _mlstm_prompt_body(bias_ref, q_ref, k_ref, v_ref, mo_ref, g_ref,
                       a_ref, c_ref, n_ref, m_ref):
    @pl.when(pl.program_id(1) == 0)
    def _():
        c_ref[...] = jnp.zeros_like(c_ref)
        n_ref[...] = jnp.zeros_like(n_ref)
        m_ref[...] = jnp.zeros_like(m_ref)

    for head in range(M_HEADS):
        qk = pl.ds(head * M_DQK, M_DQK)
        vo = pl.ds(head * M_DV, M_DV)
        one = pl.ds(head, 1)
        irow, frow = _gate_rows(g_ref[head], bias_ref, head)
        h, c_new, n_new, m_end = _mlstm_chunk(
            q_ref[:, qk], k_ref[:, qk] * (M_DQK ** -0.5), v_ref[:, vo], irow, frow,
            c_ref[0, head], n_ref[0, one, :], m_ref[0, one, 0:1], CHUNK)
        a_ref[:, vo] = (_sigmoid(mo_ref[:, vo]) * h).astype(a_ref.dtype)
        c_ref[0, head] = c_new
        n_ref[0, one, :] = n_new
        m_ref[0, one, :] = jnp.broadcast_to(m_end, (1, LANES))


def _mlstm_prompt(z_main, gates_rows, gate_bias):
    nc = SEQ // CHUNK
    return pl.pallas_call(
        _mlstm_prompt_body,
        grid=(BATCH, nc),
        in_specs=[
            pl.BlockSpec(memory_space=pltpu.SMEM),
            pl.BlockSpec((CHUNK, W_MQK), lambda b, c: (b * nc + c, 0)),
            pl.BlockSpec((CHUNK, W_MQK), lambda b, c: (b * nc + c, 1)),
            pl.BlockSpec((CHUNK, W_MV), lambda b, c: (b * nc + c, 2 * W_MQK // W_MV)),
            pl.BlockSpec((CHUNK, W_MV), lambda b, c: (b * nc + c, 2 * W_MQK // W_MV + 1)),
            pl.BlockSpec((M_HEADS, 2, CHUNK), lambda b, c: (0, 0, b * nc + c)),
        ],
        out_specs=[
            pl.BlockSpec((CHUNK, W_MV), lambda b, c: (b * nc + c, 0)),
            pl.BlockSpec((1, M_HEADS, M_DV, M_DQK), lambda b, c: (b, 0, 0, 0)),
            pl.BlockSpec((1, M_HEADS, M_DQK), lambda b, c: (b, 0, 0)),
            pl.BlockSpec((1, M_HEADS, LANES), lambda b, c: (b, 0, 0)),
        ],
        out_shape=[
            jax.ShapeDtypeStruct((N_PROMPT, W_MV), BF16),
            jax.ShapeDtypeStruct((BATCH, M_HEADS, M_DV, M_DQK), F32),
            jax.ShapeDtypeStruct((BATCH, M_HEADS, M_DQK), F32),
            jax.ShapeDtypeStruct((BATCH, M_HEADS, LANES), F32),
        ],
        compiler_params=_params(2),
        name="mlstm_prompt",
    )(gate_bias, z_main, z_main, z_main, z_main, gates_rows)


def _mlstm_sample_init(scratch):
    for scr in scratch:
        scr[...] = jnp.zeros_like(scr)


def _mlstm_sample_pieces(t, ins, outs, scratch):
    bias_ref, q_ref, k_ref, v_ref, mo_ref, g_ref, c0_ref, n0_ref, m0_ref = ins
    a_ref, c_ref, n_ref, m_ref = outs
    q_scr, k_scr, v_scr = scratch
    first_half = (t % SAMPLE_SEQS_PER_BLOCK) == 0

    def seq_rows(ref, cols):
        return jnp.where(first_half, ref[0:DEC_SEQ, cols], ref[DEC_SEQ:2 * DEC_SEQ, cols])

    def one_head(head):
        qk = pl.ds(head * M_DQK, M_DQK)
        vo = pl.ds(head * M_DV, M_DV)
        one = pl.ds(head, 1)
        q_scr[0:DEC_SEQ, qk] = seq_rows(q_ref, qk)
        k_scr[0:DEC_SEQ, qk] = seq_rows(k_ref, qk) * (M_DQK ** -0.5)
        v_scr[0:DEC_SEQ, vo] = seq_rows(v_ref, vo)
        irow, frow = _gate_rows(g_ref[0, head], bias_ref, head)
        h, c_new, n_new, m_end = _mlstm_chunk(
            q_scr[:, qk], k_scr[:, qk], v_scr[:, vo], irow, frow,
            c0_ref[0, 0, head], n0_ref[0, 0, one, :], m0_ref[0, one, :], DEC_SEQ)
        a_ref[0, :, vo] = _sigmoid(seq_rows(mo_ref, vo)) * h[0:DEC_SEQ, :]
        c_ref[0, head] = c_new
        n_ref[0, one, :] = n_new
        m_ref[0, one, :] = jnp.broadcast_to(m_end, (1, LANES))

    def all_heads():
        for head in range(M_HEADS):
            one_head(head)

    return [all_heads]


def _mlstm_sample_job(z_main, gates_rows, gate_bias, c0, n0, m0):
    per = SAMPLE_SEQS_PER_BLOCK
    rows = per * DEC_SEQ
    assert rows == SUBLANES and N_PROMPT % rows == 0
    r0 = N_PROMPT // rows
    v_blk = 2 * W_MQK // W_MV
    return _SideJob(
        n_steps=DEC_BATCH,
        inputs=(gate_bias, z_main, z_main, z_main, z_main, gates_rows, c0, n0, m0),
        in_specs=(
            pl.BlockSpec(memory_space=pltpu.SMEM),
            ((rows, W_MQK), lambda t: (r0 + t // per, 0)),
            ((rows, W_MQK), lambda t: (r0 + t // per, 1)),
            ((rows, W_MV), lambda t: (r0 + t // per, v_blk)),
            ((rows, W_MV), lambda t: (r0 + t // per, v_blk + 1)),
            ((1, M_HEADS, 2, SAMPLE_CHUNK), lambda t: (t, 0, 0, 0)),
            ((1, 1, M_HEADS, M_DV, M_DQK), lambda t: (0, t, 0, 0, 0)),
            ((1, 1, M_HEADS, M_DQK), lambda t: (0, t, 0, 0)),
            ((1, M_HEADS, 1), lambda t: (t, 0, 0)),
        ),
        out_shapes=(
            jax.ShapeDtypeStruct((DEC_BATCH, DEC_SEQ, W_MV), F32),
            jax.ShapeDtypeStruct((DEC_BATCH, M_HEADS, M_DV, M_DQK), F32),
            jax.ShapeDtypeStruct((DEC_BATCH, M_HEADS, M_DQK), F32),
            jax.ShapeDtypeStruct((DEC_BATCH, M_HEADS, LANES), F32),
        ),
        out_specs=(
            ((1, DEC_SEQ, W_MV), lambda t: (t, 0, 0)),
            ((1, M_HEADS, M_DV, M_DQK), lambda t: (t, 0, 0, 0)),
            ((1, M_HEADS, M_DQK), lambda t: (t, 0, 0)),
            ((1, M_HEADS, LANES), lambda t: (t, 0, 0)),
        ),
        scratch_shapes=(pltpu.VMEM((SAMPLE_CHUNK, W_MQK), F32), pltpu.VMEM((SAMPLE_CHUNK, W_MQK), F32),
                        pltpu.VMEM((SAMPLE_CHUNK, W_MV), F32)),
        init=_mlstm_sample_init,
        pieces=_mlstm_sample_pieces,
    )


def _dil_prompt_combine_body(q_ref, kp_ref, kc_ref, vp_ref, vc_ref, o0_ref, l0_ref, o1_ref, l1_ref,
                             y_ref, o_scr, l_scr, *, r, hb):
    _dil_prompt_body(q_ref, kp_ref, kc_ref, vp_ref, vc_ref, o_scr, l_scr, r=r, hb=hb)
    _combine_body(o0_ref, o1_ref, o_scr, l0_ref, l1_ref, l_scr, y_ref)


def _dil_prompt_heads_body(*refs, r, hb):
    ins = refs[:5 * hb]
    o_ref, l_ref = refs[5 * hb], refs[5 * hb + 1]
    scr = refs[5 * hb + 2:]
    for hh in range(hb):
        o_scr, l_scr = scr[2 * hh], scr[2 * hh + 1]
        _dil_prompt_body(*ins[5 * hh:5 * hh + 5], o_scr, l_scr, r=r, hb=1)
        cols = pl.ds(hh * A_HEAD_DIM, A_HEAD_DIM)
        o_ref[:, cols] = o_scr[...]
        l_ref[:, cols] = l_scr[...]


def _dil_prompt_body(q_ref, kp_ref, kc_ref, vp_ref, vc_ref, o_ref, l_ref, *, r, hb, sub=1):
    assert sub == 1 or r == 1
    qi = lax.broadcasted_iota(jnp.int32, (SPAN, 2 * SPAN), 0)
    ki = lax.broadcasted_iota(jnp.int32, (SPAN, 2 * SPAN), 1)
    scale = A_HEAD_DIM ** -0.5
    nt = (((1,), (1,)), ((), ()))
    for sb in range(sub):
        first_key = jnp.where(pl.program_id(2) > 0, 0, SPAN) if sb == 0 else 0
        ok = (ki >= qi) & (ki <= qi + SPAN) & (ki >= first_key)
        bias = jnp.where(ok, 0.0, NEG_INF)
        for c in range(r):
            rows = pl.ds(c, SPAN, stride=r) if r > 1 else pl.ds(sb * SPAN, SPAN)
            if sb == 0:
                p_rows = rows if r > 1 else pl.ds(0, SPAN)
                kprev, vprev = kp_ref, vp_ref
            else:
                p_rows = pl.ds((sb - 1) * SPAN, SPAN)
                kprev, vprev = kc_ref, vc_ref
            for hh in range(hb):
                cols = pl.ds(hh * A_HEAD_DIM, A_HEAD_DIM)
                q = q_ref[rows, cols].astype(BF16)
                kk = jnp.concatenate([kprev[p_rows, cols], kc_ref[rows, cols]], axis=0).astype(BF16)
                vv = jnp.concatenate([vprev[p_rows, cols], vc_ref[rows, cols]], axis=0).astype(BF16)
                s = lax.dot_general(q, kk, nt, preferred_element_type=F32) * scale + bias
                m = jnp.max(s, axis=1, keepdims=True)
                p = jnp.exp(s - m)
                den = jnp.sum(p, axis=1, keepdims=True)
                o = jnp.dot(p.astype(BF16), vv, preferred_element_type=F32) / den
                o_ref[rows, cols] = o
                l_ref[rows, cols] = jnp.broadcast_to(m + jnp.log(den), (SPAN, A_HEAD_DIM))


def _dilated_prompt(z_aqkv, g, r, hb, combine_with=None):
    sub = UNDILATED_BLOCKS_PER_STEP if r == 1 else 1
    rows = SPAN * r * sub
    nblk = SEQ // rows
    wcol = hb * A_HEAD_DIM
    per_part = W_AH // wcol

    def spec(part, prev):
        def imap(b, hg, n):
            col = (3 * g + part) * per_part + hg
            if prev:
                return ((b * nblk + n) * sub - jnp.where(n > 0, 1, 0), col)
            return (b * nblk + n, col)
        return pl.BlockSpec((rows // sub if prev else rows, wcol), imap)

    out_spec = pl.BlockSpec((rows, wcol), lambda b, hg, n: (b * nblk + n, hg))
    qkv_specs = [spec(0, False), spec(1, True), spec(1, False), spec(2, True), spec(2, False)]
    if r > 1 and hb > 1:
        assert combine_with is None

        def head_spec(part, prev, hh):
            def imap(b, hg, n):
                nn = jnp.maximum(n - 1, 0) if prev else n
                return (b * nblk + nn, (3 * g + part) * A_HEADS + hg * hb + hh)
            return pl.BlockSpec((rows, A_HEAD_DIM), imap)

        head_specs = [head_spec(part, prev, hh) for hh in range(hb)
                      for part, prev in ((0, False), (1, True), (1, False), (2, True), (2, False))]
        return pl.pallas_call(
            functools.partial(_dil_prompt_heads_body, r=r, hb=hb),
            grid=(BATCH, A_HEADS // hb, nblk),
            in_specs=head_specs,
            out_specs=[out_spec, out_spec],
            out_shape=[jax.ShapeDtypeStruct((N_PROMPT, W_AH), F32)] * 2,
            scratch_shapes=[pltpu.VMEM((rows, A_HEAD_DIM), F32)] * (2 * hb),
            compiler_params=_params(3),
            name=f"dilated_prompt_g{g}",
        )(*([z_aqkv] * (5 * hb)))
    if combine_with is None:
        return pl.pallas_call(
            functools.partial(_dil_prompt_body, r=r, hb=hb, sub=sub),
            grid=(BATCH, A_HEADS // hb, nblk),
            in_specs=qkv_specs,
            out_specs=[out_spec, out_spec],
            out_shape=[jax.ShapeDtypeStruct((N_PROMPT, W_AH), F32)] * 2,
            compiler_params=_params(3),
            name=f"dilated_prompt_g{g}",
        )(z_aqkv, z_aqkv, z_aqkv, z_aqkv, z_aqkv)
    return pl.pallas_call(
        functools.partial(_dil_prompt_combine_body, r=r, hb=hb),
        grid=(BATCH, A_HEADS // hb, nblk),
        in_specs=qkv_specs + [out_spec] * 4,
        out_specs=out_spec,
        out_shape=jax.ShapeDtypeStruct((N_PROMPT, W_AH), BF16),
        scratch_shapes=[pltpu.VMEM((rows, wcol), F32)] * 2,
        compiler_params=_params(3),
        name=f"dilated_prompt_g{g}_combine",
    )(z_aqkv, z_aqkv, z_aqkv, z_aqkv, z_aqkv, *combine_with)


def _sample_attn_pieces(t, ins, out_refs, scratch):
    (qkv_ref, k0_ref, v0_ref, k1_ref, v1_ref, k2_ref, v2_ref, b0_ref, b1_ref, b2_ref,
     xq_ref, mk_ref, mv_ref, bx_ref) = ins
    yb_ref, yc_ref = out_refs
    nt = (((1,), (1,)), ((), ()))
    outs, lses = [], []
    groups = ((k0_ref, v0_ref, b0_ref), (k1_ref, v1_ref, b1_ref), (k2_ref, v2_ref, b2_ref))

    def cross():
        s = lax.dot_general(xq_ref[0].astype(BF16), mk_ref[0].astype(BF16), nt,
                            preferred_element_type=F32) * (X_HEAD_DIM ** -0.5) + bx_ref[...]
        m = jnp.max(s, axis=1, keepdims=True)
        ex = jnp.exp(s - m)
        p = ex / jnp.sum(ex, axis=1, keepdims=True)
        yc_ref[0] = jnp.dot(p.astype(BF16), mv_ref[0].astype(BF16),
                            preferred_element_type=F32).astype(yc_ref.dtype)

    def group(g):
        kc_ref, vc_ref, bias_ref = groups[g]
        n_cached = kc_ref.shape[1] * kc_ref.shape[2]
        q = qkv_ref[0, 3 * g].astype(BF16)
        kk = jnp.concatenate([kc_ref[0].reshape(n_cached, A_HEAD_DIM), qkv_ref[0, 3 * g + 1]],
                             axis=0).astype(BF16)
        vv = jnp.concatenate([vc_ref[0].reshape(n_cached, A_HEAD_DIM), qkv_ref[0, 3 * g + 2]],
                             axis=0).astype(BF16)
        s = lax.dot_general(q, kk, nt, preferred_element_type=F32) * (A_HEAD_DIM ** -0.5) + bias_ref[...]
        m = jnp.max(s, axis=1, keepdims=True)
        p = jnp.exp(s - m)
        den = jnp.sum(p, axis=1, keepdims=True)
        outs.append(jnp.dot(p.astype(BF16), vv, preferred_element_type=F32) / den)
        lses.append(m + jnp.log(den))
        if g < N_GROUPS - 1:
            return
        mx = jnp.maximum(jnp.maximum(lses[0], lses[1]), lses[2])
        e = [jnp.exp(l - mx) for l in lses]
        tot = e[0] + e[1] + e[2]
        yb = (e[0] / tot) * outs[0] + (e[1] / tot) * outs[1] + (e[2] / tot) * outs[2]
        yb_ref[0] = yb.astype(yb_ref.dtype)

    return [cross] + [functools.partial(group, g) for g in range(N_GROUPS)]


def _dilated_sample_bias(window, r, lb, rc):
    span = window // r
    n_c = (lb // r) * rc * A_HEADS
    rows = np.arange(DEC_SEQ * A_HEADS)
    s_q, h_q = rows // A_HEADS, rows % A_HEADS
    col = np.arange(n_c)
    m_k = col // (rc * A_HEADS)
    c_k = (col % (rc * A_HEADS)) // A_HEADS
    h_k = col % A_HEADS
    delta = (lb + s_q)[:, None] - (m_k * r + c_k)[None, :]
    ok_c = (h_q[:, None] == h_k[None, :]) & (delta % r == 0) & (delta // r <= span) & (delta >= 0)
    coln = np.arange(DEC_SEQ * A_HEADS)
    s_n, h_n = coln // A_HEADS, coln % A_HEADS
    dn = s_q[:, None] - s_n[None, :]
    ok_n = (h_q[:, None] == h_n[None, :]) & (dn >= 0) & (dn % r == 0) & (dn // r <= span)
    ok = np.concatenate([ok_c, ok_n], axis=1)
    return np.where(ok, 0.0, -np.inf).astype(np.float32)


def _sample_attention_job(qkv, caches, xq, mem_k, mem_v):
    rq = DEC_SEQ * A_HEADS
    rx = DEC_SEQ * X_HEADS
    nk = MEM_LEN * X_HEADS
    cache_args, cache_specs, biases = [], [], []
    for (window, r), (cache_k, cache_v) in zip(DIL_GROUPS, caches):
        lb = cache_k.shape[1]
        assert lb % r == 0 and window % r == 0
        rc = min(r, DEC_SEQ)
        nm = lb // r
        spec = ((1, nm, rc * A_HEADS, A_HEAD_DIM), lambda t: (t, 0, 0, 0))
        for c in (cache_k, cache_v):
            cache_args.append(c.reshape(DEC_BATCH, nm, r * A_HEADS, A_HEAD_DIM))
            cache_specs.append(spec)
        biases.append(jnp.asarray(_dilated_sample_bias(window, r, lb, rc)))
    ok = (np.arange(rx) % X_HEADS)[:, None] == (np.arange(nk) % X_HEADS)[None, :]
    bias_x = jnp.asarray(np.where(ok, 0.0, -np.inf).astype(np.float32))

    def const_spec(a):
        return pl.BlockSpec(a.shape, lambda p, i: (0, 0), pipeline_mode=pl.Buffered(1))

    return _SideJob(
        n_steps=DEC_BATCH,
        inputs=(qkv, *cache_args, *biases, xq, mem_k, mem_v, bias_x),
        in_specs=(
            ((1, 3 * N_GROUPS, rq, A_HEAD_DIM), lambda t: (t, 0, 0, 0)),
            *cache_specs,
            *[const_spec(a) for a in biases],
            ((1, rx, X_HEAD_DIM), lambda t: (t, 0, 0)),
            ((1, nk, X_HEAD_DIM), lambda t: (t, 0, 0)),
            ((1, nk, X_HEAD_DIM), lambda t: (t, 0, 0)),
            const_spec(bias_x),
        ),
        out_shapes=(jax.ShapeDtypeStruct((DEC_BATCH, rq, A_HEAD_DIM), BF16),
                    jax.ShapeDtypeStruct((DEC_BATCH, rx, X_HEAD_DIM), BF16)),
        out_specs=(((1, rq, A_HEAD_DIM), lambda t: (t, 0, 0)),
                   ((1, rx, X_HEAD_DIM), lambda t: (t, 0, 0))),
        scratch_shapes=(),
        init=lambda scratch: None,
        pieces=_sample_attn_pieces,
    )


def _cross_prompt_body(q_ref, k_ref, v_ref, o_ref):
    scale = X_HEAD_DIM ** -0.5
    nt = (((1,), (1,)), ((), ()))
    for h in range(X_HEADS):
        cols = pl.ds(h * X_HEAD_DIM, X_HEAD_DIM)
        s = lax.dot_general(q_ref[:, cols].astype(BF16), k_ref[:, cols].astype(BF16), nt,
                            preferred_element_type=F32) * scale
        m = jnp.max(s, axis=1, keepdims=True)
        e = jnp.exp(s - m)
        p = e / jnp.sum(e, axis=1, keepdims=True)
        o_ref[:, cols] = jnp.dot(p.astype(BF16), v_ref[:, cols].astype(BF16),
                                 preferred_element_type=F32).astype(o_ref.dtype)


def _cross_prompt(z_xq, mem_kv, *, tq=512):
    nq = SEQ // tq
    return pl.pallas_call(
        _cross_prompt_body,
        grid=(BATCH, nq),
        in_specs=[
            pl.BlockSpec((tq, W_XQ), lambda b, i: (b * nq + i, 0)),
            pl.BlockSpec((MEM_LEN, W_XQ), lambda b, i: (b, 0)),
            pl.BlockSpec((MEM_LEN, W_XQ), lambda b, i: (b, 1)),
        ],
        out_specs=pl.BlockSpec((tq, W_XQ), lambda b, i: (b * nq + i, 0)),
        out_shape=jax.ShapeDtypeStruct((N_PROMPT, W_XQ), BF16),
        compiler_params=_params(2),
        name="cross_prompt",
    )(z_xq, mem_kv, mem_kv)


def _combine_body(o0, o1, o2, l0, l1, l2, y_ref):
    a0, a1, a2 = l0[...], l1[...], l2[...]
    mx = jnp.maximum(jnp.maximum(a0, a1), a2)
    e0, e1, e2 = jnp.exp(a0 - mx), jnp.exp(a1 - mx), jnp.exp(a2 - mx)
    tot = e0 + e1 + e2
    y = (e0 / tot) * o0[...] + (e1 / tot) * o1[...] + (e2 / tot) * o2[...]
    y_ref[...] = y.astype(y_ref.dtype)


def _rope_tables():
    pos = jnp.concatenate([
        jnp.tile(jnp.arange(SEQ, dtype=jnp.int32), BATCH),
        jnp.tile(PAST_LEN + jnp.arange(DEC_SEQ, dtype=jnp.int32), DEC_BATCH)])
    inv = ROPE_THETA ** (-jnp.arange(0, A_HEAD_DIM, 2, dtype=F32) / A_HEAD_DIM)
    ang = pos.astype(F32)[:, None] * inv[None, :]
    cos, sin = jnp.cos(ang), jnp.sin(ang)
    return jnp.concatenate([cos, cos], axis=1), jnp.concatenate([-sin, sin], axis=1)


def _layer(x_p, x_s, mem_prompt, state_c, state_n, state_m, caches, cache_mem_k, cache_mem_v,
           g_mix, w_in, b_igate, b_fgate, b_branch, g_mem, w_mem_kv,
           w_br_a, w_br_b, w_br_c, w_out, g_mlp, w_up, w_down):
    h_all = _rmsnorm_two_groups(x_p, x_s, g_mix, BF16)

    w_in_t = w_in.T
    z_main = _matmul_stream(h_all, w_in_t, n_cols=COL_GATES, b_rows_are_outputs=True, name="proj_main")
    z_gate, z_xq = _narrow_projection_pair(h_all, w_in_t, COL_GATES, 2 * M_HEADS, COL_REST + W_AQKV, W_XQ,
                                           name="proj_gates_cross_q")
    gate_bias = jnp.stack([b_igate, b_fgate]).astype(F32)

    gp = z_gate[:N_PROMPT].reshape(N_PROMPT, 2, M_HEADS).transpose(2, 1, 0)
    a_p, c_p, n_p, m_p = _mlstm_prompt(z_main, gp, gate_bias)
    gs = z_gate[N_PROMPT:].reshape(DEC_BATCH, DEC_SEQ, 2, M_HEADS).transpose(0, 3, 2, 1)
    gs = jnp.pad(gs, ((0, 0), (0, 0), (0, 0), (0, SAMPLE_CHUNK - DEC_SEQ)))
    mlstm_job = _mlstm_sample_job(
        z_main, gs, gate_bias, state_c, state_n, state_m.reshape(DEC_BATCH, M_HEADS, 1))
    cos2, sin2 = _rope_tables()
    table_spec = pl.BlockSpec((HOST_TM, A_HEAD_DIM), lambda j, i: (i, 0))
    z_aqkv, a_s, c_s, n_s, m_s = _matmul_stream(
        h_all, w_in_t, n_cols=W_AQKV, col_start=COL_REST, b_rows_are_outputs=True, tm=HOST_TM,
        epilogue=_ep_rope, extras=(cos2, sin2), extra_specs=(table_spec, table_spec),
        side=mlstm_job, name="proj_attn")
    a_ps = (a_p, a_s.reshape(N_SAMPLE, W_MV).astype(BF16))

    heads_per_step = (8, 4, 1)
    partial_p, rows_p, rows_s = [], [], []
    yb_p = None
    rq = DEC_SEQ * A_HEADS
    qkv_s = z_aqkv[N_PROMPT:].reshape(DEC_BATCH, DEC_SEQ, 3 * N_GROUPS, A_HEADS, A_HEAD_DIM)
    qkv_s = qkv_s.transpose(0, 2, 1, 3, 4).reshape(DEC_BATCH, 3 * N_GROUPS, rq, A_HEAD_DIM)
    for g, (window, r) in enumerate(DIL_GROUPS):
        if g < N_GROUPS - 1:
            partial_p += _dilated_prompt(z_aqkv, g, r, heads_per_step[g])
        else:
            yb_p = _dilated_prompt(z_aqkv, g, r, heads_per_step[g], combine_with=partial_p)
        c0 = 3 * g * W_AH
        keep = min(window, SEQ)
        for part in (1, 2):
            cs = c0 + part * W_AH
            kept = [z_aqkv[(b + 1) * SEQ - keep:(b + 1) * SEQ, cs:cs + W_AH] for b in range(BATCH)]
            rows_p.append(jnp.stack(kept).reshape(BATCH, keep, A_HEADS, A_HEAD_DIM))
            rows_s.append(qkv_s[:, 3 * g + part].reshape(DEC_BATCH, DEC_SEQ, A_HEADS, A_HEAD_DIM))

    mem_h = _rmsnorm(mem_prompt, g_mem, BF16)
    mem_kv = _matmul(mem_h, w_mem_kv, n_cols=2 * W_XQ, tm=BATCH * MEM_LEN, name="mem_kv")
    yc_p = _cross_prompt(z_xq, mem_kv)

    xq_s = z_xq[N_PROMPT:].reshape(DEC_BATCH, DEC_SEQ * X_HEADS, X_HEAD_DIM)
    attention_job = _sample_attention_job(
        qkv_s, caches, xq_s,
        cache_mem_k.reshape(DEC_BATCH, MEM_LEN * X_HEADS, X_HEAD_DIM),
        cache_mem_v.reshape(DEC_BATCH, MEM_LEN * X_HEADS, X_HEAD_DIM))
    z_br, yb_s, yc_s = _matmul_stream(
        h_all, w_in_t, n_cols=3 * D_MODEL, col_start=COL_REST + W_AQKV + W_XQ, b_rows_are_outputs=True,
        tm=HOST_TM, side=attention_job, split_cols=HOST_SPLIT_COLS, name="proj_branch_gates")
    yb_ps = (yb_p, yb_s.reshape(N_SAMPLE, W_AH))
    yc_ps = (yc_p, yc_s.reshape(N_SAMPLE, W_XQ))

    merged = _branch_merge(a_ps, yb_ps, yc_ps, w_br_a, w_br_b, w_br_c, z_br, b_branch.reshape(1, 3 * D_MODEL))
    x1 = _matmul_stream(merged, w_out, n_cols=D_MODEL, tn=MM_TN, epilogue=_ep_residual_two_groups, extras=(x_p, x_s),
                        extra_specs=_two_group_specs(MM_TM, MM_TN, lambda j: j, SUBLANES, False),
                        name="out_proj")

    h2 = _rmsnorm(x1, g_mlp, BF16)
    u = _matmul_stream(h2, w_up, n_cols=D_FF, out_dtype=BF16, epilogue=_ep_relu2, name="mlp_up")
    x2 = _matmul_ksplit_residual(u, w_down, x1)

    mem_k = mem_kv[:, :W_XQ].reshape(BATCH, MEM_LEN, X_HEADS, X_HEAD_DIM)
    mem_v = mem_kv[:, W_XQ:].reshape(BATCH, MEM_LEN, X_HEADS, X_HEAD_DIM)
    prompt_state = (c_p, n_p, m_p[:, :, 0])
    sample_state = (c_s, n_s, m_s[:, :, 0])
    return x2, prompt_state, sample_state, rows_p, rows_s, mem_k, mem_v


def kernel(x_prompt, x_sample, state_mlstm_C, state_mlstm_n, state_mlstm_m,
           cache_win_k_g0, cache_win_v_g0, cache_win_k_g1, cache_win_v_g1,
           cache_win_k_g2, cache_win_v_g2, cache_mem_k, cache_mem_v, mem_prompt,
           g_mix, w_in, b_igate, b_fgate, b_branch, g_mem, w_mem_kv,
           w_br_a, w_br_b, w_br_c, w_out, g_mlp, w_up, w_down, g_final):
    depth = g_mix.shape[0]
    assert depth == 1, "single-layer stack"
    caches = ((cache_win_k_g0[0], cache_win_v_g0[0]),
              (cache_win_k_g1[0], cache_win_v_g1[0]),
              (cache_win_k_g2[0], cache_win_v_g2[0]))
    x2, p_state, s_state, rows_p, rows_s, mem_k, mem_v = _layer(
        x_prompt.reshape(N_PROMPT, D_MODEL), x_sample.reshape(N_SAMPLE, D_MODEL),
        mem_prompt.reshape(BATCH * MEM_LEN, D_MODEL),
        state_mlstm_C, state_mlstm_n, state_mlstm_m[0], caches, cache_mem_k[0], cache_mem_v[0],
        g_mix[0], w_in[0], b_igate[0], b_fgate[0], b_branch[0], g_mem[0], w_mem_kv[0],
        w_br_a[0], w_br_b[0], w_br_c[0], w_out[0], g_mlp[0], w_up[0], w_down[0])
    y_prompt = _rmsnorm(x2, g_final, F32, row_start=0, n_rows=N_PROMPT).reshape(BATCH, SEQ, D_MODEL)
    y_sample = _rmsnorm(x2, g_final, F32, row_start=N_PROMPT, n_rows=N_SAMPLE).reshape(DEC_BATCH, DEC_SEQ, D_MODEL)
    lead = lambda a: a[None]
    return (y_prompt, y_sample,
            lead(p_state[0]), lead(p_state[1]), lead(p_state[2]),
            *[lead(r) for r in rows_p],
            lead(mem_k), lead(mem_v),
            lead(s_state[0]), lead(s_state[1]), lead(s_state[2]),
            *[lead(r) for r in rows_s])
```
